```python
import jax
import jax.numpy as jnp
from jax import lax
import numpy as np

D_MODEL = 1024
BATCH = 8
SEQ = 2048
DEPTH = 2
DEC_BATCH = 128
DEC_SEQ = 8
PAST_LEN = 16384
PAGE_SIZE = 128

N_BRANCH = 4
D_BRANCH = D_MODEL // N_BRANCH
N_GROUPS = 4
GROUP_DIM = D_BRANCH // N_GROUPS
CONV_A_W = 3
CONV_B_W = 31
POOL_WINDOWS = (2, 4, 8, 16)
POOL_MAX = 16
CHUNK = 128
D_PLE = 256
D_FF = 2816
N_EXPERTS = 8
TOP_K = 2
D_FF_EXPERT = 3584
N_DENSE = (DEPTH + 1) // 2
N_MOE = DEPTH // 2
EPS = 1e-6

OFF_A_H = 0
OFF_A_B = OFF_A_H + D_BRANCH
OFF_A_C = OFF_A_B + D_BRANCH
OFF_CF_A = OFF_A_C + D_BRANCH
OFF_CF_B = OFF_CF_A + D_BRANCH
OFF_POOL = OFF_CF_B + D_BRANCH
OFF_U = OFF_POOL + D_BRANCH
OFF_V = OFF_U + D_BRANCH
OFF_GATE = OFF_V + D_BRANCH
IN_COLS = OFF_GATE + N_BRANCH * D_MODEL

kernel_name = 'gated_conv_pool_sgu_hybrid_step'


def _rmsnorm(x, g):
    xf = x.astype(jnp.float32)
    y = xf * lax.rsqrt(jnp.mean(xf * xf, axis=-1, keepdims=True) + EPS)
    return (y * g.astype(jnp.float32)).astype(x.dtype)


def _layernorm(x, g, b):
    xf = x.astype(jnp.float32)
    xc = xf - jnp.mean(xf, axis=-1, keepdims=True)
    var = jnp.mean(xc * xc, axis=-1, keepdims=True)
    y = xc * lax.rsqrt(var + EPS) * g.astype(jnp.float32) + b.astype(jnp.float32)
    return y.astype(x.dtype)


def _cols(z, off):
    return z[..., off:off + D_BRANCH]


def _causal_depthwise_conv(x, buf, w):
    width, ch = w.shape
    xp = jnp.concatenate([buf.astype(x.dtype), x], axis=1)
    y = lax.conv_general_dilated(xp, w.astype(x.dtype)[:, None, :], window_strides=(1,), padding='VALID',
                                 dimension_numbers=('NWC', 'WIO', 'NWC'), feature_group_count=ch)
    return y, xp[:, xp.shape[1] - (width - 1):]


def _multiscale_pool(z, buf, pos0, w_pool, scale):
    n, seq, _ = z.shape
    zp = jnp.concatenate([buf.astype(z.dtype), z], axis=1)
    zf = zp.astype(jnp.float32).reshape(n, seq + POOL_MAX - 1, N_GROUPS, GROUP_DIM)
    cs = jnp.concatenate([jnp.zeros_like(zf[:, :1]), jnp.cumsum(zf, axis=1)], axis=1)
    top = cs[:, POOL_MAX:]
    pos = pos0 + jnp.arange(seq)
    means = []
    for g, w in enumerate(POOL_WINDOWS):
        s = top[:, :, g] - cs[:, POOL_MAX - w:POOL_MAX - w + seq, g]
        cnt = jnp.minimum(w, pos + 1).astype(jnp.float32)[None, :, None]
        means.append(s / cnt)
    pooled = jnp.stack(means, axis=2) - zf[:, POOL_MAX - 1:]
    y = jnp.einsum('blgc,gcd->blgd', pooled.astype(z.dtype), w_pool).reshape(n, seq, D_BRANCH)
    return y * scale, zp[:, zp.shape[1] - (POOL_MAX - 1):]


def _chunk_sgu(u, v, ws, bs):
    n, seq, _ = v.shape
    n_chunks = -(-seq // CHUNK)
    pad = n_chunks * CHUNK - seq
    vp = jnp.pad(v, ((0, 0), (0, pad), (0, 0))).reshape(n, n_chunks, CHUNK, N_GROUPS, GROUP_DIM)
    mask = jnp.tril(jnp.ones((CHUNK, CHUNK), dtype=bool))
    wm = jnp.where(mask, ws, 0).astype(v.dtype)
    s = jnp.einsum('gts,bnsgc->bntgc', wm, vp) + bs.T[:, :, None].astype(v.dtype)
    s = s.reshape(n, n_chunks * CHUNK, D_BRANCH)[:, :seq]
    last = ((seq - 1) // CHUNK) * CHUNK
    return u * s, v[:, last:]


def _swiglu(x, wg, wu, wd):
    h = jax.nn.silu(jnp.einsum('bld,df->blf', x, wg)) * jnp.einsum('bld,df->blf', x, wu)
    return jnp.einsum('blf,fd->bld', h, wd)


def _moe_swiglu(x, w_router, wg, wu, wd):
    shp = x.shape
    xt = x.reshape(-1, D_MODEL)
    logits = jnp.dot(xt.astype(jnp.float32), w_router.astype(jnp.float32))
    top_v, top_i = lax.top_k(logits, TOP_K)
    top_w = jax.nn.softmax(top_v, axis=-1)
    flat_e = top_i.reshape(-1)
    order = jnp.argsort(flat_e)
    tok = order // TOP_K
    xs = xt[tok]
    sizes = jnp.bincount(flat_e, length=N_EXPERTS).astype(jnp.int32)
    h = jax.nn.silu(lax.ragged_dot(xs, wg, sizes)) * lax.ragged_dot(xs, wu, sizes)
    ys = lax.ragged_dot(h, wd, sizes) * top_w.reshape(-1)[order][:, None].astype(x.dtype)
    y = jnp.zeros_like(xt).at[tok].add(ys)
    return y.reshape(shp)


def _trunk(x, p, buf_a, buf_b, buf_pool, pos0, W):
    n, seq, _ = x.shape
    if buf_a is None:
        buf_a = jnp.zeros((DEPTH, n, CONV_A_W - 1, D_BRANCH), x.dtype)
        buf_b = jnp.zeros((DEPTH, n, CONV_B_W - 1, D_BRANCH), x.dtype)
        buf_pool = jnp.zeros((DEPTH, n, POOL_MAX - 1, D_BRANCH), x.dtype)
    new_a, new_b, new_pool, new_v = [], [], [], []
    for i in range(DEPTH):
        xn = _rmsnorm(x, W['norm_mix_g'][i])
        z = jnp.einsum('bld,dk->blk', xn, W['w_in'][i])
        za = _cols(z, OFF_A_C) * _cols(z, OFF_A_H)
        ya, st_a = _causal_depthwise_conv(za, buf_a[i], W['conv_a_w'][i])
        br_a = _cols(z, OFF_A_B) * ya
        glu = _cols(z, OFF_CF_A) * jax.nn.sigmoid(_cols(z, OFF_CF_B))
        yb, st_b = _causal_depthwise_conv(glu, buf_b[i], W['conv_b_w'][i])
        br_b = jax.nn.silu(_layernorm(yb + W['conv_b_bias'][i], W['conv_b_ln_g'][i], W['conv_b_ln_b'][i]))
        br_c, st_p = _multiscale_pool(_cols(z, OFF_POOL), buf_pool[i], pos0, W['pool_w'][i], W['pool_scale'][i])
        v = _layernorm(_cols(z, OFF_V), W['sgu_ln_g'][i], W['sgu_ln_b'][i])
        br_d, st_v = _chunk_sgu(_cols(z, OFF_U), v, W['sgu_w'][i], W['sgu_b'][i])
        branches = jnp.stack([br_a, br_b, br_c, br_d], axis=2)
        proj = jnp.einsum('blic,icd->blid', branches, W['w_branch'][i])
        gates = jax.nn.sigmoid(z[..., OFF_GATE:].reshape(n, seq, N_BRANCH, D_MODEL))
        mixed = jnp.sum(gates * proj, axis=2)
        x = x + jnp.einsum('bld,de->ble', mixed, W['w_out'][i])
        xn = _rmsnorm(x, W['norm_ffn_g'][i])
        j = i // 2
        if i % 2 == 0:
            f = _swiglu(xn, W['ffn_w_gate'][j], W['ffn_w_up'][j], W['ffn_w_down'][j])
        else:
            f = _moe_swiglu(xn, W['router_w'][j], W['moe_w_gate'][j], W['moe_w_up'][j], W['moe_w_down'][j])
        x = x + f
        g = jax.nn.sigmoid(jnp.einsum('bld,de->ble', _rmsnorm(x, W['norm_ple_g'][i]), W['ple_w_gate'][i]))
        x = x + g * jnp.einsum('blp,pd->bld', p[i], W['ple_w_proj'][i])
        new_a.append(st_a)
        new_b.append(st_b)
        new_pool.append(st_p)
        new_v.append(st_v)
    y = _rmsnorm(x, W['final_norm_g'])
    return y, jnp.stack(new_a), jnp.stack(new_b), jnp.stack(new_pool), jnp.stack(new_v)


def setup_inputs(seed: int = 0) -> dict:
    key = jax.random.key(seed)
    ks = iter(jax.random.split(key, 48))

    def nrm(shape, scale=1.0):
        return jax.random.normal(next(ks), shape, jnp.float32) * scale

    def gain(shape):
        return 1.0 + 0.05 * nrm(shape)

    return {
        'x_prompt': nrm((BATCH, SEQ, D_MODEL)),
        'x_sample': nrm((DEC_BATCH, DEC_SEQ, D_MODEL)),
        'state_conv_a': nrm((DEPTH, DEC_BATCH, CONV_A_W - 1, D_BRANCH)),
        'state_conv_b': nrm((DEPTH, DEC_BATCH, CONV_B_W - 1, D_BRANCH)),
        'state_pool': nrm((DEPTH, DEC_BATCH, POOL_MAX - 1, D_BRANCH)),
        'p_prompt': nrm((DEPTH, BATCH, SEQ, D_PLE)),
        'p_sample': nrm((DEPTH, DEC_BATCH, DEC_SEQ, D_PLE)),
        'norm_mix_g': gain((DEPTH, D_MODEL)),
        'w_in': nrm((DEPTH, D_MODEL, IN_COLS), D_MODEL ** -0.5),
        'conv_a_w': nrm((DEPTH, CONV_A_W, D_BRANCH), CONV_A_W ** -0.5),
        'conv_b_w': nrm((DEPTH, CONV_B_W, D_BRANCH), CONV_B_W ** -0.5),
        'conv_b_bias': nrm((DEPTH, D_BRANCH), 0.02),
        'conv_b_ln_g': gain((DEPTH, D_BRANCH)),
        'conv_b_ln_b': nrm((DEPTH, D_BRANCH), 0.02),
        'pool_w': nrm((DEPTH, N_GROUPS, GROUP_DIM, GROUP_DIM), GROUP_DIM ** -0.5),
        'pool_scale': gain((DEPTH, D_BRANCH)),
        'sgu_ln_g': gain((DEPTH, D_BRANCH)),
        'sgu_ln_b': nrm((DEPTH, D_BRANCH), 0.02),
        'sgu_w': nrm((DEPTH, N_GROUPS, CHUNK, CHUNK), CHUNK ** -0.5),
        'sgu_b': gain((DEPTH, N_GROUPS, CHUNK)),
        'w_branch': nrm((DEPTH, N_BRANCH, D_BRANCH, D_MODEL), D_BRANCH ** -0.5),
        'w_out': nrm((DEPTH, D_MODEL, D_MODEL), D_MODEL ** -0.5),
        'norm_ffn_g': gain((DEPTH, D_MODEL)),
        'ffn_w_gate': nrm((N_DENSE, D_MODEL, D_FF), D_MODEL ** -0.5),
        'ffn_w_up': nrm((N_DENSE, D_MODEL, D_FF), D_MODEL ** -0.5),
        'ffn_w_down': nrm((N_DENSE, D_FF, D_MODEL), D_FF ** -0.5),
        'router_w': nrm((N_MOE, D_MODEL, N_EXPERTS), D_MODEL ** -0.5),
        'moe_w_gate': nrm((N_MOE, N_EXPERTS, D_MODEL, D_FF_EXPERT), D_MODEL ** -0.5),
        'moe_w_up': nrm((N_MOE, N_EXPERTS, D_MODEL, D_FF_EXPERT), D_MODEL ** -0.5),
        'moe_w_down': nrm((N_MOE, N_EXPERTS, D_FF_EXPERT, D_MODEL), D_FF_EXPERT ** -0.5),
        'norm_ple_g': gain((DEPTH, D_MODEL)),
        'ple_w_gate': nrm((DEPTH, D_MODEL, D_MODEL), D_MODEL ** -0.5),
        'ple_w_proj': nrm((DEPTH, D_PLE, D_MODEL), D_PLE ** -0.5),
        'final_norm_g': gain((D_MODEL,)),
    }


def reference(x_prompt, x_sample, state_conv_a, state_conv_b, state_pool, p_prompt, p_sample,
              norm_mix_g, w_in, conv_a_w, conv_b_w, conv_b_bias, conv_b_ln_g, conv_b_ln_b,
              pool_w, pool_scale, sgu_ln_g, sgu_ln_b, sgu_w, sgu_b, w_branch, w_out,
              norm_ffn_g, ffn_w_gate, ffn_w_up, ffn_w_down, router_w, moe_w_gate, moe_w_up, moe_w_down,
              norm_ple_g, ple_w_gate, ple_w_proj, final_norm_g):
    W = {
        'norm_mix_g': norm_mix_g, 'w_in': w_in, 'conv_a_w': conv_a_w, 'conv_b_w': conv_b_w,
        'conv_b_bias': conv_b_bias, 'conv_b_ln_g': conv_b_ln_g, 'conv_b_ln_b': conv_b_ln_b,
        'pool_w': pool_w, 'pool_scale': pool_scale, 'sgu_ln_g': sgu_ln_g, 'sgu_ln_b': sgu_ln_b,
        'sgu_w': sgu_w, 'sgu_b': sgu_b, 'w_branch': w_branch, 'w_out': w_out,
        'norm_ffn_g': norm_ffn_g, 'ffn_w_gate': ffn_w_gate, 'ffn_w_up': ffn_w_up, 'ffn_w_down': ffn_w_down,
        'router_w': router_w, 'moe_w_gate': moe_w_gate, 'moe_w_up': moe_w_up, 'moe_w_down': moe_w_down,
        'norm_ple_g': norm_ple_g, 'ple_w_gate': ple_w_gate, 'ple_w_proj': ple_w_proj,
        'final_norm_g': final_norm_g,
    }
    y_prompt, a_p, b_p, pool_p, v_p = _trunk(x_prompt, p_prompt, None, None, None, 0, W)
    y_sample, a_s, b_s, pool_s, v_s = _trunk(x_sample, p_sample, state_conv_a, state_conv_b, state_pool,
                                             PAST_LEN, W)
    return (y_prompt, y_sample, a_p, b_p, pool_p, v_p, a_s, b_s, pool_s, v_s)
```

```python
import functools

import jax
import jax.numpy as jnp
import numpy as np
from jax import lax
from jax.experimental import pallas as pl
from jax.experimental.pallas import tpu as pltpu

F32 = jnp.float32
BF16 = jnp.bfloat16

D_MODEL = 1024
N_BRANCH = 4
D_BRANCH = 256
N_GROUPS = 4
GROUP_DIM = 64
CONV_A_W = 3
CONV_B_W = 31
POOL_WINDOWS = (2, 4, 8, 16)
POOL_MAX = 16
CHUNK = 128
PAST_LEN = 16384
D_PLE = 256
N_EXPERTS = 8
TOP_K = 2
EPS = 1e-6

OFF_A_H = 0
OFF_A_B = 256
OFF_A_C = 512
OFF_CF_A = 768
OFF_CF_B = 1024
OFF_POOL = 1280
OFF_U = 1536
OFF_V = 1792
OFF_GATE = 2048

LANES = 128
SUBLANES = 8
TM = 512
TL = 512
NB = 64
TF_EXPERT = 1792
ROW_CHUNK = 32
SEQ_CHUNK = 8
VMEM_LIMIT = 56 * 1024 * 1024

V_CB_BIAS, V_CB_LN_G, V_CB_LN_B, V_POOL_SCALE, V_SGU_LN_G, V_SGU_LN_B = range(6)


def _rms(x, g):
    return x * lax.rsqrt(jnp.mean(x * x, axis=-1, keepdims=True) + EPS) * g


def _ln(x, g, b):
    xc = x - jnp.mean(x, axis=-1, keepdims=True)
    var = jnp.mean(xc * xc, axis=-1, keepdims=True)
    return xc * lax.rsqrt(var + EPS) * g + b


def _bdot(a, w):
    return jnp.dot(a.astype(BF16), w, preferred_element_type=F32)


def _silu(x):
    return x * jax.nn.sigmoid(x)


def _const_spec(shape):
    nd = len(shape)
    return pl.BlockSpec(shape, lambda *_: (0,) * nd, pipeline_mode=pl.Buffered(1))


def _sgu(u, v, swm, sbias):
    m = v.shape[0]
    group = lax.broadcasted_iota(jnp.int32, (CHUNK, D_BRANCH), 1) // GROUP_DIM
    outs = []
    for c in range(m // CHUNK):
        vc = v[c * CHUNK:(c + 1) * CHUNK].astype(BF16)
        r = jnp.dot(swm, vc, preferred_element_type=F32)
        s = sbias
        for g in range(N_GROUPS):
            s = s + jnp.where(group == g, r[g * CHUNK:(g + 1) * CHUNK], 0.0)
        outs.append(s)
    return u * jnp.concatenate(outs, axis=0)


def _gated_merge(x, xnb, branches, win_ref, wbr_ref, wout_ref):
    mixed = None
    for i in range(N_BRANCH):
        lo = OFF_GATE + i * D_MODEL
        gate = jax.nn.sigmoid(jnp.dot(xnb, win_ref[:, lo:lo + D_MODEL], preferred_element_type=F32))
        term = gate * _bdot(branches[i], wbr_ref[i])
        mixed = term if mixed is None else mixed + term
    return x + _bdot(mixed, wout_ref[...])


def _pool_lane_windows(half):
    small, big = POOL_WINDOWS[2 * half], POOL_WINDOWS[2 * half + 1]
    lane = lax.broadcasted_iota(jnp.int32, (1, LANES), 1)
    is_big = lane >= GROUP_DIM
    return small, big, is_big


def _mixer_prompt_kernel(x_ref, g_ref, win_ref, caw_ref, cbw_ref, vec_ref, pw_ref, sw_ref, smask_ref,
                         sbias_ref, wbr_ref, wout_ref,
                         xo_ref, sta_ref, stb_ref, stp_ref, stv_ref,
                         sa, sb, sp):
    i = pl.program_id(1)
    ha, hb, hp = SUBLANES, 4 * SUBLANES, 2 * SUBLANES

    @pl.when(i == 0)
    def _():
        sa[0:ha, :] = jnp.zeros((ha, D_BRANCH), F32)
        sb[0:hb, :] = jnp.zeros((hb, D_BRANCH), F32)
        sp[0:hp, :] = jnp.zeros((hp, D_BRANCH), F32)

    @pl.when(i > 0)
    def _():
        sa[0:ha, :] = sa[TL:TL + ha, :]
        sb[0:hb, :] = sb[TL:TL + hb, :]
        sp[0:hp, :] = sp[TL:TL + hp, :]

    x = x_ref[...]
    xnb = _rms(x, g_ref[...]).astype(BF16)
    vec = vec_ref[...]

    def zc(off):
        return jnp.dot(xnb, win_ref[:, off:off + D_BRANCH], preferred_element_type=F32)

    def tap_conv(s_ref, base, w_ref, ntaps):
        outs = []
        for c in range(TL // ROW_CHUNK):
            acc = None
            for k in range(ntaps):
                term = s_ref[pl.ds(base + k + c * ROW_CHUNK, ROW_CHUNK), :] * w_ref[k:k + 1, :]
                acc = term if acc is None else acc + term
            outs.append(acc)
        return jnp.concatenate(outs, axis=0)

    sa[ha:ha + TL, :] = zc(OFF_A_C) * zc(OFF_A_H)
    br_a = zc(OFF_A_B) * tap_conv(sa, ha - (CONV_A_W - 1), caw_ref, CONV_A_W)
    sta_ref[0] = sa[TL + ha - (CONV_A_W - 1):TL + ha, :]

    sb[hb:hb + TL, :] = zc(OFF_CF_A) * jax.nn.sigmoid(zc(OFF_CF_B))
    yb = tap_conv(sb, hb - (CONV_B_W - 1), cbw_ref, CONV_B_W) + vec[V_CB_BIAS:V_CB_BIAS + 1]
    br_b = _silu(_ln(yb, vec[V_CB_LN_G:V_CB_LN_G + 1], vec[V_CB_LN_B:V_CB_LN_B + 1]))
    stb_ref[0] = sb[TL + hb - (CONV_B_W - 1):TL + hb, :]

    sp[hp:hp + TL, :] = zc(OFF_POOL)
    halves = []
    for half in range(2):
        small, big, is_big = _pool_lane_windows(half)
        lanes = slice(half * LANES, (half + 1) * LANES)
        wl = jnp.where(is_big, float(big), float(small))
        outs = []
        for c in range(TL // ROW_CHUNK):
            r0 = hp + c * ROW_CHUNK
            acc = None
            for j in range(big):
                seg = sp[pl.ds(r0 - j, ROW_CHUNK), lanes]
                if j >= small:
                    seg = jnp.where(is_big, seg, 0.0)
                acc = seg if acc is None else acc + seg
            pos = i * TL + c * ROW_CHUNK + lax.broadcasted_iota(jnp.int32, (ROW_CHUNK, LANES), 0)
            cnt = jnp.minimum(wl, (pos + 1).astype(F32))
            outs.append(acc / cnt - sp[pl.ds(r0, ROW_CHUNK), lanes])
        halves.append(jnp.concatenate(outs, axis=0))
    pooled = jnp.concatenate(halves, axis=1)
    br_c = _bdot(pooled, pw_ref[...]) * vec[V_POOL_SCALE:V_POOL_SCALE + 1]
    stp_ref[0] = sp[TL + hp - (POOL_MAX - 1):TL + hp, :]

    v = _ln(zc(OFF_V), vec[V_SGU_LN_G:V_SGU_LN_G + 1], vec[V_SGU_LN_B:V_SGU_LN_B + 1])
    stv_ref[0] = v[TL - CHUNK:TL]
    swm = jnp.where(smask_ref[...] > 0, sw_ref[...], 0.0).astype(BF16)
    br_d = _sgu(zc(OFF_U), v, swm, sbias_ref[...])

    xo_ref[...] = _gated_merge(x, xnb, (br_a, br_b, br_c, br_d), win_ref, wbr_ref, wout_ref)


def _mixer_sample_kernel(x_ref, g_ref, win_ref, caw_ref, cbw_ref, vec_ref, pw_ref, sw_ref, smask_ref,
                         sbias_ref, wbr_ref, wout_ref, ina_ref, inb_ref, inp_ref,
                         xo_ref, sta_ref, stb_ref, stp_ref, stv_ref,
                         sa, sb, sp):
    seq = SUBLANES
    ha, hb, hp = SUBLANES, 4 * SUBLANES, 2 * SUBLANES
    sa[:, ha - (CONV_A_W - 1):ha, :] = ina_ref[...]
    sb[:, hb - (CONV_B_W - 1):hb, :] = inb_ref[...]
    sp[:, hp - (POOL_MAX - 1):hp, :] = inp_ref[...]

    x = x_ref[...]
    xnb = _rms(x, g_ref[...]).astype(BF16)
    vec = vec_ref[...]

    def zc(off):
        return jnp.dot(xnb, win_ref[:, off:off + D_BRANCH], preferred_element_type=F32)

    def to3(a):
        return a.reshape(NB, seq, D_BRANCH)

    def to2(a):
        return a.reshape(NB * seq, D_BRANCH)

    def tap_conv(s_ref, base, w_ref, ntaps):
        outs = []
        for c in range(NB // SEQ_CHUNK):
            acc = None
            for k in range(ntaps):
                seg = s_ref[c * SEQ_CHUNK:(c + 1) * SEQ_CHUNK, pl.ds(base + k, seq), :]
                term = seg * w_ref[k:k + 1, :][None]
                acc = term if acc is None else acc + term
            outs.append(acc)
        return to2(jnp.concatenate(outs, axis=0))

    sa[:, ha:ha + seq, :] = to3(zc(OFF_A_C) * zc(OFF_A_H))
    br_a = zc(OFF_A_B) * tap_conv(sa, ha - (CONV_A_W - 1), caw_ref, CONV_A_W)
    sta_ref[...] = sa[:, ha + seq - (CONV_A_W - 1):ha + seq, :]

    sb[:, hb:hb + seq, :] = to3(zc(OFF_CF_A) * jax.nn.sigmoid(zc(OFF_CF_B)))
    yb = tap_conv(sb, hb - (CONV_B_W - 1), cbw_ref, CONV_B_W) + vec[V_CB_BIAS:V_CB_BIAS + 1]
    br_b = _silu(_ln(yb, vec[V_CB_LN_G:V_CB_LN_G + 1], vec[V_CB_LN_B:V_CB_LN_B + 1]))
    stb_ref[...] = sb[:, hb + seq - (CONV_B_W - 1):hb + seq, :]

    sp[:, hp:hp + seq, :] = to3(zc(OFF_POOL))
    halves = []
    for half in range(2):
        small, big, is_big = _pool_lane_windows(half)
        lanes = slice(half * LANES, (half + 1) * LANES)
        wl = jnp.where(is_big, float(big), float(small))[None]
        outs = []
        for c in range(NB // SEQ_CHUNK):
            rows = slice(c * SEQ_CHUNK, (c + 1) * SEQ_CHUNK)
            acc = None
            for j in range(big):
                seg = sp[rows, pl.ds(hp - j, seq), lanes]
                if j >= small:
                    seg = jnp.where(is_big[None], seg, 0.0)
                acc = seg if acc is None else acc + seg
            outs.append(acc / wl - sp[rows, pl.ds(hp, seq), lanes])
        halves.append(jnp.concatenate(outs, axis=0).reshape(NB * seq, LANES))
    pooled = jnp.concatenate(halves, axis=1)
    br_c = _bdot(pooled, pw_ref[...]) * vec[V_POOL_SCALE:V_POOL_SCALE + 1]
    stp_ref[...] = sp[:, hp + seq - (POOL_MAX - 1):hp + seq, :]

    v = _ln(zc(OFF_V), vec[V_SGU_LN_G:V_SGU_LN_G + 1], vec[V_SGU_LN_B:V_SGU_LN_B + 1])
    stv_ref[...] = to3(v)
    swm = jnp.where(smask_ref[...] > 0, sw_ref[...], 0.0).astype(BF16)
    br_d = _sgu(zc(OFF_U), v, swm, sbias_ref[...])

    xo_ref[...] = _gated_merge(x, xnb, (br_a, br_b, br_c, br_d), win_ref, wbr_ref, wout_ref)


def _mixer_weight_specs():
    return [
        _const_spec((1, D_MODEL)),
        _const_spec((D_MODEL, OFF_GATE + N_BRANCH * D_MODEL)),
        _const_spec((CONV_A_W, D_BRANCH)),
        _const_spec((CONV_B_W, D_BRANCH)),
        _const_spec((SUBLANES, D_BRANCH)),
        _const_spec((D_BRANCH, D_BRANCH)),
        _const_spec((N_GROUPS * CHUNK, CHUNK)),
        _const_spec((N_GROUPS * CHUNK, CHUNK)),
        _const_spec((CHUNK, D_BRANCH)),
        _const_spec((N_BRANCH, D_BRANCH, D_MODEL)),
        _const_spec((D_MODEL, D_MODEL)),
    ]


def _mixer_prompt(xbuf, lw, batch, seq):
    nt = seq // TL
    rows = lambda b, i: (b * nt + i, 0)
    st = lambda n: pl.BlockSpec((1, n, D_BRANCH), lambda b, i: (b, 0, 0))
    return pl.pallas_call(
        _mixer_prompt_kernel,
        grid=(batch, nt),
        in_specs=[pl.BlockSpec((TL, D_MODEL), rows)] + _mixer_weight_specs(),
        out_specs=[pl.BlockSpec((TL, D_MODEL), rows), st(CONV_A_W - 1), st(CONV_B_W - 1),
                   st(POOL_MAX - 1), st(CHUNK)],
        out_shape=[jax.ShapeDtypeStruct(xbuf.shape, F32),
                   jax.ShapeDtypeStruct((batch, CONV_A_W - 1, D_BRANCH), F32),
                   jax.ShapeDtypeStruct((batch, CONV_B_W - 1, D_BRANCH), F32),
                   jax.ShapeDtypeStruct((batch, POOL_MAX - 1, D_BRANCH), F32),
                   jax.ShapeDtypeStruct((batch, CHUNK, D_BRANCH), F32)],
        scratch_shapes=[pltpu.VMEM((TL + SUBLANES, D_BRANCH), F32),
                        pltpu.VMEM((TL + 4 * SUBLANES, D_BRANCH), F32),
                        pltpu.VMEM((TL + 2 * SUBLANES, D_BRANCH), F32)],
        input_output_aliases={0: 0},
        compiler_params=pltpu.CompilerParams(dimension_semantics=("arbitrary", "arbitrary"),
                                             vmem_limit_bytes=VMEM_LIMIT),
        name="mixer_prompt",
    )(xbuf, lw["norm_mix_g"], lw["w_in"], lw["conv_a_w"], lw["conv_b_w"], lw["vec"], lw["pool_bd"],
      lw["sgu_w_p"], lw["sgu_mask_p"], lw["sgu_bias_p"], lw["w_branch"], lw["w_out"])


def _mixer_sample(xbuf, lw, st_a, st_b, st_p, row0, nseq, seq):
    assert seq == SUBLANES and nseq % NB == 0 and row0 % (NB * seq) == 0
    blk0 = row0 // (NB * seq)
    rows = lambda i: (blk0 + i, 0)
    st = lambda n: pl.BlockSpec((NB, n, D_BRANCH), lambda i: (i, 0, 0))
    return pl.pallas_call(
        _mixer_sample_kernel,
        grid=(nseq // NB,),
        in_specs=[pl.BlockSpec((NB * seq, D_MODEL), rows)] + _mixer_weight_specs()
                 + [st(CONV_A_W - 1), st(CONV_B_W - 1), st(POOL_MAX - 1)],
        out_specs=[pl.BlockSpec((NB * seq, D_MODEL), rows), st(CONV_A_W - 1), st(CONV_B_W - 1),
                   st(POOL_MAX - 1), st(seq)],
        out_shape=[jax.ShapeDtypeStruct(xbuf.shape, F32),
                   jax.ShapeDtypeStruct((nseq, CONV_A_W - 1, D_BRANCH), F32),
                   jax.ShapeDtypeStruct((nseq, CONV_B_W - 1, D_BRANCH), F32),
                   jax.ShapeDtypeStruct((nseq, POOL_MAX - 1, D_BRANCH), F32),
                   jax.ShapeDtypeStruct((nseq, seq, D_BRANCH), F32)],
        scratch_shapes=[pltpu.VMEM((NB, 2 * SUBLANES, D_BRANCH), F32),
                        pltpu.VMEM((NB, 5 * SUBLANES, D_BRANCH), F32),
                        pltpu.VMEM((NB, 3 * SUBLANES, D_BRANCH), F32)],
        input_output_aliases={0: 0},
        compiler_params=pltpu.CompilerParams(dimension_semantics=("arbitrary",),
                                             vmem_limit_bytes=VMEM_LIMIT),
        name="mixer_sample",
    )(xbuf, lw["norm_mix_g"], lw["w_in"], lw["conv_a_w"], lw["conv_b_w"], lw["vec"], lw["pool_bd"],
      lw["sgu_w_s"], lw["sgu_mask_s"], lw["sgu_bias_s"], lw["w_branch"], lw["w_out"],
      st_a, st_b, st_p)


def _ple_tail(x, p, gple_ref, wpg_ref, wpp_ref):
    gate = jax.nn.sigmoid(_bdot(_rms(x, gple_ref[...]), wpg_ref[...]))
    return x + gate * _bdot(p, wpp_ref[...])


def _dense_ffn_kernel(n_chunks, x_ref, p_ref, gffn_ref, wg_ref, wu_ref, wd_ref, gple_ref, wpg_ref, wpp_ref,
                      xo_ref):
    x = x_ref[...]
    xnb = _rms(x, gffn_ref[...]).astype(BF16)
    tf = wg_ref.shape[1] // n_chunks
    f = None
    for c in range(n_chunks):
        cols = slice(c * tf, (c + 1) * tf)
        h = _silu(jnp.dot(xnb, wg_ref[:, cols], preferred_element_type=F32)) * \
            jnp.dot(xnb, wu_ref[:, cols], preferred_element_type=F32)
        t = _bdot(h, wd_ref[cols, :])
        f = t if f is None else f + t
    xo_ref[...] = _ple_tail(x + f, p_ref[...], gple_ref, wpg_ref, wpp_ref)


def _dense_ffn(xbuf, pbuf, lw):
    t = xbuf.shape[0]
    d_ff = lw["ffn_w_gate"].shape[1]
    rows = lambda i: (i, 0)
    return pl.pallas_call(
        functools.partial(_dense_ffn_kernel, 2),
        grid=(t // TM,),
        in_specs=[pl.BlockSpec((TM, D_MODEL), rows), pl.BlockSpec((TM, D_PLE), rows),
                  _const_spec((1, D_MODEL)), _const_spec((D_MODEL, d_ff)), _const_spec((D_MODEL, d_ff)),
                  _const_spec((d_ff, D_MODEL)), _const_spec((1, D_MODEL)),
                  _const_spec((D_MODEL, D_MODEL)), _const_spec((D_PLE, D_MODEL))],
        out_specs=pl.BlockSpec((TM, D_MODEL), rows),
        out_shape=jax.ShapeDtypeStruct(xbuf.shape, F32),
        input_output_aliases={0: 0},
        compiler_params=pltpu.CompilerParams(dimension_semantics=("arbitrary",),
                                             vmem_limit_bytes=VMEM_LIMIT),
        name="dense_ffn_ple",
    )(xbuf, pbuf, lw["norm_ffn_g"], lw["ffn_w_gate"], lw["ffn_w_up"], lw["ffn_w_down"],
      lw["norm_ple_g"], lw["ple_w_gate"], lw["ple_w_proj"])


def _router_kernel(x_ref, g_ref, wr_ref, idx_ref, wgt_ref):
    xn = _rms(x_ref[...], g_ref[...])
    logits = jnp.dot(xn, wr_ref[...], preferred_element_type=F32, precision=lax.Precision.HIGHEST)
    lane = lax.broadcasted_iota(jnp.int32, logits.shape, 1)
    lane_f = lane.astype(F32)
    logits = jnp.where(lane < N_EXPERTS, logits, -jnp.inf)
    m1 = jnp.max(logits, axis=-1, keepdims=True)
    i1 = jnp.min(jnp.where(logits == m1, lane_f, float(LANES)), axis=-1, keepdims=True)
    rest = jnp.where(lane_f == i1, -jnp.inf, logits)
    m2 = jnp.max(rest, axis=-1, keepdims=True)
    i2 = jnp.min(jnp.where(rest == m2, lane_f, float(LANES)), axis=-1, keepdims=True)
    e2 = jnp.exp(m2 - m1)
    den = 1.0 + e2
    idx_ref[...] = jnp.where(lane == 0, i1, jnp.where(lane == 1, i2, 0.0)).astype(jnp.int32)
    wgt_ref[...] = jnp.where(lane == 0, 1.0 / den, jnp.where(lane == 1, e2 / den, 0.0))


def _router(xbuf, lw):
    t = xbuf.shape[0]
    rows = lambda i: (i, 0)
    return pl.pallas_call(
        _router_kernel,
        grid=(t // TM,),
        in_specs=[pl.BlockSpec((TM, D_MODEL), rows), _const_spec((1, D_MODEL)),
                  _const_spec((D_MODEL, LANES))],
        out_specs=[pl.BlockSpec((TM, LANES), rows), pl.BlockSpec((TM, LANES), rows)],
        out_shape=[jax.ShapeDtypeStruct((t, LANES), jnp.int32), jax.ShapeDtypeStruct((t, LANES), F32)],
        compiler_params=pltpu.CompilerParams(dimension_semantics=("arbitrary",)),
        name="router",
    )(xbuf, lw["norm_ffn_g"], lw["router_w"])


def _row_gather(idx_ref, src_hbm, dst, sem, n_rows):
    def issue(r, carry):
        pltpu.make_async_copy(src_hbm.at[pl.ds(idx_ref[0, r], 1)], dst.at[pl.ds(r, 1)], sem).start()
        return carry
    lax.fori_loop(0, n_rows, issue, 0, unroll=8)
    pltpu.make_async_copy(src_hbm.at[pl.ds(0, n_rows)], dst, sem).wait()


def _dispatch_kernel(tok_ref, x_hbm, g_ref, xs_ref, buf, sem):
    _row_gather(tok_ref, x_hbm, buf, sem, TM)
    xs_ref[...] = _rms(buf[...], g_ref[...]).astype(BF16)


def _dispatch(xbuf, slot_tok, lw):
    n_tiles = slot_tok.shape[0]
    return pl.pallas_call(
        _dispatch_kernel,
        grid=(n_tiles,),
        in_specs=[pl.BlockSpec((None, 1, TM), lambda i: (i, 0, 0), memory_space=pltpu.SMEM),
                  pl.BlockSpec(memory_space=pl.ANY), _const_spec((1, D_MODEL))],
        out_specs=pl.BlockSpec((TM, D_MODEL), lambda i: (i, 0)),
        out_shape=jax.ShapeDtypeStruct((n_tiles * TM, D_MODEL), BF16),
        scratch_shapes=[pltpu.VMEM((TM, D_MODEL), F32), pltpu.SemaphoreType.DMA(())],
        compiler_params=pltpu.CompilerParams(dimension_semantics=("arbitrary",)),
        name="moe_dispatch",
    )(slot_tok, xbuf, lw["norm_ffn_g"])


def _expert_kernel(te_ref, tr_ref, na_ref, xs_ref, wg_ref, wu_ref, wd_ref, ys_ref):
    i, j = pl.program_id(0), pl.program_id(1)

    @pl.when(i < na_ref[0])
    def _():
        xs = xs_ref[...]
        h = _silu(jnp.dot(xs, wg_ref[...], preferred_element_type=F32)) * \
            jnp.dot(xs, wu_ref[...], preferred_element_type=F32)
        part = _bdot(h, wd_ref[...])

        @pl.when(j == 0)
        def _():
            ys_ref[...] = part

        @pl.when(j > 0)
        def _():
            ys_ref[...] += part

    @pl.when(jnp.logical_and(i >= na_ref[0], j == 0))
    def _():
        ys_ref[...] = jnp.zeros(ys_ref.shape, F32)


def _experts(xs, tile_expert, tile_row, n_active, lw):
    n_tiles = tile_expert.shape[0]
    d_ff = lw["moe_w_gate"].shape[2]
    n_fc = d_ff // TF_EXPERT

    def fcol(i, j, na):
        return jnp.where(i < na[0], j, n_fc - 1)

    grid_spec = pltpu.PrefetchScalarGridSpec(
        num_scalar_prefetch=3,
        grid=(n_tiles, n_fc),
        in_specs=[
            pl.BlockSpec((TM, D_MODEL), lambda i, j, te, tr, na: (tr[i], 0)),
            pl.BlockSpec((None, D_MODEL, TF_EXPERT), lambda i, j, te, tr, na: (te[i], 0, fcol(i, j, na))),
            pl.BlockSpec((None, D_MODEL, TF_EXPERT), lambda i, j, te, tr, na: (te[i], 0, fcol(i, j, na))),
            pl.BlockSpec((None, TF_EXPERT, D_MODEL), lambda i, j, te, tr, na: (te[i], fcol(i, j, na), 0)),
        ],
        out_specs=pl.BlockSpec((TM, D_MODEL), lambda i, j, te, tr, na: (i, 0)),
    )
    return pl.pallas_call(
        _expert_kernel,
        grid_spec=grid_spec,
        out_shape=jax.ShapeDtypeStruct((n_tiles * TM, D_MODEL), F32),
        compiler_params=pltpu.CompilerParams(dimension_semantics=("arbitrary", "arbitrary"),
                                             vmem_limit_bytes=VMEM_LIMIT),
        name="moe_experts",
    )(tile_expert, tile_row, n_active, xs, lw["moe_w_gate"], lw["moe_w_up"], lw["moe_w_down"])


def _combine_kernel(final, pos_ref, x_ref, p_ref, wgt_ref, ys_hbm, gple_ref, wpg_ref, wpp_ref, gfin_ref,
                    o_ref, buf0, buf1, sem0, sem1):
    def issue(r, carry):
        pltpu.make_async_copy(ys_hbm.at[pl.ds(pos_ref[0, r], 1)], buf0.at[pl.ds(r, 1)], sem0).start()
        pltpu.make_async_copy(ys_hbm.at[pl.ds(pos_ref[1, r], 1)], buf1.at[pl.ds(r, 1)], sem1).start()
        return carry
    lax.fori_loop(0, TM, issue, 0, unroll=8)
    pltpu.make_async_copy(ys_hbm.at[pl.ds(0, TM)], buf0, sem0).wait()
    pltpu.make_async_copy(ys_hbm.at[pl.ds(0, TM)], buf1, sem1).wait()
    wgt = wgt_ref[...]
    x = x_ref[...] + (buf0[...] * wgt[:, 0:1] + buf1[...] * wgt[:, 1:2])
    x = _ple_tail(x, p_ref[...], gple_ref, wpg_ref, wpp_ref)
    o_ref[...] = _rms(x, gfin_ref[...]) if final else x


def _combine(xbuf, pbuf, pos, wgt, ys, lw, final_g, row0, n_rows):
    blk0 = row0 // TM
    rows = lambda i: (blk0 + i, 0)
    final = final_g is not None
    gfin = final_g if final else lw["norm_ple_g"]
    return pl.pallas_call(
        functools.partial(_combine_kernel, final),
        grid=(n_rows // TM,),
        in_specs=[pl.BlockSpec((None, TOP_K, TM), lambda i: (blk0 + i, 0, 0), memory_space=pltpu.SMEM),
                  pl.BlockSpec((TM, D_MODEL), rows), pl.BlockSpec((TM, D_PLE), rows),
                  pl.BlockSpec((TM, LANES), rows), pl.BlockSpec(memory_space=pl.ANY),
                  _const_spec((1, D_MODEL)), _const_spec((D_MODEL, D_MODEL)),
                  _const_spec((D_PLE, D_MODEL)), _const_spec((1, D_MODEL))],
        out_specs=pl.BlockSpec((TM, D_MODEL), lambda i: (i, 0)),
        out_shape=jax.ShapeDtypeStruct((n_rows, D_MODEL), F32),
        scratch_shapes=[pltpu.VMEM((TM, D_MODEL), F32), pltpu.VMEM((TM, D_MODEL), F32),
                        pltpu.SemaphoreType.DMA(()), pltpu.SemaphoreType.DMA(())],
        compiler_params=pltpu.CompilerParams(dimension_semantics=("arbitrary",),
                                             vmem_limit_bytes=VMEM_LIMIT),
        name="moe_combine_ple",
    )(pos, xbuf, pbuf, wgt, ys, lw["norm_ple_g"], lw["ple_w_gate"], lw["ple_w_proj"], gfin)


def _routing_tables(idx, n_tok):
    n_pairs = n_tok * TOP_K
    n_tiles = n_pairs // TM + N_EXPERTS
    e_flat = idx[:, :TOP_K].reshape(-1)
    onehot = (e_flat[:, None] == jnp.arange(N_EXPERTS, dtype=jnp.int32)[None, :]).astype(jnp.int32)
    csum = jnp.cumsum(onehot, axis=0)
    sizes = csum[-1]
    padded = ((sizes + TM - 1) // TM) * TM
    pend = jnp.cumsum(padded)
    pstart = pend - padded
    pos = jnp.sum(onehot * (csum - 1 + pstart[None, :]), axis=1)
    slot_tok = jnp.zeros((n_tiles * TM,), jnp.int32).at[pos].set(
        jnp.arange(n_pairs, dtype=jnp.int32) // TOP_K, unique_indices=True)
    n_active = (pend[-1] // TM).astype(jnp.int32)
    tile_row = jnp.minimum(jnp.arange(n_tiles, dtype=jnp.int32), n_active - 1)
    tile_expert = jnp.minimum(
        jnp.sum((tile_row[:, None] * TM >= pend[None, :]).astype(jnp.int32), axis=1), N_EXPERTS - 1)
    pos_tiles = pos.reshape(n_tok // TM, TM, TOP_K).transpose(0, 2, 1)
    return slot_tok.reshape(n_tiles, 1, TM), tile_expert, tile_row, n_active.reshape(1), pos_tiles


def _sgu_tables(sgu_w, sgu_b, seq):
    r = np.arange(CHUNK)
    bias = jnp.repeat(sgu_b.T, GROUP_DIM, axis=1)
    if seq >= CHUNK:
        w = sgu_w
        mask = r[None, :] <= r[:, None]
    else:
        reps = CHUNK // seq
        w = jnp.tile(sgu_w[:, :seq, :seq], (1, reps, reps))
        mask = (r[:, None] // seq == r[None, :] // seq) & (r[None, :] % seq <= r[:, None] % seq)
        bias = jnp.tile(bias[:seq], (reps, 1))
    mask = np.tile(mask.astype(np.float32), (N_GROUPS, 1))
    return w.reshape(N_GROUPS * CHUNK, CHUNK), jnp.asarray(mask), bias


def _block_diag(w):
    out = jnp.zeros((D_BRANCH, D_BRANCH), w.dtype)
    for g in range(N_GROUPS):
        out = out.at[g * GROUP_DIM:(g + 1) * GROUP_DIM, g * GROUP_DIM:(g + 1) * GROUP_DIM].set(w[g])
    return out


def kernel(x_prompt, x_sample, state_conv_a, state_conv_b, state_pool, p_prompt, p_sample, norm_mix_g, w_in, conv_a_w, conv_b_w, conv_b_bias, conv_b_ln_g, conv_b_ln_b, pool_w, pool_scale, sgu_ln_g, sgu_ln_b, sgu_w, sgu_b, w_branch, w_out, norm_ffn_g, ffn_w_gate, ffn_w_up, ffn_w_down, router_w, moe_w_gate, moe_w_up, moe_w_down, norm_ple_g, ple_w_gate, ple_w_proj, final_norm_g):
    batch, seq, _ = x_prompt.shape
    nseq, dseq, _ = x_sample.shape
    depth = w_in.shape[0]
    assert depth % 2 == 0
    assert PAST_LEN + 1 >= POOL_MAX
    assert seq % TL == 0 and TL % CHUNK == 0 and TL >= CONV_B_W + 1
    n_prompt, n_sample = batch * seq, nseq * dseq
    n_tok = n_prompt + n_sample
    assert n_prompt % TM == 0 and n_sample % TM == 0

    xbuf = jnp.concatenate([x_prompt.reshape(n_prompt, D_MODEL), x_sample.reshape(n_sample, D_MODEL)], axis=0)
    pbuf = jnp.concatenate([p_prompt.reshape(depth, n_prompt, D_PLE), p_sample.reshape(depth, n_sample, D_PLE)],
                           axis=1)
    row = lambda a: a.reshape(1, -1)
    zeros = jnp.zeros((D_BRANCH,), F32)

    states_p, states_s = [], []
    y_prompt = y_sample = None
    for i in range(depth):
        sw_p, sm_p, sb_p = _sgu_tables(sgu_w[i], sgu_b[i], seq)
        sw_s, sm_s, sb_s = _sgu_tables(sgu_w[i], sgu_b[i], dseq)
        lw = {
            "norm_mix_g": row(norm_mix_g[i]), "w_in": w_in[i].astype(BF16),
            "conv_a_w": conv_a_w[i], "conv_b_w": conv_b_w[i],
            "vec": jnp.stack([conv_b_bias[i], conv_b_ln_g[i], conv_b_ln_b[i], pool_scale[i],
                              sgu_ln_g[i], sgu_ln_b[i], zeros, zeros]),
            "pool_bd": _block_diag(pool_w[i]).astype(BF16),
            "sgu_w_p": sw_p, "sgu_mask_p": sm_p, "sgu_bias_p": sb_p,
            "sgu_w_s": sw_s, "sgu_mask_s": sm_s, "sgu_bias_s": sb_s,
            "w_branch": w_branch[i].astype(BF16), "w_out": w_out[i].astype(BF16),
            "norm_ffn_g": row(norm_ffn_g[i]), "norm_ple_g": row(norm_ple_g[i]),
            "ple_w_gate": ple_w_gate[i].astype(BF16), "ple_w_proj": ple_w_proj[i].astype(BF16),
        }
        xbuf, a_p, b_p, pool_p, v_p = _mixer_prompt(xbuf, lw, batch, seq)
        xbuf, a_s, b_s, pool_s, v_s = _mixer_sample(xbuf, lw, state_conv_a[i], state_conv_b[i], state_pool[i],
                                                    n_prompt, nseq, dseq)
        states_p.append((a_p, b_p, pool_p, v_p))
        states_s.append((a_s, b_s, pool_s, v_s))

        j = i // 2
        last = i == depth - 1
        if i % 2 == 0:
            lw.update({"ffn_w_gate": ffn_w_gate[j].astype(BF16), "ffn_w_up": ffn_w_up[j].astype(BF16),
                       "ffn_w_down": ffn_w_down[j].astype(BF16)})
            xbuf = _dense_ffn(xbuf, pbuf[i], lw)
        else:
            lw.update({"router_w": jnp.pad(router_w[j], ((0, 0), (0, LANES - N_EXPERTS))),
                       "moe_w_gate": moe_w_gate[j].astype(BF16), "moe_w_up": moe_w_up[j].astype(BF16),
                       "moe_w_down": moe_w_down[j].astype(BF16)})
            idx, wgt = _router(xbuf, lw)
            slot_tok, tile_expert, tile_row, n_active, pos = _routing_tables(idx, n_tok)
            xs = _dispatch(xbuf, slot_tok, lw)
            ys = _experts(xs, tile_expert, tile_row, n_active, lw)
            fin = row(final_norm_g) if last else None
            out_p = _combine(xbuf, pbuf[i], pos, wgt, ys, lw, fin, 0, n_prompt)
            out_s = _combine(xbuf, pbuf[i], pos, wgt, ys, lw, fin, n_prompt, n_sample)
            if last:
                y_prompt, y_sample = out_p, out_s
            else:
                xbuf = jnp.concatenate([out_p, out_s], axis=0)

    stack = lambda k, sts: jnp.stack([s[k] for s in sts])
    return (y_prompt.reshape(batch, seq, D_MODEL), y_sample.reshape(nseq, dseq, D_MODEL),
            stack(0, states_p), stack(1, states_p), stack(2, states_p), stack(3, states_p),
            stack(0, states_s), stack(1, states_s), stack(2, states_s), stack(3, states_s))
```

```python
import functools

import jax
import jax.numpy as jnp
import numpy as np
from jax import lax
from jax.experimental import pallas as pl
from jax.experimental.pallas import tpu as pltpu

F32 = jnp.float32
BF16 = jnp.bfloat16

D_MODEL = 1024
N_BRANCH = 4
D_BRANCH = 256
N_GROUPS = 4
GROUP_DIM = 64
CONV_A_W = 3
CONV_B_W = 31
POOL_WINDOWS = (2, 4, 8, 16)
POOL_MAX = 16
CHUNK = 128
PAST_LEN = 16384
D_PLE = 256
N_EXPERTS = 8
TOP_K = 2
EPS = 1e-6

OFF_A_H = 0
OFF_A_B = 256
OFF_A_C = 512
OFF_CF_A = 768
OFF_CF_B = 1024
OFF_POOL = 1280
OFF_U = 1536
OFF_V = 1792
OFF_GATE = 2048

LANES = 128
SUBLANES = 8
TM = 512
TL = 512
NB = 64
TF_EXPERT = 1792
ROW_CHUNK = 32
SEQ_CHUNK = 8
VMEM_LIMIT = 56 * 1024 * 1024

V_CB_BIAS, V_CB_LN_G, V_CB_LN_B, V_POOL_SCALE, V_SGU_LN_G, V_SGU_LN_B = range(6)


def _rms(x, g):
    return x * lax.rsqrt(jnp.mean(x * x, axis=-1, keepdims=True) + EPS) * g


def _ln(x, g, b):
    xc = x - jnp.mean(x, axis=-1, keepdims=True)
    var = jnp.mean(xc * xc, axis=-1, keepdims=True)
    return xc * lax.rsqrt(var + EPS) * g + b


def _bdot(a, w):
    return jnp.dot(a.astype(BF16), w, preferred_element_type=F32)


def _silu(x):
    return x * jax.nn.sigmoid(x)


def _const_spec(shape):
    nd = len(shape)
    return pl.BlockSpec(shape, lambda *_: (0,) * nd, pipeline_mode=pl.Buffered(1))


def _sgu(u, v, swm, sbias):
    m = v.shape[0]
    group = lax.broadcasted_iota(jnp.int32, (CHUNK, D_BRANCH), 1) // GROUP_DIM
    outs = []
    for c in range(m // CHUNK):
        vc = v[c * CHUNK:(c + 1) * CHUNK].astype(BF16)
        r = jnp.dot(swm, vc, preferred_element_type=F32)
        s = sbias
        for g in range(N_GROUPS):
            s = s + jnp.where(group == g, r[g * CHUNK:(g + 1) * CHUNK], 0.0)
        outs.append(s)
    return u * jnp.concatenate(outs, axis=0)


def _gated_merge(x, xnb, branches, win_ref, wbr_ref, wout_ref):
    mixed = None
    for i in range(N_BRANCH):
        lo = OFF_GATE + i * D_MODEL
        gate = jax.nn.sigmoid(jnp.dot(xnb, win_ref[:, lo:lo + D_MODEL], preferred_element_type=F32))
        term = gate * _bdot(branches[i], wbr_ref[i])
        mixed = term if mixed is None else mixed + term
    return x + _bdot(mixed, wout_ref[...])


def _pool_lane_windows(half):
    small, big = POOL_WINDOWS[2 * half], POOL_WINDOWS[2 * half + 1]
    lane = lax.broadcasted_iota(jnp.int32, (1, LANES), 1)
    is_big = lane >= GROUP_DIM
    return small, big, is_big


def _mixer_prompt_kernel(x_ref, g_ref, win_ref, caw_ref, cbw_ref, vec_ref, pw_ref, sw_ref, smask_ref,
                         sbias_ref, wbr_ref, wout_ref,
                         xo_ref, sta_ref, stb_ref, stp_ref, stv_ref,
                         sa, sb, sp):
    i = pl.program_id(1)
    ha, hb, hp = SUBLANES, 4 * SUBLANES, 2 * SUBLANES

    @pl.when(i == 0)
    def _():
        sa[0:ha, :] = jnp.zeros((ha, D_BRANCH), F32)
        sb[0:hb, :] = jnp.zeros((hb, D_BRANCH), F32)
        sp[0:hp, :] = jnp.zeros((hp, D_BRANCH), F32)

    @pl.when(i > 0)
    def _():
        sa[0:ha, :] = sa[TL:TL + ha, :]
        sb[0:hb, :] = sb[TL:TL + hb, :]
        sp[0:hp, :] = sp[TL:TL + hp, :]

    x = x_ref[...]
    xnb = _rms(x, g_ref[...]).astype(BF16)
    vec = vec_ref[...]

    def zc(off):
        return jnp.dot(xnb, win_ref[:, off:off + D_BRANCH], preferred_element_type=F32)

    def tap_conv(s_ref, base, w_ref, ntaps):
        outs = []
        for c in range(TL // ROW_CHUNK):
            acc = None
            for k in range(ntaps):
                term = s_ref[pl.ds(base + k + c * ROW_CHUNK, ROW_CHUNK), :] * w_ref[k:k + 1, :]
                acc = term if acc is None else acc + term
            outs.append(acc)
        return jnp.concatenate(outs, axis=0)

    sa[ha:ha + TL, :] = zc(OFF_A_C) * zc(OFF_A_H)
    br_a = zc(OFF_A_B) * tap_conv(sa, ha - (CONV_A_W - 1), caw_ref, CONV_A_W)
    sta_ref[0] = sa[TL + ha - (CONV_A_W - 1):TL + ha, :]

    sb[hb:hb + TL, :] = zc(OFF_CF_A) * jax.nn.sigmoid(zc(OFF_CF_B))
    yb = tap_conv(sb, hb - (CONV_B_W - 1), cbw_ref, CONV_B_W) + vec[V_CB_BIAS:V_CB_BIAS + 1]
    br_b = _silu(_ln(yb, vec[V_CB_LN_G:V_CB_LN_G + 1], vec[V_CB_LN_B:V_CB_LN_B + 1]))
    stb_ref[0] = sb[TL + hb - (CONV_B_W - 1):TL + hb, :]

    sp[hp:hp + TL, :] = zc(OFF_POOL)
    halves = []
    for half in range(2):
        small, big, is_big = _pool_lane_windows(half)
        lanes = slice(half * LANES, (half + 1) * LANES)
        wl = jnp.where(is_big, float(big), float(small))
        outs = []
        for c in range(TL // ROW_CHUNK):
            r0 = hp + c * ROW_CHUNK
            acc = None
            for j in range(big):
                seg = sp[pl.ds(r0 - j, ROW_CHUNK), lanes]
                if j >= small:
                    seg = jnp.where(is_big, seg, 0.0)
                acc = seg if acc is None else acc + seg
            pos = i * TL + c * ROW_CHUNK + lax.broadcasted_iota(jnp.int32, (ROW_CHUNK, LANES), 0)
            cnt = jnp.minimum(wl, (pos + 1).astype(F32))
            outs.append(acc / cnt - sp[pl.ds(r0, ROW_CHUNK), lanes])
        halves.append(jnp.concatenate(outs, axis=0))
    pooled = jnp.concatenate(halves, axis=1)
    br_c = _bdot(pooled, pw_ref[...]) * vec[V_POOL_SCALE:V_POOL_SCALE + 1]
    stp_ref[0] = sp[TL + hp - (POOL_MAX - 1):TL + hp, :]

    v = _ln(zc(OFF_V), vec[V_SGU_LN_G:V_SGU_LN_G + 1], vec[V_SGU_LN_B:V_SGU_LN_B + 1])
    stv_ref[0] = v[TL - CHUNK:TL]
    swm = jnp.where(smask_ref[...] > 0, sw_ref[...], 0.0).astype(BF16)
    br_d = _sgu(zc(OFF_U), v, swm, sbias_ref[...])

    xo_ref[...] = _gated_merge(x, xnb, (br_a, br_b, br_c, br_d), win_ref, wbr_ref, wout_ref)


def _mixer_sample_kernel(x_ref, g_ref, win_ref, caw_ref, cbw_ref, vec_ref, pw_ref, sw_ref, smask_ref,
                         sbias_ref, wbr_ref, wout_ref, ina_ref, inb_ref, inp_ref,
                         xo_ref, sta_ref, stb_ref, stp_ref, stv_ref,
                         sa, sb, sp):
    seq = SUBLANES
    ha, hb, hp = SUBLANES, 4 * SUBLANES, 2 * SUBLANES
    sa[:, ha - (CONV_A_W - 1):ha, :] = ina_ref[...]
    sb[:, hb - (CONV_B_W - 1):hb, :] = inb_ref[...]
    sp[:, hp - (POOL_MAX - 1):hp, :] = inp_ref[...]

    x = x_ref[...]
    xnb = _rms(x, g_ref[...]).astype(BF16)
    vec = vec_ref[...]

    def zc(off):
        return jnp.dot(xnb, win_ref[:, off:off + D_BRANCH], preferred_element_type=F32)

    def to3(a):
        return a.reshape(NB, seq, D_BRANCH)

    def to2(a):
        return a.reshape(NB * seq, D_BRANCH)

    def tap_conv(s_ref, base, w_ref, ntaps):
        outs = []
        for c in range(NB // SEQ_CHUNK):
            acc = None
            for k in range(ntaps):
                seg = s_ref[c * SEQ_CHUNK:(c + 1) * SEQ_CHUNK, pl.ds(base + k, seq), :]
                term = seg * w_ref[k:k + 1, :][None]
                acc = term if acc is None else acc + term
            outs.append(acc)
        return to2(jnp.concatenate(outs, axis=0))

    sa[:, ha:ha + seq, :] = to3(zc(OFF_A_C) * zc(OFF_A_H))
    br_a = zc(OFF_A_B) * tap_conv(sa, ha - (CONV_A_W - 1), caw_ref, CONV_A_W)
    sta_ref[...] = sa[:, ha + seq - (CONV_A_W - 1):ha + seq, :]

    sb[:, hb:hb + seq, :] = to3(zc(OFF_CF_A) * jax.nn.sigmoid(zc(OFF_CF_B)))
    yb = tap_conv(sb, hb - (CONV_B_W - 1), cbw_ref, CONV_B_W) + vec[V_CB_BIAS:V_CB_BIAS + 1]
    br_b = _silu(_ln(yb, vec[V_CB_LN_G:V_CB_LN_G + 1], vec[V_CB_LN_B:V_CB_LN_B + 1]))
    stb_ref[...] = sb[:, hb + seq - (CONV_B_W - 1):hb + seq, :]

    sp[:, hp:hp + seq, :] = to3(zc(OFF_POOL))
    halves = []
    for half in range(2):
        small, big, is_big = _pool_lane_windows(half)
        lanes = slice(half * LANES, (half + 1) * LANES)
        wl = jnp.where(is_big, float(big), float(small))[None]
        outs = []
        for c in range(NB // SEQ_CHUNK):
            rows = slice(c * SEQ_CHUNK, (c + 1) * SEQ_CHUNK)
            acc = None
            for j in range(big):
                seg = sp[rows, pl.ds(hp - j, seq), lanes]
                if j >= small:
                    seg = jnp.where(is_big[None], seg, 0.0)
                acc = seg if acc is None else acc + seg
            outs.append(acc / wl - sp[rows, pl.ds(hp, seq), lanes])
        halves.append(jnp.concatenate(outs, axis=0).reshape(NB * seq, LANES))
    pooled = jnp.concatenate(halves, axis=1)
    br_c = _bdot(pooled, pw_ref[...]) * vec[V_POOL_SCALE:V_POOL_SCALE + 1]
    stp_ref[...] = sp[:, hp + seq - (POOL_MAX - 1):hp + seq, :]

    v = _ln(zc(OFF_V), vec[V_SGU_LN_G:V_SGU_LN_G + 1], vec[V_SGU_LN_B:V_SGU_LN_B + 1])
    stv_ref[...] = to3(v)
    swm = jnp.where(smask_ref[...] > 0, sw_ref[...], 0.0).astype(BF16)
    br_d = _sgu(zc(OFF_U), v, swm, sbias_ref[...])

    xo_ref[...] = _gated_merge(x, xnb, (br_a, br_b, br_c, br_d), win_ref, wbr_ref, wout_ref)


def _mixer_weight_specs():
    return [
        _const_spec((1, D_MODEL)),
        _const_spec((D_MODEL, OFF_GATE + N_BRANCH * D_MODEL)),
        _const_spec((CONV_A_W, D_BRANCH)),
        _const_spec((CONV_B_W, D_BRANCH)),
        _const_spec((SUBLANES, D_BRANCH)),
        _const_spec((D_BRANCH, D_BRANCH)),
        _const_spec((N_GROUPS * CHUNK, CHUNK)),
        _const_spec((N_GROUPS * CHUNK, CHUNK)),
        _const_spec((CHUNK, D_BRANCH)),
        _const_spec((N_BRANCH, D_BRANCH, D_MODEL)),
        _const_spec((D_MODEL, D_MODEL)),
    ]


def _mixer_prompt(xbuf, lw, batch, seq):
    nt = seq // TL
    rows = lambda b, i: (b * nt + i, 0)
    st = lambda n: pl.BlockSpec((1, n, D_BRANCH), lambda b, i: (b, 0, 0))
    return pl.pallas_call(
        _mixer_prompt_kernel,
        grid=(batch, nt),
        in_specs=[pl.BlockSpec((TL, D_MODEL), rows)] + _mixer_weight_specs(),
        out_specs=[pl.BlockSpec((TL, D_MODEL), rows), st(CONV_A_W - 1), st(CONV_B_W - 1),
                   st(POOL_MAX - 1), st(CHUNK)],
        out_shape=[jax.ShapeDtypeStruct(xbuf.shape, F32),
                   jax.ShapeDtypeStruct((batch, CONV_A_W - 1, D_BRANCH), F32),
                   jax.ShapeDtypeStruct((batch, CONV_B_W - 1, D_BRANCH), F32),
                   jax.ShapeDtypeStruct((batch, POOL_MAX - 1, D_BRANCH), F32),
                   jax.ShapeDtypeStruct((batch, CHUNK, D_BRANCH), F32)],
        scratch_shapes=[pltpu.VMEM((TL + SUBLANES, D_BRANCH), F32),
                        pltpu.VMEM((TL + 4 * SUBLANES, D_BRANCH), F32),
                        pltpu.VMEM((TL + 2 * SUBLANES, D_BRANCH), F32)],
        input_output_aliases={0: 0},
        compiler_params=pltpu.CompilerParams(dimension_semantics=("arbitrary", "arbitrary"),
                                             vmem_limit_bytes=VMEM_LIMIT),
        name="mixer_prompt",
    )(xbuf, lw["norm_mix_g"], lw["w_in"], lw["conv_a_w"], lw["conv_b_w"], lw["vec"], lw["pool_bd"],
      lw["sgu_w_p"], lw["sgu_mask_p"], lw["sgu_bias_p"], lw["w_branch"], lw["w_out"])


def _mixer_sample(xbuf, lw, st_a, st_b, st_p, row0, nseq, seq):
    assert seq == SUBLANES and nseq % NB == 0 and row0 % (NB * seq) == 0
    blk0 = row0 // (NB * seq)
    rows = lambda i: (blk0 + i, 0)
    st = lambda n: pl.BlockSpec((NB, n, D_BRANCH), lambda i: (i, 0, 0))
    return pl.pallas_call(
        _mixer_sample_kernel,
        grid=(nseq // NB,),
        in_specs=[pl.BlockSpec((NB * seq, D_MODEL), rows)] + _mixer_weight_specs()
                 + [st(CONV_A_W - 1), st(CONV_B_W - 1), st(POOL_MAX - 1)],
        out_specs=[pl.BlockSpec((NB * seq, D_MODEL), rows), st(CONV_A_W - 1), st(CONV_B_W - 1),
                   st(POOL_MAX - 1), st(seq)],
        out_shape=[jax.ShapeDtypeStruct(xbuf.shape, F32),
                   jax.ShapeDtypeStruct((nseq, CONV_A_W - 1, D_BRANCH), F32),
                   jax.ShapeDtypeStruct((nseq, CONV_B_W - 1, D_BRANCH), F32),
                   jax.ShapeDtypeStruct((nseq, POOL_MAX - 1, D_BRANCH), F32),
                   jax.ShapeDtypeStruct((nseq, seq, D_BRANCH), F32)],
        scratch_shapes=[pltpu.VMEM((NB, 2 * SUBLANES, D_BRANCH), F32),
                        pltpu.VMEM((NB, 5 * SUBLANES, D_BRANCH), F32),
                        pltpu.VMEM((NB, 3 * SUBLANES, D_BRANCH), F32)],
        input_output_aliases={0: 0},
        compiler_params=pltpu.CompilerParams(dimension_semantics=("arbitrary",),
                                             vmem_limit_bytes=VMEM_LIMIT),
        name="mixer_sample",
    )(xbuf, lw["norm_mix_g"], lw["w_in"], lw["conv_a_w"], lw["conv_b_w"], lw["vec"], lw["pool_bd"],
      lw["sgu_w_s"], lw["sgu_mask_s"], lw["sgu_bias_s"], lw["w_branch"], lw["w_out"],
      st_a, st_b, st_p)


def _ple_tail(x, p, gple_ref, wpg_ref, wpp_ref):
    gate = jax.nn.sigmoid(_bdot(_rms(x, gple_ref[...]), wpg_ref[...]))
    return x + gate * _bdot(p, wpp_ref[...])


def _dense_ffn_kernel(n_chunks, x_ref, p_ref, gffn_ref, wg_ref, wu_ref, wd_ref, gple_ref, wpg_ref, wpp_ref,
                      xo_ref):
    x = x_ref[...]
    xnb = _rms(x, gffn_ref[...]).astype(BF16)
    tf = wg_ref.shape[1] // n_chunks
    f = None
    for c in range(n_chunks):
        cols = slice(c * tf, (c + 1) * tf)
        h = _silu(jnp.dot(xnb, wg_ref[:, cols], preferred_element_type=F32)) * \
            jnp.dot(xnb, wu_ref[:, cols], preferred_element_type=F32)
        t = _bdot(h, wd_ref[cols, :])
        f = t if f is None else f + t
    xo_ref[...] = _ple_tail(x + f, p_ref[...], gple_ref, wpg_ref, wpp_ref)


def _dense_ffn(xbuf, pbuf, lw):
    t = xbuf.shape[0]
    d_ff = lw["ffn_w_gate"].shape[1]
    rows = lambda i: (i, 0)
    return pl.pallas_call(
        functools.partial(_dense_ffn_kernel, 2),
        grid=(t // TM,),
        in_specs=[pl.BlockSpec((TM, D_MODEL), rows), pl.BlockSpec((TM, D_PLE), rows),
                  _const_spec((1, D_MODEL)), _const_spec((D_MODEL, d_ff)), _const_spec((D_MODEL, d_ff)),
                  _const_spec((d_ff, D_MODEL)), _const_spec((1, D_MODEL)),
                  _const_spec((D_MODEL, D_MODEL)), _const_spec((D_PLE, D_MODEL))],
        out_specs=pl.BlockSpec((TM, D_MODEL), rows),
        out_shape=jax.ShapeDtypeStruct(xbuf.shape, F32),
        input_output_aliases={0: 0},
        compiler_params=pltpu.CompilerParams(dimension_semantics=("arbitrary",),
                                             vmem_limit_bytes=VMEM_LIMIT),
        name="dense_ffn_ple",
    )(xbuf, pbuf, lw["norm_ffn_g"], lw["ffn_w_gate"], lw["ffn_w_up"], lw["ffn_w_down"],
      lw["norm_ple_g"], lw["ple_w_gate"], lw["ple_w_proj"])


def _router_kernel(x_ref, g_ref, wr_ref, idx_ref, wgt_ref):
    xn = _rms(x_ref[...], g_ref[...])
    logits = jnp.dot(xn, wr_ref[...], preferred_element_type=F32, precision=lax.Precision.HIGHEST)
    lane = lax.broadcasted_iota(jnp.int32, logits.shape, 1)
    lane_f = lane.astype(F32)
    logits = jnp.where(lane < N_EXPERTS, logits, -jnp.inf)
    m1 = jnp.max(logits, axis=-1, keepdims=True)
    i1 = jnp.min(jnp.where(logits == m1, lane_f, float(LANES)), axis=-1, keepdims=True)
    rest = jnp.where(lane_f == i1, -jnp.inf, logits)
    m2 = jnp.max(rest, axis=-1, keepdims=True)
    i2 = jnp.min(jnp.where(rest == m2, lane_f, float(LANES)), axis=-1, keepdims=True)
    e2 = jnp.exp(m2 - m1)
    den = 1.0 + e2
    idx_ref[...] = jnp.where(lane == 0, i1, jnp.where(lane == 1, i2, 0.0)).astype(jnp.int32)
    wgt_ref[...] = jnp.where(lane == 0, 1.0 / den, jnp.where(lane == 1, e2 / den, 0.0))


def _router(xbuf, lw):
    t = xbuf.shape[0]
    rows = lambda i: (i, 0)
    return pl.pallas_call(
        _router_kernel,
        grid=(t // TM,),
        in_specs=[pl.BlockSpec((TM, D_MODEL), rows), _const_spec((1, D_MODEL)),
                  _const_spec((D_MODEL, LANES))],
        out_specs=[pl.BlockSpec((TM, LANES), rows), pl.BlockSpec((TM, LANES), rows)],
        out_shape=[jax.ShapeDtypeStruct((t, LANES), jnp.int32), jax.ShapeDtypeStruct((t, LANES), F32)],
        compiler_params=pltpu.CompilerParams(dimension_semantics=("arbitrary",)),
        name="router",
    )(xbuf, lw["norm_ffn_g"], lw["router_w"])


def _start_row_gather(idx_ref, k, src_hbm, dst, sem):
    def issue(r, carry):
        pltpu.make_async_copy(src_hbm.at[pl.ds(idx_ref[k, r], 1)], dst.at[pl.ds(r, 1)], sem).start()
        return carry
    lax.fori_loop(0, dst.shape[0], issue, 0, unroll=8)


def _wait_row_gather(src_hbm, dst, sem):
    pltpu.make_async_copy(src_hbm.at[pl.ds(0, dst.shape[0])], dst, sem).wait()


def _expert_kernel(te_ref, na_ref, tokc_ref, tokn_ref, x_hbm, g_ref, wg_ref, wu_ref, wd_ref, ys_ref,
                   gbuf, xs, sems):
    i, j = pl.program_id(0), pl.program_id(1)
    na = na_ref[0]
    slot = i % 2

    @pl.when(jnp.logical_and(i < na, j == 0))
    def _():
        @pl.when(i == 0)
        def _():
            _start_row_gather(tokc_ref, 0, x_hbm, gbuf.at[0], sems.at[0])

        @pl.when(i + 1 < na)
        def _():
            _start_row_gather(tokn_ref, 0, x_hbm, gbuf.at[1 - slot], sems.at[1 - slot])

        _wait_row_gather(x_hbm, gbuf.at[slot], sems.at[slot])
        xs[...] = _rms(gbuf[slot], g_ref[...]).astype(BF16)

    @pl.when(i < na)
    def _():
        h = _silu(jnp.dot(xs[...], wg_ref[...], preferred_element_type=F32)) * \
            jnp.dot(xs[...], wu_ref[...], preferred_element_type=F32)
        part = _bdot(h, wd_ref[...])

        @pl.when(j == 0)
        def _():
            ys_ref[...] = part

        @pl.when(j > 0)
        def _():
            ys_ref[...] += part

    @pl.when(jnp.logical_and(i >= na, j == 0))
    def _():
        ys_ref[...] = jnp.zeros(ys_ref.shape, F32)


def _experts(xbuf, slot_tok, tile_expert, n_active, lw):
    n_tiles = tile_expert.shape[0]
    d_ff = lw["moe_w_gate"].shape[2]
    n_fc = d_ff // TF_EXPERT

    def fcol(i, j, na):
        return jnp.where(i < na[0], j, n_fc - 1)

    grid_spec = pltpu.PrefetchScalarGridSpec(
        num_scalar_prefetch=2,
        grid=(n_tiles, n_fc),
        in_specs=[
            pl.BlockSpec((None, 1, TM), lambda i, j, te, na: (i, 0, 0), memory_space=pltpu.SMEM),
            pl.BlockSpec((None, 1, TM), lambda i, j, te, na: (jnp.minimum(i + 1, n_tiles - 1), 0, 0),
                         memory_space=pltpu.SMEM),
            pl.BlockSpec(memory_space=pl.ANY),
            pl.BlockSpec((1, D_MODEL), lambda i, j, te, na: (0, 0)),
            pl.BlockSpec((None, D_MODEL, TF_EXPERT), lambda i, j, te, na: (te[i], 0, fcol(i, j, na))),
            pl.BlockSpec((None, D_MODEL, TF_EXPERT), lambda i, j, te, na: (te[i], 0, fcol(i, j, na))),
            pl.BlockSpec((None, TF_EXPERT, D_MODEL), lambda i, j, te, na: (te[i], fcol(i, j, na), 0)),
        ],
        out_specs=pl.BlockSpec((TM, D_MODEL), lambda i, j, te, na: (i, 0)),
        scratch_shapes=[pltpu.VMEM((2, TM, D_MODEL), F32), pltpu.VMEM((TM, D_MODEL), BF16),
                        pltpu.SemaphoreType.DMA((2,))],
    )
    return pl.pallas_call(
        _expert_kernel,
        grid_spec=grid_spec,
        out_shape=jax.ShapeDtypeStruct((n_tiles * TM, D_MODEL), F32),
        compiler_params=pltpu.CompilerParams(dimension_semantics=("arbitrary", "arbitrary"),
                                             vmem_limit_bytes=VMEM_LIMIT),
        name="moe_experts",
    )(tile_expert, n_active, slot_tok, slot_tok, xbuf, lw["norm_ffn_g"],
      lw["moe_w_gate"], lw["moe_w_up"], lw["moe_w_down"])


def _combine_kernel(final, n_steps, posc_ref, posn_ref, x_ref, p_ref, wgt_ref, ys_hbm, gple_ref, wpg_ref,
                    wpp_ref, gfin_ref, o_ref, buf, sems):
    i = pl.program_id(0)
    slot = i % 2

    def start(pos_ref, s):
        for k in range(TOP_K):
            _start_row_gather(pos_ref, k, ys_hbm, buf.at[s, k], sems.at[s, k])

    @pl.when(i == 0)
    def _():
        start(posc_ref, 0)

    @pl.when(i + 1 < n_steps)
    def _():
        start(posn_ref, 1 - slot)

    for k in range(TOP_K):
        _wait_row_gather(ys_hbm, buf.at[slot, k], sems.at[slot, k])
    wgt = wgt_ref[...]
    x = x_ref[...] + (buf[slot, 0] * wgt[:, 0:1] + buf[slot, 1] * wgt[:, 1:2])
    x = _ple_tail(x, p_ref[...], gple_ref, wpg_ref, wpp_ref)
    o_ref[...] = _rms(x, gfin_ref[...]) if final else x


def _combine(xbuf, pbuf, pos, wgt, ys, lw, final_g, row0, n_rows):
    blk0 = row0 // TM
    n_steps = n_rows // TM
    rows = lambda i: (blk0 + i, 0)
    final = final_g is not None
    gfin = final_g if final else lw["norm_ple_g"]
    return pl.pallas_call(
        functools.partial(_combine_kernel, final, n_steps),
        grid=(n_steps,),
        in_specs=[pl.BlockSpec((None, TOP_K, TM), lambda i: (blk0 + i, 0, 0), memory_space=pltpu.SMEM),
                  pl.BlockSpec((None, TOP_K, TM), lambda i: (blk0 + jnp.minimum(i + 1, n_steps - 1), 0, 0),
                               memory_space=pltpu.SMEM),
                  pl.BlockSpec((TM, D_MODEL), rows), pl.BlockSpec((TM, D_PLE), rows),
                  pl.BlockSpec((TM, LANES), rows), pl.BlockSpec(memory_space=pl.ANY),
                  _const_spec((1, D_MODEL)), _const_spec((D_MODEL, D_MODEL)),
                  _const_spec((D_PLE, D_MODEL)), _const_spec((1, D_MODEL))],
        out_specs=pl.BlockSpec((TM, D_MODEL), lambda i: (i, 0)),
        out_shape=jax.ShapeDtypeStruct((n_rows, D_MODEL), F32),
        scratch_shapes=[pltpu.VMEM((2, TOP_K, TM, D_MODEL), F32), pltpu.SemaphoreType.DMA((2, TOP_K))],
        compiler_params=pltpu.CompilerParams(dimension_semantics=("arbitrary",),
                                             vmem_limit_bytes=VMEM_LIMIT),
        name="moe_combine_ple",
    )(pos, pos, xbuf, pbuf, wgt, ys, lw["norm_ple_g"], lw["ple_w_gate"], lw["ple_w_proj"], gfin)


def _routing_tables(idx, n_tok):
    n_pairs = n_tok * TOP_K
    n_tiles = n_pairs // TM + N_EXPERTS
    experts = jnp.arange(N_EXPERTS, dtype=jnp.int32)
    e_flat = idx[:, :TOP_K].reshape(-1)
    onehot = (e_flat[:, None] == experts[None, :]).astype(jnp.int32)
    csum = jnp.cumsum(onehot, axis=0)
    sizes = csum[-1]
    padded = ((sizes + TM - 1) // TM) * TM
    pend = jnp.cumsum(padded)
    pstart = pend - padded
    pos = jnp.sum(onehot * (csum - 1 + pstart[None, :]), axis=1)
    fill_e = jnp.repeat(experts, TM)
    fill_r = jnp.tile(jnp.arange(TM, dtype=jnp.int32), N_EXPERTS)
    fill_need = jnp.sum((fill_e[:, None] == experts[None, :]) * (padded - sizes)[None, :], axis=1)
    keys = jnp.concatenate([e_flat, jnp.where(fill_r < fill_need, fill_e, N_EXPERTS)])
    toks = jnp.concatenate([jnp.arange(n_pairs, dtype=jnp.int32) // TOP_K,
                            jnp.zeros((N_EXPERTS * TM,), jnp.int32)])
    _, slot_tok = lax.sort((keys, toks), num_keys=1, is_stable=True)
    n_active = (pend[-1] // TM).astype(jnp.int32)
    tile_row = jnp.minimum(jnp.arange(n_tiles, dtype=jnp.int32), n_active - 1)
    tile_expert = jnp.minimum(
        jnp.sum((tile_row[:, None] * TM >= pend[None, :]).astype(jnp.int32), axis=1), N_EXPERTS - 1)
    pos_tiles = pos.reshape(n_tok // TM, TM, TOP_K).transpose(0, 2, 1)
    return slot_tok.reshape(n_tiles, 1, TM), tile_expert, n_active.reshape(1), pos_tiles


def _sgu_tables(sgu_w, sgu_b, seq):
    r = np.arange(CHUNK)
    bias = jnp.repeat(sgu_b.T, GROUP_DIM, axis=1)
    if seq >= CHUNK:
        w = sgu_w
        mask = r[None, :] <= r[:, None]
    else:
        reps = CHUNK // seq
        w = jnp.tile(sgu_w[:, :seq, :seq], (1, reps, reps))
        mask = (r[:, None] // seq == r[None, :] // seq) & (r[None, :] % seq <= r[:, None] % seq)
        bias = jnp.tile(bias[:seq], (reps, 1))
    mask = np.tile(mask.astype(np.float32), (N_GROUPS, 1))
    return w.reshape(N_GROUPS * CHUNK, CHUNK), jnp.asarray(mask), bias


def _block_diag(w):
    out = jnp.zeros((D_BRANCH, D_BRANCH), w.dtype)
    for g in range(N_GROUPS):
        out = out.at[g * GROUP_DIM:(g + 1) * GROUP_DIM, g * GROUP_DIM:(g + 1) * GROUP_DIM].set(w[g])
    return out


def kernel(x_prompt, x_sample, state_conv_a, state_conv_b, state_pool, p_prompt, p_sample, norm_mix_g, w_in, conv_a_w, conv_b_w, conv_b_bias, conv_b_ln_g, conv_b_ln_b, pool_w, pool_scale, sgu_ln_g, sgu_ln_b, sgu_w, sgu_b, w_branch, w_out, norm_ffn_g, ffn_w_gate, ffn_w_up, ffn_w_down, router_w, moe_w_gate, moe_w_up, moe_w_down, norm_ple_g, ple_w_gate, ple_w_proj, final_norm_g):
    batch, seq, _ = x_prompt.shape
    nseq, dseq, _ = x_sample.shape
    depth = w_in.shape[0]
    assert depth % 2 == 0
    assert PAST_LEN + 1 >= POOL_MAX
    assert seq % TL == 0 and TL % CHUNK == 0 and TL >= CONV_B_W + 1
    n_prompt, n_sample = batch * seq, nseq * dseq
    n_tok = n_prompt + n_sample
    assert n_prompt % TM == 0 and n_sample % TM == 0

    xbuf = jnp.concatenate([x_prompt.reshape(n_prompt, D_MODEL), x_sample.reshape(n_sample, D_MODEL)], axis=0)
    pbuf = jnp.concatenate([p_prompt.reshape(depth, n_prompt, D_PLE), p_sample.reshape(depth, n_sample, D_PLE)],
                           axis=1)
    row = lambda a: a.reshape(1, -1)
    zeros = jnp.zeros((D_BRANCH,), F32)

    states_p, states_s = [], []
    y_prompt = y_sample = None
    for i in range(depth):
        sw_p, sm_p, sb_p = _sgu_tables(sgu_w[i], sgu_b[i], seq)
        sw_s, sm_s, sb_s = _sgu_tables(sgu_w[i], sgu_b[i], dseq)
        lw = {
            "norm_mix_g": row(norm_mix_g[i]), "w_in": w_in[i].astype(BF16),
            "conv_a_w": conv_a_w[i], "conv_b_w": conv_b_w[i],
            "vec": jnp.stack([conv_b_bias[i], conv_b_ln_g[i], conv_b_ln_b[i], pool_scale[i],
                              sgu_ln_g[i], sgu_ln_b[i], zeros, zeros]),
            "pool_bd": _block_diag(pool_w[i]).astype(BF16),
            "sgu_w_p": sw_p, "sgu_mask_p": sm_p, "sgu_bias_p": sb_p,
            "sgu_w_s": sw_s, "sgu_mask_s": sm_s, "sgu_bias_s": sb_s,
            "w_branch": w_branch[i].astype(BF16), "w_out": w_out[i].astype(BF16),
            "norm_ffn_g": row(norm_ffn_g[i]), "norm_ple_g": row(norm_ple_g[i]),
            "ple_w_gate": ple_w_gate[i].astype(BF16), "ple_w_proj": ple_w_proj[i].astype(BF16),
        }
        xbuf, a_p, b_p, pool_p, v_p = _mixer_prompt(xbuf, lw, batch, seq)
        xbuf, a_s, b_s, pool_s, v_s = _mixer_sample(xbuf, lw, state_conv_a[i], state_conv_b[i], state_pool[i],
                                                    n_prompt, nseq, dseq)
        states_p.append((a_p, b_p, pool_p, v_p))
        states_s.append((a_s, b_s, pool_s, v_s))

        j = i // 2
        last = i == depth - 1
        if i % 2 == 0:
            lw.update({"ffn_w_gate": ffn_w_gate[j].astype(BF16), "ffn_w_up": ffn_w_up[j].astype(BF16),
                       "ffn_w_down": ffn_w_down[j].astype(BF16)})
            xbuf = _dense_ffn(xbuf, pbuf[i], lw)
        else:
            lw.update({"router_w": jnp.pad(router_w[j], ((0, 0), (0, LANES - N_EXPERTS))),
                       "moe_w_gate": moe_w_gate[j].astype(BF16), "moe_w_up": moe_w_up[j].astype(BF16),
                       "moe_w_down": moe_w_down[j].astype(BF16)})
            idx, wgt = _router(xbuf, lw)
            slot_tok, tile_expert, n_active, pos = _routing_tables(idx, n_tok)
            ys = _experts(xbuf, slot_tok, tile_expert, n_active, lw)
            fin = row(final_norm_g) if last else None
            out_p = _combine(xbuf, pbuf[i], pos, wgt, ys, lw, fin, 0, n_prompt)
            out_s = _combine(xbuf, pbuf[i], pos, wgt, ys, lw, fin, n_prompt, n_sample)
            if last:
                y_prompt, y_sample = out_p, out_s
            else:
                xbuf = jnp.concatenate([out_p, out_s], axis=0)

    stack = lambda k, sts: jnp.stack([s[k] for s in sts])
    return (y_prompt.reshape(batch, seq, D_MODEL), y_sample.reshape(nseq, dseq, D_MODEL),
            stack(0, states_p), stack(1, states_p), stack(2, states_p), stack(3, states_p),
            stack(0, states_s), stack(1, states_s), stack(2, states_s), stack(3, states_s))
```

```python
import functools

import jax
import jax.numpy as jnp
import numpy as np
from jax import lax
from jax.experimental import pallas as pl
from jax.experimental.pallas import tpu as pltpu

F32 = jnp.float32
BF16 = jnp.bfloat16

D_MODEL = 1024
N_BRANCH = 4
D_BRANCH = 256
N_GROUPS = 4
GROUP_DIM = 64
CONV_A_W = 3
CONV_B_W = 31
POOL_WINDOWS = (2, 4, 8, 16)
POOL_MAX = 16
CHUNK = 128
PAST_LEN = 16384
D_PLE = 256
N_EXPERTS = 8
TOP_K = 2
EPS = 1e-6

OFF_A_H = 0
OFF_A_B = 256
OFF_A_C = 512
OFF_CF_A = 768
OFF_CF_B = 1024
OFF_POOL = 1280
OFF_U = 1536
OFF_V = 1792
OFF_GATE = 2048

LANES = 128
SUBLANES = 8
TM = 512
TL = 512
NB = 64
TF_EXPERT = 1792
ROW_CHUNK = 32
SEQ_CHUNK = 8
VMEM_LIMIT = 56 * 1024 * 1024

V_CB_BIAS, V_CB_LN_G, V_CB_LN_B, V_POOL_SCALE, V_SGU_LN_G, V_SGU_LN_B = range(6)


def _rms(x, g):
    return x * lax.rsqrt(jnp.mean(x * x, axis=-1, keepdims=True) + EPS) * g


def _ln(x, g, b):
    xc = x - jnp.mean(x, axis=-1, keepdims=True)
    var = jnp.mean(xc * xc, axis=-1, keepdims=True)
    return xc * lax.rsqrt(var + EPS) * g + b


def _bdot(a, w):
    return jnp.dot(a.astype(BF16), w, preferred_element_type=F32)


def _silu(x):
    return x * jax.nn.sigmoid(x)


def _const_spec(shape):
    nd = len(shape)
    return pl.BlockSpec(shape, lambda *_: (0,) * nd, pipeline_mode=pl.Buffered(1))


def _sgu(u, v, swm, sbias):
    m = v.shape[0]
    group = lax.broadcasted_iota(jnp.int32, (CHUNK, D_BRANCH), 1) // GROUP_DIM
    outs = []
    for c in range(m // CHUNK):
        vc = v[c * CHUNK:(c + 1) * CHUNK].astype(BF16)
        r = jnp.dot(swm, vc, preferred_element_type=F32)
        s = sbias
        for g in range(N_GROUPS):
            s = s + jnp.where(group == g, r[g * CHUNK:(g + 1) * CHUNK], 0.0)
        outs.append(s)
    return u * jnp.concatenate(outs, axis=0)


def _gated_merge(x, xnb, branches, win_ref, wbr_ref, wout_ref):
    mixed = None
    for i in range(N_BRANCH):
        lo = OFF_GATE + i * D_MODEL
        gate = jax.nn.sigmoid(jnp.dot(xnb, win_ref[:, lo:lo + D_MODEL], preferred_element_type=F32))
        term = gate * _bdot(branches[i], wbr_ref[i])
        mixed = term if mixed is None else mixed + term
    return x + _bdot(mixed, wout_ref[...])


def _pool_lane_windows(half):
    small, big = POOL_WINDOWS[2 * half], POOL_WINDOWS[2 * half + 1]
    lane = lax.broadcasted_iota(jnp.int32, (1, LANES), 1)
    is_big = lane >= GROUP_DIM
    return small, big, is_big


def _mixer_prompt_kernel(x_ref, g_ref, win_ref, caw_ref, cbw_ref, vec_ref, pw_ref, sw_ref, smask_ref,
                         sbias_ref, wbr_ref, wout_ref,
                         xo_ref, sta_ref, stb_ref, stp_ref, stv_ref,
                         sa, sb, sp):
    i = pl.program_id(1)
    ha, hb, hp = SUBLANES, 4 * SUBLANES, 2 * SUBLANES

    @pl.when(i == 0)
    def _():
        sa[0:ha, :] = jnp.zeros((ha, D_BRANCH), F32)
        sb[0:hb, :] = jnp.zeros((hb, D_BRANCH), F32)
        sp[0:hp, :] = jnp.zeros((hp, D_BRANCH), F32)

    @pl.when(i > 0)
    def _():
        sa[0:ha, :] = sa[TL:TL + ha, :]
        sb[0:hb, :] = sb[TL:TL + hb, :]
        sp[0:hp, :] = sp[TL:TL + hp, :]

    x = x_ref[...]
    xnb = _rms(x, g_ref[...]).astype(BF16)
    vec = vec_ref[...]

    def zc(off):
        return jnp.dot(xnb, win_ref[:, off:off + D_BRANCH], preferred_element_type=F32)

    def tap_conv(s_ref, base, w_ref, ntaps):
        outs = []
        for c in range(TL // ROW_CHUNK):
            acc = None
            for k in range(ntaps):
                term = s_ref[pl.ds(base + k + c * ROW_CHUNK, ROW_CHUNK), :] * w_ref[k:k + 1, :]
                acc = term if acc is None else acc + term
            outs.append(acc)
        return jnp.concatenate(outs, axis=0)

    sa[ha:ha + TL, :] = zc(OFF_A_C) * zc(OFF_A_H)
    br_a = zc(OFF_A_B) * tap_conv(sa, ha - (CONV_A_W - 1), caw_ref, CONV_A_W)
    sta_ref[0] = sa[TL + ha - (CONV_A_W - 1):TL + ha, :]

    sb[hb:hb + TL, :] = zc(OFF_CF_A) * jax.nn.sigmoid(zc(OFF_CF_B))
    yb = tap_conv(sb, hb - (CONV_B_W - 1), cbw_ref, CONV_B_W) + vec[V_CB_BIAS:V_CB_BIAS + 1]
    br_b = _silu(_ln(yb, vec[V_CB_LN_G:V_CB_LN_G + 1], vec[V_CB_LN_B:V_CB_LN_B + 1]))
    stb_ref[0] = sb[TL + hb - (CONV_B_W - 1):TL + hb, :]

    sp[hp:hp + TL, :] = zc(OFF_POOL)
    halves = []
    for half in range(2):
        small, big, is_big = _pool_lane_windows(half)
        lanes = slice(half * LANES, (half + 1) * LANES)
        wl = jnp.where(is_big, float(big), float(small))
        outs = []
        for c in range(TL // ROW_CHUNK):
            r0 = hp + c * ROW_CHUNK
            acc = None
            for j in range(big):
                seg = sp[pl.ds(r0 - j, ROW_CHUNK), lanes]
                if j >= small:
                    seg = jnp.where(is_big, seg, 0.0)
                acc = seg if acc is None else acc + seg
            pos = i * TL + c * ROW_CHUNK + lax.broadcasted_iota(jnp.int32, (ROW_CHUNK, LANES), 0)
            cnt = jnp.minimum(wl, (pos + 1).astype(F32))
            outs.append(acc / cnt - sp[pl.ds(r0, ROW_CHUNK), lanes])
        halves.append(jnp.concatenate(outs, axis=0))
    pooled = jnp.concatenate(halves, axis=1)
    br_c = _bdot(pooled, pw_ref[...]) * vec[V_POOL_SCALE:V_POOL_SCALE + 1]
    stp_ref[0] = sp[TL + hp - (POOL_MAX - 1):TL + hp, :]

    v = _ln(zc(OFF_V), vec[V_SGU_LN_G:V_SGU_LN_G + 1], vec[V_SGU_LN_B:V_SGU_LN_B + 1])
    stv_ref[0] = v[TL - CHUNK:TL]
    swm = jnp.where(smask_ref[...] > 0, sw_ref[...], 0.0).astype(BF16)
    br_d = _sgu(zc(OFF_U), v, swm, sbias_ref[...])

    xo_ref[...] = _gated_merge(x, xnb, (br_a, br_b, br_c, br_d), win_ref, wbr_ref, wout_ref)


def _mixer_sample_kernel(x_ref, g_ref, win_ref, caw_ref, cbw_ref, vec_ref, pw_ref, sw_ref, smask_ref,
                         sbias_ref, wbr_ref, wout_ref, ina_ref, inb_ref, inp_ref,
                         xo_ref, sta_ref, stb_ref, stp_ref, stv_ref,
                         sa, sb, sp):
    seq = SUBLANES
    ha, hb, hp = SUBLANES, 4 * SUBLANES, 2 * SUBLANES
    sa[:, ha - (CONV_A_W - 1):ha, :] = ina_ref[...]
    sb[:, hb - (CONV_B_W - 1):hb, :] = inb_ref[...]
    sp[:, hp - (POOL_MAX - 1):hp, :] = inp_ref[...]

    x = x_ref[...]
    xnb = _rms(x, g_ref[...]).astype(BF16)
    vec = vec_ref[...]

    def zc(off):
        return jnp.dot(xnb, win_ref[:, off:off + D_BRANCH], preferred_element_type=F32)

    def to3(a):
        return a.reshape(NB, seq, D_BRANCH)

    def to2(a):
        return a.reshape(NB * seq, D_BRANCH)

    def tap_conv(s_ref, base, w_ref, ntaps):
        outs = []
        for c in range(NB // SEQ_CHUNK):
            acc = None
            for k in range(ntaps):
                seg = s_ref[c * SEQ_CHUNK:(c + 1) * SEQ_CHUNK, pl.ds(base + k, seq), :]
                term = seg * w_ref[k:k + 1, :][None]
                acc = term if acc is None else acc + term
            outs.append(acc)
        return to2(jnp.concatenate(outs, axis=0))

    sa[:, ha:ha + seq, :] = to3(zc(OFF_A_C) * zc(OFF_A_H))
    br_a = zc(OFF_A_B) * tap_conv(sa, ha - (CONV_A_W - 1), caw_ref, CONV_A_W)
    sta_ref[...] = sa[:, ha + seq - (CONV_A_W - 1):ha + seq, :]

    sb[:, hb:hb + seq, :] = to3(zc(OFF_CF_A) * jax.nn.sigmoid(zc(OFF_CF_B)))
    yb = tap_conv(sb, hb - (CONV_B_W - 1), cbw_ref, CONV_B_W) + vec[V_CB_BIAS:V_CB_BIAS + 1]
    br_b = _silu(_ln(yb, vec[V_CB_LN_G:V_CB_LN_G + 1], vec[V_CB_LN_B:V_CB_LN_B + 1]))
    stb_ref[...] = sb[:, hb + seq - (CONV_B_W - 1):hb + seq, :]

    sp[:, hp:hp + seq, :] = to3(zc(OFF_POOL))
    halves = []
    for half in range(2):
        small, big, is_big = _pool_lane_windows(half)
        lanes = slice(half * LANES, (half + 1) * LANES)
        wl = jnp.where(is_big, float(big), float(small))[None]
        outs = []
        for c in range(NB // SEQ_CHUNK):
            rows = slice(c * SEQ_CHUNK, (c + 1) * SEQ_CHUNK)
            acc = None
            for j in range(big):
                seg = sp[rows, pl.ds(hp - j, seq), lanes]
                if j >= small:
                    seg = jnp.where(is_big[None], seg, 0.0)
                acc = seg if acc is None else acc + seg
            outs.append(acc / wl - sp[rows, pl.ds(hp, seq), lanes])
        halves.append(jnp.concatenate(outs, axis=0).reshape(NB * seq, LANES))
    pooled = jnp.concatenate(halves, axis=1)
    br_c = _bdot(pooled, pw_ref[...]) * vec[V_POOL_SCALE:V_POOL_SCALE + 1]
    stp_ref[...] = sp[:, hp + seq - (POOL_MAX - 1):hp + seq, :]

    v = _ln(zc(OFF_V), vec[V_SGU_LN_G:V_SGU_LN_G + 1], vec[V_SGU_LN_B:V_SGU_LN_B + 1])
    stv_ref[...] = to3(v)
    swm = jnp.where(smask_ref[...] > 0, sw_ref[...], 0.0).astype(BF16)
    br_d = _sgu(zc(OFF_U), v, swm, sbias_ref[...])

    xo_ref[...] = _gated_merge(x, xnb, (br_a, br_b, br_c, br_d), win_ref, wbr_ref, wout_ref)


def _mixer_weight_specs():
    return [
        _const_spec((1, D_MODEL)),
        _const_spec((D_MODEL, OFF_GATE + N_BRANCH * D_MODEL)),
        _const_spec((CONV_A_W, D_BRANCH)),
        _const_spec((CONV_B_W, D_BRANCH)),
        _const_spec((SUBLANES, D_BRANCH)),
        _const_spec((D_BRANCH, D_BRANCH)),
        _const_spec((N_GROUPS * CHUNK, CHUNK)),
        _const_spec((N_GROUPS * CHUNK, CHUNK)),
        _const_spec((CHUNK, D_BRANCH)),
        _const_spec((N_BRANCH, D_BRANCH, D_MODEL)),
        _const_spec((D_MODEL, D_MODEL)),
    ]


def _mixer_prompt(xbuf, lw, batch, seq):
    nt = seq // TL
    rows = lambda b, i: (b * nt + i, 0)
    st = lambda n: pl.BlockSpec((1, n, D_BRANCH), lambda b, i: (b, 0, 0))
    return pl.pallas_call(
        _mixer_prompt_kernel,
        grid=(batch, nt),
        in_specs=[pl.BlockSpec((TL, D_MODEL), rows)] + _mixer_weight_specs(),
        out_specs=[pl.BlockSpec((TL, D_MODEL), rows), st(CONV_A_W - 1), st(CONV_B_W - 1),
                   st(POOL_MAX - 1), st(CHUNK)],
        out_shape=[jax.ShapeDtypeStruct(xbuf.shape, F32),
                   jax.ShapeDtypeStruct((batch, CONV_A_W - 1, D_BRANCH), F32),
                   jax.ShapeDtypeStruct((batch, CONV_B_W - 1, D_BRANCH), F32),
                   jax.ShapeDtypeStruct((batch, POOL_MAX - 1, D_BRANCH), F32),
                   jax.ShapeDtypeStruct((batch, CHUNK, D_BRANCH), F32)],
        scratch_shapes=[pltpu.VMEM((TL + SUBLANES, D_BRANCH), F32),
                        pltpu.VMEM((TL + 4 * SUBLANES, D_BRANCH), F32),
                        pltpu.VMEM((TL + 2 * SUBLANES, D_BRANCH), F32)],
        input_output_aliases={0: 0},
        compiler_params=pltpu.CompilerParams(dimension_semantics=("arbitrary", "arbitrary"),
                                             vmem_limit_bytes=VMEM_LIMIT),
        name="mixer_prompt",
    )(xbuf, lw["norm_mix_g"], lw["w_in"], lw["conv_a_w"], lw["conv_b_w"], lw["vec"], lw["pool_bd"],
      lw["sgu_w_p"], lw["sgu_mask_p"], lw["sgu_bias_p"], lw["w_branch"], lw["w_out"])


def _mixer_sample(xbuf, lw, st_a, st_b, st_p, row0, nseq, seq):
    assert seq == SUBLANES and nseq % NB == 0 and row0 % (NB * seq) == 0
    blk0 = row0 // (NB * seq)
    rows = lambda i: (blk0 + i, 0)
    st = lambda n: pl.BlockSpec((NB, n, D_BRANCH), lambda i: (i, 0, 0))
    return pl.pallas_call(
        _mixer_sample_kernel,
        grid=(nseq // NB,),
        in_specs=[pl.BlockSpec((NB * seq, D_MODEL), rows)] + _mixer_weight_specs()
                 + [st(CONV_A_W - 1), st(CONV_B_W - 1), st(POOL_MAX - 1)],
        out_specs=[pl.BlockSpec((NB * seq, D_MODEL), rows), st(CONV_A_W - 1), st(CONV_B_W - 1),
                   st(POOL_MAX - 1), st(seq)],
        out_shape=[jax.ShapeDtypeStruct(xbuf.shape, F32),
                   jax.ShapeDtypeStruct((nseq, CONV_A_W - 1, D_BRANCH), F32),
                   jax.ShapeDtypeStruct((nseq, CONV_B_W - 1, D_BRANCH), F32),
                   jax.ShapeDtypeStruct((nseq, POOL_MAX - 1, D_BRANCH), F32),
                   jax.ShapeDtypeStruct((nseq, seq, D_BRANCH), F32)],
        scratch_shapes=[pltpu.VMEM((NB, 2 * SUBLANES, D_BRANCH), F32),
                        pltpu.VMEM((NB, 5 * SUBLANES, D_BRANCH), F32),
                        pltpu.VMEM((NB, 3 * SUBLANES, D_BRANCH), F32)],
        input_output_aliases={0: 0},
        compiler_params=pltpu.CompilerParams(dimension_semantics=("arbitrary",),
                                             vmem_limit_bytes=VMEM_LIMIT),
        name="mixer_sample",
    )(xbuf, lw["norm_mix_g"], lw["w_in"], lw["conv_a_w"], lw["conv_b_w"], lw["vec"], lw["pool_bd"],
      lw["sgu_w_s"], lw["sgu_mask_s"], lw["sgu_bias_s"], lw["w_branch"], lw["w_out"],
      st_a, st_b, st_p)


def _ple_tail(x, p, gple_ref, wpg_ref, wpp_ref):
    gate = jax.nn.sigmoid(_bdot(_rms(x, gple_ref[...]), wpg_ref[...]))
    return x + gate * _bdot(p, wpp_ref[...])


def _dense_ffn_kernel(n_chunks, x_ref, p_ref, gffn_ref, wg_ref, wu_ref, wd_ref, gple_ref, wpg_ref, wpp_ref,
                      xo_ref):
    x = x_ref[...]
    xnb = _rms(x, gffn_ref[...]).astype(BF16)
    tf = wg_ref.shape[1] // n_chunks
    f = None
    for c in range(n_chunks):
        cols = slice(c * tf, (c + 1) * tf)
        h = _silu(jnp.dot(xnb, wg_ref[:, cols], preferred_element_type=F32)) * \
            jnp.dot(xnb, wu_ref[:, cols], preferred_element_type=F32)
        t = _bdot(h, wd_ref[cols, :])
        f = t if f is None else f + t
    xo_ref[...] = _ple_tail(x + f, p_ref[...], gple_ref, wpg_ref, wpp_ref)


def _dense_ffn(xbuf, pbuf, lw):
    t = xbuf.shape[0]
    d_ff = lw["ffn_w_gate"].shape[1]
    rows = lambda i: (i, 0)
    return pl.pallas_call(
        functools.partial(_dense_ffn_kernel, 2),
        grid=(t // TM,),
        in_specs=[pl.BlockSpec((TM, D_MODEL), rows), pl.BlockSpec((TM, D_PLE), rows),
                  _const_spec((1, D_MODEL)), _const_spec((D_MODEL, d_ff)), _const_spec((D_MODEL, d_ff)),
                  _const_spec((d_ff, D_MODEL)), _const_spec((1, D_MODEL)),
                  _const_spec((D_MODEL, D_MODEL)), _const_spec((D_PLE, D_MODEL))],
        out_specs=pl.BlockSpec((TM, D_MODEL), rows),
        out_shape=jax.ShapeDtypeStruct(xbuf.shape, F32),
        input_output_aliases={0: 0},
        compiler_params=pltpu.CompilerParams(dimension_semantics=("arbitrary",),
                                             vmem_limit_bytes=VMEM_LIMIT),
        name="dense_ffn_ple",
    )(xbuf, pbuf, lw["norm_ffn_g"], lw["ffn_w_gate"], lw["ffn_w_up"], lw["ffn_w_down"],
      lw["norm_ple_g"], lw["ple_w_gate"], lw["ple_w_proj"])


def _router_kernel(x_ref, g_ref, wr_ref, idx_ref, wgt_ref, xn_ref):
    xn = _rms(x_ref[...], g_ref[...])
    _store_row_tiled(xn_ref, xn)
    logits = jnp.dot(xn, wr_ref[...], preferred_element_type=F32, precision=lax.Precision.HIGHEST)
    lane = lax.broadcasted_iota(jnp.int32, logits.shape, 1)
    lane_f = lane.astype(F32)
    logits = jnp.where(lane < N_EXPERTS, logits, -jnp.inf)
    m1 = jnp.max(logits, axis=-1, keepdims=True)
    i1 = jnp.min(jnp.where(logits == m1, lane_f, float(LANES)), axis=-1, keepdims=True)
    rest = jnp.where(lane_f == i1, -jnp.inf, logits)
    m2 = jnp.max(rest, axis=-1, keepdims=True)
    i2 = jnp.min(jnp.where(rest == m2, lane_f, float(LANES)), axis=-1, keepdims=True)
    e2 = jnp.exp(m2 - m1)
    den = 1.0 + e2
    idx_ref[...] = jnp.where(lane == 0, i1, jnp.where(lane == 1, i2, 0.0)).astype(jnp.int32)
    wgt_ref[...] = jnp.where(lane == 0, 1.0 / den, jnp.where(lane == 1, e2 / den, 0.0))


def _router(xbuf, lw):
    t = xbuf.shape[0]
    rows = lambda i: (i, 0)
    return pl.pallas_call(
        _router_kernel,
        grid=(t // TM,),
        in_specs=[pl.BlockSpec((TM, D_MODEL), rows), _const_spec((1, D_MODEL)),
                  _const_spec((D_MODEL, LANES))],
        out_specs=[pl.BlockSpec((TM, LANES), rows), pl.BlockSpec((TM, LANES), rows),
                   pl.BlockSpec((TM * ROW_PIECES, LANES), rows)],
        out_shape=[jax.ShapeDtypeStruct((t, LANES), jnp.int32), jax.ShapeDtypeStruct((t, LANES), F32),
                   jax.ShapeDtypeStruct((t * ROW_PIECES, LANES), F32)],
        compiler_params=pltpu.CompilerParams(dimension_semantics=("arbitrary",)),
        name="router",
    )(xbuf, lw["norm_ffn_g"], lw["router_w"])


ROW_PIECES = D_MODEL // LANES
assert ROW_PIECES == SUBLANES


def _store_row_tiled(ref, x):
    for s in range(ROW_PIECES):
        ref[pl.ds(s, x.shape[0], stride=ROW_PIECES), :] = x[:, s * LANES:(s + 1) * LANES]


def _load_row_tiled(ref, n_rows):
    return jnp.concatenate([ref[pl.ds(s, n_rows, stride=ROW_PIECES), :] for s in range(ROW_PIECES)], axis=1)


def _start_row_gather(idx_ref, k, src_hbm, dst, sem):
    def issue(r, carry):
        src_row = pl.multiple_of(idx_ref[k, r] * ROW_PIECES, ROW_PIECES)
        dst_row = pl.multiple_of(r * ROW_PIECES, ROW_PIECES)
        pltpu.make_async_copy(src_hbm.at[pl.ds(src_row, ROW_PIECES)], dst.at[pl.ds(dst_row, ROW_PIECES)],
                              sem).start()
        return carry
    lax.fori_loop(0, dst.shape[0] // ROW_PIECES, issue, 0, unroll=8)


def _wait_row_gather(src_hbm, dst, sem):
    pltpu.make_async_copy(src_hbm.at[pl.ds(0, dst.shape[0])], dst, sem).wait()


def _expert_kernel(n_fc, te_ref, na_ref, tokc_ref, tokn_ref, xn_hbm, wg_ref, wu_ref, wd_ref, ys_ref,
                   gbuf, xs, acc, sems):
    i, j = pl.program_id(0), pl.program_id(1)
    na = na_ref[0]
    slot = i % 2

    @pl.when(jnp.logical_and(i < na, j == 0))
    def _():
        @pl.when(i == 0)
        def _():
            _start_row_gather(tokc_ref, 0, xn_hbm, gbuf.at[0], sems.at[0])

        @pl.when(i + 1 < na)
        def _():
            _start_row_gather(tokn_ref, 0, xn_hbm, gbuf.at[1 - slot], sems.at[1 - slot])

        _wait_row_gather(xn_hbm, gbuf.at[slot], sems.at[slot])
        xs[...] = _load_row_tiled(gbuf.at[slot], TM).astype(BF16)

    @pl.when(i < na)
    def _():
        h = _silu(jnp.dot(xs[...], wg_ref[...], preferred_element_type=F32)) * \
            jnp.dot(xs[...], wu_ref[...], preferred_element_type=F32)
        part = _bdot(h, wd_ref[...])

        if n_fc == 1:
            _store_row_tiled(ys_ref, part)
        else:
            @pl.when(j == 0)
            def _():
                acc[...] = part

            @pl.when(jnp.logical_and(j > 0, j < n_fc - 1))
            def _():
                acc[...] += part

            @pl.when(j == n_fc - 1)
            def _():
                _store_row_tiled(ys_ref, acc[...] + part)

    @pl.when(jnp.logical_and(i >= na, j == 0))
    def _():
        ys_ref[...] = jnp.zeros(ys_ref.shape, F32)


def _experts(xn_tiled, slot_tok, tile_expert, n_active, lw):
    n_tiles = tile_expert.shape[0]
    d_ff = lw["moe_w_gate"].shape[2]
    n_fc = d_ff // TF_EXPERT

    def fcol(i, j, na):
        return jnp.where(i < na[0], j, n_fc - 1)

    grid_spec = pltpu.PrefetchScalarGridSpec(
        num_scalar_prefetch=2,
        grid=(n_tiles, n_fc),
        in_specs=[
            pl.BlockSpec((None, 1, TM), lambda i, j, te, na: (i, 0, 0), memory_space=pltpu.SMEM),
            pl.BlockSpec((None, 1, TM), lambda i, j, te, na: (jnp.minimum(i + 1, n_tiles - 1), 0, 0),
                         memory_space=pltpu.SMEM),
            pl.BlockSpec(memory_space=pl.ANY),
            pl.BlockSpec((None, D_MODEL, TF_EXPERT), lambda i, j, te, na: (te[i], 0, fcol(i, j, na))),
            pl.BlockSpec((None, D_MODEL, TF_EXPERT), lambda i, j, te, na: (te[i], 0, fcol(i, j, na))),
            pl.BlockSpec((None, TF_EXPERT, D_MODEL), lambda i, j, te, na: (te[i], fcol(i, j, na), 0)),
        ],
        out_specs=pl.BlockSpec((TM * ROW_PIECES, LANES), lambda i, j, te, na: (i, 0)),
        scratch_shapes=[pltpu.VMEM((2, TM * ROW_PIECES, LANES), F32), pltpu.VMEM((TM, D_MODEL), BF16),
                        pltpu.VMEM((TM, D_MODEL), F32), pltpu.SemaphoreType.DMA((2,))],
    )
    return pl.pallas_call(
        functools.partial(_expert_kernel, n_fc),
        grid_spec=grid_spec,
        out_shape=jax.ShapeDtypeStruct((n_tiles * TM * ROW_PIECES, LANES), F32),
        compiler_params=pltpu.CompilerParams(dimension_semantics=("arbitrary", "arbitrary"),
                                             vmem_limit_bytes=VMEM_LIMIT),
        name="moe_experts",
    )(tile_expert, n_active, slot_tok, slot_tok, xn_tiled, lw["moe_w_gate"], lw["moe_w_up"], lw["moe_w_down"])


def _combine_kernel(final, n_steps, posc_ref, posn_ref, x_ref, p_ref, wgt_ref, ys_hbm, gple_ref, wpg_ref,
                    wpp_ref, gfin_ref, o_ref, buf, sems):
    i = pl.program_id(0)
    slot = i % 2

    def start(pos_ref, s):
        for k in range(TOP_K):
            _start_row_gather(pos_ref, k, ys_hbm, buf.at[s, k], sems.at[s, k])

    @pl.when(i == 0)
    def _():
        start(posc_ref, 0)

    @pl.when(i + 1 < n_steps)
    def _():
        start(posn_ref, 1 - slot)

    for k in range(TOP_K):
        _wait_row_gather(ys_hbm, buf.at[slot, k], sems.at[slot, k])
    wgt = wgt_ref[...]
    y0 = _load_row_tiled(buf.at[slot, 0], TM)
    y1 = _load_row_tiled(buf.at[slot, 1], TM)
    x = x_ref[...] + (y0 * wgt[:, 0:1] + y1 * wgt[:, 1:2])
    x = _ple_tail(x, p_ref[...], gple_ref, wpg_ref, wpp_ref)
    o_ref[...] = _rms(x, gfin_ref[...]) if final else x


def _combine(xbuf, pbuf, pos, wgt, ys, lw, final_g, row0, n_rows):
    blk0 = row0 // TM
    n_steps = n_rows // TM
    rows = lambda i: (blk0 + i, 0)
    final = final_g is not None
    gfin = final_g if final else lw["norm_ple_g"]
    return pl.pallas_call(
        functools.partial(_combine_kernel, final, n_steps),
        grid=(n_steps,),
        in_specs=[pl.BlockSpec((None, TOP_K, TM), lambda i: (blk0 + i, 0, 0), memory_space=pltpu.SMEM),
                  pl.BlockSpec((None, TOP_K, TM), lambda i: (blk0 + jnp.minimum(i + 1, n_steps - 1), 0, 0),
                               memory_space=pltpu.SMEM),
                  pl.BlockSpec((TM, D_MODEL), rows), pl.BlockSpec((TM, D_PLE), rows),
                  pl.BlockSpec((TM, LANES), rows), pl.BlockSpec(memory_space=pl.ANY),
                  _const_spec((1, D_MODEL)), _const_spec((D_MODEL, D_MODEL)),
                  _const_spec((D_PLE, D_MODEL)), _const_spec((1, D_MODEL))],
        out_specs=pl.BlockSpec((TM, D_MODEL), lambda i: (i, 0)),
        out_shape=jax.ShapeDtypeStruct((n_rows, D_MODEL), F32),
        scratch_shapes=[pltpu.VMEM((2, TOP_K, TM * ROW_PIECES, LANES), F32),
                        pltpu.SemaphoreType.DMA((2, TOP_K))],
        compiler_params=pltpu.CompilerParams(dimension_semantics=("arbitrary",),
                                             vmem_limit_bytes=VMEM_LIMIT),
        name="moe_combine_ple",
    )(pos, pos, xbuf, pbuf, wgt, ys, lw["norm_ple_g"], lw["ple_w_gate"], lw["ple_w_proj"], gfin)


def _routing_tables(idx, n_tok):
    n_pairs = n_tok * TOP_K
    n_tiles = n_pairs // TM + N_EXPERTS
    experts = jnp.arange(N_EXPERTS, dtype=jnp.int32)
    e_flat = idx[:, :TOP_K].reshape(-1)
    onehot = (e_flat[:, None] == experts[None, :]).astype(jnp.int32)
    csum = jnp.cumsum(onehot, axis=0)
    sizes = csum[-1]
    padded = ((sizes + TM - 1) // TM) * TM
    pend = jnp.cumsum(padded)
    pstart = pend - padded
    pos = jnp.sum(onehot * (csum - 1 + pstart[None, :]), axis=1)
    fill_e = jnp.repeat(experts, TM)
    fill_r = jnp.tile(jnp.arange(TM, dtype=jnp.int32), N_EXPERTS)
    fill_need = jnp.sum((fill_e[:, None] == experts[None, :]) * (padded - sizes)[None, :], axis=1)
    keys = jnp.concatenate([e_flat, jnp.where(fill_r < fill_need, fill_e, N_EXPERTS)])
    toks = jnp.concatenate([jnp.arange(n_pairs, dtype=jnp.int32) // TOP_K,
                            jnp.zeros((N_EXPERTS * TM,), jnp.int32)])
    _, slot_tok = lax.sort((keys, toks), num_keys=1, is_stable=True)
    n_active = (pend[-1] // TM).astype(jnp.int32)
    tile_row = jnp.minimum(jnp.arange(n_tiles, dtype=jnp.int32), n_active - 1)
    tile_expert = jnp.minimum(
        jnp.sum((tile_row[:, None] * TM >= pend[None, :]).astype(jnp.int32), axis=1), N_EXPERTS - 1)
    pos_tiles = pos.reshape(n_tok // TM, TM, TOP_K).transpose(0, 2, 1)
    return slot_tok.reshape(n_tiles, 1, TM), tile_expert, n_active.reshape(1), pos_tiles


def _sgu_tables(sgu_w, sgu_b, seq):
    r = np.arange(CHUNK)
    bias = jnp.repeat(sgu_b.T, GROUP_DIM, axis=1)
    if seq >= CHUNK:
        w = sgu_w
        mask = r[None, :] <= r[:, None]
    else:
        reps = CHUNK // seq
        w = jnp.tile(sgu_w[:, :seq, :seq], (1, reps, reps))
        mask = (r[:, None] // seq == r[None, :] // seq) & (r[None, :] % seq <= r[:, None] % seq)
        bias = jnp.tile(bias[:seq], (reps, 1))
    mask = np.tile(mask.astype(np.float32), (N_GROUPS, 1))
    return w.reshape(N_GROUPS * CHUNK, CHUNK), jnp.asarray(mask), bias


def _block_diag(w):
    out = jnp.zeros((D_BRANCH, D_BRANCH), w.dtype)
    for g in range(N_GROUPS):
        out = out.at[g * GROUP_DIM:(g + 1) * GROUP_DIM, g * GROUP_DIM:(g + 1) * GROUP_DIM].set(w[g])
    return out


def kernel(x_prompt, x_sample, state_conv_a, state_conv_b, state_pool, p_prompt, p_sample, norm_mix_g, w_in, conv_a_w, conv_b_w, conv_b_bias, conv_b_ln_g, conv_b_ln_b, pool_w, pool_scale, sgu_ln_g, sgu_ln_b, sgu_w, sgu_b, w_branch, w_out, norm_ffn_g, ffn_w_gate, ffn_w_up, ffn_w_down, router_w, moe_w_gate, moe_w_up, moe_w_down, norm_ple_g, ple_w_gate, ple_w_proj, final_norm_g):
    batch, seq, _ = x_prompt.shape
    nseq, dseq, _ = x_sample.shape
    depth = w_in.shape[0]
    assert depth % 2 == 0
    assert PAST_LEN + 1 >= POOL_MAX
    assert seq % TL == 0 and TL % CHUNK == 0 and TL >= CONV_B_W + 1
    n_prompt, n_sample = batch * seq, nseq * dseq
    n_tok = n_prompt + n_sample
    assert n_prompt % TM == 0 and n_sample % TM == 0

    xbuf = jnp.concatenate([x_prompt.reshape(n_prompt, D_MODEL), x_sample.reshape(n_sample, D_MODEL)], axis=0)
    pbuf = jnp.concatenate([p_prompt.reshape(depth, n_prompt, D_PLE), p_sample.reshape(depth, n_sample, D_PLE)],
                           axis=1)
    row = lambda a: a.reshape(1, -1)
    zeros = jnp.zeros((D_BRANCH,), F32)

    states_p, states_s = [], []
    y_prompt = y_sample = None
    for i in range(depth):
        sw_p, sm_p, sb_p = _sgu_tables(sgu_w[i], sgu_b[i], seq)
        sw_s, sm_s, sb_s = _sgu_tables(sgu_w[i], sgu_b[i], dseq)
        lw = {
            "norm_mix_g": row(norm_mix_g[i]), "w_in": w_in[i].astype(BF16),
            "conv_a_w": conv_a_w[i], "conv_b_w": conv_b_w[i],
            "vec": jnp.stack([conv_b_bias[i], conv_b_ln_g[i], conv_b_ln_b[i], pool_scale[i],
                              sgu_ln_g[i], sgu_ln_b[i], zeros, zeros]),
            "pool_bd": _block_diag(pool_w[i]).astype(BF16),
            "sgu_w_p": sw_p, "sgu_mask_p": sm_p, "sgu_bias_p": sb_p,
            "sgu_w_s": sw_s, "sgu_mask_s": sm_s, "sgu_bias_s": sb_s,
            "w_branch": w_branch[i].astype(BF16), "w_out": w_out[i].astype(BF16),
            "norm_ffn_g": row(norm_ffn_g[i]), "norm_ple_g": row(norm_ple_g[i]),
            "ple_w_gate": ple_w_gate[i].astype(BF16), "ple_w_proj": ple_w_proj[i].astype(BF16),
        }
        xbuf, a_p, b_p, pool_p, v_p = _mixer_prompt(xbuf, lw, batch, seq)
        xbuf, a_s, b_s, pool_s, v_s = _mixer_sample(xbuf, lw, state_conv_a[i], state_conv_b[i], state_pool[i],
                                                    n_prompt, nseq, dseq)
        states_p.append((a_p, b_p, pool_p, v_p))
        states_s.append((a_s, b_s, pool_s, v_s))

        j = i // 2
        last = i == depth - 1
        if i % 2 == 0:
            lw.update({"ffn_w_gate": ffn_w_gate[j].astype(BF16), "ffn_w_up": ffn_w_up[j].astype(BF16),
                       "ffn_w_down": ffn_w_down[j].astype(BF16)})
            xbuf = _dense_ffn(xbuf, pbuf[i], lw)
        else:
            lw.update({"router_w": jnp.pad(router_w[j], ((0, 0), (0, LANES - N_EXPERTS))),
                       "moe_w_gate": moe_w_gate[j].astype(BF16), "moe_w_up": moe_w_up[j].astype(BF16),
                       "moe_w_down": moe_w_down[j].astype(BF16)})
            idx, wgt, xn_tiled = _router(xbuf, lw)
            slot_tok, tile_expert, n_active, pos = _routing_tables(idx, n_tok)
            ys = _experts(xn_tiled, slot_tok, tile_expert, n_active, lw)
            fin = row(final_norm_g) if last else None
            out_p = _combine(xbuf, pbuf[i], pos, wgt, ys, lw, fin, 0, n_prompt)
            out_s = _combine(xbuf, pbuf[i], pos, wgt, ys, lw, fin, n_prompt, n_sample)
            if last:
                y_prompt, y_sample = out_p, out_s
            else:
                xbuf = jnp.concatenate([out_p, out_s], axis=0)

    stack = lambda k, sts: jnp.stack([s[k] for s in sts])
    return (y_prompt.reshape(batch, seq, D_MODEL), y_sample.reshape(nseq, dseq, D_MODEL),
            stack(0, states_p), stack(1, states_p), stack(2, states_p), stack(3, states_p),
            stack(0, states_s), stack(1, states_s), stack(2, states_s), stack(3, states_s))
```

```python
import functools

import jax
import jax.numpy as jnp
import numpy as np
from jax import lax
from jax.experimental import pallas as pl
from jax.experimental.pallas import tpu as pltpu

F32 = jnp.float32
BF16 = jnp.bfloat16

D_MODEL = 1024
N_BRANCH = 4
D_BRANCH = 256
N_GROUPS = 4
GROUP_DIM = 64
CONV_A_W = 3
CONV_B_W = 31
POOL_WINDOWS = (2, 4, 8, 16)
POOL_MAX = 16
CHUNK = 128
PAST_LEN = 16384
D_PLE = 256
N_EXPERTS = 8
TOP_K = 2
EPS = 1e-6

OFF_A_H = 0
OFF_A_B = 256
OFF_A_C = 512
OFF_CF_A = 768
OFF_CF_B = 1024
OFF_POOL = 1280
OFF_U = 1536
OFF_V = 1792
OFF_GATE = 2048

LANES = 128
SUBLANES = 8
TM = 512
TL = 512
NB = 64
TF_EXPERT = 1792
ROW_CHUNK = 32
SEQ_CHUNK = 8
VMEM_LIMIT = 56 * 1024 * 1024

V_CB_BIAS, V_CB_LN_G, V_CB_LN_B, V_POOL_SCALE, V_SGU_LN_G, V_SGU_LN_B = range(6)


def _rms(x, g):
    return x * lax.rsqrt(jnp.mean(x * x, axis=-1, keepdims=True) + EPS) * g


def _ln(x, g, b):
    xc = x - jnp.mean(x, axis=-1, keepdims=True)
    var = jnp.mean(xc * xc, axis=-1, keepdims=True)
    return xc * lax.rsqrt(var + EPS) * g + b


def _bdot(a, w):
    return jnp.dot(a.astype(BF16), w, preferred_element_type=F32)


def _silu(x):
    return x * jax.nn.sigmoid(x)


def _const_spec(shape):
    nd = len(shape)
    return pl.BlockSpec(shape, lambda *_: (0,) * nd, pipeline_mode=pl.Buffered(1))


def _sgu(u, v, swm, sbias):
    m = v.shape[0]
    group = lax.broadcasted_iota(jnp.int32, (CHUNK, D_BRANCH), 1) // GROUP_DIM
    outs = []
    for c in range(m // CHUNK):
        vc = v[c * CHUNK:(c + 1) * CHUNK].astype(BF16)
        r = jnp.dot(swm, vc, preferred_element_type=F32)
        s = sbias
        for g in range(N_GROUPS):
            s = s + jnp.where(group == g, r[g * CHUNK:(g + 1) * CHUNK], 0.0)
        outs.append(s)
    return u * jnp.concatenate(outs, axis=0)


GATE_PIECES_PER_BRANCH = D_MODEL // D_BRANCH
N_GATE_PIECES = N_BRANCH * GATE_PIECES_PER_BRANCH


def _gate_emitter(xnb, win_ref):
    pieces = []

    def emit(n=1):
        for _ in range(n):
            if len(pieces) < N_GATE_PIECES:
                lo = OFF_GATE + len(pieces) * D_BRANCH
                z = jnp.dot(xnb, win_ref[:, lo:lo + D_BRANCH], preferred_element_type=F32)
                pieces.append(jax.nn.sigmoid(z))
    return pieces, emit


def _gated_merge(x, branches, gates, wbr_ref, wout_ref):
    bb = [b.astype(BF16) for b in branches]
    cols = []
    for q in range(GATE_PIECES_PER_BRANCH):
        cs = slice(q * D_BRANCH, (q + 1) * D_BRANCH)
        m = None
        for i in range(N_BRANCH):
            t = gates[i * GATE_PIECES_PER_BRANCH + q] * \
                jnp.dot(bb[i], wbr_ref[i, :, cs], preferred_element_type=F32)
            m = t if m is None else m + t
        cols.append(m.astype(BF16))
    return x + jnp.dot(jnp.concatenate(cols, axis=1), wout_ref[...], preferred_element_type=F32)


def _pool_lane_windows(half):
    small, big = POOL_WINDOWS[2 * half], POOL_WINDOWS[2 * half + 1]
    lane = lax.broadcasted_iota(jnp.int32, (1, LANES), 1)
    is_big = lane >= GROUP_DIM
    return small, big, is_big


def _mixer_prompt_kernel(x_ref, g_ref, win_ref, caw_ref, cbw_ref, vec_ref, pw_ref, sw_ref, smask_ref,
                         sbias_ref, wbr_ref, wout_ref,
                         xo_ref, sta_ref, stb_ref, stp_ref, stv_ref,
                         sa, sb, sp):
    i = pl.program_id(1)
    ha, hb, hp = SUBLANES, 4 * SUBLANES, 2 * SUBLANES

    @pl.when(i == 0)
    def _():
        sa[0:ha, :] = jnp.zeros((ha, D_BRANCH), F32)
        sb[0:hb, :] = jnp.zeros((hb, D_BRANCH), F32)
        sp[0:hp, :] = jnp.zeros((hp, D_BRANCH), F32)

    @pl.when(i > 0)
    def _():
        sa[0:ha, :] = sa[TL:TL + ha, :]
        sb[0:hb, :] = sb[TL:TL + hb, :]
        sp[0:hp, :] = sp[TL:TL + hp, :]

    x = x_ref[...]
    xnb = _rms(x, g_ref[...]).astype(BF16)
    vec = vec_ref[...]

    def zc(off):
        return jnp.dot(xnb, win_ref[:, off:off + D_BRANCH], preferred_element_type=F32)

    gates, emit_gate = _gate_emitter(xnb, win_ref)
    n_chunks = TL // ROW_CHUNK

    def tap_conv(s_ref, base, w_ref, ntaps, gate_chunks=()):
        outs = []
        for c in range(n_chunks):
            acc = None
            for k in range(ntaps):
                term = s_ref[pl.ds(base + k + c * ROW_CHUNK, ROW_CHUNK), :] * w_ref[k:k + 1, :]
                acc = term if acc is None else acc + term
            outs.append(acc)
            if c in gate_chunks:
                emit_gate()
        return jnp.concatenate(outs, axis=0)

    sa[ha:ha + TL, :] = zc(OFF_A_C) * zc(OFF_A_H)
    br_a = zc(OFF_A_B) * tap_conv(sa, ha - (CONV_A_W - 1), caw_ref, CONV_A_W)
    sta_ref[0] = sa[TL + ha - (CONV_A_W - 1):TL + ha, :]

    sb[hb:hb + TL, :] = zc(OFF_CF_A) * jax.nn.sigmoid(zc(OFF_CF_B))
    yb = tap_conv(sb, hb - (CONV_B_W - 1), cbw_ref, CONV_B_W, gate_chunks=range(3 * n_chunks // 4))
    yb = yb + vec[V_CB_BIAS:V_CB_BIAS + 1]
    br_b = _silu(_ln(yb, vec[V_CB_LN_G:V_CB_LN_G + 1], vec[V_CB_LN_B:V_CB_LN_B + 1]))
    stb_ref[0] = sb[TL + hb - (CONV_B_W - 1):TL + hb, :]

    sp[hp:hp + TL, :] = zc(OFF_POOL)
    halves = []
    for half in range(2):
        small, big, is_big = _pool_lane_windows(half)
        lanes = slice(half * LANES, (half + 1) * LANES)
        wl = jnp.where(is_big, float(big), float(small))
        outs = []
        for c in range(n_chunks):
            r0 = hp + c * ROW_CHUNK
            acc = None
            for j in range(big):
                seg = sp[pl.ds(r0 - j, ROW_CHUNK), lanes]
                if j >= small:
                    seg = jnp.where(is_big, seg, 0.0)
                acc = seg if acc is None else acc + seg
            pos = i * TL + c * ROW_CHUNK + lax.broadcasted_iota(jnp.int32, (ROW_CHUNK, LANES), 0)
            cnt = jnp.minimum(wl, (pos + 1).astype(F32))
            outs.append(acc / cnt - sp[pl.ds(r0, ROW_CHUNK), lanes])
            if half == 1 and c % 4 == 0:
                emit_gate()
        halves.append(jnp.concatenate(outs, axis=0))
    pooled = jnp.concatenate(halves, axis=1)
    br_c = _bdot(pooled, pw_ref[...]) * vec[V_POOL_SCALE:V_POOL_SCALE + 1]
    stp_ref[0] = sp[TL + hp - (POOL_MAX - 1):TL + hp, :]

    v = _ln(zc(OFF_V), vec[V_SGU_LN_G:V_SGU_LN_G + 1], vec[V_SGU_LN_B:V_SGU_LN_B + 1])
    stv_ref[0] = v[TL - CHUNK:TL]
    swm = jnp.where(smask_ref[...] > 0, sw_ref[...], 0.0).astype(BF16)
    br_d = _sgu(zc(OFF_U), v, swm, sbias_ref[...])

    emit_gate(N_GATE_PIECES)
    xo_ref[...] = _gated_merge(x, (br_a, br_b, br_c, br_d), gates, wbr_ref, wout_ref)


def _mixer_sample_kernel(x_ref, g_ref, win_ref, caw_ref, cbw_ref, vec_ref, pw_ref, sw_ref, smask_ref,
                         sbias_ref, wbr_ref, wout_ref, ina_ref, inb_ref, inp_ref,
                         xo_ref, sta_ref, stb_ref, stp_ref, stv_ref,
                         sa, sb, sp):
    seq = SUBLANES
    ha, hb, hp = SUBLANES, 4 * SUBLANES, 2 * SUBLANES
    sa[:, ha - (CONV_A_W - 1):ha, :] = ina_ref[...]
    sb[:, hb - (CONV_B_W - 1):hb, :] = inb_ref[...]
    sp[:, hp - (POOL_MAX - 1):hp, :] = inp_ref[...]

    x = x_ref[...]
    xnb = _rms(x, g_ref[...]).astype(BF16)
    vec = vec_ref[...]

    def zc(off):
        return jnp.dot(xnb, win_ref[:, off:off + D_BRANCH], preferred_element_type=F32)

    def to3(a):
        return a.reshape(NB, seq, D_BRANCH)

    def to2(a):
        return a.reshape(NB * seq, D_BRANCH)

    def tap_conv(s_ref, base, w_ref, ntaps):
        outs = []
        for c in range(NB // SEQ_CHUNK):
            acc = None
            for k in range(ntaps):
                seg = s_ref[c * SEQ_CHUNK:(c + 1) * SEQ_CHUNK, pl.ds(base + k, seq), :]
                term = seg * w_ref[k:k + 1, :][None]
                acc = term if acc is None else acc + term
            outs.append(acc)
        return to2(jnp.concatenate(outs, axis=0))

    sa[:, ha:ha + seq, :] = to3(zc(OFF_A_C) * zc(OFF_A_H))
    br_a = zc(OFF_A_B) * tap_conv(sa, ha - (CONV_A_W - 1), caw_ref, CONV_A_W)
    sta_ref[...] = sa[:, ha + seq - (CONV_A_W - 1):ha + seq, :]

    sb[:, hb:hb + seq, :] = to3(zc(OFF_CF_A) * jax.nn.sigmoid(zc(OFF_CF_B)))
    yb = tap_conv(sb, hb - (CONV_B_W - 1), cbw_ref, CONV_B_W) + vec[V_CB_BIAS:V_CB_BIAS + 1]
    br_b = _silu(_ln(yb, vec[V_CB_LN_G:V_CB_LN_G + 1], vec[V_CB_LN_B:V_CB_LN_B + 1]))
    stb_ref[...] = sb[:, hb + seq - (CONV_B_W - 1):hb + seq, :]

    sp[:, hp:hp + seq, :] = to3(zc(OFF_POOL))
    halves = []
    for half in range(2):
        small, big, is_big = _pool_lane_windows(half)
        lanes = slice(half * LANES, (half + 1) * LANES)
        wl = jnp.where(is_big, float(big), float(small))[None]
        outs = []
        for c in range(NB // SEQ_CHUNK):
            rows = slice(c * SEQ_CHUNK, (c + 1) * SEQ_CHUNK)
            acc = None
            for j in range(big):
                seg = sp[rows, pl.ds(hp - j, seq), lanes]
                if j >= small:
                    seg = jnp.where(is_big[None], seg, 0.0)
                acc = seg if acc is None else acc + seg
            outs.append(acc / wl - sp[rows, pl.ds(hp, seq), lanes])
        halves.append(jnp.concatenate(outs, axis=0).reshape(NB * seq, LANES))
    pooled = jnp.concatenate(halves, axis=1)
    br_c = _bdot(pooled, pw_ref[...]) * vec[V_POOL_SCALE:V_POOL_SCALE + 1]
    stp_ref[...] = sp[:, hp + seq - (POOL_MAX - 1):hp + seq, :]

    v = _ln(zc(OFF_V), vec[V_SGU_LN_G:V_SGU_LN_G + 1], vec[V_SGU_LN_B:V_SGU_LN_B + 1])
    stv_ref[...] = to3(v)
    swm = jnp.where(smask_ref[...] > 0, sw_ref[...], 0.0).astype(BF16)
    br_d = _sgu(zc(OFF_U), v, swm, sbias_ref[...])

    gates, emit_gate = _gate_emitter(xnb, win_ref)
    emit_gate(N_GATE_PIECES)
    xo_ref[...] = _gated_merge(x, (br_a, br_b, br_c, br_d), gates, wbr_ref, wout_ref)


def _mixer_weight_specs():
    return [
        _const_spec((1, D_MODEL)),
        _const_spec((D_MODEL, OFF_GATE + N_BRANCH * D_MODEL)),
        _const_spec((CONV_A_W, D_BRANCH)),
        _const_spec((CONV_B_W, D_BRANCH)),
        _const_spec((SUBLANES, D_BRANCH)),
        _const_spec((D_BRANCH, D_BRANCH)),
        _const_spec((N_GROUPS * CHUNK, CHUNK)),
        _const_spec((N_GROUPS * CHUNK, CHUNK)),
        _const_spec((CHUNK, D_BRANCH)),
        _const_spec((N_BRANCH, D_BRANCH, D_MODEL)),
        _const_spec((D_MODEL, D_MODEL)),
    ]


def _mixer_prompt(xbuf, lw, batch, seq):
    nt = seq // TL
    rows = lambda b, i: (b * nt + i, 0)
    st = lambda n: pl.BlockSpec((1, n, D_BRANCH), lambda b, i: (b, 0, 0))
    return pl.pallas_call(
        _mixer_prompt_kernel,
        grid=(batch, nt),
        in_specs=[pl.BlockSpec((TL, D_MODEL), rows)] + _mixer_weight_specs(),
        out_specs=[pl.BlockSpec((TL, D_MODEL), rows), st(CONV_A_W - 1), st(CONV_B_W - 1),
                   st(POOL_MAX - 1), st(CHUNK)],
        out_shape=[jax.ShapeDtypeStruct(xbuf.shape, F32),
                   jax.ShapeDtypeStruct((batch, CONV_A_W - 1, D_BRANCH), F32),
                   jax.ShapeDtypeStruct((batch, CONV_B_W - 1, D_BRANCH), F32),
                   jax.ShapeDtypeStruct((batch, POOL_MAX - 1, D_BRANCH), F32),
                   jax.ShapeDtypeStruct((batch, CHUNK, D_BRANCH), F32)],
        scratch_shapes=[pltpu.VMEM((TL + SUBLANES, D_BRANCH), F32),
                        pltpu.VMEM((TL + 4 * SUBLANES, D_BRANCH), F32),
                        pltpu.VMEM((TL + 2 * SUBLANES, D_BRANCH), F32)],
        input_output_aliases={0: 0},
        compiler_params=pltpu.CompilerParams(dimension_semantics=("arbitrary", "arbitrary"),
                                             vmem_limit_bytes=VMEM_LIMIT),
        name="mixer_prompt",
    )(xbuf, lw["norm_mix_g"], lw["w_in"], lw["conv_a_w"], lw["conv_b_w"], lw["vec"], lw["pool_bd"],
      lw["sgu_w_p"], lw["sgu_mask_p"], lw["sgu_bias_p"], lw["w_branch"], lw["w_out"])


def _mixer_sample(xbuf, lw, st_a, st_b, st_p, row0, nseq, seq):
    assert seq == SUBLANES and nseq % NB == 0 and row0 % (NB * seq) == 0
    blk0 = row0 // (NB * seq)
    rows = lambda i: (blk0 + i, 0)
    st = lambda n: pl.BlockSpec((NB, n, D_BRANCH), lambda i: (i, 0, 0))
    return pl.pallas_call(
        _mixer_sample_kernel,
        grid=(nseq // NB,),
        in_specs=[pl.BlockSpec((NB * seq, D_MODEL), rows)] + _mixer_weight_specs()
                 + [st(CONV_A_W - 1), st(CONV_B_W - 1), st(POOL_MAX - 1)],
        out_specs=[pl.BlockSpec((NB * seq, D_MODEL), rows), st(CONV_A_W - 1), st(CONV_B_W - 1),
                   st(POOL_MAX - 1), st(seq)],
        out_shape=[jax.ShapeDtypeStruct(xbuf.shape, F32),
                   jax.ShapeDtypeStruct((nseq, CONV_A_W - 1, D_BRANCH), F32),
                   jax.ShapeDtypeStruct((nseq, CONV_B_W - 1, D_BRANCH), F32),
                   jax.ShapeDtypeStruct((nseq, POOL_MAX - 1, D_BRANCH), F32),
                   jax.ShapeDtypeStruct((nseq, seq, D_BRANCH), F32)],
        scratch_shapes=[pltpu.VMEM((NB, 2 * SUBLANES, D_BRANCH), F32),
                        pltpu.VMEM((NB, 5 * SUBLANES, D_BRANCH), F32),
                        pltpu.VMEM((NB, 3 * SUBLANES, D_BRANCH), F32)],
        input_output_aliases={0: 0},
        compiler_params=pltpu.CompilerParams(dimension_semantics=("arbitrary",),
                                             vmem_limit_bytes=VMEM_LIMIT),
        name="mixer_sample",
    )(xbuf, lw["norm_mix_g"], lw["w_in"], lw["conv_a_w"], lw["conv_b_w"], lw["vec"], lw["pool_bd"],
      lw["sgu_w_s"], lw["sgu_mask_s"], lw["sgu_bias_s"], lw["w_branch"], lw["w_out"],
      st_a, st_b, st_p)


def _ple_tail(x, p, gple_ref, wpg_ref, wpp_ref):
    gate = jax.nn.sigmoid(_bdot(_rms(x, gple_ref[...]), wpg_ref[...]))
    return x + gate * _bdot(p, wpp_ref[...])


def _dense_ffn_kernel(n_chunks, x_ref, p_ref, gffn_ref, wg_ref, wu_ref, wd_ref, gple_ref, wpg_ref, wpp_ref,
                      xo_ref):
    x = x_ref[...]
    xnb = _rms(x, gffn_ref[...]).astype(BF16)
    tf = wg_ref.shape[1] // n_chunks
    f = None
    for c in range(n_chunks):
        cols = slice(c * tf, (c + 1) * tf)
        h = _silu(jnp.dot(xnb, wg_ref[:, cols], preferred_element_type=F32)) * \
            jnp.dot(xnb, wu_ref[:, cols], preferred_element_type=F32)
        t = _bdot(h, wd_ref[cols, :])
        f = t if f is None else f + t
    xo_ref[...] = _ple_tail(x + f, p_ref[...], gple_ref, wpg_ref, wpp_ref)


def _dense_ffn(xbuf, pbuf, lw):
    t = xbuf.shape[0]
    d_ff = lw["ffn_w_gate"].shape[1]
    rows = lambda i: (i, 0)
    return pl.pallas_call(
        functools.partial(_dense_ffn_kernel, 2),
        grid=(t // TM,),
        in_specs=[pl.BlockSpec((TM, D_MODEL), rows), pl.BlockSpec((TM, D_PLE), rows),
                  _const_spec((1, D_MODEL)), _const_spec((D_MODEL, d_ff)), _const_spec((D_MODEL, d_ff)),
                  _const_spec((d_ff, D_MODEL)), _const_spec((1, D_MODEL)),
                  _const_spec((D_MODEL, D_MODEL)), _const_spec((D_PLE, D_MODEL))],
        out_specs=pl.BlockSpec((TM, D_MODEL), rows),
        out_shape=jax.ShapeDtypeStruct(xbuf.shape, F32),
        input_output_aliases={0: 0},
        compiler_params=pltpu.CompilerParams(dimension_semantics=("arbitrary",),
                                             vmem_limit_bytes=VMEM_LIMIT),
        name="dense_ffn_ple",
    )(xbuf, pbuf, lw["norm_ffn_g"], lw["ffn_w_gate"], lw["ffn_w_up"], lw["ffn_w_down"],
      lw["norm_ple_g"], lw["ple_w_gate"], lw["ple_w_proj"])


def _router_kernel(x_ref, g_ref, wr_ref, idx_ref, wgt_ref, xn_ref):
    xn = _rms(x_ref[...], g_ref[...])
    _store_row_tiled(xn_ref, xn)
    logits = jnp.dot(xn, wr_ref[...], preferred_element_type=F32, precision=lax.Precision.HIGHEST)
    lane = lax.broadcasted_iota(jnp.int32, logits.shape, 1)
    lane_f = lane.astype(F32)
    logits = jnp.where(lane < N_EXPERTS, logits, -jnp.inf)
    m1 = jnp.max(logits, axis=-1, keepdims=True)
    i1 = jnp.min(jnp.where(logits == m1, lane_f, float(LANES)), axis=-1, keepdims=True)
    rest = jnp.where(lane_f == i1, -jnp.inf, logits)
    m2 = jnp.max(rest, axis=-1, keepdims=True)
    i2 = jnp.min(jnp.where(rest == m2, lane_f, float(LANES)), axis=-1, keepdims=True)
    e2 = jnp.exp(m2 - m1)
    den = 1.0 + e2
    idx_ref[...] = jnp.where(lane == 0, i1, jnp.where(lane == 1, i2, 0.0)).astype(jnp.int32)
    wgt_ref[...] = jnp.where(lane == 0, 1.0 / den, jnp.where(lane == 1, e2 / den, 0.0))


def _router(xbuf, lw):
    t = xbuf.shape[0]
    rows = lambda i: (i, 0)
    return pl.pallas_call(
        _router_kernel,
        grid=(t // TM,),
        in_specs=[pl.BlockSpec((TM, D_MODEL), rows), _const_spec((1, D_MODEL)),
                  _const_spec((D_MODEL, LANES))],
        out_specs=[pl.BlockSpec((TM, LANES), rows), pl.BlockSpec((TM, LANES), rows),
                   pl.BlockSpec((TM * ROW_PIECES, LANES), rows)],
        out_shape=[jax.ShapeDtypeStruct((t, LANES), jnp.int32), jax.ShapeDtypeStruct((t, LANES), F32),
                   jax.ShapeDtypeStruct((t * ROW_PIECES, LANES), F32)],
        compiler_params=pltpu.CompilerParams(dimension_semantics=("arbitrary",)),
        name="router",
    )(xbuf, lw["norm_ffn_g"], lw["router_w"])


ROW_PIECES = D_MODEL // LANES
assert ROW_PIECES == SUBLANES


def _store_row_tiled(ref, x):
    for s in range(ROW_PIECES):
        ref[pl.ds(s, x.shape[0], stride=ROW_PIECES), :] = x[:, s * LANES:(s + 1) * LANES]


def _load_row_tiled(ref, n_rows):
    return jnp.concatenate([ref[pl.ds(s, n_rows, stride=ROW_PIECES), :] for s in range(ROW_PIECES)], axis=1)


def _start_row_gather(idx_ref, k, src_hbm, dst, sem):
    def issue(r, carry):
        src_row = pl.multiple_of(idx_ref[k, r] * ROW_PIECES, ROW_PIECES)
        dst_row = pl.multiple_of(r * ROW_PIECES, ROW_PIECES)
        pltpu.make_async_copy(src_hbm.at[pl.ds(src_row, ROW_PIECES)], dst.at[pl.ds(dst_row, ROW_PIECES)],
                              sem).start()
        return carry
    lax.fori_loop(0, dst.shape[0] // ROW_PIECES, issue, 0, unroll=8)


def _wait_row_gather(src_hbm, dst, sem):
    pltpu.make_async_copy(src_hbm.at[pl.ds(0, dst.shape[0])], dst, sem).wait()


def _expert_kernel(n_fc, te_ref, na_ref, tokc_ref, tokn_ref, xn_hbm, wg_ref, wu_ref, wd_ref, ys_ref,
                   gbuf, xs, acc, sems):
    i, j = pl.program_id(0), pl.program_id(1)
    na = na_ref[0]
    slot = i % 2

    @pl.when(jnp.logical_and(i < na, j == 0))
    def _():
        @pl.when(i == 0)
        def _():
            _start_row_gather(tokc_ref, 0, xn_hbm, gbuf.at[0], sems.at[0])

        @pl.when(i + 1 < na)
        def _():
            _start_row_gather(tokn_ref, 0, xn_hbm, gbuf.at[1 - slot], sems.at[1 - slot])

        _wait_row_gather(xn_hbm, gbuf.at[slot], sems.at[slot])
        xs[...] = _load_row_tiled(gbuf.at[slot], TM).astype(BF16)

    @pl.when(i < na)
    def _():
        h = _silu(jnp.dot(xs[...], wg_ref[...], preferred_element_type=F32)) * \
            jnp.dot(xs[...], wu_ref[...], preferred_element_type=F32)
        part = _bdot(h, wd_ref[...])

        if n_fc == 1:
            _store_row_tiled(ys_ref, part)
        else:
            @pl.when(j == 0)
            def _():
                acc[...] = part

            @pl.when(jnp.logical_and(j > 0, j < n_fc - 1))
            def _():
                acc[...] += part

            @pl.when(j == n_fc - 1)
            def _():
                _store_row_tiled(ys_ref, acc[...] + part)

    @pl.when(jnp.logical_and(i >= na, j == 0))
    def _():
        ys_ref[...] = jnp.zeros(ys_ref.shape, F32)


def _experts(xn_tiled, slot_tok, tile_expert, n_active, lw):
    n_tiles = tile_expert.shape[0]
    d_ff = lw["moe_w_gate"].shape[2]
    n_fc = d_ff // TF_EXPERT

    def fcol(i, j, na):
        return jnp.where(i < na[0], j, n_fc - 1)

    grid_spec = pltpu.PrefetchScalarGridSpec(
        num_scalar_prefetch=2,
        grid=(n_tiles, n_fc),
        in_specs=[
            pl.BlockSpec((None, 1, TM), lambda i, j, te, na: (i, 0, 0), memory_space=pltpu.SMEM),
            pl.BlockSpec((None, 1, TM), lambda i, j, te, na: (jnp.minimum(i + 1, n_tiles - 1), 0, 0),
                         memory_space=pltpu.SMEM),
            pl.BlockSpec(memory_space=pl.ANY),
            pl.BlockSpec((None, D_MODEL, TF_EXPERT), lambda i, j, te, na: (te[i], 0, fcol(i, j, na))),
            pl.BlockSpec((None, D_MODEL, TF_EXPERT), lambda i, j, te, na: (te[i], 0, fcol(i, j, na))),
            pl.BlockSpec((None, TF_EXPERT, D_MODEL), lambda i, j, te, na: (te[i], fcol(i, j, na), 0)),
        ],
        out_specs=pl.BlockSpec((TM * ROW_PIECES, LANES), lambda i, j, te, na: (i, 0)),
        scratch_shapes=[pltpu.VMEM((2, TM * ROW_PIECES, LANES), F32), pltpu.VMEM((TM, D_MODEL), BF16),
                        pltpu.VMEM((TM, D_MODEL), F32), pltpu.SemaphoreType.DMA((2,))],
    )
    return pl.pallas_call(
        functools.partial(_expert_kernel, n_fc),
        grid_spec=grid_spec,
        out_shape=jax.ShapeDtypeStruct((n_tiles * TM * ROW_PIECES, LANES), F32),
        compiler_params=pltpu.CompilerParams(dimension_semantics=("arbitrary", "arbitrary"),
                                             vmem_limit_bytes=VMEM_LIMIT),
        name="moe_experts",
    )(tile_expert, n_active, slot_tok, slot_tok, xn_tiled, lw["moe_w_gate"], lw["moe_w_up"], lw["moe_w_down"])


def _combine_kernel(final, n_steps, posc_ref, posn_ref, x_ref, p_ref, wgt_ref, ys_hbm, gple_ref, wpg_ref,
                    wpp_ref, gfin_ref, o_ref, buf, sems):
    i = pl.program_id(0)
    slot = i % 2

    def start(pos_ref, s):
        for k in range(TOP_K):
            _start_row_gather(pos_ref, k, ys_hbm, buf.at[s, k], sems.at[s, k])

    @pl.when(i == 0)
    def _():
        start(posc_ref, 0)

    @pl.when(i + 1 < n_steps)
    def _():
        start(posn_ref, 1 - slot)

    for k in range(TOP_K):
        _wait_row_gather(ys_hbm, buf.at[slot, k], sems.at[slot, k])
    wgt = wgt_ref[...]
    y0 = _load_row_tiled(buf.at[slot, 0], TM)
    y1 = _load_row_tiled(buf.at[slot, 1], TM)
    x = x_ref[...] + (y0 * wgt[:, 0:1] + y1 * wgt[:, 1:2])
    x = _ple_tail(x, p_ref[...], gple_ref, wpg_ref, wpp_ref)
    o_ref[...] = _rms(x, gfin_ref[...]) if final else x


def _combine(xbuf, pbuf, pos, wgt, ys, lw, final_g, row0, n_rows):
    blk0 = row0 // TM
    n_steps = n_rows // TM
    rows = lambda i: (blk0 + i, 0)
    final = final_g is not None
    gfin = final_g if final else lw["norm_ple_g"]
    return pl.pallas_call(
        functools.partial(_combine_kernel, final, n_steps),
        grid=(n_steps,),
        in_specs=[pl.BlockSpec((None, TOP_K, TM), lambda i: (blk0 + i, 0, 0), memory_space=pltpu.SMEM),
                  pl.BlockSpec((None, TOP_K, TM), lambda i: (blk0 + jnp.minimum(i + 1, n_steps - 1), 0, 0),
                               memory_space=pltpu.SMEM),
                  pl.BlockSpec((TM, D_MODEL), rows), pl.BlockSpec((TM, D_PLE), rows),
                  pl.BlockSpec((TM, LANES), rows), pl.BlockSpec(memory_space=pl.ANY),
                  _const_spec((1, D_MODEL)), _const_spec((D_MODEL, D_MODEL)),
                  _const_spec((D_PLE, D_MODEL)), _const_spec((1, D_MODEL))],
        out_specs=pl.BlockSpec((TM, D_MODEL), lambda i: (i, 0)),
        out_shape=jax.ShapeDtypeStruct((n_rows, D_MODEL), F32),
        scratch_shapes=[pltpu.VMEM((2, TOP_K, TM * ROW_PIECES, LANES), F32),
                        pltpu.SemaphoreType.DMA((2, TOP_K))],
        compiler_params=pltpu.CompilerParams(dimension_semantics=("arbitrary",),
                                             vmem_limit_bytes=VMEM_LIMIT),
        name="moe_combine_ple",
    )(pos, pos, xbuf, pbuf, wgt, ys, lw["norm_ple_g"], lw["ple_w_gate"], lw["ple_w_proj"], gfin)


def _routing_tables(idx, n_tok):
    n_pairs = n_tok * TOP_K
    n_tiles = n_pairs // TM + N_EXPERTS
    experts = jnp.arange(N_EXPERTS, dtype=jnp.int32)
    e_flat = idx[:, :TOP_K].reshape(-1)
    onehot = (e_flat[:, None] == experts[None, :]).astype(jnp.int32)
    csum = jnp.cumsum(onehot, axis=0)
    sizes = csum[-1]
    padded = ((sizes + TM - 1) // TM) * TM
    pend = jnp.cumsum(padded)
    pstart = pend - padded
    pos = jnp.sum(onehot * (csum - 1 + pstart[None, :]), axis=1)
    fill_e = jnp.repeat(experts, TM)
    fill_r = jnp.tile(jnp.arange(TM, dtype=jnp.int32), N_EXPERTS)
    fill_need = jnp.sum((fill_e[:, None] == experts[None, :]) * (padded - sizes)[None, :], axis=1)
    keys = jnp.concatenate([e_flat, jnp.where(fill_r < fill_need, fill_e, N_EXPERTS)])
    toks = jnp.concatenate([jnp.arange(n_pairs, dtype=jnp.int32) // TOP_K,
                            jnp.zeros((N_EXPERTS * TM,), jnp.int32)])
    _, slot_tok = lax.sort((keys, toks), num_keys=1, is_stable=True)
    n_active = (pend[-1] // TM).astype(jnp.int32)
    tile_row = jnp.minimum(jnp.arange(n_tiles, dtype=jnp.int32), n_active - 1)
    tile_expert = jnp.minimum(
        jnp.sum((tile_row[:, None] * TM >= pend[None, :]).astype(jnp.int32), axis=1), N_EXPERTS - 1)
    pos_tiles = pos.reshape(n_tok // TM, TM, TOP_K).transpose(0, 2, 1)
    return slot_tok.reshape(n_tiles, 1, TM), tile_expert, n_active.reshape(1), pos_tiles


def _sgu_tables(sgu_w, sgu_b, seq):
    r = np.arange(CHUNK)
    bias = jnp.repeat(sgu_b.T, GROUP_DIM, axis=1)
    if seq >= CHUNK:
        w = sgu_w
        mask = r[None, :] <= r[:, None]
    else:
        reps = CHUNK // seq
        w = jnp.tile(sgu_w[:, :seq, :seq], (1, reps, reps))
        mask = (r[:, None] // seq == r[None, :] // seq) & (r[None, :] % seq <= r[:, None] % seq)
        bias = jnp.tile(bias[:seq], (reps, 1))
    mask = np.tile(mask.astype(np.float32), (N_GROUPS, 1))
    return w.reshape(N_GROUPS * CHUNK, CHUNK), jnp.asarray(mask), bias


def _block_diag(w):
    out = jnp.zeros((D_BRANCH, D_BRANCH), w.dtype)
    for g in range(N_GROUPS):
        out = out.at[g * GROUP_DIM:(g + 1) * GROUP_DIM, g * GROUP_DIM:(g + 1) * GROUP_DIM].set(w[g])
    return out


def kernel(x_prompt, x_sample, state_conv_a, state_conv_b, state_pool, p_prompt, p_sample, norm_mix_g, w_in, conv_a_w, conv_b_w, conv_b_bias, conv_b_ln_g, conv_b_ln_b, pool_w, pool_scale, sgu_ln_g, sgu_ln_b, sgu_w, sgu_b, w_branch, w_out, norm_ffn_g, ffn_w_gate, ffn_w_up, ffn_w_down, router_w, moe_w_gate, moe_w_up, moe_w_down, norm_ple_g, ple_w_gate, ple_w_proj, final_norm_g):
    batch, seq, _ = x_prompt.shape
    nseq, dseq, _ = x_sample.shape
    depth = w_in.shape[0]
    assert depth % 2 == 0
    assert PAST_LEN + 1 >= POOL_MAX
    assert seq % TL == 0 and TL % CHUNK == 0 and TL >= CONV_B_W + 1
    n_prompt, n_sample = batch * seq, nseq * dseq
    n_tok = n_prompt + n_sample
    assert n_prompt % TM == 0 and n_sample % TM == 0

    xbuf = jnp.concatenate([x_prompt.reshape(n_prompt, D_MODEL), x_sample.reshape(n_sample, D_MODEL)], axis=0)
    pbuf = jnp.concatenate([p_prompt.reshape(depth, n_prompt, D_PLE), p_sample.reshape(depth, n_sample, D_PLE)],
                           axis=1)
    row = lambda a: a.reshape(1, -1)
    zeros = jnp.zeros((D_BRANCH,), F32)

    states_p, states_s = [], []
    y_prompt = y_sample = None
    for i in range(depth):
        sw_p, sm_p, sb_p = _sgu_tables(sgu_w[i], sgu_b[i], seq)
        sw_s, sm_s, sb_s = _sgu_tables(sgu_w[i], sgu_b[i], dseq)
        lw = {
            "norm_mix_g": row(norm_mix_g[i]), "w_in": w_in[i].astype(BF16),
            "conv_a_w": conv_a_w[i], "conv_b_w": conv_b_w[i],
            "vec": jnp.stack([conv_b_bias[i], conv_b_ln_g[i], conv_b_ln_b[i], pool_scale[i],
                              sgu_ln_g[i], sgu_ln_b[i], zeros, zeros]),
            "pool_bd": _block_diag(pool_w[i]).astype(BF16),
            "sgu_w_p": sw_p, "sgu_mask_p": sm_p, "sgu_bias_p": sb_p,
            "sgu_w_s": sw_s, "sgu_mask_s": sm_s, "sgu_bias_s": sb_s,
            "w_branch": w_branch[i].astype(BF16), "w_out": w_out[i].astype(BF16),
            "norm_ffn_g": row(norm_ffn_g[i]), "norm_ple_g": row(norm_ple_g[i]),
            "ple_w_gate": ple_w_gate[i].astype(BF16), "ple_w_proj": ple_w_proj[i].astype(BF16),
        }
        xbuf, a_p, b_p, pool_p, v_p = _mixer_prompt(xbuf, lw, batch, seq)
        xbuf, a_s, b_s, pool_s, v_s = _mixer_sample(xbuf, lw, state_conv_a[i], state_conv_b[i], state_pool[i],
                                                    n_prompt, nseq, dseq)
        states_p.append((a_p, b_p, pool_p, v_p))
        states_s.append((a_s, b_s, pool_s, v_s))

        j = i // 2
        last = i == depth - 1
        if i % 2 == 0:
            lw.update({"ffn_w_gate": ffn_w_gate[j].astype(BF16), "ffn_w_up": ffn_w_up[j].astype(BF16),
                       "ffn_w_down": ffn_w_down[j].astype(BF16)})
            xbuf = _dense_ffn(xbuf, pbuf[i], lw)
        else:
            lw.update({"router_w": jnp.pad(router_w[j], ((0, 0), (0, LANES - N_EXPERTS))),
                       "moe_w_gate": moe_w_gate[j].astype(BF16), "moe_w_up": moe_w_up[j].astype(BF16),
                       "moe_w_down": moe_w_down[j].astype(BF16)})
            idx, wgt, xn_tiled = _router(xbuf, lw)
            slot_tok, tile_expert, n_active, pos = _routing_tables(idx, n_tok)
            ys = _experts(xn_tiled, slot_tok, tile_expert, n_active, lw)
            fin = row(final_norm_g) if last else None
            out_p = _combine(xbuf, pbuf[i], pos, wgt, ys, lw, fin, 0, n_prompt)
            out_s = _combine(xbuf, pbuf[i], pos, wgt, ys, lw, fin, n_prompt, n_sample)
            if last:
                y_prompt, y_sample = out_p, out_s
            else:
                xbuf = jnp.concatenate([out_p, out_s], axis=0)

    stack = lambda k, sts: jnp.stack([s[k] for s in sts])
    return (y_prompt.reshape(batch, seq, D_MODEL), y_sample.reshape(nseq, dseq, D_MODEL),
            stack(0, states_p), stack(1, states_p), stack(2, states_p), stack(3, states_p),
            stack(0, states_s), stack(1, states_s), stack(2, states_s), stack(3, states_s))
```

```python
import functools

import jax
import jax.numpy as jnp
import numpy as np
from jax import lax
from jax.experimental import pallas as pl
from jax.experimental.pallas import tpu as pltpu

F32 = jnp.float32
BF16 = jnp.bfloat16

D_MODEL = 1024
N_BRANCH = 4
D_BRANCH = 256
N_GROUPS = 4
GROUP_DIM = 64
CONV_A_W = 3
CONV_B_W = 31
POOL_WINDOWS = (2, 4, 8, 16)
POOL_MAX = 16
CHUNK = 128
PAST_LEN = 16384
D_PLE = 256
N_EXPERTS = 8
TOP_K = 2
EPS = 1e-6

OFF_A_H = 0
OFF_A_B = 256
OFF_A_C = 512
OFF_CF_A = 768
OFF_CF_B = 1024
OFF_POOL = 1280
OFF_U = 1536
OFF_V = 1792
OFF_GATE = 2048

LANES = 128
SUBLANES = 8
TM = 512
TL = 512
NB = 64
TF_EXPERT = 1792
ROW_CHUNK = 32
SEQ_CHUNK = 8
VMEM_LIMIT = 56 * 1024 * 1024

V_CB_BIAS, V_CB_LN_G, V_CB_LN_B, V_POOL_SCALE, V_SGU_LN_G, V_SGU_LN_B = range(6)


def _rms(x, g):
    return x * lax.rsqrt(jnp.mean(x * x, axis=-1, keepdims=True) + EPS) * g


def _ln(x, g, b):
    xc = x - jnp.mean(x, axis=-1, keepdims=True)
    var = jnp.mean(xc * xc, axis=-1, keepdims=True)
    return xc * lax.rsqrt(var + EPS) * g + b


def _bdot(a, w):
    return jnp.dot(a.astype(BF16), w, preferred_element_type=F32)


def _silu(x):
    return x * jax.nn.sigmoid(x)


def _const_spec(shape):
    nd = len(shape)
    return pl.BlockSpec(shape, lambda *_: (0,) * nd, pipeline_mode=pl.Buffered(1))


def _sgu(u, v, swm, sbias):
    m = v.shape[0]
    group = lax.broadcasted_iota(jnp.int32, (CHUNK, D_BRANCH), 1) // GROUP_DIM
    outs = []
    for c in range(m // CHUNK):
        vc = v[c * CHUNK:(c + 1) * CHUNK].astype(BF16)
        r = jnp.dot(swm, vc, preferred_element_type=F32)
        s = sbias
        for g in range(N_GROUPS):
            s = s + jnp.where(group == g, r[g * CHUNK:(g + 1) * CHUNK], 0.0)
        outs.append(s)
    return u * jnp.concatenate(outs, axis=0)


GATE_PIECES_PER_BRANCH = D_MODEL // D_BRANCH
N_GATE_PIECES = N_BRANCH * GATE_PIECES_PER_BRANCH


def _gate_emitter(xnb, win_ref):
    pieces = []

    def emit(n=1):
        for _ in range(n):
            if len(pieces) < N_GATE_PIECES:
                lo = OFF_GATE + len(pieces) * D_BRANCH
                z = jnp.dot(xnb, win_ref[:, lo:lo + D_BRANCH], preferred_element_type=F32)
                pieces.append(jax.nn.sigmoid(z))
    return pieces, emit


def _gated_merge(x, branches, gates, wbr_ref, wout_ref):
    bb = [b.astype(BF16) for b in branches]
    cols = []
    for q in range(GATE_PIECES_PER_BRANCH):
        cs = slice(q * D_BRANCH, (q + 1) * D_BRANCH)
        m = None
        for i in range(N_BRANCH):
            t = gates[i * GATE_PIECES_PER_BRANCH + q] * \
                jnp.dot(bb[i], wbr_ref[i, :, cs], preferred_element_type=F32)
            m = t if m is None else m + t
        cols.append(m.astype(BF16))
    return x + jnp.dot(jnp.concatenate(cols, axis=1), wout_ref[...], preferred_element_type=F32)


def _pool_lane_windows(half):
    small, big = POOL_WINDOWS[2 * half], POOL_WINDOWS[2 * half + 1]
    lane = lax.broadcasted_iota(jnp.int32, (1, LANES), 1)
    is_big = lane >= GROUP_DIM
    return small, big, is_big


def _mixer_prompt_kernel(x_ref, g_ref, win_ref, caw_ref, cbw_ref, vec_ref, pw_ref, sw_ref, smask_ref,
                         sbias_ref, wbr_ref, wout_ref,
                         xo_ref, sta_ref, stb_ref, stp_ref, stv_ref,
                         sa, sb, sp):
    i = pl.program_id(1)
    ha, hb, hp = SUBLANES, 4 * SUBLANES, 2 * SUBLANES

    @pl.when(i == 0)
    def _():
        sa[0:ha, :] = jnp.zeros((ha, D_BRANCH), F32)
        sb[0:hb, :] = jnp.zeros((hb, D_BRANCH), F32)
        sp[0:hp, :] = jnp.zeros((hp, D_BRANCH), F32)

    @pl.when(i > 0)
    def _():
        sa[0:ha, :] = sa[TL:TL + ha, :]
        sb[0:hb, :] = sb[TL:TL + hb, :]
        sp[0:hp, :] = sp[TL:TL + hp, :]

    x = x_ref[...]
    xnb = _rms(x, g_ref[...]).astype(BF16)
    vec = vec_ref[...]

    def zc(off):
        return jnp.dot(xnb, win_ref[:, off:off + D_BRANCH], preferred_element_type=F32)

    gates, emit_gate = _gate_emitter(xnb, win_ref)
    n_chunks = TL // ROW_CHUNK

    def tap_conv(s_ref, base, w_ref, ntaps, gate_chunks=()):
        outs = []
        for c in range(n_chunks):
            acc = None
            for k in range(ntaps):
                term = s_ref[pl.ds(base + k + c * ROW_CHUNK, ROW_CHUNK), :] * w_ref[k:k + 1, :]
                acc = term if acc is None else acc + term
            outs.append(acc)
            if c in gate_chunks:
                emit_gate()
        return jnp.concatenate(outs, axis=0)

    sa[ha:ha + TL, :] = zc(OFF_A_C) * zc(OFF_A_H)
    br_a = zc(OFF_A_B) * tap_conv(sa, ha - (CONV_A_W - 1), caw_ref, CONV_A_W)
    sta_ref[0] = sa[TL + ha - (CONV_A_W - 1):TL + ha, :]

    sb[hb:hb + TL, :] = zc(OFF_CF_A) * jax.nn.sigmoid(zc(OFF_CF_B))
    yb = tap_conv(sb, hb - (CONV_B_W - 1), cbw_ref, CONV_B_W, gate_chunks=range(3 * n_chunks // 4))
    yb = yb + vec[V_CB_BIAS:V_CB_BIAS + 1]
    br_b = _silu(_ln(yb, vec[V_CB_LN_G:V_CB_LN_G + 1], vec[V_CB_LN_B:V_CB_LN_B + 1]))
    stb_ref[0] = sb[TL + hb - (CONV_B_W - 1):TL + hb, :]

    sp[hp:hp + TL, :] = zc(OFF_POOL)
    halves = []
    for half in range(2):
        small, big, is_big = _pool_lane_windows(half)
        lanes = slice(half * LANES, (half + 1) * LANES)
        wl = jnp.where(is_big, float(big), float(small))
        outs = []
        for c in range(n_chunks):
            r0 = hp + c * ROW_CHUNK
            acc = None
            for j in range(big):
                seg = sp[pl.ds(r0 - j, ROW_CHUNK), lanes]
                if j >= small:
                    seg = jnp.where(is_big, seg, 0.0)
                acc = seg if acc is None else acc + seg
            pos = i * TL + c * ROW_CHUNK + lax.broadcasted_iota(jnp.int32, (ROW_CHUNK, LANES), 0)
            cnt = jnp.minimum(wl, (pos + 1).astype(F32))
            outs.append(acc / cnt - sp[pl.ds(r0, ROW_CHUNK), lanes])
            if half == 1 and c % 4 == 0:
                emit_gate()
        halves.append(jnp.concatenate(outs, axis=0))
    pooled = jnp.concatenate(halves, axis=1)
    br_c = _bdot(pooled, pw_ref[...]) * vec[V_POOL_SCALE:V_POOL_SCALE + 1]
    stp_ref[0] = sp[TL + hp - (POOL_MAX - 1):TL + hp, :]

    v = _ln(zc(OFF_V), vec[V_SGU_LN_G:V_SGU_LN_G + 1], vec[V_SGU_LN_B:V_SGU_LN_B + 1])
    stv_ref[0] = v[TL - CHUNK:TL]
    swm = jnp.where(smask_ref[...] > 0, sw_ref[...], 0.0).astype(BF16)
    br_d = _sgu(zc(OFF_U), v, swm, sbias_ref[...])

    emit_gate(N_GATE_PIECES)
    xo_ref[...] = _gated_merge(x, (br_a, br_b, br_c, br_d), gates, wbr_ref, wout_ref)


def _mixer_sample_kernel(x_ref, g_ref, win_ref, caw_ref, cbw_ref, vec_ref, pw_ref, sw_ref, smask_ref,
                         sbias_ref, wbr_ref, wout_ref, ina_ref, inb_ref, inp_ref,
                         xo_ref, sta_ref, stb_ref, stp_ref, stv_ref,
                         sa, sb, sp):
    seq = SUBLANES
    ha, hb, hp = SUBLANES, 4 * SUBLANES, 2 * SUBLANES
    sa[:, ha - (CONV_A_W - 1):ha, :] = ina_ref[...]
    sb[:, hb - (CONV_B_W - 1):hb, :] = inb_ref[...]
    sp[:, hp - (POOL_MAX - 1):hp, :] = inp_ref[...]

    x = x_ref[...]
    xnb = _rms(x, g_ref[...]).astype(BF16)
    vec = vec_ref[...]

    def zc(off):
        return jnp.dot(xnb, win_ref[:, off:off + D_BRANCH], preferred_element_type=F32)

    def to3(a):
        return a.reshape(NB, seq, D_BRANCH)

    def to2(a):
        return a.reshape(NB * seq, D_BRANCH)

    def tap_conv(s_ref, base, w_ref, ntaps):
        outs = []
        for c in range(NB // SEQ_CHUNK):
            acc = None
            for k in range(ntaps):
                seg = s_ref[c * SEQ_CHUNK:(c + 1) * SEQ_CHUNK, pl.ds(base + k, seq), :]
                term = seg * w_ref[k:k + 1, :][None]
                acc = term if acc is None else acc + term
            outs.append(acc)
        return to2(jnp.concatenate(outs, axis=0))

    sa[:, ha:ha + seq, :] = to3(zc(OFF_A_C) * zc(OFF_A_H))
    br_a = zc(OFF_A_B) * tap_conv(sa, ha - (CONV_A_W - 1), caw_ref, CONV_A_W)
    sta_ref[...] = sa[:, ha + seq - (CONV_A_W - 1):ha + seq, :]

    sb[:, hb:hb + seq, :] = to3(zc(OFF_CF_A) * jax.nn.sigmoid(zc(OFF_CF_B)))
    yb = tap_conv(sb, hb - (CONV_B_W - 1), cbw_ref, CONV_B_W) + vec[V_CB_BIAS:V_CB_BIAS + 1]
    br_b = _silu(_ln(yb, vec[V_CB_LN_G:V_CB_LN_G + 1], vec[V_CB_LN_B:V_CB_LN_B + 1]))
    stb_ref[...] = sb[:, hb + seq - (CONV_B_W - 1):hb + seq, :]

    sp[:, hp:hp + seq, :] = to3(zc(OFF_POOL))
    halves = []
    for half in range(2):
        small, big, is_big = _pool_lane_windows(half)
        lanes = slice(half * LANES, (half + 1) * LANES)
        wl = jnp.where(is_big, float(big), float(small))[None]
        outs = []
        for c in range(NB // SEQ_CHUNK):
            rows = slice(c * SEQ_CHUNK, (c + 1) * SEQ_CHUNK)
            acc = None
            for j in range(big):
                seg = sp[rows, pl.ds(hp - j, seq), lanes]
                if j >= small:
                    seg = jnp.where(is_big[None], seg, 0.0)
                acc = seg if acc is None else acc + seg
            outs.append(acc / wl - sp[rows, pl.ds(hp, seq), lanes])
        halves.append(jnp.concatenate(outs, axis=0).reshape(NB * seq, LANES))
    pooled = jnp.concatenate(halves, axis=1)
    br_c = _bdot(pooled, pw_ref[...]) * vec[V_POOL_SCALE:V_POOL_SCALE + 1]
    stp_ref[...] = sp[:, hp + seq - (POOL_MAX - 1):hp + seq, :]

    v = _ln(zc(OFF_V), vec[V_SGU_LN_G:V_SGU_LN_G + 1], vec[V_SGU_LN_B:V_SGU_LN_B + 1])
    stv_ref[...] = to3(v)
    swm = jnp.where(smask_ref[...] > 0, sw_ref[...], 0.0).astype(BF16)
    br_d = _sgu(zc(OFF_U), v, swm, sbias_ref[...])

    gates, emit_gate = _gate_emitter(xnb, win_ref)
    emit_gate(N_GATE_PIECES)
    xo_ref[...] = _gated_merge(x, (br_a, br_b, br_c, br_d), gates, wbr_ref, wout_ref)


def _mixer_weight_specs():
    return [
        _const_spec((1, D_MODEL)),
        _const_spec((D_MODEL, OFF_GATE + N_BRANCH * D_MODEL)),
        _const_spec((CONV_A_W, D_BRANCH)),
        _const_spec((CONV_B_W, D_BRANCH)),
        _const_spec((SUBLANES, D_BRANCH)),
        _const_spec((D_BRANCH, D_BRANCH)),
        _const_spec((N_GROUPS * CHUNK, CHUNK)),
        _const_spec((N_GROUPS * CHUNK, CHUNK)),
        _const_spec((CHUNK, D_BRANCH)),
        _const_spec((N_BRANCH, D_BRANCH, D_MODEL)),
        _const_spec((D_MODEL, D_MODEL)),
    ]


def _mixer_prompt(xbuf, lw, batch, seq, in_place):
    nt = seq // TL
    rows = lambda b, i: (b * nt + i, 0)
    st = lambda n: pl.BlockSpec((1, n, D_BRANCH), lambda b, i: (b, 0, 0))
    return pl.pallas_call(
        _mixer_prompt_kernel,
        grid=(batch, nt),
        in_specs=[pl.BlockSpec((TL, D_MODEL), rows)] + _mixer_weight_specs(),
        out_specs=[pl.BlockSpec((TL, D_MODEL), rows), st(CONV_A_W - 1), st(CONV_B_W - 1),
                   st(POOL_MAX - 1), st(CHUNK)],
        out_shape=[jax.ShapeDtypeStruct(xbuf.shape, F32),
                   jax.ShapeDtypeStruct((batch, CONV_A_W - 1, D_BRANCH), F32),
                   jax.ShapeDtypeStruct((batch, CONV_B_W - 1, D_BRANCH), F32),
                   jax.ShapeDtypeStruct((batch, POOL_MAX - 1, D_BRANCH), F32),
                   jax.ShapeDtypeStruct((batch, CHUNK, D_BRANCH), F32)],
        scratch_shapes=[pltpu.VMEM((TL + SUBLANES, D_BRANCH), F32),
                        pltpu.VMEM((TL + 4 * SUBLANES, D_BRANCH), F32),
                        pltpu.VMEM((TL + 2 * SUBLANES, D_BRANCH), F32)],
        input_output_aliases={0: 0} if in_place else {},
        compiler_params=pltpu.CompilerParams(dimension_semantics=("arbitrary", "arbitrary"),
                                             vmem_limit_bytes=VMEM_LIMIT),
        name="mixer_prompt",
    )(xbuf, lw["norm_mix_g"], lw["w_in"], lw["conv_a_w"], lw["conv_b_w"], lw["vec"], lw["pool_bd"],
      lw["sgu_w_p"], lw["sgu_mask_p"], lw["sgu_bias_p"], lw["w_branch"], lw["w_out"])


def _mixer_sample(xbuf, lw, st_a, st_b, st_p, nseq, seq, in_place):
    assert seq == SUBLANES and nseq % NB == 0
    rows = lambda i: (i, 0)
    st = lambda n: pl.BlockSpec((NB, n, D_BRANCH), lambda i: (i, 0, 0))
    return pl.pallas_call(
        _mixer_sample_kernel,
        grid=(nseq // NB,),
        in_specs=[pl.BlockSpec((NB * seq, D_MODEL), rows)] + _mixer_weight_specs()
                 + [st(CONV_A_W - 1), st(CONV_B_W - 1), st(POOL_MAX - 1)],
        out_specs=[pl.BlockSpec((NB * seq, D_MODEL), rows), st(CONV_A_W - 1), st(CONV_B_W - 1),
                   st(POOL_MAX - 1), st(seq)],
        out_shape=[jax.ShapeDtypeStruct(xbuf.shape, F32),
                   jax.ShapeDtypeStruct((nseq, CONV_A_W - 1, D_BRANCH), F32),
                   jax.ShapeDtypeStruct((nseq, CONV_B_W - 1, D_BRANCH), F32),
                   jax.ShapeDtypeStruct((nseq, POOL_MAX - 1, D_BRANCH), F32),
                   jax.ShapeDtypeStruct((nseq, seq, D_BRANCH), F32)],
        scratch_shapes=[pltpu.VMEM((NB, 2 * SUBLANES, D_BRANCH), F32),
                        pltpu.VMEM((NB, 5 * SUBLANES, D_BRANCH), F32),
                        pltpu.VMEM((NB, 3 * SUBLANES, D_BRANCH), F32)],
        input_output_aliases={0: 0} if in_place else {},
        compiler_params=pltpu.CompilerParams(dimension_semantics=("arbitrary",),
                                             vmem_limit_bytes=VMEM_LIMIT),
        name="mixer_sample",
    )(xbuf, lw["norm_mix_g"], lw["w_in"], lw["conv_a_w"], lw["conv_b_w"], lw["vec"], lw["pool_bd"],
      lw["sgu_w_s"], lw["sgu_mask_s"], lw["sgu_bias_s"], lw["w_branch"], lw["w_out"],
      st_a, st_b, st_p)


def _ple_tail(x, p, gple_ref, wpg_ref, wpp_ref):
    gate = jax.nn.sigmoid(_bdot(_rms(x, gple_ref[...]), wpg_ref[...]))
    return x + gate * _bdot(p, wpp_ref[...])


def _dense_ffn_kernel(n_chunks, x_ref, p_ref, gffn_ref, wg_ref, wu_ref, wd_ref, gple_ref, wpg_ref, wpp_ref,
                      xo_ref):
    x = x_ref[...]
    xnb = _rms(x, gffn_ref[...]).astype(BF16)
    tf = wg_ref.shape[1] // n_chunks
    f = None
    for c in range(n_chunks):
        cols = slice(c * tf, (c + 1) * tf)
        h = _silu(jnp.dot(xnb, wg_ref[:, cols], preferred_element_type=F32)) * \
            jnp.dot(xnb, wu_ref[:, cols], preferred_element_type=F32)
        t = _bdot(h, wd_ref[cols, :])
        f = t if f is None else f + t
    xo_ref[...] = _ple_tail(x + f, p_ref[...], gple_ref, wpg_ref, wpp_ref)


def _dense_ffn(xbuf, p_all, layer, lw):
    t = xbuf.shape[0]
    d_ff = lw["ffn_w_gate"].shape[1]
    rows = lambda i: (i, 0)
    return pl.pallas_call(
        functools.partial(_dense_ffn_kernel, 2),
        grid=(t // TM,),
        in_specs=[pl.BlockSpec((TM, D_MODEL), rows), pl.BlockSpec((None, TM, D_PLE), lambda i: (layer, i, 0)),
                  _const_spec((1, D_MODEL)), _const_spec((D_MODEL, d_ff)), _const_spec((D_MODEL, d_ff)),
                  _const_spec((d_ff, D_MODEL)), _const_spec((1, D_MODEL)),
                  _const_spec((D_MODEL, D_MODEL)), _const_spec((D_PLE, D_MODEL))],
        out_specs=pl.BlockSpec((TM, D_MODEL), rows),
        out_shape=jax.ShapeDtypeStruct(xbuf.shape, F32),
        input_output_aliases={0: 0},
        compiler_params=pltpu.CompilerParams(dimension_semantics=("arbitrary",),
                                             vmem_limit_bytes=VMEM_LIMIT),
        name="dense_ffn_ple",
    )(xbuf, p_all, lw["norm_ffn_g"], lw["ffn_w_gate"], lw["ffn_w_up"], lw["ffn_w_down"],
      lw["norm_ple_g"], lw["ple_w_gate"], lw["ple_w_proj"])


def _router_kernel(n_first, xa_ref, xb_ref, g_ref, wr_ref, idx_ref, wgt_ref, xn_ref):
    x = jnp.where(pl.program_id(0) < n_first, xa_ref[...], xb_ref[...])
    xn = _rms(x, g_ref[...])
    _store_row_tiled(xn_ref, xn)
    logits = jnp.dot(xn, wr_ref[...], preferred_element_type=F32, precision=lax.Precision.HIGHEST)
    lane = lax.broadcasted_iota(jnp.int32, logits.shape, 1)
    lane_f = lane.astype(F32)
    logits = jnp.where(lane < N_EXPERTS, logits, -jnp.inf)
    m1 = jnp.max(logits, axis=-1, keepdims=True)
    i1 = jnp.min(jnp.where(logits == m1, lane_f, float(LANES)), axis=-1, keepdims=True)
    rest = jnp.where(lane_f == i1, -jnp.inf, logits)
    m2 = jnp.max(rest, axis=-1, keepdims=True)
    i2 = jnp.min(jnp.where(rest == m2, lane_f, float(LANES)), axis=-1, keepdims=True)
    e2 = jnp.exp(m2 - m1)
    den = 1.0 + e2
    idx_ref[...] = jnp.where(lane == 0, i1, jnp.where(lane == 1, i2, 0.0)).astype(jnp.int32)
    wgt_ref[...] = jnp.where(lane == 0, 1.0 / den, jnp.where(lane == 1, e2 / den, 0.0))


def _router(xa, xb, lw):
    na, nb = xa.shape[0] // TM, xb.shape[0] // TM
    t = xa.shape[0] + xb.shape[0]
    rows = lambda i: (i, 0)
    return pl.pallas_call(
        functools.partial(_router_kernel, na),
        grid=(na + nb,),
        in_specs=[pl.BlockSpec((TM, D_MODEL), lambda i: (jnp.minimum(i, na - 1), 0)),
                  pl.BlockSpec((TM, D_MODEL), lambda i: (jnp.maximum(i - na, 0), 0)),
                  _const_spec((1, D_MODEL)), _const_spec((D_MODEL, LANES))],
        out_specs=[pl.BlockSpec((TM, LANES), rows), pl.BlockSpec((TM, LANES), rows),
                   pl.BlockSpec((TM * ROW_PIECES, LANES), rows)],
        out_shape=[jax.ShapeDtypeStruct((t, LANES), jnp.int32), jax.ShapeDtypeStruct((t, LANES), F32),
                   jax.ShapeDtypeStruct((t * ROW_PIECES, LANES), F32)],
        compiler_params=pltpu.CompilerParams(dimension_semantics=("arbitrary",)),
        name="router",
    )(xa, xb, lw["norm_ffn_g"], lw["router_w"])


ROW_PIECES = D_MODEL // LANES
assert ROW_PIECES == SUBLANES


def _store_row_tiled(ref, x):
    for s in range(ROW_PIECES):
        ref[pl.ds(s, x.shape[0], stride=ROW_PIECES), :] = x[:, s * LANES:(s + 1) * LANES]


def _load_row_tiled(ref, n_rows):
    return jnp.concatenate([ref[pl.ds(s, n_rows, stride=ROW_PIECES), :] for s in range(ROW_PIECES)], axis=1)


def _start_row_gather(idx_ref, k, src_hbm, dst, sem):
    def issue(r, carry):
        src_row = pl.multiple_of(idx_ref[k, r] * ROW_PIECES, ROW_PIECES)
        dst_row = pl.multiple_of(r * ROW_PIECES, ROW_PIECES)
        pltpu.make_async_copy(src_hbm.at[pl.ds(src_row, ROW_PIECES)], dst.at[pl.ds(dst_row, ROW_PIECES)],
                              sem).start()
        return carry
    lax.fori_loop(0, dst.shape[0] // ROW_PIECES, issue, 0, unroll=8)


def _wait_row_gather(src_hbm, dst, sem):
    pltpu.make_async_copy(src_hbm.at[pl.ds(0, dst.shape[0])], dst, sem).wait()


def _expert_kernel(n_fc, te_ref, na_ref, tokc_ref, tokn_ref, xn_hbm, wg_ref, wu_ref, wd_ref, ys_ref,
                   gbuf, xs, acc, sems):
    i, j = pl.program_id(0), pl.program_id(1)
    na = na_ref[0]
    slot = i % 2

    @pl.when(jnp.logical_and(i < na, j == 0))
    def _():
        @pl.when(i == 0)
        def _():
            _start_row_gather(tokc_ref, 0, xn_hbm, gbuf.at[0], sems.at[0])

        @pl.when(i + 1 < na)
        def _():
            _start_row_gather(tokn_ref, 0, xn_hbm, gbuf.at[1 - slot], sems.at[1 - slot])

        _wait_row_gather(xn_hbm, gbuf.at[slot], sems.at[slot])
        xs[...] = _load_row_tiled(gbuf.at[slot], TM).astype(BF16)

    def swiglu_piecewise(sink):
        h = _silu(jnp.dot(xs[...], wg_ref[...], preferred_element_type=F32)) * \
            jnp.dot(xs[...], wu_ref[...], preferred_element_type=F32)
        hb = h.astype(BF16)
        for q in range(D_MODEL // D_BRANCH):
            cs = slice(q * D_BRANCH, (q + 1) * D_BRANCH)
            sink(cs, jnp.dot(hb, wd_ref[:, cs], preferred_element_type=F32))

    def to_acc(first):
        def sink(cs, part):
            acc[:, cs] = part if first else acc[:, cs] + part
        return sink

    def to_out(with_acc):
        def sink(cs, part):
            tot = acc[:, cs] + part if with_acc else part
            for s in range(cs.start // LANES, cs.stop // LANES):
                lo = s * LANES - cs.start
                ys_ref[pl.ds(s, TM, stride=ROW_PIECES), :] = tot[:, lo:lo + LANES]
        return sink

    active = i < na
    if n_fc == 1:
        pl.when(active)(lambda: swiglu_piecewise(to_out(False)))
    else:
        pl.when(jnp.logical_and(active, j == 0))(lambda: swiglu_piecewise(to_acc(True)))
        if n_fc > 2:
            pl.when(jnp.logical_and(active, jnp.logical_and(j > 0, j < n_fc - 1)))(
                lambda: swiglu_piecewise(to_acc(False)))
        pl.when(jnp.logical_and(active, j == n_fc - 1))(lambda: swiglu_piecewise(to_out(True)))

    @pl.when(jnp.logical_and(i >= na, j == 0))
    def _():
        ys_ref[...] = jnp.zeros(ys_ref.shape, F32)


def _experts(xn_tiled, slot_tok, tile_expert, n_active, lw):
    n_tiles = tile_expert.shape[0]
    d_ff = lw["moe_w_gate"].shape[2]
    n_fc = d_ff // TF_EXPERT

    def fcol(i, j, na):
        return jnp.where(i < na[0], j, n_fc - 1)

    grid_spec = pltpu.PrefetchScalarGridSpec(
        num_scalar_prefetch=2,
        grid=(n_tiles, n_fc),
        in_specs=[
            pl.BlockSpec((None, 1, TM), lambda i, j, te, na: (i, 0, 0), memory_space=pltpu.SMEM),
            pl.BlockSpec((None, 1, TM), lambda i, j, te, na: (jnp.minimum(i + 1, n_tiles - 1), 0, 0),
                         memory_space=pltpu.SMEM),
            pl.BlockSpec(memory_space=pl.ANY),
            pl.BlockSpec((None, D_MODEL, TF_EXPERT), lambda i, j, te, na: (te[i], 0, fcol(i, j, na))),
            pl.BlockSpec((None, D_MODEL, TF_EXPERT), lambda i, j, te, na: (te[i], 0, fcol(i, j, na))),
            pl.BlockSpec((None, TF_EXPERT, D_MODEL), lambda i, j, te, na: (te[i], fcol(i, j, na), 0)),
        ],
        out_specs=pl.BlockSpec((TM * ROW_PIECES, LANES), lambda i, j, te, na: (i, 0)),
        scratch_shapes=[pltpu.VMEM((2, TM * ROW_PIECES, LANES), F32), pltpu.VMEM((TM, D_MODEL), BF16),
                        pltpu.VMEM((TM, D_MODEL), F32), pltpu.SemaphoreType.DMA((2,))],
    )
    return pl.pallas_call(
        functools.partial(_expert_kernel, n_fc),
        grid_spec=grid_spec,
        out_shape=jax.ShapeDtypeStruct((n_tiles * TM * ROW_PIECES, LANES), F32),
        compiler_params=pltpu.CompilerParams(dimension_semantics=("arbitrary", "arbitrary"),
                                             vmem_limit_bytes=VMEM_LIMIT),
        name="moe_experts",
    )(tile_expert, n_active, slot_tok, slot_tok, xn_tiled, lw["moe_w_gate"], lw["moe_w_up"], lw["moe_w_down"])


def _combine_kernel(final, n_steps, pos0c_ref, pos1c_ref, pos0n_ref, pos1n_ref, x_ref, p_ref, wgt_ref, ys_hbm,
                    gple_ref, wpg_ref, wpp_ref, gfin_ref, o_ref, buf, sems):
    i = pl.program_id(0)
    slot = i % 2

    def start(pos_refs, s):
        for k in range(TOP_K):
            _start_row_gather(pos_refs[k], 0, ys_hbm, buf.at[s, k], sems.at[s, k])

    @pl.when(i == 0)
    def _():
        start((pos0c_ref, pos1c_ref), 0)

    @pl.when(i + 1 < n_steps)
    def _():
        start((pos0n_ref, pos1n_ref), 1 - slot)

    for k in range(TOP_K):
        _wait_row_gather(ys_hbm, buf.at[slot, k], sems.at[slot, k])
    wgt = wgt_ref[...]
    y0 = _load_row_tiled(buf.at[slot, 0], TM)
    y1 = _load_row_tiled(buf.at[slot, 1], TM)
    x = x_ref[...] + (y0 * wgt[:, 0:1] + y1 * wgt[:, 1:2])
    x = _ple_tail(x, p_ref[...], gple_ref, wpg_ref, wpp_ref)
    o_ref[...] = _rms(x, gfin_ref[...]) if final else x


def _combine(xbuf, p_all, layer, pos, wgt, ys, lw, final_g, row0):
    blk0 = row0 // TM
    n_rows = xbuf.shape[0]
    n_steps = n_rows // TM
    rows = lambda i: (i, 0)
    tok_rows = lambda i: (blk0 + i, 0)
    cur = pl.BlockSpec((None, 1, TM), lambda i: (blk0 + i, 0, 0), memory_space=pltpu.SMEM)
    nxt = pl.BlockSpec((None, 1, TM), lambda i: (blk0 + jnp.minimum(i + 1, n_steps - 1), 0, 0),
                       memory_space=pltpu.SMEM)
    final = final_g is not None
    gfin = final_g if final else lw["norm_ple_g"]
    return pl.pallas_call(
        functools.partial(_combine_kernel, final, n_steps),
        grid=(n_steps,),
        in_specs=[cur, cur, nxt, nxt,
                  pl.BlockSpec((TM, D_MODEL), rows), pl.BlockSpec((None, TM, D_PLE), lambda i: (layer, i, 0)),
                  pl.BlockSpec((TM, LANES), tok_rows), pl.BlockSpec(memory_space=pl.ANY),
                  _const_spec((1, D_MODEL)), _const_spec((D_MODEL, D_MODEL)),
                  _const_spec((D_PLE, D_MODEL)), _const_spec((1, D_MODEL))],
        out_specs=pl.BlockSpec((TM, D_MODEL), lambda i: (i, 0)),
        out_shape=jax.ShapeDtypeStruct((n_rows, D_MODEL), F32),
        scratch_shapes=[pltpu.VMEM((2, TOP_K, TM * ROW_PIECES, LANES), F32),
                        pltpu.SemaphoreType.DMA((2, TOP_K))],
        compiler_params=pltpu.CompilerParams(dimension_semantics=("arbitrary",),
                                             vmem_limit_bytes=VMEM_LIMIT),
        name="moe_combine_ple",
    )(pos[0], pos[1], pos[0], pos[1], xbuf, p_all, wgt, ys, lw["norm_ple_g"], lw["ple_w_gate"],
      lw["ple_w_proj"], gfin)


def _routing_tables(idx, n_tok):
    n_pairs = n_tok * TOP_K
    n_tiles = n_pairs // TM + N_EXPERTS
    experts = jnp.arange(N_EXPERTS, dtype=jnp.int32)
    e_flat = idx[:, :TOP_K].reshape(-1)
    onehot = (e_flat[:, None] == experts[None, :]).astype(jnp.int32)
    csum = jnp.cumsum(onehot, axis=0)
    sizes = csum[-1]
    padded = ((sizes + TM - 1) // TM) * TM
    pend = jnp.cumsum(padded)
    pstart = pend - padded
    pos = jnp.sum(onehot * (csum - 1 + pstart[None, :]), axis=1)
    fill_e = jnp.repeat(experts, TM)
    fill_r = jnp.tile(jnp.arange(TM, dtype=jnp.int32), N_EXPERTS)
    fill_need = jnp.sum((fill_e[:, None] == experts[None, :]) * (padded - sizes)[None, :], axis=1)
    keys = jnp.concatenate([e_flat, jnp.where(fill_r < fill_need, fill_e, N_EXPERTS)])
    toks = jnp.concatenate([jnp.arange(n_pairs, dtype=jnp.int32) // TOP_K,
                            jnp.zeros((N_EXPERTS * TM,), jnp.int32)])
    _, slot_tok = lax.sort((keys, toks), num_keys=1, is_stable=True)
    n_active = (pend[-1] // TM).astype(jnp.int32)
    tile_row = jnp.minimum(jnp.arange(n_tiles, dtype=jnp.int32), n_active - 1)
    tile_expert = jnp.minimum(
        jnp.sum((tile_row[:, None] * TM >= pend[None, :]).astype(jnp.int32), axis=1), N_EXPERTS - 1)
    pos = pos.reshape(n_tok, TOP_K)
    pos_tiles = tuple(pos[:, k].reshape(n_tok // TM, 1, TM) for k in range(TOP_K))
    return slot_tok.reshape(n_tiles, 1, TM), tile_expert, n_active.reshape(1), pos_tiles


def _sgu_tables(sgu_w, sgu_b, seq):
    r = np.arange(CHUNK)
    bias = jnp.repeat(sgu_b.T, GROUP_DIM, axis=1)
    if seq >= CHUNK:
        w = sgu_w
        mask = r[None, :] <= r[:, None]
    else:
        reps = CHUNK // seq
        w = jnp.tile(sgu_w[:, :seq, :seq], (1, reps, reps))
        mask = (r[:, None] // seq == r[None, :] // seq) & (r[None, :] % seq <= r[:, None] % seq)
        bias = jnp.tile(bias[:seq], (reps, 1))
    mask = np.tile(mask.astype(np.float32), (N_GROUPS, 1))
    return w.reshape(N_GROUPS * CHUNK, CHUNK), jnp.asarray(mask), bias


def _block_diag(w):
    out = jnp.zeros((D_BRANCH, D_BRANCH), w.dtype)
    for g in range(N_GROUPS):
        out = out.at[g * GROUP_DIM:(g + 1) * GROUP_DIM, g * GROUP_DIM:(g + 1) * GROUP_DIM].set(w[g])
    return out


def kernel(x_prompt, x_sample, state_conv_a, state_conv_b, state_pool, p_prompt, p_sample, norm_mix_g, w_in, conv_a_w, conv_b_w, conv_b_bias, conv_b_ln_g, conv_b_ln_b, pool_w, pool_scale, sgu_ln_g, sgu_ln_b, sgu_w, sgu_b, w_branch, w_out, norm_ffn_g, ffn_w_gate, ffn_w_up, ffn_w_down, router_w, moe_w_gate, moe_w_up, moe_w_down, norm_ple_g, ple_w_gate, ple_w_proj, final_norm_g):
    batch, seq, _ = x_prompt.shape
    nseq, dseq, _ = x_sample.shape
    depth = w_in.shape[0]
    assert depth % 2 == 0
    assert PAST_LEN + 1 >= POOL_MAX
    assert seq % TL == 0 and TL % CHUNK == 0 and TL >= CONV_B_W + 1
    n_prompt, n_sample = batch * seq, nseq * dseq
    n_tok = n_prompt + n_sample
    assert n_prompt % TM == 0 and n_sample % TM == 0

    xp, xs = x_prompt.reshape(n_prompt, D_MODEL), x_sample.reshape(n_sample, D_MODEL)
    pp, ps = p_prompt.reshape(depth, n_prompt, D_PLE), p_sample.reshape(depth, n_sample, D_PLE)
    row = lambda a: a.reshape(1, -1)
    zeros = jnp.zeros((D_BRANCH,), F32)

    states_p, states_s = [], []
    y_prompt = y_sample = None
    for i in range(depth):
        sw_p, sm_p, sb_p = _sgu_tables(sgu_w[i], sgu_b[i], seq)
        sw_s, sm_s, sb_s = _sgu_tables(sgu_w[i], sgu_b[i], dseq)
        lw = {
            "norm_mix_g": row(norm_mix_g[i]), "w_in": w_in[i].astype(BF16),
            "conv_a_w": conv_a_w[i], "conv_b_w": conv_b_w[i],
            "vec": jnp.stack([conv_b_bias[i], conv_b_ln_g[i], conv_b_ln_b[i], pool_scale[i],
                              sgu_ln_g[i], sgu_ln_b[i], zeros, zeros]),
            "pool_bd": _block_diag(pool_w[i]).astype(BF16),
            "sgu_w_p": sw_p, "sgu_mask_p": sm_p, "sgu_bias_p": sb_p,
            "sgu_w_s": sw_s, "sgu_mask_s": sm_s, "sgu_bias_s": sb_s,
            "w_branch": w_branch[i].astype(BF16), "w_out": w_out[i].astype(BF16),
            "norm_ffn_g": row(norm_ffn_g[i]), "norm_ple_g": row(norm_ple_g[i]),
            "ple_w_gate": ple_w_gate[i].astype(BF16), "ple_w_proj": ple_w_proj[i].astype(BF16),
        }
        in_place = i > 0
        xp, a_p, b_p, pool_p, v_p = _mixer_prompt(xp, lw, batch, seq, in_place)
        xs, a_s, b_s, pool_s, v_s = _mixer_sample(xs, lw, state_conv_a[i], state_conv_b[i], state_pool[i],
                                                  nseq, dseq, in_place)
        states_p.append((a_p, b_p, pool_p, v_p))
        states_s.append((a_s, b_s, pool_s, v_s))

        j = i // 2
        last = i == depth - 1
        if i % 2 == 0:
            lw.update({"ffn_w_gate": ffn_w_gate[j].astype(BF16), "ffn_w_up": ffn_w_up[j].astype(BF16),
                       "ffn_w_down": ffn_w_down[j].astype(BF16)})
            xp = _dense_ffn(xp, pp, i, lw)
            xs = _dense_ffn(xs, ps, i, lw)
        else:
            lw.update({"router_w": jnp.pad(router_w[j], ((0, 0), (0, LANES - N_EXPERTS))),
                       "moe_w_gate": moe_w_gate[j].astype(BF16), "moe_w_up": moe_w_up[j].astype(BF16),
                       "moe_w_down": moe_w_down[j].astype(BF16)})
            idx, wgt, xn_tiled = _router(xp, xs, lw)
            slot_tok, tile_expert, n_active, pos = _routing_tables(idx, n_tok)
            ys = _experts(xn_tiled, slot_tok, tile_expert, n_active, lw)
            fin = row(final_norm_g) if last else None
            xp = _combine(xp, pp, i, pos, wgt, ys, lw, fin, 0)
            xs = _combine(xs, ps, i, pos, wgt, ys, lw, fin, n_prompt)
            if last:
                y_prompt, y_sample = xp, xs

    stack = lambda k, sts: jnp.stack([s[k] for s in sts])
    return (y_prompt.reshape(batch, seq, D_MODEL), y_sample.reshape(nseq, dseq, D_MODEL),
            stack(0, states_p), stack(1, states_p), stack(2, states_p), stack(3, states_p),
            stack(0, states_s), stack(1, states_s), stack(2, states_s), stack(3, states_s))
```

```python
import functools

import jax
import jax.numpy as jnp
import numpy as np
from jax import lax
from jax.experimental import pallas as pl
from jax.experimental.pallas import tpu as pltpu

F32 = jnp.float32
BF16 = jnp.bfloat16

D_MODEL = 1024
N_BRANCH = 4
D_BRANCH = 256
N_GROUPS = 4
GROUP_DIM = 64
CONV_A_W = 3
CONV_B_W = 31
POOL_WINDOWS = (2, 4, 8, 16)
POOL_MAX = 16
CHUNK = 128
PAST_LEN = 16384
D_PLE = 256
N_EXPERTS = 8
TOP_K = 2
EPS = 1e-6

OFF_A_H = 0
OFF_A_B = 256
OFF_A_C = 512
OFF_CF_A = 768
OFF_CF_B = 1024
OFF_POOL = 1280
OFF_U = 1536
OFF_V = 1792
OFF_GATE = 2048

LANES = 128
SUBLANES = 8
TM = 512
TL = 512
NB = 64
TF_EXPERT = 1792
ROW_CHUNK = 32
SEQ_CHUNK = 8
VMEM_LIMIT = 56 * 1024 * 1024

V_CB_BIAS, V_CB_LN_G, V_CB_LN_B, V_POOL_SCALE, V_SGU_LN_G, V_SGU_LN_B = range(6)


def _rms(x, g):
    return x * lax.rsqrt(jnp.mean(x * x, axis=-1, keepdims=True) + EPS) * g


def _ln(x, g, b):
    xc = x - jnp.mean(x, axis=-1, keepdims=True)
    var = jnp.mean(xc * xc, axis=-1, keepdims=True)
    return xc * lax.rsqrt(var + EPS) * g + b


def _bdot(a, w):
    return jnp.dot(a.astype(BF16), w, preferred_element_type=F32)


def _silu(x):
    return x * jax.nn.sigmoid(x)


def _const_spec(shape):
    nd = len(shape)
    return pl.BlockSpec(shape, lambda *_: (0,) * nd, pipeline_mode=pl.Buffered(1))


def _sgu(u, v, swm, sbias):
    m = v.shape[0]
    group = lax.broadcasted_iota(jnp.int32, (CHUNK, D_BRANCH), 1) // GROUP_DIM
    outs = []
    for c in range(m // CHUNK):
        vc = v[c * CHUNK:(c + 1) * CHUNK].astype(BF16)
        r = jnp.dot(swm, vc, preferred_element_type=F32)
        s = sbias
        for g in range(N_GROUPS):
            s = s + jnp.where(group == g, r[g * CHUNK:(g + 1) * CHUNK], 0.0)
        outs.append(s)
    return u * jnp.concatenate(outs, axis=0)


GATE_PIECES_PER_BRANCH = D_MODEL // D_BRANCH
N_GATE_PIECES = N_BRANCH * GATE_PIECES_PER_BRANCH


def _gate_emitter(xnb, win_ref):
    pieces = []

    def emit(n=1):
        for _ in range(n):
            if len(pieces) < N_GATE_PIECES:
                lo = OFF_GATE + len(pieces) * D_BRANCH
                z = jnp.dot(xnb, win_ref[:, lo:lo + D_BRANCH], preferred_element_type=F32)
                pieces.append(jax.nn.sigmoid(z))
    return pieces, emit


def _gated_merge(x, branches, gates, wbr_ref, wout_ref):
    bb = [b.astype(BF16) for b in branches]
    cols = []
    for q in range(GATE_PIECES_PER_BRANCH):
        cs = slice(q * D_BRANCH, (q + 1) * D_BRANCH)
        m = None
        for i in range(N_BRANCH):
            t = gates[i * GATE_PIECES_PER_BRANCH + q] * \
                jnp.dot(bb[i], wbr_ref[i, :, cs], preferred_element_type=F32)
            m = t if m is None else m + t
        cols.append(m.astype(BF16))
    return x + jnp.dot(jnp.concatenate(cols, axis=1), wout_ref[...], preferred_element_type=F32)


def _pool_lane_windows(half):
    small, big = POOL_WINDOWS[2 * half], POOL_WINDOWS[2 * half + 1]
    lane = lax.broadcasted_iota(jnp.int32, (1, LANES), 1)
    is_big = lane >= GROUP_DIM
    return small, big, is_big


def _mixer_prompt_kernel(x_ref, g_ref, win_ref, caw_ref, cbw_ref, vec_ref, pw_ref, sw_ref, smask_ref,
                         sbias_ref, wbr_ref, wout_ref,
                         xo_ref, sta_ref, stb_ref, stp_ref, stv_ref,
                         sa, sb, sp):
    i = pl.program_id(1)
    ha, hb, hp = SUBLANES, 4 * SUBLANES, 2 * SUBLANES

    @pl.when(i == 0)
    def _():
        sa[0:ha, :] = jnp.zeros((ha, D_BRANCH), F32)
        sb[0:hb, :] = jnp.zeros((hb, D_BRANCH), F32)
        sp[0:hp, :] = jnp.zeros((hp, D_BRANCH), F32)

    @pl.when(i > 0)
    def _():
        sa[0:ha, :] = sa[TL:TL + ha, :]
        sb[0:hb, :] = sb[TL:TL + hb, :]
        sp[0:hp, :] = sp[TL:TL + hp, :]

    x = x_ref[...]
    xnb = _rms(x, g_ref[...]).astype(BF16)
    vec = vec_ref[...]

    def zc(off):
        return jnp.dot(xnb, win_ref[:, off:off + D_BRANCH], preferred_element_type=F32)

    gates, emit_gate = _gate_emitter(xnb, win_ref)
    n_chunks = TL // ROW_CHUNK

    def tap_conv(s_ref, base, w_ref, ntaps, gate_chunks=()):
        outs = []
        for c in range(n_chunks):
            acc = None
            for k in range(ntaps):
                term = s_ref[pl.ds(base + k + c * ROW_CHUNK, ROW_CHUNK), :] * w_ref[k:k + 1, :]
                acc = term if acc is None else acc + term
            outs.append(acc)
            if c in gate_chunks:
                emit_gate()
        return jnp.concatenate(outs, axis=0)

    sa[ha:ha + TL, :] = zc(OFF_A_C) * zc(OFF_A_H)
    br_a = zc(OFF_A_B) * tap_conv(sa, ha - (CONV_A_W - 1), caw_ref, CONV_A_W)
    sta_ref[0] = sa[TL + ha - (CONV_A_W - 1):TL + ha, :]

    sb[hb:hb + TL, :] = zc(OFF_CF_A) * jax.nn.sigmoid(zc(OFF_CF_B))
    yb = tap_conv(sb, hb - (CONV_B_W - 1), cbw_ref, CONV_B_W, gate_chunks=range(3 * n_chunks // 4))
    yb = yb + vec[V_CB_BIAS:V_CB_BIAS + 1]
    br_b = _silu(_ln(yb, vec[V_CB_LN_G:V_CB_LN_G + 1], vec[V_CB_LN_B:V_CB_LN_B + 1]))
    stb_ref[0] = sb[TL + hb - (CONV_B_W - 1):TL + hb, :]

    sp[hp:hp + TL, :] = zc(OFF_POOL)
    halves = []
    for half in range(2):
        small, big, is_big = _pool_lane_windows(half)
        lanes = slice(half * LANES, (half + 1) * LANES)
        wl = jnp.where(is_big, float(big), float(small))
        outs = []
        for c in range(n_chunks):
            r0 = hp + c * ROW_CHUNK
            acc = None
            for j in range(big):
                seg = sp[pl.ds(r0 - j, ROW_CHUNK), lanes]
                if j >= small:
                    seg = jnp.where(is_big, seg, 0.0)
                acc = seg if acc is None else acc + seg
            pos = i * TL + c * ROW_CHUNK + lax.broadcasted_iota(jnp.int32, (ROW_CHUNK, LANES), 0)
            cnt = jnp.minimum(wl, (pos + 1).astype(F32))
            outs.append(acc / cnt - sp[pl.ds(r0, ROW_CHUNK), lanes])
            if half == 1 and c % 4 == 0:
                emit_gate()
        halves.append(jnp.concatenate(outs, axis=0))
    pooled = jnp.concatenate(halves, axis=1)
    br_c = _bdot(pooled, pw_ref[...]) * vec[V_POOL_SCALE:V_POOL_SCALE + 1]
    stp_ref[0] = sp[TL + hp - (POOL_MAX - 1):TL + hp, :]

    v = _ln(zc(OFF_V), vec[V_SGU_LN_G:V_SGU_LN_G + 1], vec[V_SGU_LN_B:V_SGU_LN_B + 1])
    stv_ref[0] = v[TL - CHUNK:TL]
    swm = jnp.where(smask_ref[...] > 0, sw_ref[...], 0.0).astype(BF16)
    br_d = _sgu(zc(OFF_U), v, swm, sbias_ref[...])

    emit_gate(N_GATE_PIECES)
    xo_ref[...] = _gated_merge(x, (br_a, br_b, br_c, br_d), gates, wbr_ref, wout_ref)


def _mixer_sample_kernel(x_ref, g_ref, win_ref, caw_ref, cbw_ref, vec_ref, pw_ref, sw_ref, smask_ref,
                         sbias_ref, wbr_ref, wout_ref, ina_ref, inb_ref, inp_ref,
                         xo_ref, sta_ref, stb_ref, stp_ref, stv_ref,
                         sa, sb, sp):
    seq = SUBLANES
    ha, hb, hp = SUBLANES, 4 * SUBLANES, 2 * SUBLANES
    sa[:, ha - (CONV_A_W - 1):ha, :] = ina_ref[...]
    sb[:, hb - (CONV_B_W - 1):hb, :] = inb_ref[...]
    sp[:, hp - (POOL_MAX - 1):hp, :] = inp_ref[...]

    x = x_ref[...]
    xnb = _rms(x, g_ref[...]).astype(BF16)
    vec = vec_ref[...]

    def zc(off):
        return jnp.dot(xnb, win_ref[:, off:off + D_BRANCH], preferred_element_type=F32)

    def to3(a):
        return a.reshape(NB, seq, D_BRANCH)

    def to2(a):
        return a.reshape(NB * seq, D_BRANCH)

    gates, emit_gate = _gate_emitter(xnb, win_ref)
    n_chunks = NB // SEQ_CHUNK

    def tap_conv(s_ref, base, w_ref, ntaps, gates_per_chunk=0):
        outs = []
        for c in range(n_chunks):
            acc = None
            for k in range(ntaps):
                seg = s_ref[c * SEQ_CHUNK:(c + 1) * SEQ_CHUNK, pl.ds(base + k, seq), :]
                term = seg * w_ref[k:k + 1, :][None]
                acc = term if acc is None else acc + term
            outs.append(acc)
            emit_gate(gates_per_chunk)
        return to2(jnp.concatenate(outs, axis=0))

    sa[:, ha:ha + seq, :] = to3(zc(OFF_A_C) * zc(OFF_A_H))
    br_a = zc(OFF_A_B) * tap_conv(sa, ha - (CONV_A_W - 1), caw_ref, CONV_A_W)
    sta_ref[...] = sa[:, ha + seq - (CONV_A_W - 1):ha + seq, :]

    sb[:, hb:hb + seq, :] = to3(zc(OFF_CF_A) * jax.nn.sigmoid(zc(OFF_CF_B)))
    yb = tap_conv(sb, hb - (CONV_B_W - 1), cbw_ref, CONV_B_W, gates_per_chunk=-(-N_GATE_PIECES // n_chunks))
    yb = yb + vec[V_CB_BIAS:V_CB_BIAS + 1]
    br_b = _silu(_ln(yb, vec[V_CB_LN_G:V_CB_LN_G + 1], vec[V_CB_LN_B:V_CB_LN_B + 1]))
    stb_ref[...] = sb[:, hb + seq - (CONV_B_W - 1):hb + seq, :]

    sp[:, hp:hp + seq, :] = to3(zc(OFF_POOL))
    halves = []
    for half in range(2):
        small, big, is_big = _pool_lane_windows(half)
        lanes = slice(half * LANES, (half + 1) * LANES)
        wl = jnp.where(is_big, float(big), float(small))[None]
        outs = []
        for c in range(NB // SEQ_CHUNK):
            rows = slice(c * SEQ_CHUNK, (c + 1) * SEQ_CHUNK)
            acc = None
            for j in range(big):
                seg = sp[rows, pl.ds(hp - j, seq), lanes]
                if j >= small:
                    seg = jnp.where(is_big[None], seg, 0.0)
                acc = seg if acc is None else acc + seg
            outs.append(acc / wl - sp[rows, pl.ds(hp, seq), lanes])
        halves.append(jnp.concatenate(outs, axis=0).reshape(NB * seq, LANES))
    pooled = jnp.concatenate(halves, axis=1)
    br_c = _bdot(pooled, pw_ref[...]) * vec[V_POOL_SCALE:V_POOL_SCALE + 1]
    stp_ref[...] = sp[:, hp + seq - (POOL_MAX - 1):hp + seq, :]

    v = _ln(zc(OFF_V), vec[V_SGU_LN_G:V_SGU_LN_G + 1], vec[V_SGU_LN_B:V_SGU_LN_B + 1])
    stv_ref[...] = to3(v)
    swm = jnp.where(smask_ref[...] > 0, sw_ref[...], 0.0).astype(BF16)
    br_d = _sgu(zc(OFF_U), v, swm, sbias_ref[...])

    emit_gate(N_GATE_PIECES)
    xo_ref[...] = _gated_merge(x, (br_a, br_b, br_c, br_d), gates, wbr_ref, wout_ref)


def _mixer_weight_specs():
    return [
        _const_spec((1, D_MODEL)),
        _const_spec((D_MODEL, OFF_GATE + N_BRANCH * D_MODEL)),
        _const_spec((CONV_A_W, D_BRANCH)),
        _const_spec((CONV_B_W, D_BRANCH)),
        _const_spec((SUBLANES, D_BRANCH)),
        _const_spec((D_BRANCH, D_BRANCH)),
        _const_spec((N_GROUPS * CHUNK, CHUNK)),
        _const_spec((N_GROUPS * CHUNK, CHUNK)),
        _const_spec((CHUNK, D_BRANCH)),
        _const_spec((N_BRANCH, D_BRANCH, D_MODEL)),
        _const_spec((D_MODEL, D_MODEL)),
    ]


def _mixer_prompt(xbuf, lw, batch, seq, in_place):
    nt = seq // TL
    rows = lambda b, i: (b * nt + i, 0)
    st = lambda n: pl.BlockSpec((1, n, D_BRANCH), lambda b, i: (b, 0, 0))
    return pl.pallas_call(
        _mixer_prompt_kernel,
        grid=(batch, nt),
        in_specs=[pl.BlockSpec((TL, D_MODEL), rows)] + _mixer_weight_specs(),
        out_specs=[pl.BlockSpec((TL, D_MODEL), rows), st(CONV_A_W - 1), st(CONV_B_W - 1),
                   st(POOL_MAX - 1), st(CHUNK)],
        out_shape=[jax.ShapeDtypeStruct(xbuf.shape, F32),
                   jax.ShapeDtypeStruct((batch, CONV_A_W - 1, D_BRANCH), F32),
                   jax.ShapeDtypeStruct((batch, CONV_B_W - 1, D_BRANCH), F32),
                   jax.ShapeDtypeStruct((batch, POOL_MAX - 1, D_BRANCH), F32),
                   jax.ShapeDtypeStruct((batch, CHUNK, D_BRANCH), F32)],
        scratch_shapes=[pltpu.VMEM((TL + SUBLANES, D_BRANCH), F32),
                        pltpu.VMEM((TL + 4 * SUBLANES, D_BRANCH), F32),
                        pltpu.VMEM((TL + 2 * SUBLANES, D_BRANCH), F32)],
        input_output_aliases={0: 0} if in_place else {},
        compiler_params=pltpu.CompilerParams(dimension_semantics=("arbitrary", "arbitrary"),
                                             vmem_limit_bytes=VMEM_LIMIT),
        name="mixer_prompt",
    )(xbuf, lw["norm_mix_g"], lw["w_in"], lw["conv_a_w"], lw["conv_b_w"], lw["vec"], lw["pool_bd"],
      lw["sgu_w_p"], lw["sgu_mask_p"], lw["sgu_bias_p"], lw["w_branch"], lw["w_out"])


def _mixer_sample(xbuf, lw, st_a, st_b, st_p, nseq, seq, in_place):
    assert seq == SUBLANES and nseq % NB == 0
    rows = lambda i: (i, 0)
    st = lambda n: pl.BlockSpec((NB, n, D_BRANCH), lambda i: (i, 0, 0))
    return pl.pallas_call(
        _mixer_sample_kernel,
        grid=(nseq // NB,),
        in_specs=[pl.BlockSpec((NB * seq, D_MODEL), rows)] + _mixer_weight_specs()
                 + [st(CONV_A_W - 1), st(CONV_B_W - 1), st(POOL_MAX - 1)],
        out_specs=[pl.BlockSpec((NB * seq, D_MODEL), rows), st(CONV_A_W - 1), st(CONV_B_W - 1),
                   st(POOL_MAX - 1), st(seq)],
        out_shape=[jax.ShapeDtypeStruct(xbuf.shape, F32),
                   jax.ShapeDtypeStruct((nseq, CONV_A_W - 1, D_BRANCH), F32),
                   jax.ShapeDtypeStruct((nseq, CONV_B_W - 1, D_BRANCH), F32),
                   jax.ShapeDtypeStruct((nseq, POOL_MAX - 1, D_BRANCH), F32),
                   jax.ShapeDtypeStruct((nseq, seq, D_BRANCH), F32)],
        scratch_shapes=[pltpu.VMEM((NB, 2 * SUBLANES, D_BRANCH), F32),
                        pltpu.VMEM((NB, 5 * SUBLANES, D_BRANCH), F32),
                        pltpu.VMEM((NB, 3 * SUBLANES, D_BRANCH), F32)],
        input_output_aliases={0: 0} if in_place else {},
        compiler_params=pltpu.CompilerParams(dimension_semantics=("arbitrary",),
                                             vmem_limit_bytes=VMEM_LIMIT),
        name="mixer_sample",
    )(xbuf, lw["norm_mix_g"], lw["w_in"], lw["conv_a_w"], lw["conv_b_w"], lw["vec"], lw["pool_bd"],
      lw["sgu_w_s"], lw["sgu_mask_s"], lw["sgu_bias_s"], lw["w_branch"], lw["w_out"],
      st_a, st_b, st_p)


def _ple_tail(x, p, gple_ref, wpg_ref, wpp_ref):
    gate = jax.nn.sigmoid(_bdot(_rms(x, gple_ref[...]), wpg_ref[...]))
    return x + gate * _bdot(p, wpp_ref[...])


def _dense_ffn_kernel(n_chunks, x_ref, p_ref, gffn_ref, wg_ref, wu_ref, wd_ref, gple_ref, wpg_ref, wpp_ref,
                      xo_ref):
    x = x_ref[...]
    xnb = _rms(x, gffn_ref[...]).astype(BF16)
    tf = wg_ref.shape[1] // n_chunks
    f = None
    for c in range(n_chunks):
        cols = slice(c * tf, (c + 1) * tf)
        h = _silu(jnp.dot(xnb, wg_ref[:, cols], preferred_element_type=F32)) * \
            jnp.dot(xnb, wu_ref[:, cols], preferred_element_type=F32)
        t = _bdot(h, wd_ref[cols, :])
        f = t if f is None else f + t
    xo_ref[...] = _ple_tail(x + f, p_ref[...], gple_ref, wpg_ref, wpp_ref)


def _dense_ffn(xbuf, p_all, layer, lw):
    t = xbuf.shape[0]
    d_ff = lw["ffn_w_gate"].shape[1]
    rows = lambda i: (i, 0)
    return pl.pallas_call(
        functools.partial(_dense_ffn_kernel, 2),
        grid=(t // TM,),
        in_specs=[pl.BlockSpec((TM, D_MODEL), rows), pl.BlockSpec((None, TM, D_PLE), lambda i: (layer, i, 0)),
                  _const_spec((1, D_MODEL)), _const_spec((D_MODEL, d_ff)), _const_spec((D_MODEL, d_ff)),
                  _const_spec((d_ff, D_MODEL)), _const_spec((1, D_MODEL)),
                  _const_spec((D_MODEL, D_MODEL)), _const_spec((D_PLE, D_MODEL))],
        out_specs=pl.BlockSpec((TM, D_MODEL), rows),
        out_shape=jax.ShapeDtypeStruct(xbuf.shape, F32),
        input_output_aliases={0: 0},
        compiler_params=pltpu.CompilerParams(dimension_semantics=("arbitrary",),
                                             vmem_limit_bytes=VMEM_LIMIT),
        name="dense_ffn_ple",
    )(xbuf, p_all, lw["norm_ffn_g"], lw["ffn_w_gate"], lw["ffn_w_up"], lw["ffn_w_down"],
      lw["norm_ple_g"], lw["ple_w_gate"], lw["ple_w_proj"])


def _router_kernel(n_first, xa_ref, xb_ref, g_ref, wr_ref, idx_ref, wgt_ref, xn_ref):
    x = jnp.where(pl.program_id(0) < n_first, xa_ref[...], xb_ref[...])
    xn = _rms(x, g_ref[...])
    _store_row_tiled(xn_ref, xn)
    logits = jnp.dot(xn, wr_ref[...], preferred_element_type=F32, precision=lax.Precision.HIGHEST)
    lane = lax.broadcasted_iota(jnp.int32, logits.shape, 1)
    lane_f = lane.astype(F32)
    logits = jnp.where(lane < N_EXPERTS, logits, -jnp.inf)
    m1 = jnp.max(logits, axis=-1, keepdims=True)
    i1 = jnp.min(jnp.where(logits == m1, lane_f, float(LANES)), axis=-1, keepdims=True)
    rest = jnp.where(lane_f == i1, -jnp.inf, logits)
    m2 = jnp.max(rest, axis=-1, keepdims=True)
    i2 = jnp.min(jnp.where(rest == m2, lane_f, float(LANES)), axis=-1, keepdims=True)
    e2 = jnp.exp(m2 - m1)
    den = 1.0 + e2
    picks = jnp.where(lane == 0, i1, jnp.where(lane == 1, i2, 0.0))
    idx_ref[...] = jnp.transpose(picks)[0:TOP_K, :].astype(jnp.int32)
    wgt_ref[...] = jnp.where(lane == 0, 1.0 / den, jnp.where(lane == 1, e2 / den, 0.0))


def _router(xa, xb, lw):
    na, nb = xa.shape[0] // TM, xb.shape[0] // TM
    t = xa.shape[0] + xb.shape[0]
    rows = lambda i: (i, 0)
    return pl.pallas_call(
        functools.partial(_router_kernel, na),
        grid=(na + nb,),
        in_specs=[pl.BlockSpec((TM, D_MODEL), lambda i: (jnp.minimum(i, na - 1), 0)),
                  pl.BlockSpec((TM, D_MODEL), lambda i: (jnp.maximum(i - na, 0), 0)),
                  _const_spec((1, D_MODEL)), _const_spec((D_MODEL, LANES))],
        out_specs=[pl.BlockSpec((None, TOP_K, TM), lambda i: (i, 0, 0)), pl.BlockSpec((TM, LANES), rows),
                   pl.BlockSpec((TM * ROW_PIECES, LANES), rows)],
        out_shape=[jax.ShapeDtypeStruct((na + nb, TOP_K, TM), jnp.int32), jax.ShapeDtypeStruct((t, LANES), F32),
                   jax.ShapeDtypeStruct((t * ROW_PIECES, LANES), F32)],
        compiler_params=pltpu.CompilerParams(dimension_semantics=("arbitrary",)),
        name="router",
    )(xa, xb, lw["norm_ffn_g"], lw["router_w"])


ROW_PIECES = D_MODEL // LANES
assert ROW_PIECES == SUBLANES


def _store_row_tiled(ref, x):
    for s in range(ROW_PIECES):
        ref[pl.ds(s, x.shape[0], stride=ROW_PIECES), :] = x[:, s * LANES:(s + 1) * LANES]


def _load_row_tiled(ref, n_rows):
    return jnp.concatenate([ref[pl.ds(s, n_rows, stride=ROW_PIECES), :] for s in range(ROW_PIECES)], axis=1)


def _start_row_gather(idx_ref, k, src_hbm, dst, sem):
    def issue(r, carry):
        src_row = pl.multiple_of(idx_ref[k, r] * ROW_PIECES, ROW_PIECES)
        dst_row = pl.multiple_of(r * ROW_PIECES, ROW_PIECES)
        pltpu.make_async_copy(src_hbm.at[pl.ds(src_row, ROW_PIECES)], dst.at[pl.ds(dst_row, ROW_PIECES)],
                              sem).start()
        return carry
    lax.fori_loop(0, dst.shape[0] // ROW_PIECES, issue, 0, unroll=8)


def _wait_row_gather(src_hbm, dst, sem):
    pltpu.make_async_copy(src_hbm.at[pl.ds(0, dst.shape[0])], dst, sem).wait()


def _expert_kernel(n_fc, te_ref, na_ref, tokc_ref, tokn_ref, xn_hbm, wg_ref, wu_ref, wd_ref, ys_ref,
                   gbuf, xs, acc, sems):
    i, j = pl.program_id(0), pl.program_id(1)
    na = na_ref[0]
    slot = i % 2

    @pl.when(jnp.logical_and(i < na, j == 0))
    def _():
        @pl.when(i == 0)
        def _():
            _start_row_gather(tokc_ref, 0, xn_hbm, gbuf.at[0], sems.at[0])

        @pl.when(i + 1 < na)
        def _():
            _start_row_gather(tokn_ref, 0, xn_hbm, gbuf.at[1 - slot], sems.at[1 - slot])

        _wait_row_gather(xn_hbm, gbuf.at[slot], sems.at[slot])
        xs[...] = _load_row_tiled(gbuf.at[slot], TM).astype(BF16)

    def swiglu_piecewise(sink):
        h = _silu(jnp.dot(xs[...], wg_ref[...], preferred_element_type=F32)) * \
            jnp.dot(xs[...], wu_ref[...], preferred_element_type=F32)
        hb = h.astype(BF16)
        for q in range(D_MODEL // D_BRANCH):
            cs = slice(q * D_BRANCH, (q + 1) * D_BRANCH)
            sink(cs, jnp.dot(hb, wd_ref[:, cs], preferred_element_type=F32))

    def to_acc(first):
        def sink(cs, part):
            acc[:, cs] = part if first else acc[:, cs] + part
        return sink

    def to_out(with_acc):
        def sink(cs, part):
            tot = acc[:, cs] + part if with_acc else part
            for s in range(cs.start // LANES, cs.stop // LANES):
                lo = s * LANES - cs.start
                ys_ref[pl.ds(s, TM, stride=ROW_PIECES), :] = tot[:, lo:lo + LANES]
        return sink

    active = i < na
    if n_fc == 1:
        pl.when(active)(lambda: swiglu_piecewise(to_out(False)))
    else:
        pl.when(jnp.logical_and(active, j == 0))(lambda: swiglu_piecewise(to_acc(True)))
        if n_fc > 2:
            pl.when(jnp.logical_and(active, jnp.logical_and(j > 0, j < n_fc - 1)))(
                lambda: swiglu_piecewise(to_acc(False)))
        pl.when(jnp.logical_and(active, j == n_fc - 1))(lambda: swiglu_piecewise(to_out(True)))

    @pl.when(jnp.logical_and(i >= na, j == 0))
    def _():
        ys_ref[...] = jnp.zeros(ys_ref.shape, F32)


def _experts(xn_tiled, slot_tok, tile_expert, n_active, lw):
    n_tiles = tile_expert.shape[0]
    d_ff = lw["moe_w_gate"].shape[2]
    n_fc = d_ff // TF_EXPERT

    def fcol(i, j, na):
        return jnp.where(i < na[0], j, n_fc - 1)

    grid_spec = pltpu.PrefetchScalarGridSpec(
        num_scalar_prefetch=2,
        grid=(n_tiles, n_fc),
        in_specs=[
            pl.BlockSpec((None, 1, TM), lambda i, j, te, na: (i, 0, 0), memory_space=pltpu.SMEM),
            pl.BlockSpec((None, 1, TM), lambda i, j, te, na: (jnp.minimum(i + 1, n_tiles - 1), 0, 0),
                         memory_space=pltpu.SMEM),
            pl.BlockSpec(memory_space=pl.ANY),
            pl.BlockSpec((None, D_MODEL, TF_EXPERT), lambda i, j, te, na: (te[i], 0, fcol(i, j, na))),
            pl.BlockSpec((None, D_MODEL, TF_EXPERT), lambda i, j, te, na: (te[i], 0, fcol(i, j, na))),
            pl.BlockSpec((None, TF_EXPERT, D_MODEL), lambda i, j, te, na: (te[i], fcol(i, j, na), 0)),
        ],
        out_specs=pl.BlockSpec((TM * ROW_PIECES, LANES), lambda i, j, te, na: (i, 0)),
        scratch_shapes=[pltpu.VMEM((2, TM * ROW_PIECES, LANES), F32), pltpu.VMEM((TM, D_MODEL), BF16),
                        pltpu.VMEM((TM, D_MODEL), F32), pltpu.SemaphoreType.DMA((2,))],
    )
    return pl.pallas_call(
        functools.partial(_expert_kernel, n_fc),
        grid_spec=grid_spec,
        out_shape=jax.ShapeDtypeStruct((n_tiles * TM * ROW_PIECES, LANES), F32),
        compiler_params=pltpu.CompilerParams(dimension_semantics=("arbitrary", "arbitrary"),
                                             vmem_limit_bytes=VMEM_LIMIT),
        name="moe_experts",
    )(tile_expert, n_active, slot_tok, slot_tok, xn_tiled, lw["moe_w_gate"], lw["moe_w_up"], lw["moe_w_down"])


def _combine_kernel(final, n_steps, pos0c_ref, pos1c_ref, pos0n_ref, pos1n_ref, x_ref, p_ref, wgt_ref, ys_hbm,
                    gple_ref, wpg_ref, wpp_ref, gfin_ref, o_ref, buf, sems):
    i = pl.program_id(0)
    slot = i % 2

    def start(pos_refs, s):
        for k in range(TOP_K):
            _start_row_gather(pos_refs[k], 0, ys_hbm, buf.at[s, k], sems.at[s, k])

    @pl.when(i == 0)
    def _():
        start((pos0c_ref, pos1c_ref), 0)

    @pl.when(i + 1 < n_steps)
    def _():
        start((pos0n_ref, pos1n_ref), 1 - slot)

    for k in range(TOP_K):
        _wait_row_gather(ys_hbm, buf.at[slot, k], sems.at[slot, k])
    wgt = wgt_ref[...]
    y0 = _load_row_tiled(buf.at[slot, 0], TM)
    y1 = _load_row_tiled(buf.at[slot, 1], TM)
    x = x_ref[...] + (y0 * wgt[:, 0:1] + y1 * wgt[:, 1:2])
    x = _ple_tail(x, p_ref[...], gple_ref, wpg_ref, wpp_ref)
    o_ref[...] = _rms(x, gfin_ref[...]) if final else x


def _combine(xbuf, p_all, layer, pos, wgt, ys, lw, final_g, row0):
    blk0 = row0 // TM
    n_rows = xbuf.shape[0]
    n_steps = n_rows // TM
    rows = lambda i: (i, 0)
    tok_rows = lambda i: (blk0 + i, 0)
    def pos_spec(k, ahead):
        return pl.BlockSpec((None, None, 1, TM),
                            lambda i: (blk0 + jnp.minimum(i + ahead, n_steps - 1), k, 0, 0),
                            memory_space=pltpu.SMEM)
    final = final_g is not None
    gfin = final_g if final else lw["norm_ple_g"]
    return pl.pallas_call(
        functools.partial(_combine_kernel, final, n_steps),
        grid=(n_steps,),
        in_specs=[pos_spec(0, 0), pos_spec(1, 0), pos_spec(0, 1), pos_spec(1, 1),
                  pl.BlockSpec((TM, D_MODEL), rows), pl.BlockSpec((None, TM, D_PLE), lambda i: (layer, i, 0)),
                  pl.BlockSpec((TM, LANES), tok_rows), pl.BlockSpec(memory_space=pl.ANY),
                  _const_spec((1, D_MODEL)), _const_spec((D_MODEL, D_MODEL)),
                  _const_spec((D_PLE, D_MODEL)), _const_spec((1, D_MODEL))],
        out_specs=pl.BlockSpec((TM, D_MODEL), lambda i: (i, 0)),
        out_shape=jax.ShapeDtypeStruct((n_rows, D_MODEL), F32),
        scratch_shapes=[pltpu.VMEM((2, TOP_K, TM * ROW_PIECES, LANES), F32),
                        pltpu.SemaphoreType.DMA((2, TOP_K))],
        compiler_params=pltpu.CompilerParams(dimension_semantics=("arbitrary",),
                                             vmem_limit_bytes=VMEM_LIMIT),
        name="moe_combine_ple",
    )(pos, pos, pos, pos, xbuf, p_all, wgt, ys, lw["norm_ple_g"], lw["ple_w_gate"],
      lw["ple_w_proj"], gfin)


def _routing_tables(idx, n_tok):
    n_pairs = n_tok * TOP_K
    n_tiles = n_pairs // TM + N_EXPERTS
    experts = jnp.arange(N_EXPERTS, dtype=jnp.int32)
    e_flat = idx.reshape(-1)
    pair = jnp.arange(n_pairs, dtype=jnp.int32)
    pair_tok = (pair // (TOP_K * TM)) * TM + pair % TM
    onehot = (experts[:, None] == e_flat[None, :]).astype(jnp.int32)
    csum = jnp.cumsum(onehot, axis=1)
    sizes = csum[:, -1]
    padded = ((sizes + TM - 1) // TM) * TM
    pend = jnp.cumsum(padded)
    pstart = pend - padded
    pos = jnp.sum(onehot * (csum - 1 + pstart[:, None]), axis=0)
    fill_e = jnp.repeat(experts, TM)
    fill_r = jnp.tile(jnp.arange(TM, dtype=jnp.int32), N_EXPERTS)
    fill_need = jnp.sum((fill_e[:, None] == experts[None, :]) * (padded - sizes)[None, :], axis=1)
    keys = jnp.concatenate([e_flat, jnp.where(fill_r < fill_need, fill_e, N_EXPERTS)])
    toks = jnp.concatenate([pair_tok, jnp.zeros((N_EXPERTS * TM,), jnp.int32)])
    _, slot_tok = lax.sort((keys, toks), num_keys=1, is_stable=True)
    n_active = (pend[-1] // TM).astype(jnp.int32)
    tile_row = jnp.minimum(jnp.arange(n_tiles, dtype=jnp.int32), n_active - 1)
    tile_expert = jnp.minimum(
        jnp.sum((tile_row[:, None] * TM >= pend[None, :]).astype(jnp.int32), axis=1), N_EXPERTS - 1)
    return (slot_tok.reshape(n_tiles, 1, TM), tile_expert, n_active.reshape(1),
            pos.reshape(n_tok // TM, TOP_K, 1, TM))


def _sgu_tables(sgu_w, sgu_b, seq):
    r = np.arange(CHUNK)
    bias = jnp.repeat(sgu_b.T, GROUP_DIM, axis=1)
    if seq >= CHUNK:
        w = sgu_w
        mask = r[None, :] <= r[:, None]
    else:
        reps = CHUNK // seq
        w = jnp.tile(sgu_w[:, :seq, :seq], (1, reps, reps))
        mask = (r[:, None] // seq == r[None, :] // seq) & (r[None, :] % seq <= r[:, None] % seq)
        bias = jnp.tile(bias[:seq], (reps, 1))
    mask = np.tile(mask.astype(np.float32), (N_GROUPS, 1))
    return w.reshape(N_GROUPS * CHUNK, CHUNK), jnp.asarray(mask), bias


def _block_diag(w):
    out = jnp.zeros((D_BRANCH, D_BRANCH), w.dtype)
    for g in range(N_GROUPS):
        out = out.at[g * GROUP_DIM:(g + 1) * GROUP_DIM, g * GROUP_DIM:(g + 1) * GROUP_DIM].set(w[g])
    return out


def kernel(x_prompt, x_sample, state_conv_a, state_conv_b, state_pool, p_prompt, p_sample, norm_mix_g, w_in, conv_a_w, conv_b_w, conv_b_bias, conv_b_ln_g, conv_b_ln_b, pool_w, pool_scale, sgu_ln_g, sgu_ln_b, sgu_w, sgu_b, w_branch, w_out, norm_ffn_g, ffn_w_gate, ffn_w_up, ffn_w_down, router_w, moe_w_gate, moe_w_up, moe_w_down, norm_ple_g, ple_w_gate, ple_w_proj, final_norm_g):
    batch, seq, _ = x_prompt.shape
    nseq, dseq, _ = x_sample.shape
    depth = w_in.shape[0]
    assert depth % 2 == 0
    assert PAST_LEN + 1 >= POOL_MAX
    assert seq % TL == 0 and TL % CHUNK == 0 and TL >= CONV_B_W + 1
    n_prompt, n_sample = batch * seq, nseq * dseq
    n_tok = n_prompt + n_sample
    assert n_prompt % TM == 0 and n_sample % TM == 0

    xp, xs = x_prompt.reshape(n_prompt, D_MODEL), x_sample.reshape(n_sample, D_MODEL)
    pp, ps = p_prompt.reshape(depth, n_prompt, D_PLE), p_sample.reshape(depth, n_sample, D_PLE)
    row = lambda a: a.reshape(1, -1)
    zeros = jnp.zeros((D_BRANCH,), F32)

    states_p, states_s = [], []
    y_prompt = y_sample = None
    for i in range(depth):
        sw_p, sm_p, sb_p = _sgu_tables(sgu_w[i], sgu_b[i], seq)
        sw_s, sm_s, sb_s = _sgu_tables(sgu_w[i], sgu_b[i], dseq)
        lw = {
            "norm_mix_g": row(norm_mix_g[i]), "w_in": w_in[i].astype(BF16),
            "conv_a_w": conv_a_w[i], "conv_b_w": conv_b_w[i],
            "vec": jnp.stack([conv_b_bias[i], conv_b_ln_g[i], conv_b_ln_b[i], pool_scale[i],
                              sgu_ln_g[i], sgu_ln_b[i], zeros, zeros]),
            "pool_bd": _block_diag(pool_w[i]).astype(BF16),
            "sgu_w_p": sw_p, "sgu_mask_p": sm_p, "sgu_bias_p": sb_p,
            "sgu_w_s": sw_s, "sgu_mask_s": sm_s, "sgu_bias_s": sb_s,
            "w_branch": w_branch[i].astype(BF16), "w_out": w_out[i].astype(BF16),
            "norm_ffn_g": row(norm_ffn_g[i]), "norm_ple_g": row(norm_ple_g[i]),
            "ple_w_gate": ple_w_gate[i].astype(BF16), "ple_w_proj": ple_w_proj[i].astype(BF16),
        }
        in_place = i > 0
        xp, a_p, b_p, pool_p, v_p = _mixer_prompt(xp, lw, batch, seq, in_place)
        xs, a_s, b_s, pool_s, v_s = _mixer_sample(xs, lw, state_conv_a[i], state_conv_b[i], state_pool[i],
                                                  nseq, dseq, in_place)
        states_p.append((a_p, b_p, pool_p, v_p))
        states_s.append((a_s, b_s, pool_s, v_s))

        j = i // 2
        last = i == depth - 1
        if i % 2 == 0:
            lw.update({"ffn_w_gate": ffn_w_gate[j].astype(BF16), "ffn_w_up": ffn_w_up[j].astype(BF16),
                       "ffn_w_down": ffn_w_down[j].astype(BF16)})
            xp = _dense_ffn(xp, pp, i, lw)
            xs = _dense_ffn(xs, ps, i, lw)
        else:
            lw.update({"router_w": jnp.pad(router_w[j], ((0, 0), (0, LANES - N_EXPERTS))),
                       "moe_w_gate": moe_w_gate[j].astype(BF16), "moe_w_up": moe_w_up[j].astype(BF16),
                       "moe_w_down": moe_w_down[j].astype(BF16)})
            idx, wgt, xn_tiled = _router(xp, xs, lw)
            slot_tok, tile_expert, n_active, pos = _routing_tables(idx, n_tok)
            ys = _experts(xn_tiled, slot_tok, tile_expert, n_active, lw)
            fin = row(final_norm_g) if last else None
            xp = _combine(xp, pp, i, pos, wgt, ys, lw, fin, 0)
            xs = _combine(xs, ps, i, pos, wgt, ys, lw, fin, n_prompt)
            if last:
                y_prompt, y_sample = xp, xs

    stack = lambda k, sts: jnp.stack([s[k] for s in sts])
    return (y_prompt.reshape(batch, seq, D_MODEL), y_sample.reshape(nseq, dseq, D_MODEL),
            stack(0, states_p), stack(1, states_p), stack(2, states_p), stack(3, states_p),
            stack(0, states_s), stack(1, states_s), stack(2, states_s), stack(3, states_s))
```

```python
import functools

import jax
import jax.numpy as jnp
import numpy as np
from jax import lax
from jax.experimental import pallas as pl
from jax.experimental.pallas import tpu as pltpu

F32 = jnp.float32
BF16 = jnp.bfloat16

D_MODEL = 1024
N_BRANCH = 4
D_BRANCH = 256
N_GROUPS = 4
GROUP_DIM = 64
CONV_A_W = 3
CONV_B_W = 31
POOL_WINDOWS = (2, 4, 8, 16)
POOL_MAX = 16
CHUNK = 128
PAST_LEN = 16384
D_PLE = 256
N_EXPERTS = 8
TOP_K = 2
EPS = 1e-6

OFF_A_H = 0
OFF_A_B = 256
OFF_A_C = 512
OFF_CF_A = 768
OFF_CF_B = 1024
OFF_POOL = 1280
OFF_U = 1536
OFF_V = 1792
OFF_GATE = 2048

LANES = 128
SUBLANES = 8
TM = 512
TL = 512
NB = 64
TF_EXPERT = 1792
ROW_CHUNK = 32
SEQ_CHUNK = 8
VMEM_LIMIT = 56 * 1024 * 1024

V_CB_BIAS, V_CB_LN_G, V_CB_LN_B, V_POOL_SCALE, V_SGU_LN_G, V_SGU_LN_B = range(6)


def _rms(x, g):
    return x * lax.rsqrt(jnp.mean(x * x, axis=-1, keepdims=True) + EPS) * g


def _ln(x, g, b):
    xc = x - jnp.mean(x, axis=-1, keepdims=True)
    var = jnp.mean(xc * xc, axis=-1, keepdims=True)
    return xc * lax.rsqrt(var + EPS) * g + b


def _bdot(a, w):
    return jnp.dot(a.astype(BF16), w, preferred_element_type=F32)


def _silu(x):
    return x * jax.nn.sigmoid(x)


def _const_spec(shape):
    nd = len(shape)
    return pl.BlockSpec(shape, lambda *_: (0,) * nd, pipeline_mode=pl.Buffered(1))


def _sgu(u, v, swm, sbias):
    m = v.shape[0]
    group = lax.broadcasted_iota(jnp.int32, (CHUNK, D_BRANCH), 1) // GROUP_DIM
    outs = []
    for c in range(m // CHUNK):
        vc = v[c * CHUNK:(c + 1) * CHUNK].astype(BF16)
        r = jnp.dot(swm, vc, preferred_element_type=F32)
        s = sbias
        for g in range(N_GROUPS):
            s = s + jnp.where(group == g, r[g * CHUNK:(g + 1) * CHUNK], 0.0)
        outs.append(s)
    return u * jnp.concatenate(outs, axis=0)


GATE_PIECES_PER_BRANCH = D_MODEL // D_BRANCH
N_GATE_PIECES = N_BRANCH * GATE_PIECES_PER_BRANCH


def _gate_emitter(xnb, win_ref):
    pieces = []

    def emit(n=1):
        for _ in range(n):
            if len(pieces) < N_GATE_PIECES:
                lo = OFF_GATE + len(pieces) * D_BRANCH
                z = jnp.dot(xnb, win_ref[:, lo:lo + D_BRANCH], preferred_element_type=F32)
                pieces.append(jax.nn.sigmoid(z))
    return pieces, emit


def _gated_merge(x, branches, gates, wbr_ref, wout_ref):
    bb = [b.astype(BF16) for b in branches]
    cols = []
    for q in range(GATE_PIECES_PER_BRANCH):
        cs = slice(q * D_BRANCH, (q + 1) * D_BRANCH)
        m = None
        for i in range(N_BRANCH):
            t = gates[i * GATE_PIECES_PER_BRANCH + q] * \
                jnp.dot(bb[i], wbr_ref[i, :, cs], preferred_element_type=F32)
            m = t if m is None else m + t
        cols.append(m.astype(BF16))
    return x + jnp.dot(jnp.concatenate(cols, axis=1), wout_ref[...], preferred_element_type=F32)


def _pool_lane_windows(half):
    small, big = POOL_WINDOWS[2 * half], POOL_WINDOWS[2 * half + 1]
    lane = lax.broadcasted_iota(jnp.int32, (1, LANES), 1)
    is_big = lane >= GROUP_DIM
    return small, big, is_big


def _mixer_prompt_kernel(x_ref, g_ref, win_ref, caw_ref, cbw_ref, vec_ref, pw_ref, sw_ref, smask_ref,
                         sbias_ref, wbr_ref, wout_ref,
                         xo_ref, sta_ref, stb_ref, stp_ref, stv_ref,
                         sa, sb, sp):
    i = pl.program_id(1)
    ha, hb, hp = SUBLANES, 4 * SUBLANES, 2 * SUBLANES

    @pl.when(i == 0)
    def _():
        sa[0:ha, :] = jnp.zeros((ha, D_BRANCH), F32)
        sb[0:hb, :] = jnp.zeros((hb, D_BRANCH), F32)
        sp[0:hp, :] = jnp.zeros((hp, D_BRANCH), F32)

    @pl.when(i > 0)
    def _():
        sa[0:ha, :] = sa[TL:TL + ha, :]
        sb[0:hb, :] = sb[TL:TL + hb, :]
        sp[0:hp, :] = sp[TL:TL + hp, :]

    x = x_ref[...]
    xnb = _rms(x, g_ref[...]).astype(BF16)
    vec = vec_ref[...]

    def zc(off):
        return jnp.dot(xnb, win_ref[:, off:off + D_BRANCH], preferred_element_type=F32)

    gates, emit_gate = _gate_emitter(xnb, win_ref)
    n_chunks = TL // ROW_CHUNK

    def tap_conv(s_ref, base, w_ref, ntaps, gate_chunks=()):
        outs = []
        for c in range(n_chunks):
            acc = None
            for k in range(ntaps):
                term = s_ref[pl.ds(base + k + c * ROW_CHUNK, ROW_CHUNK), :] * w_ref[k:k + 1, :]
                acc = term if acc is None else acc + term
            outs.append(acc)
            if c in gate_chunks:
                emit_gate()
        return jnp.concatenate(outs, axis=0)

    sa[ha:ha + TL, :] = zc(OFF_A_C) * zc(OFF_A_H)
    br_a = zc(OFF_A_B) * tap_conv(sa, ha - (CONV_A_W - 1), caw_ref, CONV_A_W)
    sta_ref[0] = sa[TL + ha - (CONV_A_W - 1):TL + ha, :]

    sb[hb:hb + TL, :] = zc(OFF_CF_A) * jax.nn.sigmoid(zc(OFF_CF_B))
    yb = tap_conv(sb, hb - (CONV_B_W - 1), cbw_ref, CONV_B_W, gate_chunks=range(3 * n_chunks // 4))
    yb = yb + vec[V_CB_BIAS:V_CB_BIAS + 1]
    br_b = _silu(_ln(yb, vec[V_CB_LN_G:V_CB_LN_G + 1], vec[V_CB_LN_B:V_CB_LN_B + 1]))
    stb_ref[0] = sb[TL + hb - (CONV_B_W - 1):TL + hb, :]

    sp[hp:hp + TL, :] = zc(OFF_POOL)
    halves = []
    for half in range(2):
        small, big, is_big = _pool_lane_windows(half)
        lanes = slice(half * LANES, (half + 1) * LANES)
        wl = jnp.where(is_big, float(big), float(small))
        outs = []
        for c in range(n_chunks):
            r0 = hp + c * ROW_CHUNK
            acc = None
            for j in range(big):
                seg = sp[pl.ds(r0 - j, ROW_CHUNK), lanes]
                if j >= small:
                    seg = jnp.where(is_big, seg, 0.0)
                acc = seg if acc is None else acc + seg
            pos = i * TL + c * ROW_CHUNK + lax.broadcasted_iota(jnp.int32, (ROW_CHUNK, LANES), 0)
            cnt = jnp.minimum(wl, (pos + 1).astype(F32))
            outs.append(acc / cnt - sp[pl.ds(r0, ROW_CHUNK), lanes])
            if half == 1 and c % 4 == 0:
                emit_gate()
        halves.append(jnp.concatenate(outs, axis=0))
    pooled = jnp.concatenate(halves, axis=1)
    br_c = _bdot(pooled, pw_ref[...]) * vec[V_POOL_SCALE:V_POOL_SCALE + 1]
    stp_ref[0] = sp[TL + hp - (POOL_MAX - 1):TL + hp, :]

    v = _ln(zc(OFF_V), vec[V_SGU_LN_G:V_SGU_LN_G + 1], vec[V_SGU_LN_B:V_SGU_LN_B + 1])
    stv_ref[0] = v[TL - CHUNK:TL]
    swm = jnp.where(smask_ref[...] > 0, sw_ref[...], 0.0).astype(BF16)
    br_d = _sgu(zc(OFF_U), v, swm, sbias_ref[...])

    emit_gate(N_GATE_PIECES)
    xo_ref[...] = _gated_merge(x, (br_a, br_b, br_c, br_d), gates, wbr_ref, wout_ref)


def _mixer_sample_kernel(x_ref, g_ref, win_ref, caw_ref, cbw_ref, vec_ref, pw_ref, sw_ref, smask_ref,
                         sbias_ref, wbr_ref, wout_ref, ina_ref, inb_ref, inp_ref,
                         xo_ref, sta_ref, stb_ref, stp_ref, stv_ref,
                         sa, sb, sp):
    seq = SUBLANES
    ha, hb, hp = SUBLANES, 4 * SUBLANES, 2 * SUBLANES
    sa[:, ha - (CONV_A_W - 1):ha, :] = ina_ref[...]
    sb[:, hb - (CONV_B_W - 1):hb, :] = inb_ref[...]
    sp[:, hp - (POOL_MAX - 1):hp, :] = inp_ref[...]

    x = x_ref[...]
    xnb = _rms(x, g_ref[...]).astype(BF16)
    vec = vec_ref[...]

    def zc(off):
        return jnp.dot(xnb, win_ref[:, off:off + D_BRANCH], preferred_element_type=F32)

    def to3(a):
        return a.reshape(NB, seq, D_BRANCH)

    def to2(a):
        return a.reshape(NB * seq, D_BRANCH)

    gates, emit_gate = _gate_emitter(xnb, win_ref)
    n_chunks = NB // SEQ_CHUNK

    def tap_conv(s_ref, base, w_ref, ntaps, gates_per_chunk=0):
        outs = []
        for c in range(n_chunks):
            acc = None
            for k in range(ntaps):
                seg = s_ref[c * SEQ_CHUNK:(c + 1) * SEQ_CHUNK, pl.ds(base + k, seq), :]
                term = seg * w_ref[k:k + 1, :][None]
                acc = term if acc is None else acc + term
            outs.append(acc)
            emit_gate(gates_per_chunk)
        return to2(jnp.concatenate(outs, axis=0))

    sa[:, ha:ha + seq, :] = to3(zc(OFF_A_C) * zc(OFF_A_H))
    br_a = zc(OFF_A_B) * tap_conv(sa, ha - (CONV_A_W - 1), caw_ref, CONV_A_W)
    sta_ref[...] = sa[:, ha + seq - (CONV_A_W - 1):ha + seq, :]

    sb[:, hb:hb + seq, :] = to3(zc(OFF_CF_A) * jax.nn.sigmoid(zc(OFF_CF_B)))
    yb = tap_conv(sb, hb - (CONV_B_W - 1), cbw_ref, CONV_B_W, gates_per_chunk=-(-N_GATE_PIECES // n_chunks))
    yb = yb + vec[V_CB_BIAS:V_CB_BIAS + 1]
    br_b = _silu(_ln(yb, vec[V_CB_LN_G:V_CB_LN_G + 1], vec[V_CB_LN_B:V_CB_LN_B + 1]))
    stb_ref[...] = sb[:, hb + seq - (CONV_B_W - 1):hb + seq, :]

    sp[:, hp:hp + seq, :] = to3(zc(OFF_POOL))
    halves = []
    for half in range(2):
        small, big, is_big = _pool_lane_windows(half)
        lanes = slice(half * LANES, (half + 1) * LANES)
        wl = jnp.where(is_big, float(big), float(small))[None]
        outs = []
        for c in range(NB // SEQ_CHUNK):
            rows = slice(c * SEQ_CHUNK, (c + 1) * SEQ_CHUNK)
            acc = None
            for j in range(big):
                seg = sp[rows, pl.ds(hp - j, seq), lanes]
                if j >= small:
                    seg = jnp.where(is_big[None], seg, 0.0)
                acc = seg if acc is None else acc + seg
            outs.append(acc / wl - sp[rows, pl.ds(hp, seq), lanes])
        halves.append(jnp.concatenate(outs, axis=0).reshape(NB * seq, LANES))
    pooled = jnp.concatenate(halves, axis=1)
    br_c = _bdot(pooled, pw_ref[...]) * vec[V_POOL_SCALE:V_POOL_SCALE + 1]
    stp_ref[...] = sp[:, hp + seq - (POOL_MAX - 1):hp + seq, :]

    v = _ln(zc(OFF_V), vec[V_SGU_LN_G:V_SGU_LN_G + 1], vec[V_SGU_LN_B:V_SGU_LN_B + 1])
    stv_ref[...] = to3(v)
    swm = jnp.where(smask_ref[...] > 0, sw_ref[...], 0.0).astype(BF16)
    br_d = _sgu(zc(OFF_U), v, swm, sbias_ref[...])

    emit_gate(N_GATE_PIECES)
    xo_ref[...] = _gated_merge(x, (br_a, br_b, br_c, br_d), gates, wbr_ref, wout_ref)


def _mixer_weight_specs():
    return [
        _const_spec((1, D_MODEL)),
        _const_spec((D_MODEL, OFF_GATE + N_BRANCH * D_MODEL)),
        _const_spec((CONV_A_W, D_BRANCH)),
        _const_spec((CONV_B_W, D_BRANCH)),
        _const_spec((SUBLANES, D_BRANCH)),
        _const_spec((D_BRANCH, D_BRANCH)),
        _const_spec((N_GROUPS * CHUNK, CHUNK)),
        _const_spec((N_GROUPS * CHUNK, CHUNK)),
        _const_spec((CHUNK, D_BRANCH)),
        _const_spec((N_BRANCH, D_BRANCH, D_MODEL)),
        _const_spec((D_MODEL, D_MODEL)),
    ]


def _mixer_prompt(xbuf, lw, batch, seq, in_place):
    nt = seq // TL
    rows = lambda b, i: (b * nt + i, 0)
    st = lambda n: pl.BlockSpec((1, n, D_BRANCH), lambda b, i: (b, 0, 0))
    return pl.pallas_call(
        _mixer_prompt_kernel,
        grid=(batch, nt),
        in_specs=[pl.BlockSpec((TL, D_MODEL), rows)] + _mixer_weight_specs(),
        out_specs=[pl.BlockSpec((TL, D_MODEL), rows), st(CONV_A_W - 1), st(CONV_B_W - 1),
                   st(POOL_MAX - 1), st(CHUNK)],
        out_shape=[jax.ShapeDtypeStruct(xbuf.shape, F32),
                   jax.ShapeDtypeStruct((batch, CONV_A_W - 1, D_BRANCH), F32),
                   jax.ShapeDtypeStruct((batch, CONV_B_W - 1, D_BRANCH), F32),
                   jax.ShapeDtypeStruct((batch, POOL_MAX - 1, D_BRANCH), F32),
                   jax.ShapeDtypeStruct((batch, CHUNK, D_BRANCH), F32)],
        scratch_shapes=[pltpu.VMEM((TL + SUBLANES, D_BRANCH), F32),
                        pltpu.VMEM((TL + 4 * SUBLANES, D_BRANCH), F32),
                        pltpu.VMEM((TL + 2 * SUBLANES, D_BRANCH), F32)],
        input_output_aliases={0: 0} if in_place else {},
        compiler_params=pltpu.CompilerParams(dimension_semantics=("arbitrary", "arbitrary"),
                                             vmem_limit_bytes=VMEM_LIMIT),
        name="mixer_prompt",
    )(xbuf, lw["norm_mix_g"], lw["w_in"], lw["conv_a_w"], lw["conv_b_w"], lw["vec"], lw["pool_bd"],
      lw["sgu_w_p"], lw["sgu_mask_p"], lw["sgu_bias_p"], lw["w_branch"], lw["w_out"])


def _mixer_sample(xbuf, lw, st_a, st_b, st_p, nseq, seq, in_place):
    assert seq == SUBLANES and nseq % NB == 0
    rows = lambda i: (i, 0)
    st = lambda n: pl.BlockSpec((NB, n, D_BRANCH), lambda i: (i, 0, 0))
    return pl.pallas_call(
        _mixer_sample_kernel,
        grid=(nseq // NB,),
        in_specs=[pl.BlockSpec((NB * seq, D_MODEL), rows)] + _mixer_weight_specs()
                 + [st(CONV_A_W - 1), st(CONV_B_W - 1), st(POOL_MAX - 1)],
        out_specs=[pl.BlockSpec((NB * seq, D_MODEL), rows), st(CONV_A_W - 1), st(CONV_B_W - 1),
                   st(POOL_MAX - 1), st(seq)],
        out_shape=[jax.ShapeDtypeStruct(xbuf.shape, F32),
                   jax.ShapeDtypeStruct((nseq, CONV_A_W - 1, D_BRANCH), F32),
                   jax.ShapeDtypeStruct((nseq, CONV_B_W - 1, D_BRANCH), F32),
                   jax.ShapeDtypeStruct((nseq, POOL_MAX - 1, D_BRANCH), F32),
                   jax.ShapeDtypeStruct((nseq, seq, D_BRANCH), F32)],
        scratch_shapes=[pltpu.VMEM((NB, 2 * SUBLANES, D_BRANCH), F32),
                        pltpu.VMEM((NB, 5 * SUBLANES, D_BRANCH), F32),
                        pltpu.VMEM((NB, 3 * SUBLANES, D_BRANCH), F32)],
        input_output_aliases={0: 0} if in_place else {},
        compiler_params=pltpu.CompilerParams(dimension_semantics=("arbitrary",),
                                             vmem_limit_bytes=VMEM_LIMIT),
        name="mixer_sample",
    )(xbuf, lw["norm_mix_g"], lw["w_in"], lw["conv_a_w"], lw["conv_b_w"], lw["vec"], lw["pool_bd"],
      lw["sgu_w_s"], lw["sgu_mask_s"], lw["sgu_bias_s"], lw["w_branch"], lw["w_out"],
      st_a, st_b, st_p)


def _ple_tail(x, p, gple_ref, wpg_ref, wpp_ref):
    gate = jax.nn.sigmoid(_bdot(_rms(x, gple_ref[...]), wpg_ref[...]))
    return x + gate * _bdot(p, wpp_ref[...])


def _dense_ffn_kernel(n_chunks, x_ref, p_ref, gffn_ref, wg_ref, wu_ref, wd_ref, gple_ref, wpg_ref, wpp_ref,
                      xo_ref):
    x = x_ref[...]
    xnb = _rms(x, gffn_ref[...]).astype(BF16)
    tf = wg_ref.shape[1] // n_chunks
    f = None
    for c in range(n_chunks):
        cols = slice(c * tf, (c + 1) * tf)
        h = _silu(jnp.dot(xnb, wg_ref[:, cols], preferred_element_type=F32)) * \
            jnp.dot(xnb, wu_ref[:, cols], preferred_element_type=F32)
        t = _bdot(h, wd_ref[cols, :])
        f = t if f is None else f + t
    xo_ref[...] = _ple_tail(x + f, p_ref[...], gple_ref, wpg_ref, wpp_ref)


def _dense_ffn(xbuf, p_all, layer, lw):
    t = xbuf.shape[0]
    d_ff = lw["ffn_w_gate"].shape[1]
    rows = lambda i: (i, 0)
    return pl.pallas_call(
        functools.partial(_dense_ffn_kernel, 2),
        grid=(t // TM,),
        in_specs=[pl.BlockSpec((TM, D_MODEL), rows), pl.BlockSpec((None, TM, D_PLE), lambda i: (layer, i, 0)),
                  _const_spec((1, D_MODEL)), _const_spec((D_MODEL, d_ff)), _const_spec((D_MODEL, d_ff)),
                  _const_spec((d_ff, D_MODEL)), _const_spec((1, D_MODEL)),
                  _const_spec((D_MODEL, D_MODEL)), _const_spec((D_PLE, D_MODEL))],
        out_specs=pl.BlockSpec((TM, D_MODEL), rows),
        out_shape=jax.ShapeDtypeStruct(xbuf.shape, F32),
        input_output_aliases={0: 0},
        compiler_params=pltpu.CompilerParams(dimension_semantics=("arbitrary",),
                                             vmem_limit_bytes=VMEM_LIMIT),
        name="dense_ffn_ple",
    )(xbuf, p_all, lw["norm_ffn_g"], lw["ffn_w_gate"], lw["ffn_w_up"], lw["ffn_w_down"],
      lw["norm_ple_g"], lw["ple_w_gate"], lw["ple_w_proj"])


def _router_kernel(n_first, xa_ref, xb_ref, g_ref, wr_ref, wrh_ref, idx_ref, wgt_ref, xn_ref):
    x = jnp.where(pl.program_id(0) < n_first, xa_ref[...], xb_ref[...])
    xn = _rms(x, g_ref[...])
    _store_row_tiled(xn_ref, xn)
    xh = xn.astype(BF16)
    xl = (xn - xh.astype(F32)).astype(BF16)
    d = jnp.dot(xh, wr_ref[...], preferred_element_type=F32) + \
        jnp.dot(xl, wrh_ref[...], preferred_element_type=F32)
    logits = d + pltpu.roll(d, LANES - N_EXPERTS, axis=1)
    lane = lax.broadcasted_iota(jnp.int32, logits.shape, 1)
    lane_f = lane.astype(F32)
    logits = jnp.where(lane < N_EXPERTS, logits, -jnp.inf)
    m1 = jnp.max(logits, axis=-1, keepdims=True)
    i1 = jnp.min(jnp.where(logits == m1, lane_f, float(LANES)), axis=-1, keepdims=True)
    rest = jnp.where(lane_f == i1, -jnp.inf, logits)
    m2 = jnp.max(rest, axis=-1, keepdims=True)
    i2 = jnp.min(jnp.where(rest == m2, lane_f, float(LANES)), axis=-1, keepdims=True)
    e2 = jnp.exp(m2 - m1)
    den = 1.0 + e2
    picks = jnp.where(lane == 0, i1, jnp.where(lane == 1, i2, 0.0))
    idx_ref[...] = jnp.transpose(picks)[0:TOP_K, :].astype(jnp.int32)
    wgt_ref[...] = jnp.where(lane == 0, 1.0 / den, jnp.where(lane == 1, e2 / den, 0.0))


def _router(xa, xb, lw):
    na, nb = xa.shape[0] // TM, xb.shape[0] // TM
    t = xa.shape[0] + xb.shape[0]
    rows = lambda i: (i, 0)
    return pl.pallas_call(
        functools.partial(_router_kernel, na),
        grid=(na + nb,),
        in_specs=[pl.BlockSpec((TM, D_MODEL), lambda i: (jnp.minimum(i, na - 1), 0)),
                  pl.BlockSpec((TM, D_MODEL), lambda i: (jnp.maximum(i - na, 0), 0)),
                  _const_spec((1, D_MODEL)), _const_spec((D_MODEL, LANES)), _const_spec((D_MODEL, LANES))],
        out_specs=[pl.BlockSpec((None, TOP_K, TM), lambda i: (i, 0, 0)), pl.BlockSpec((TM, LANES), rows),
                   pl.BlockSpec((TM * ROW_PIECES, LANES), rows)],
        out_shape=[jax.ShapeDtypeStruct((na + nb, TOP_K, TM), jnp.int32), jax.ShapeDtypeStruct((t, LANES), F32),
                   jax.ShapeDtypeStruct((t * ROW_PIECES, LANES), F32)],
        compiler_params=pltpu.CompilerParams(dimension_semantics=("arbitrary",)),
        name="router",
    )(xa, xb, lw["norm_ffn_g"], lw["router_w"], lw["router_wh"])


ROW_PIECES = D_MODEL // LANES
assert ROW_PIECES == SUBLANES
GATHER_UNROLL = 8


def _store_row_tiled(ref, x):
    for s in range(ROW_PIECES):
        ref[pl.ds(s, x.shape[0], stride=ROW_PIECES), :] = x[:, s * LANES:(s + 1) * LANES]


def _load_row_tiled(ref, n_rows):
    return jnp.concatenate([ref[pl.ds(s, n_rows, stride=ROW_PIECES), :] for s in range(ROW_PIECES)], axis=1)


def _start_row_gather(idx_ref, k, src_hbm, dst, sem):
    group = GATHER_UNROLL * ROW_PIECES

    def issue(it, carry):
        dst_base = pl.multiple_of(it * group, group)
        for u in range(GATHER_UNROLL):
            src_row = pl.multiple_of(idx_ref[k, it * GATHER_UNROLL + u] * ROW_PIECES, ROW_PIECES)
            pltpu.make_async_copy(src_hbm.at[pl.ds(src_row, ROW_PIECES)],
                                  dst.at[pl.ds(dst_base + u * ROW_PIECES, ROW_PIECES)], sem).start()
        return carry
    lax.fori_loop(0, dst.shape[0] // group, issue, 0)


def _wait_row_gather(src_hbm, dst, sem):
    pltpu.make_async_copy(src_hbm.at[pl.ds(0, dst.shape[0])], dst, sem).wait()


def _expert_kernel(n_fc, te_ref, na_ref, tokc_ref, tokn_ref, xn_hbm, wg_ref, wu_ref, wd_ref, ys_ref,
                   gbuf, xs, acc, sems):
    i, j = pl.program_id(0), pl.program_id(1)
    na = na_ref[0]
    slot = i % 2

    @pl.when(jnp.logical_and(i < na, j == 0))
    def _():
        @pl.when(i == 0)
        def _():
            _start_row_gather(tokc_ref, 0, xn_hbm, gbuf.at[0], sems.at[0])

        @pl.when(i + 1 < na)
        def _():
            _start_row_gather(tokn_ref, 0, xn_hbm, gbuf.at[1 - slot], sems.at[1 - slot])

        _wait_row_gather(xn_hbm, gbuf.at[slot], sems.at[slot])
        xs[...] = _load_row_tiled(gbuf.at[slot], TM).astype(BF16)

    def swiglu_piecewise(sink):
        h = _silu(jnp.dot(xs[...], wg_ref[...], preferred_element_type=F32)) * \
            jnp.dot(xs[...], wu_ref[...], preferred_element_type=F32)
        hb = h.astype(BF16)
        for q in range(D_MODEL // D_BRANCH):
            cs = slice(q * D_BRANCH, (q + 1) * D_BRANCH)
            sink(cs, jnp.dot(hb, wd_ref[:, cs], preferred_element_type=F32))

    def to_acc(first):
        def sink(cs, part):
            acc[:, cs] = part if first else acc[:, cs] + part
        return sink

    def to_out(with_acc):
        def sink(cs, part):
            tot = acc[:, cs] + part if with_acc else part
            for s in range(cs.start // LANES, cs.stop // LANES):
                lo = s * LANES - cs.start
                ys_ref[pl.ds(s, TM, stride=ROW_PIECES), :] = tot[:, lo:lo + LANES]
        return sink

    active = i < na
    if n_fc == 1:
        pl.when(active)(lambda: swiglu_piecewise(to_out(False)))
    else:
        pl.when(jnp.logical_and(active, j == 0))(lambda: swiglu_piecewise(to_acc(True)))
        if n_fc > 2:
            pl.when(jnp.logical_and(active, jnp.logical_and(j > 0, j < n_fc - 1)))(
                lambda: swiglu_piecewise(to_acc(False)))
        pl.when(jnp.logical_and(active, j == n_fc - 1))(lambda: swiglu_piecewise(to_out(True)))

    @pl.when(jnp.logical_and(i >= na, j == 0))
    def _():
        ys_ref[...] = jnp.zeros(ys_ref.shape, F32)


def _experts(xn_tiled, slot_tok, tile_expert, n_active, lw):
    n_tiles = tile_expert.shape[0]
    d_ff = lw["moe_w_gate"].shape[2]
    n_fc = d_ff // TF_EXPERT

    def fcol(i, j, na):
        return jnp.where(i < na[0], j, n_fc - 1)

    grid_spec = pltpu.PrefetchScalarGridSpec(
        num_scalar_prefetch=2,
        grid=(n_tiles, n_fc),
        in_specs=[
            pl.BlockSpec((None, 1, TM), lambda i, j, te, na: (i, 0, 0), memory_space=pltpu.SMEM),
            pl.BlockSpec((None, 1, TM), lambda i, j, te, na: (jnp.minimum(i + 1, n_tiles - 1), 0, 0),
                         memory_space=pltpu.SMEM),
            pl.BlockSpec(memory_space=pl.ANY),
            pl.BlockSpec((None, D_MODEL, TF_EXPERT), lambda i, j, te, na: (te[i], 0, fcol(i, j, na))),
            pl.BlockSpec((None, D_MODEL, TF_EXPERT), lambda i, j, te, na: (te[i], 0, fcol(i, j, na))),
            pl.BlockSpec((None, TF_EXPERT, D_MODEL), lambda i, j, te, na: (te[i], fcol(i, j, na), 0)),
        ],
        out_specs=pl.BlockSpec((TM * ROW_PIECES, LANES), lambda i, j, te, na: (i, 0)),
        scratch_shapes=[pltpu.VMEM((2, TM * ROW_PIECES, LANES), F32), pltpu.VMEM((TM, D_MODEL), BF16),
                        pltpu.VMEM((TM, D_MODEL), F32), pltpu.SemaphoreType.DMA((2,))],
    )
    return pl.pallas_call(
        functools.partial(_expert_kernel, n_fc),
        grid_spec=grid_spec,
        out_shape=jax.ShapeDtypeStruct((n_tiles * TM * ROW_PIECES, LANES), F32),
        compiler_params=pltpu.CompilerParams(dimension_semantics=("arbitrary", "arbitrary"),
                                             vmem_limit_bytes=VMEM_LIMIT),
        name="moe_experts",
    )(tile_expert, n_active, slot_tok, slot_tok, xn_tiled, lw["moe_w_gate"], lw["moe_w_up"], lw["moe_w_down"])


def _combine_kernel(final, n_steps, pos0c_ref, pos1c_ref, pos0n_ref, pos1n_ref, x_ref, p_ref, wgt_ref, ys_hbm,
                    gple_ref, wpg_ref, wpp_ref, gfin_ref, o_ref, buf, sems):
    i = pl.program_id(0)
    slot = i % 2

    def start(pos_refs, s):
        for k in range(TOP_K):
            _start_row_gather(pos_refs[k], 0, ys_hbm, buf.at[s, k], sems.at[s, k])

    @pl.when(i == 0)
    def _():
        start((pos0c_ref, pos1c_ref), 0)

    @pl.when(i + 1 < n_steps)
    def _():
        start((pos0n_ref, pos1n_ref), 1 - slot)

    for k in range(TOP_K):
        _wait_row_gather(ys_hbm, buf.at[slot, k], sems.at[slot, k])
    wgt = wgt_ref[...]
    y0 = _load_row_tiled(buf.at[slot, 0], TM)
    y1 = _load_row_tiled(buf.at[slot, 1], TM)
    x = x_ref[...] + (y0 * wgt[:, 0:1] + y1 * wgt[:, 1:2])
    x = _ple_tail(x, p_ref[...], gple_ref, wpg_ref, wpp_ref)
    o_ref[...] = _rms(x, gfin_ref[...]) if final else x


def _combine(xbuf, p_all, layer, pos, wgt, ys, lw, final_g, row0):
    blk0 = row0 // TM
    n_rows = xbuf.shape[0]
    n_steps = n_rows // TM
    rows = lambda i: (i, 0)
    tok_rows = lambda i: (blk0 + i, 0)
    def pos_spec(k, ahead):
        return pl.BlockSpec((None, None, 1, TM),
                            lambda i: (blk0 + jnp.minimum(i + ahead, n_steps - 1), k, 0, 0),
                            memory_space=pltpu.SMEM)
    final = final_g is not None
    gfin = final_g if final else lw["norm_ple_g"]
    return pl.pallas_call(
        functools.partial(_combine_kernel, final, n_steps),
        grid=(n_steps,),
        in_specs=[pos_spec(0, 0), pos_spec(1, 0), pos_spec(0, 1), pos_spec(1, 1),
                  pl.BlockSpec((TM, D_MODEL), rows), pl.BlockSpec((None, TM, D_PLE), lambda i: (layer, i, 0)),
                  pl.BlockSpec((TM, LANES), tok_rows), pl.BlockSpec(memory_space=pl.ANY),
                  _const_spec((1, D_MODEL)), _const_spec((D_MODEL, D_MODEL)),
                  _const_spec((D_PLE, D_MODEL)), _const_spec((1, D_MODEL))],
        out_specs=pl.BlockSpec((TM, D_MODEL), lambda i: (i, 0)),
        out_shape=jax.ShapeDtypeStruct((n_rows, D_MODEL), F32),
        scratch_shapes=[pltpu.VMEM((2, TOP_K, TM * ROW_PIECES, LANES), F32),
                        pltpu.SemaphoreType.DMA((2, TOP_K))],
        compiler_params=pltpu.CompilerParams(dimension_semantics=("arbitrary",),
                                             vmem_limit_bytes=VMEM_LIMIT),
        name="moe_combine_ple",
    )(pos, pos, pos, pos, xbuf, p_all, wgt, ys, lw["norm_ple_g"], lw["ple_w_gate"],
      lw["ple_w_proj"], gfin)


def _routing_tables(idx, n_tok):
    n_pairs = n_tok * TOP_K
    n_tiles = n_pairs // TM + N_EXPERTS
    experts = jnp.arange(N_EXPERTS, dtype=jnp.int32)
    e_flat = idx.reshape(-1)
    pair = jnp.arange(n_pairs, dtype=jnp.int32)
    pair_tok = (pair // (TOP_K * TM)) * TM + pair % TM
    onehot = (experts[:, None] == e_flat[None, :]).astype(jnp.int32)
    csum = jnp.cumsum(onehot, axis=1)
    sizes = csum[:, -1]
    padded = ((sizes + TM - 1) // TM) * TM
    pend = jnp.cumsum(padded)
    pstart = pend - padded
    pos = jnp.sum(onehot * (csum - 1 + pstart[:, None]), axis=0)
    fill_e = jnp.repeat(experts, TM)
    fill_r = jnp.tile(jnp.arange(TM, dtype=jnp.int32), N_EXPERTS)
    fill_need = jnp.sum((fill_e[:, None] == experts[None, :]) * (padded - sizes)[None, :], axis=1)
    keys = jnp.concatenate([e_flat, jnp.where(fill_r < fill_need, fill_e, N_EXPERTS)])
    toks = jnp.concatenate([pair_tok, jnp.zeros((N_EXPERTS * TM,), jnp.int32)])
    _, slot_tok = lax.sort((keys, toks), num_keys=1, is_stable=True)
    n_active = (pend[-1] // TM).astype(jnp.int32)
    tile_row = jnp.minimum(jnp.arange(n_tiles, dtype=jnp.int32), n_active - 1)
    tile_expert = jnp.minimum(
        jnp.sum((tile_row[:, None] * TM >= pend[None, :]).astype(jnp.int32), axis=1), N_EXPERTS - 1)
    return (slot_tok.reshape(n_tiles, 1, TM), tile_expert, n_active.reshape(1),
            pos.reshape(n_tok // TM, TOP_K, 1, TM))


def _sgu_tables(sgu_w, sgu_b, seq):
    r = np.arange(CHUNK)
    bias = jnp.repeat(sgu_b.T, GROUP_DIM, axis=1)
    if seq >= CHUNK:
        w = sgu_w
        mask = r[None, :] <= r[:, None]
    else:
        reps = CHUNK // seq
        w = jnp.tile(sgu_w[:, :seq, :seq], (1, reps, reps))
        mask = (r[:, None] // seq == r[None, :] // seq) & (r[None, :] % seq <= r[:, None] % seq)
        bias = jnp.tile(bias[:seq], (reps, 1))
    mask = np.tile(mask.astype(np.float32), (N_GROUPS, 1))
    return w.reshape(N_GROUPS * CHUNK, CHUNK), jnp.asarray(mask), bias


def _block_diag(w):
    out = jnp.zeros((D_BRANCH, D_BRANCH), w.dtype)
    for g in range(N_GROUPS):
        out = out.at[g * GROUP_DIM:(g + 1) * GROUP_DIM, g * GROUP_DIM:(g + 1) * GROUP_DIM].set(w[g])
    return out


def kernel(x_prompt, x_sample, state_conv_a, state_conv_b, state_pool, p_prompt, p_sample, norm_mix_g, w_in, conv_a_w, conv_b_w, conv_b_bias, conv_b_ln_g, conv_b_ln_b, pool_w, pool_scale, sgu_ln_g, sgu_ln_b, sgu_w, sgu_b, w_branch, w_out, norm_ffn_g, ffn_w_gate, ffn_w_up, ffn_w_down, router_w, moe_w_gate, moe_w_up, moe_w_down, norm_ple_g, ple_w_gate, ple_w_proj, final_norm_g):
    batch, seq, _ = x_prompt.shape
    nseq, dseq, _ = x_sample.shape
    depth = w_in.shape[0]
    assert depth % 2 == 0
    assert PAST_LEN + 1 >= POOL_MAX
    assert seq % TL == 0 and TL % CHUNK == 0 and TL >= CONV_B_W + 1
    n_prompt, n_sample = batch * seq, nseq * dseq
    n_tok = n_prompt + n_sample
    assert n_prompt % TM == 0 and n_sample % TM == 0

    xp, xs = x_prompt.reshape(n_prompt, D_MODEL), x_sample.reshape(n_sample, D_MODEL)
    pp, ps = p_prompt.reshape(depth, n_prompt, D_PLE), p_sample.reshape(depth, n_sample, D_PLE)
    row = lambda a: a.reshape(1, -1)
    zeros = jnp.zeros((D_BRANCH,), F32)

    states_p, states_s = [], []
    y_prompt = y_sample = None
    for i in range(depth):
        sw_p, sm_p, sb_p = _sgu_tables(sgu_w[i], sgu_b[i], seq)
        sw_s, sm_s, sb_s = _sgu_tables(sgu_w[i], sgu_b[i], dseq)
        lw = {
            "norm_mix_g": row(norm_mix_g[i]), "w_in": w_in[i].astype(BF16),
            "conv_a_w": conv_a_w[i], "conv_b_w": conv_b_w[i],
            "vec": jnp.stack([conv_b_bias[i], conv_b_ln_g[i], conv_b_ln_b[i], pool_scale[i],
                              sgu_ln_g[i], sgu_ln_b[i], zeros, zeros]),
            "pool_bd": _block_diag(pool_w[i]).astype(BF16),
            "sgu_w_p": sw_p, "sgu_mask_p": sm_p, "sgu_bias_p": sb_p,
            "sgu_w_s": sw_s, "sgu_mask_s": sm_s, "sgu_bias_s": sb_s,
            "w_branch": w_branch[i].astype(BF16), "w_out": w_out[i].astype(BF16),
            "norm_ffn_g": row(norm_ffn_g[i]), "norm_ple_g": row(norm_ple_g[i]),
            "ple_w_gate": ple_w_gate[i].astype(BF16), "ple_w_proj": ple_w_proj[i].astype(BF16),
        }
        in_place = i > 0
        xp, a_p, b_p, pool_p, v_p = _mixer_prompt(xp, lw, batch, seq, in_place)
        xs, a_s, b_s, pool_s, v_s = _mixer_sample(xs, lw, state_conv_a[i], state_conv_b[i], state_pool[i],
                                                  nseq, dseq, in_place)
        states_p.append((a_p, b_p, pool_p, v_p))
        states_s.append((a_s, b_s, pool_s, v_s))

        j = i // 2
        last = i == depth - 1
        if i % 2 == 0:
            lw.update({"ffn_w_gate": ffn_w_gate[j].astype(BF16), "ffn_w_up": ffn_w_up[j].astype(BF16),
                       "ffn_w_down": ffn_w_down[j].astype(BF16)})
            xp = _dense_ffn(xp, pp, i, lw)
            xs = _dense_ffn(xs, ps, i, lw)
        else:
            r_hi = router_w[j].astype(BF16)
            r_lo = (router_w[j] - r_hi.astype(F32)).astype(BF16)
            lane_pad = lambda a: jnp.pad(a, ((0, 0), (0, LANES - a.shape[1])))
            lw.update({"router_w": lane_pad(jnp.concatenate([r_hi, r_lo], axis=1)), "router_wh": lane_pad(r_hi),
                       "moe_w_gate": moe_w_gate[j].astype(BF16), "moe_w_up": moe_w_up[j].astype(BF16),
                       "moe_w_down": moe_w_down[j].astype(BF16)})
            idx, wgt, xn_tiled = _router(xp, xs, lw)
            slot_tok, tile_expert, n_active, pos = _routing_tables(idx, n_tok)
            ys = _experts(xn_tiled, slot_tok, tile_expert, n_active, lw)
            fin = row(final_norm_g) if last else None
            xp = _combine(xp, pp, i, pos, wgt, ys, lw, fin, 0)
            xs = _combine(xs, ps, i, pos, wgt, ys, lw, fin, n_prompt)
            if last:
                y_prompt, y_sample = xp, xs

    stack = lambda k, sts: jnp.stack([s[k] for s in sts])
    return (y_prompt.reshape(batch, seq, D_MODEL), y_sample.reshape(nseq, dseq, D_MODEL),
            stack(0, states_p), stack(1, states_p), stack(2, states_p), stack(3, states_p),
            stack(0, states_s), stack(1, states_s), stack(2, states_s), stack(3, states_s))
```

```python
import functools

import jax
import jax.numpy as jnp
import numpy as np
from jax import lax
from jax.experimental import pallas as pl
from jax.experimental.pallas import tpu as pltpu

F32 = jnp.float32
BF16 = jnp.bfloat16

D_MODEL = 1024
N_BRANCH = 4
D_BRANCH = 256
N_GROUPS = 4
GROUP_DIM = 64
CONV_A_W = 3
CONV_B_W = 31
POOL_WINDOWS = (2, 4, 8, 16)
POOL_MAX = 16
CHUNK = 128
PAST_LEN = 16384
D_PLE = 256
N_EXPERTS = 8
TOP_K = 2
EPS = 1e-6

OFF_A_H = 0
OFF_A_B = 256
OFF_A_C = 512
OFF_CF_A = 768
OFF_CF_B = 1024
OFF_POOL = 1280
OFF_U = 1536
OFF_V = 1792
OFF_GATE = 2048

LANES = 128
SUBLANES = 8
TM = 512
TL = 512
NB = 64
TF_EXPERT = 1792
ROW_CHUNK = 32
SEQ_CHUNK = 8
VMEM_LIMIT = 56 * 1024 * 1024

V_CB_BIAS, V_CB_LN_G, V_CB_LN_B, V_POOL_SCALE, V_SGU_LN_G, V_SGU_LN_B = range(6)


def _rms(x, g):
    return x * lax.rsqrt(jnp.mean(x * x, axis=-1, keepdims=True) + EPS) * g


def _ln(x, g, b):
    xc = x - jnp.mean(x, axis=-1, keepdims=True)
    var = jnp.mean(xc * xc, axis=-1, keepdims=True)
    return xc * lax.rsqrt(var + EPS) * g + b


def _bdot(a, w):
    return jnp.dot(a.astype(BF16), w, preferred_element_type=F32)


def _silu(x):
    return x * jax.nn.sigmoid(x)


def _const_spec(shape):
    nd = len(shape)
    return pl.BlockSpec(shape, lambda *_: (0,) * nd, pipeline_mode=pl.Buffered(1))


def _sgu(u, v, swm, sbias):
    m = v.shape[0]
    group = lax.broadcasted_iota(jnp.int32, (CHUNK, D_BRANCH), 1) // GROUP_DIM
    outs = []
    for c in range(m // CHUNK):
        vc = v[c * CHUNK:(c + 1) * CHUNK].astype(BF16)
        r = jnp.dot(swm, vc, preferred_element_type=F32)
        s = sbias
        for g in range(N_GROUPS):
            s = s + jnp.where(group == g, r[g * CHUNK:(g + 1) * CHUNK], 0.0)
        outs.append(s)
    return u * jnp.concatenate(outs, axis=0)


GATE_PIECES_PER_BRANCH = D_MODEL // D_BRANCH
N_GATE_PIECES = N_BRANCH * GATE_PIECES_PER_BRANCH


def _gate_emitter(xnb, win_ref):
    pieces = []

    def emit(n=1):
        for _ in range(n):
            if len(pieces) < N_GATE_PIECES:
                lo = OFF_GATE + len(pieces) * D_BRANCH
                z = jnp.dot(xnb, win_ref[:, lo:lo + D_BRANCH], preferred_element_type=F32)
                pieces.append(jax.nn.sigmoid(z))
    return pieces, emit


def _gated_merge(x, branches, gates, wbr_ref, wout_ref):
    bb = [b.astype(BF16) for b in branches]
    cols = []
    for q in range(GATE_PIECES_PER_BRANCH):
        cs = slice(q * D_BRANCH, (q + 1) * D_BRANCH)
        m = None
        for i in range(N_BRANCH):
            t = gates[i * GATE_PIECES_PER_BRANCH + q] * \
                jnp.dot(bb[i], wbr_ref[i, :, cs], preferred_element_type=F32)
            m = t if m is None else m + t
        cols.append(m.astype(BF16))
    return x + jnp.dot(jnp.concatenate(cols, axis=1), wout_ref[...], preferred_element_type=F32)


def _pool_lane_windows(half):
    small, big = POOL_WINDOWS[2 * half], POOL_WINDOWS[2 * half + 1]
    lane = lax.broadcasted_iota(jnp.int32, (1, LANES), 1)
    is_big = lane >= GROUP_DIM
    return small, big, is_big


def _mixer_prompt_kernel(x_ref, g_ref, win_ref, caw_ref, cbw_ref, vec_ref, pw_ref, sw_ref, smask_ref,
                         sbias_ref, wbr_ref, wout_ref,
                         xo_ref, sta_ref, stb_ref, stp_ref, stv_ref,
                         sa, sb, sp):
    i = pl.program_id(1)
    ha, hb, hp = SUBLANES, 4 * SUBLANES, 2 * SUBLANES

    @pl.when(i == 0)
    def _():
        sa[0:ha, :] = jnp.zeros((ha, D_BRANCH), F32)
        sb[0:hb, :] = jnp.zeros((hb, D_BRANCH), F32)
        sp[0:hp, :] = jnp.zeros((hp, D_BRANCH), F32)

    @pl.when(i > 0)
    def _():
        sa[0:ha, :] = sa[TL:TL + ha, :]
        sb[0:hb, :] = sb[TL:TL + hb, :]
        sp[0:hp, :] = sp[TL:TL + hp, :]

    x = x_ref[...]
    xnb = _rms(x, g_ref[...]).astype(BF16)
    vec = vec_ref[...]

    def zc(off):
        return jnp.dot(xnb, win_ref[:, off:off + D_BRANCH], preferred_element_type=F32)

    gates, emit_gate = _gate_emitter(xnb, win_ref)
    n_chunks = TL // ROW_CHUNK

    def spread(c, total):
        return ((c + 1) * total) // n_chunks - (c * total) // n_chunks

    def tap_conv(s_ref, base, w_ref, ntaps, n_gates=0):
        outs = []
        for c in range(n_chunks):
            acc = None
            for k in range(ntaps):
                term = s_ref[pl.ds(base + k + c * ROW_CHUNK, ROW_CHUNK), :] * w_ref[k:k + 1, :]
                acc = term if acc is None else acc + term
            outs.append(acc)
            emit_gate(spread(c, n_gates))
        return jnp.concatenate(outs, axis=0)

    sa[ha:ha + TL, :] = zc(OFF_A_C) * zc(OFF_A_H)
    br_a = zc(OFF_A_B) * tap_conv(sa, ha - (CONV_A_W - 1), caw_ref, CONV_A_W)
    sta_ref[0] = sa[TL + ha - (CONV_A_W - 1):TL + ha, :]

    sb[hb:hb + TL, :] = zc(OFF_CF_A) * jax.nn.sigmoid(zc(OFF_CF_B))
    yb = tap_conv(sb, hb - (CONV_B_W - 1), cbw_ref, CONV_B_W, n_gates=3 * N_GATE_PIECES // 4)
    yb = yb + vec[V_CB_BIAS:V_CB_BIAS + 1]
    br_b = _silu(_ln(yb, vec[V_CB_LN_G:V_CB_LN_G + 1], vec[V_CB_LN_B:V_CB_LN_B + 1]))
    stb_ref[0] = sb[TL + hb - (CONV_B_W - 1):TL + hb, :]

    sp[hp:hp + TL, :] = zc(OFF_POOL)
    halves = []
    for half in range(2):
        small, big, is_big = _pool_lane_windows(half)
        lanes = slice(half * LANES, (half + 1) * LANES)
        wl = jnp.where(is_big, float(big), float(small))
        outs = []
        for c in range(n_chunks):
            r0 = hp + c * ROW_CHUNK
            acc = None
            for j in range(big):
                seg = sp[pl.ds(r0 - j, ROW_CHUNK), lanes]
                if j >= small:
                    seg = jnp.where(is_big, seg, 0.0)
                acc = seg if acc is None else acc + seg
            pos = i * TL + c * ROW_CHUNK + lax.broadcasted_iota(jnp.int32, (ROW_CHUNK, LANES), 0)
            cnt = jnp.minimum(wl, (pos + 1).astype(F32))
            outs.append(acc / cnt - sp[pl.ds(r0, ROW_CHUNK), lanes])
            if half == 1:
                emit_gate(spread(c, N_GATE_PIECES // 4))
        halves.append(jnp.concatenate(outs, axis=0))
    pooled = jnp.concatenate(halves, axis=1)
    br_c = _bdot(pooled, pw_ref[...]) * vec[V_POOL_SCALE:V_POOL_SCALE + 1]
    stp_ref[0] = sp[TL + hp - (POOL_MAX - 1):TL + hp, :]

    v = _ln(zc(OFF_V), vec[V_SGU_LN_G:V_SGU_LN_G + 1], vec[V_SGU_LN_B:V_SGU_LN_B + 1])
    stv_ref[0] = v[TL - CHUNK:TL]
    swm = jnp.where(smask_ref[...] > 0, sw_ref[...], 0.0).astype(BF16)
    br_d = _sgu(zc(OFF_U), v, swm, sbias_ref[...])

    emit_gate(N_GATE_PIECES)
    xo_ref[...] = _gated_merge(x, (br_a, br_b, br_c, br_d), gates, wbr_ref, wout_ref)


def _mixer_sample_kernel(x_ref, g_ref, win_ref, caw_ref, cbw_ref, vec_ref, pw_ref, sw_ref, smask_ref,
                         sbias_ref, wbr_ref, wout_ref, ina_ref, inb_ref, inp_ref,
                         xo_ref, sta_ref, stb_ref, stp_ref, stv_ref,
                         sa, sb, sp):
    seq = SUBLANES
    ha, hb, hp = SUBLANES, 4 * SUBLANES, 2 * SUBLANES
    sa[:, ha - (CONV_A_W - 1):ha, :] = ina_ref[...]
    sb[:, hb - (CONV_B_W - 1):hb, :] = inb_ref[...]
    sp[:, hp - (POOL_MAX - 1):hp, :] = inp_ref[...]

    x = x_ref[...]
    xnb = _rms(x, g_ref[...]).astype(BF16)
    vec = vec_ref[...]

    def zc(off):
        return jnp.dot(xnb, win_ref[:, off:off + D_BRANCH], preferred_element_type=F32)

    def to3(a):
        return a.reshape(NB, seq, D_BRANCH)

    def to2(a):
        return a.reshape(NB * seq, D_BRANCH)

    gates, emit_gate = _gate_emitter(xnb, win_ref)
    n_chunks = NB // SEQ_CHUNK

    def tap_conv(s_ref, base, w_ref, ntaps, gates_per_chunk=0):
        outs = []
        for c in range(n_chunks):
            acc = None
            for k in range(ntaps):
                seg = s_ref[c * SEQ_CHUNK:(c + 1) * SEQ_CHUNK, pl.ds(base + k, seq), :]
                term = seg * w_ref[k:k + 1, :][None]
                acc = term if acc is None else acc + term
            outs.append(acc)
            emit_gate(gates_per_chunk)
        return to2(jnp.concatenate(outs, axis=0))

    sa[:, ha:ha + seq, :] = to3(zc(OFF_A_C) * zc(OFF_A_H))
    br_a = zc(OFF_A_B) * tap_conv(sa, ha - (CONV_A_W - 1), caw_ref, CONV_A_W)
    sta_ref[...] = sa[:, ha + seq - (CONV_A_W - 1):ha + seq, :]

    sb[:, hb:hb + seq, :] = to3(zc(OFF_CF_A) * jax.nn.sigmoid(zc(OFF_CF_B)))
    yb = tap_conv(sb, hb - (CONV_B_W - 1), cbw_ref, CONV_B_W, gates_per_chunk=-(-N_GATE_PIECES // n_chunks))
    yb = yb + vec[V_CB_BIAS:V_CB_BIAS + 1]
    br_b = _silu(_ln(yb, vec[V_CB_LN_G:V_CB_LN_G + 1], vec[V_CB_LN_B:V_CB_LN_B + 1]))
    stb_ref[...] = sb[:, hb + seq - (CONV_B_W - 1):hb + seq, :]

    sp[:, hp:hp + seq, :] = to3(zc(OFF_POOL))
    halves = []
    for half in range(2):
        small, big, is_big = _pool_lane_windows(half)
        lanes = slice(half * LANES, (half + 1) * LANES)
        wl = jnp.where(is_big, float(big), float(small))[None]
        outs = []
        for c in range(NB // SEQ_CHUNK):
            rows = slice(c * SEQ_CHUNK, (c + 1) * SEQ_CHUNK)
            acc = None
            for j in range(big):
                seg = sp[rows, pl.ds(hp - j, seq), lanes]
                if j >= small:
                    seg = jnp.where(is_big[None], seg, 0.0)
                acc = seg if acc is None else acc + seg
            outs.append(acc / wl - sp[rows, pl.ds(hp, seq), lanes])
        halves.append(jnp.concatenate(outs, axis=0).reshape(NB * seq, LANES))
    pooled = jnp.concatenate(halves, axis=1)
    br_c = _bdot(pooled, pw_ref[...]) * vec[V_POOL_SCALE:V_POOL_SCALE + 1]
    stp_ref[...] = sp[:, hp + seq - (POOL_MAX - 1):hp + seq, :]

    v = _ln(zc(OFF_V), vec[V_SGU_LN_G:V_SGU_LN_G + 1], vec[V_SGU_LN_B:V_SGU_LN_B + 1])
    stv_ref[...] = to3(v)
    swm = jnp.where(smask_ref[...] > 0, sw_ref[...], 0.0).astype(BF16)
    br_d = _sgu(zc(OFF_U), v, swm, sbias_ref[...])

    emit_gate(N_GATE_PIECES)
    xo_ref[...] = _gated_merge(x, (br_a, br_b, br_c, br_d), gates, wbr_ref, wout_ref)


def _mixer_weight_specs():
    return [
        _const_spec((1, D_MODEL)),
        _const_spec((D_MODEL, OFF_GATE + N_BRANCH * D_MODEL)),
        _const_spec((CONV_A_W, D_BRANCH)),
        _const_spec((CONV_B_W, D_BRANCH)),
        _const_spec((SUBLANES, D_BRANCH)),
        _const_spec((D_BRANCH, D_BRANCH)),
        _const_spec((N_GROUPS * CHUNK, CHUNK)),
        _const_spec((N_GROUPS * CHUNK, CHUNK)),
        _const_spec((CHUNK, D_BRANCH)),
        _const_spec((N_BRANCH, D_BRANCH, D_MODEL)),
        _const_spec((D_MODEL, D_MODEL)),
    ]


def _mixer_prompt(xbuf, lw, batch, seq, in_place):
    nt = seq // TL
    rows = lambda b, i: (b * nt + i, 0)
    st = lambda n: pl.BlockSpec((1, n, D_BRANCH), lambda b, i: (b, 0, 0))
    return pl.pallas_call(
        _mixer_prompt_kernel,
        grid=(batch, nt),
        in_specs=[pl.BlockSpec((TL, D_MODEL), rows)] + _mixer_weight_specs(),
        out_specs=[pl.BlockSpec((TL, D_MODEL), rows), st(CONV_A_W - 1), st(CONV_B_W - 1),
                   st(POOL_MAX - 1), st(CHUNK)],
        out_shape=[jax.ShapeDtypeStruct(xbuf.shape, F32),
                   jax.ShapeDtypeStruct((batch, CONV_A_W - 1, D_BRANCH), F32),
                   jax.ShapeDtypeStruct((batch, CONV_B_W - 1, D_BRANCH), F32),
                   jax.ShapeDtypeStruct((batch, POOL_MAX - 1, D_BRANCH), F32),
                   jax.ShapeDtypeStruct((batch, CHUNK, D_BRANCH), F32)],
        scratch_shapes=[pltpu.VMEM((TL + SUBLANES, D_BRANCH), F32),
                        pltpu.VMEM((TL + 4 * SUBLANES, D_BRANCH), F32),
                        pltpu.VMEM((TL + 2 * SUBLANES, D_BRANCH), F32)],
        input_output_aliases={0: 0} if in_place else {},
        compiler_params=pltpu.CompilerParams(dimension_semantics=("arbitrary", "arbitrary"),
                                             vmem_limit_bytes=VMEM_LIMIT),
        name="mixer_prompt",
    )(xbuf, lw["norm_mix_g"], lw["w_in"], lw["conv_a_w"], lw["conv_b_w"], lw["vec"], lw["pool_bd"],
      lw["sgu_w_p"], lw["sgu_mask_p"], lw["sgu_bias_p"], lw["w_branch"], lw["w_out"])


def _mixer_sample(xbuf, lw, st_a, st_b, st_p, nseq, seq, in_place):
    assert seq == SUBLANES and nseq % NB == 0
    rows = lambda i: (i, 0)
    st = lambda n: pl.BlockSpec((NB, n, D_BRANCH), lambda i: (i, 0, 0))
    return pl.pallas_call(
        _mixer_sample_kernel,
        grid=(nseq // NB,),
        in_specs=[pl.BlockSpec((NB * seq, D_MODEL), rows)] + _mixer_weight_specs()
                 + [st(CONV_A_W - 1), st(CONV_B_W - 1), st(POOL_MAX - 1)],
        out_specs=[pl.BlockSpec((NB * seq, D_MODEL), rows), st(CONV_A_W - 1), st(CONV_B_W - 1),
                   st(POOL_MAX - 1), st(seq)],
        out_shape=[jax.ShapeDtypeStruct(xbuf.shape, F32),
                   jax.ShapeDtypeStruct((nseq, CONV_A_W - 1, D_BRANCH), F32),
                   jax.ShapeDtypeStruct((nseq, CONV_B_W - 1, D_BRANCH), F32),
                   jax.ShapeDtypeStruct((nseq, POOL_MAX - 1, D_BRANCH), F32),
                   jax.ShapeDtypeStruct((nseq, seq, D_BRANCH), F32)],
        scratch_shapes=[pltpu.VMEM((NB, 2 * SUBLANES, D_BRANCH), F32),
                        pltpu.VMEM((NB, 5 * SUBLANES, D_BRANCH), F32),
                        pltpu.VMEM((NB, 3 * SUBLANES, D_BRANCH), F32)],
        input_output_aliases={0: 0} if in_place else {},
        compiler_params=pltpu.CompilerParams(dimension_semantics=("arbitrary",),
                                             vmem_limit_bytes=VMEM_LIMIT),
        name="mixer_sample",
    )(xbuf, lw["norm_mix_g"], lw["w_in"], lw["conv_a_w"], lw["conv_b_w"], lw["vec"], lw["pool_bd"],
      lw["sgu_w_s"], lw["sgu_mask_s"], lw["sgu_bias_s"], lw["w_branch"], lw["w_out"],
      st_a, st_b, st_p)


def _ple_tail(x, p, gple_ref, wpg_ref, wpp_ref):
    gate = jax.nn.sigmoid(_bdot(_rms(x, gple_ref[...]), wpg_ref[...]))
    return x + gate * _bdot(p, wpp_ref[...])


def _dense_ffn_kernel(n_chunks, x_ref, p_ref, gffn_ref, wg_ref, wu_ref, wd_ref, gple_ref, wpg_ref, wpp_ref,
                      xo_ref):
    x = x_ref[...]
    xnb = _rms(x, gffn_ref[...]).astype(BF16)
    tf = wg_ref.shape[1] // n_chunks
    f = None
    for c in range(n_chunks):
        cols = slice(c * tf, (c + 1) * tf)
        h = _silu(jnp.dot(xnb, wg_ref[:, cols], preferred_element_type=F32)) * \
            jnp.dot(xnb, wu_ref[:, cols], preferred_element_type=F32)
        t = _bdot(h, wd_ref[cols, :])
        f = t if f is None else f + t
    xo_ref[...] = _ple_tail(x + f, p_ref[...], gple_ref, wpg_ref, wpp_ref)


def _dense_ffn(xbuf, p_all, layer, lw):
    t = xbuf.shape[0]
    d_ff = lw["ffn_w_gate"].shape[1]
    rows = lambda i: (i, 0)
    return pl.pallas_call(
        functools.partial(_dense_ffn_kernel, 2),
        grid=(t // TM,),
        in_specs=[pl.BlockSpec((TM, D_MODEL), rows), pl.BlockSpec((None, TM, D_PLE), lambda i: (layer, i, 0)),
                  _const_spec((1, D_MODEL)), _const_spec((D_MODEL, d_ff)), _const_spec((D_MODEL, d_ff)),
                  _const_spec((d_ff, D_MODEL)), _const_spec((1, D_MODEL)),
                  _const_spec((D_MODEL, D_MODEL)), _const_spec((D_PLE, D_MODEL))],
        out_specs=pl.BlockSpec((TM, D_MODEL), rows),
        out_shape=jax.ShapeDtypeStruct(xbuf.shape, F32),
        input_output_aliases={0: 0},
        compiler_params=pltpu.CompilerParams(dimension_semantics=("arbitrary",),
                                             vmem_limit_bytes=VMEM_LIMIT),
        name="dense_ffn_ple",
    )(xbuf, p_all, lw["norm_ffn_g"], lw["ffn_w_gate"], lw["ffn_w_up"], lw["ffn_w_down"],
      lw["norm_ple_g"], lw["ple_w_gate"], lw["ple_w_proj"])


def _router_kernel(n_first, xa_ref, xb_ref, g_ref, wr_ref, wrh_ref, idx_ref, wgt_ref, xn_ref):
    x = jnp.where(pl.program_id(0) < n_first, xa_ref[...], xb_ref[...])
    xn = _rms(x, g_ref[...])
    _store_row_tiled(xn_ref, xn)
    xh = xn.astype(BF16)
    xl = (xn - xh.astype(F32)).astype(BF16)
    d = jnp.dot(xh, wr_ref[...], preferred_element_type=F32) + \
        jnp.dot(xl, wrh_ref[...], preferred_element_type=F32)
    logits = d + pltpu.roll(d, LANES - N_EXPERTS, axis=1)
    lane = lax.broadcasted_iota(jnp.int32, logits.shape, 1)
    lane_f = lane.astype(F32)
    logits = jnp.where(lane < N_EXPERTS, logits, -jnp.inf)
    m1 = jnp.max(logits, axis=-1, keepdims=True)
    i1 = jnp.min(jnp.where(logits == m1, lane_f, float(LANES)), axis=-1, keepdims=True)
    rest = jnp.where(lane_f == i1, -jnp.inf, logits)
    m2 = jnp.max(rest, axis=-1, keepdims=True)
    i2 = jnp.min(jnp.where(rest == m2, lane_f, float(LANES)), axis=-1, keepdims=True)
    e2 = jnp.exp(m2 - m1)
    den = 1.0 + e2
    picks = jnp.where(lane == 0, i1, jnp.where(lane == 1, i2, 0.0))
    idx_ref[...] = jnp.transpose(picks)[0:TOP_K, :].astype(jnp.int32)
    wgt_ref[...] = jnp.where(lane == 0, 1.0 / den, jnp.where(lane == 1, e2 / den, 0.0))


def _router(xa, xb, lw):
    na, nb = xa.shape[0] // TM, xb.shape[0] // TM
    t = xa.shape[0] + xb.shape[0]
    rows = lambda i: (i, 0)
    return pl.pallas_call(
        functools.partial(_router_kernel, na),
        grid=(na + nb,),
        in_specs=[pl.BlockSpec((TM, D_MODEL), lambda i: (jnp.minimum(i, na - 1), 0)),
                  pl.BlockSpec((TM, D_MODEL), lambda i: (jnp.maximum(i - na, 0), 0)),
                  _const_spec((1, D_MODEL)), _const_spec((D_MODEL, LANES)), _const_spec((D_MODEL, LANES))],
        out_specs=[pl.BlockSpec((None, TOP_K, TM), lambda i: (i, 0, 0)), pl.BlockSpec((TM, LANES), rows),
                   pl.BlockSpec((TM * ROW_PIECES, LANES), rows)],
        out_shape=[jax.ShapeDtypeStruct((na + nb, TOP_K, TM), jnp.int32), jax.ShapeDtypeStruct((t, LANES), F32),
                   jax.ShapeDtypeStruct((t * ROW_PIECES, LANES), F32)],
        compiler_params=pltpu.CompilerParams(dimension_semantics=("arbitrary",)),
        name="router",
    )(xa, xb, lw["norm_ffn_g"], lw["router_w"], lw["router_wh"])


ROW_PIECES = D_MODEL // LANES
assert ROW_PIECES == SUBLANES
GATHER_UNROLL = 8


def _store_row_tiled(ref, x):
    for s in range(ROW_PIECES):
        ref[pl.ds(s, x.shape[0], stride=ROW_PIECES), :] = x[:, s * LANES:(s + 1) * LANES]


def _load_row_tiled(ref, n_rows):
    return jnp.concatenate([ref[pl.ds(s, n_rows, stride=ROW_PIECES), :] for s in range(ROW_PIECES)], axis=1)


def _start_row_gather(idx_ref, k, src_hbm, dst, sem, priorities=(0,)):
    group = GATHER_UNROLL * ROW_PIECES

    def issue(it, carry):
        dst_base = pl.multiple_of(it * group, group)
        for u in range(GATHER_UNROLL):
            src_row = pl.multiple_of(idx_ref[k, it * GATHER_UNROLL + u] * ROW_PIECES, ROW_PIECES)
            pltpu.make_async_copy(src_hbm.at[pl.ds(src_row, ROW_PIECES)],
                                  dst.at[pl.ds(dst_base + u * ROW_PIECES, ROW_PIECES)],
                                  sem).start(priority=priorities[u % len(priorities)])
        return carry
    lax.fori_loop(0, dst.shape[0] // group, issue, 0)


def _wait_row_gather(src_hbm, dst, sem):
    pltpu.make_async_copy(src_hbm.at[pl.ds(0, dst.shape[0])], dst, sem).wait()


def _expert_kernel(n_fc, te_ref, na_ref, tokc_ref, tokn_ref, xn_hbm, wg_ref, wu_ref, wd_ref, ys_ref,
                   gbuf, xs, acc, sems):
    i, j = pl.program_id(0), pl.program_id(1)
    na = na_ref[0]
    slot = i % 2

    @pl.when(jnp.logical_and(i < na, j == 0))
    def _():
        @pl.when(i == 0)
        def _():
            _start_row_gather(tokc_ref, 0, xn_hbm, gbuf.at[0], sems.at[0])

        @pl.when(i + 1 < na)
        def _():
            _start_row_gather(tokn_ref, 0, xn_hbm, gbuf.at[1 - slot], sems.at[1 - slot], priorities=(1,))

        _wait_row_gather(xn_hbm, gbuf.at[slot], sems.at[slot])
        xs[...] = _load_row_tiled(gbuf.at[slot], TM).astype(BF16)

    def swiglu_piecewise(sink):
        h = _silu(jnp.dot(xs[...], wg_ref[...], preferred_element_type=F32)) * \
            jnp.dot(xs[...], wu_ref[...], preferred_element_type=F32)
        hb = h.astype(BF16)
        for q in range(D_MODEL // D_BRANCH):
            cs = slice(q * D_BRANCH, (q + 1) * D_BRANCH)
            sink(cs, jnp.dot(hb, wd_ref[:, cs], preferred_element_type=F32))

    def to_acc(first):
        def sink(cs, part):
            acc[:, cs] = part if first else acc[:, cs] + part
        return sink

    def to_out(with_acc):
        def sink(cs, part):
            tot = acc[:, cs] + part if with_acc else part
            for s in range(cs.start // LANES, cs.stop // LANES):
                lo = s * LANES - cs.start
                ys_ref[pl.ds(s, TM, stride=ROW_PIECES), :] = tot[:, lo:lo + LANES]
        return sink

    active = i < na
    if n_fc == 1:
        pl.when(active)(lambda: swiglu_piecewise(to_out(False)))
    else:
        pl.when(jnp.logical_and(active, j == 0))(lambda: swiglu_piecewise(to_acc(True)))
        if n_fc > 2:
            pl.when(jnp.logical_and(active, jnp.logical_and(j > 0, j < n_fc - 1)))(
                lambda: swiglu_piecewise(to_acc(False)))
        pl.when(jnp.logical_and(active, j == n_fc - 1))(lambda: swiglu_piecewise(to_out(True)))

    @pl.when(jnp.logical_and(i >= na, j == 0))
    def _():
        ys_ref[...] = jnp.zeros(ys_ref.shape, F32)


def _experts(xn_tiled, slot_tok, tile_expert, n_active, lw):
    n_tiles = tile_expert.shape[0]
    d_ff = lw["moe_w_gate"].shape[2]
    n_fc = d_ff // TF_EXPERT

    def fcol(i, j, na):
        return jnp.where(i < na[0], j, n_fc - 1)

    grid_spec = pltpu.PrefetchScalarGridSpec(
        num_scalar_prefetch=2,
        grid=(n_tiles, n_fc),
        in_specs=[
            pl.BlockSpec((None, 1, TM), lambda i, j, te, na: (i, 0, 0), memory_space=pltpu.SMEM),
            pl.BlockSpec((None, 1, TM), lambda i, j, te, na: (jnp.minimum(i + 1, n_tiles - 1), 0, 0),
                         memory_space=pltpu.SMEM),
            pl.BlockSpec(memory_space=pl.ANY),
            pl.BlockSpec((None, D_MODEL, TF_EXPERT), lambda i, j, te, na: (te[i], 0, fcol(i, j, na))),
            pl.BlockSpec((None, D_MODEL, TF_EXPERT), lambda i, j, te, na: (te[i], 0, fcol(i, j, na))),
            pl.BlockSpec((None, TF_EXPERT, D_MODEL), lambda i, j, te, na: (te[i], fcol(i, j, na), 0)),
        ],
        out_specs=pl.BlockSpec((TM * ROW_PIECES, LANES), lambda i, j, te, na: (i, 0)),
        scratch_shapes=[pltpu.VMEM((2, TM * ROW_PIECES, LANES), F32), pltpu.VMEM((TM, D_MODEL), BF16),
                        pltpu.VMEM((TM, D_MODEL), F32), pltpu.SemaphoreType.DMA((2,))],
    )
    return pl.pallas_call(
        functools.partial(_expert_kernel, n_fc),
        grid_spec=grid_spec,
        out_shape=jax.ShapeDtypeStruct((n_tiles * TM * ROW_PIECES, LANES), F32),
        compiler_params=pltpu.CompilerParams(dimension_semantics=("arbitrary", "arbitrary"),
                                             vmem_limit_bytes=VMEM_LIMIT),
        name="moe_experts",
    )(tile_expert, n_active, slot_tok, slot_tok, xn_tiled, lw["moe_w_gate"], lw["moe_w_up"], lw["moe_w_down"])


def _combine_kernel(final, n_steps, pos0c_ref, pos1c_ref, pos0n_ref, pos1n_ref, x_ref, p_ref, wgt_ref, ys_hbm,
                    gple_ref, wpg_ref, wpp_ref, gfin_ref, o_ref, buf, sems):
    i = pl.program_id(0)
    slot = i % 2

    def start(pos_refs, s):
        for k in range(TOP_K):
            _start_row_gather(pos_refs[k], 0, ys_hbm, buf.at[s, k], sems.at[s, k], priorities=(0, 1))

    @pl.when(i == 0)
    def _():
        start((pos0c_ref, pos1c_ref), 0)

    @pl.when(i + 1 < n_steps)
    def _():
        start((pos0n_ref, pos1n_ref), 1 - slot)

    for k in range(TOP_K):
        _wait_row_gather(ys_hbm, buf.at[slot, k], sems.at[slot, k])
    wgt = wgt_ref[...]
    y0 = _load_row_tiled(buf.at[slot, 0], TM)
    y1 = _load_row_tiled(buf.at[slot, 1], TM)
    x = x_ref[...] + (y0 * wgt[:, 0:1] + y1 * wgt[:, 1:2])
    x = _ple_tail(x, p_ref[...], gple_ref, wpg_ref, wpp_ref)
    o_ref[...] = _rms(x, gfin_ref[...]) if final else x


def _combine(xbuf, p_all, layer, pos, wgt, ys, lw, final_g, row0):
    blk0 = row0 // TM
    n_rows = xbuf.shape[0]
    n_steps = n_rows // TM
    rows = lambda i: (i, 0)
    tok_rows = lambda i: (blk0 + i, 0)
    def pos_spec(k, ahead):
        return pl.BlockSpec((None, None, 1, TM),
                            lambda i: (blk0 + jnp.minimum(i + ahead, n_steps - 1), k, 0, 0),
                            memory_space=pltpu.SMEM)
    final = final_g is not None
    gfin = final_g if final else lw["norm_ple_g"]
    return pl.pallas_call(
        functools.partial(_combine_kernel, final, n_steps),
        grid=(n_steps,),
        in_specs=[pos_spec(0, 0), pos_spec(1, 0), pos_spec(0, 1), pos_spec(1, 1),
                  pl.BlockSpec((TM, D_MODEL), rows), pl.BlockSpec((None, TM, D_PLE), lambda i: (layer, i, 0)),
                  pl.BlockSpec((TM, LANES), tok_rows), pl.BlockSpec(memory_space=pl.ANY),
                  _const_spec((1, D_MODEL)), _const_spec((D_MODEL, D_MODEL)),
                  _const_spec((D_PLE, D_MODEL)), _const_spec((1, D_MODEL))],
        out_specs=pl.BlockSpec((TM, D_MODEL), lambda i: (i, 0)),
        out_shape=jax.ShapeDtypeStruct((n_rows, D_MODEL), F32),
        scratch_shapes=[pltpu.VMEM((2, TOP_K, TM * ROW_PIECES, LANES), F32),
                        pltpu.SemaphoreType.DMA((2, TOP_K))],
        compiler_params=pltpu.CompilerParams(dimension_semantics=("arbitrary",),
                                             vmem_limit_bytes=VMEM_LIMIT),
        name="moe_combine_ple",
    )(pos, pos, pos, pos, xbuf, p_all, wgt, ys, lw["norm_ple_g"], lw["ple_w_gate"],
      lw["ple_w_proj"], gfin)


def _routing_tables(idx, n_tok):
    n_pairs = n_tok * TOP_K
    n_tiles = n_pairs // TM + N_EXPERTS
    experts = jnp.arange(N_EXPERTS, dtype=jnp.int32)
    e_flat = idx.reshape(-1)
    pair = jnp.arange(n_pairs, dtype=jnp.int32)
    pair_tok = (pair // (TOP_K * TM)) * TM + pair % TM
    onehot = (experts[:, None] == e_flat[None, :]).astype(jnp.int32)
    csum = jnp.cumsum(onehot, axis=1)
    sizes = csum[:, -1]
    padded = ((sizes + TM - 1) // TM) * TM
    pend = jnp.cumsum(padded)
    pstart = pend - padded
    pos = jnp.sum(onehot * (csum - 1 + pstart[:, None]), axis=0)
    fill_e = jnp.repeat(experts, TM)
    fill_r = jnp.tile(jnp.arange(TM, dtype=jnp.int32), N_EXPERTS)
    fill_need = jnp.sum((fill_e[:, None] == experts[None, :]) * (padded - sizes)[None, :], axis=1)
    keys = jnp.concatenate([e_flat, jnp.where(fill_r < fill_need, fill_e, N_EXPERTS)])
    toks = jnp.concatenate([pair_tok, jnp.zeros((N_EXPERTS * TM,), jnp.int32)])
    _, slot_tok = lax.sort((keys, toks), num_keys=1, is_stable=True)
    n_active = (pend[-1] // TM).astype(jnp.int32)
    tile_row = jnp.minimum(jnp.arange(n_tiles, dtype=jnp.int32), n_active - 1)
    tile_expert = jnp.minimum(
        jnp.sum((tile_row[:, None] * TM >= pend[None, :]).astype(jnp.int32), axis=1), N_EXPERTS - 1)
    return (slot_tok.reshape(n_tiles, 1, TM), tile_expert, n_active.reshape(1),
            pos.reshape(n_tok // TM, TOP_K, 1, TM))


def _sgu_tables(sgu_w, sgu_b, seq):
    r = np.arange(CHUNK)
    bias = jnp.repeat(sgu_b.T, GROUP_DIM, axis=1)
    if seq >= CHUNK:
        w = sgu_w
        mask = r[None, :] <= r[:, None]
    else:
        reps = CHUNK // seq
        w = jnp.tile(sgu_w[:, :seq, :seq], (1, reps, reps))
        mask = (r[:, None] // seq == r[None, :] // seq) & (r[None, :] % seq <= r[:, None] % seq)
        bias = jnp.tile(bias[:seq], (reps, 1))
    mask = np.tile(mask.astype(np.float32), (N_GROUPS, 1))
    return w.reshape(N_GROUPS * CHUNK, CHUNK), jnp.asarray(mask), bias


def _block_diag(w):
    out = jnp.zeros((D_BRANCH, D_BRANCH), w.dtype)
    for g in range(N_GROUPS):
        out = out.at[g * GROUP_DIM:(g + 1) * GROUP_DIM, g * GROUP_DIM:(g + 1) * GROUP_DIM].set(w[g])
    return out


def kernel(x_prompt, x_sample, state_conv_a, state_conv_b, state_pool, p_prompt, p_sample, norm_mix_g, w_in, conv_a_w, conv_b_w, conv_b_bias, conv_b_ln_g, conv_b_ln_b, pool_w, pool_scale, sgu_ln_g, sgu_ln_b, sgu_w, sgu_b, w_branch, w_out, norm_ffn_g, ffn_w_gate, ffn_w_up, ffn_w_down, router_w, moe_w_gate, moe_w_up, moe_w_down, norm_ple_g, ple_w_gate, ple_w_proj, final_norm_g):
    batch, seq, _ = x_prompt.shape
    nseq, dseq, _ = x_sample.shape
    depth = w_in.shape[0]
    assert depth % 2 == 0
    assert PAST_LEN + 1 >= POOL_MAX
    assert seq % TL == 0 and TL % CHUNK == 0 and TL >= CONV_B_W + 1
    n_prompt, n_sample = batch * seq, nseq * dseq
    n_tok = n_prompt + n_sample
    assert n_prompt % TM == 0 and n_sample % TM == 0

    xp, xs = x_prompt.reshape(n_prompt, D_MODEL), x_sample.reshape(n_sample, D_MODEL)
    pp, ps = p_prompt.reshape(depth, n_prompt, D_PLE), p_sample.reshape(depth, n_sample, D_PLE)
    row = lambda a: a.reshape(1, -1)
    zeros = jnp.zeros((D_BRANCH,), F32)

    states_p, states_s = [], []
    y_prompt = y_sample = None
    for i in range(depth):
        sw_p, sm_p, sb_p = _sgu_tables(sgu_w[i], sgu_b[i], seq)
        sw_s, sm_s, sb_s = _sgu_tables(sgu_w[i], sgu_b[i], dseq)
        lw = {
            "norm_mix_g": row(norm_mix_g[i]), "w_in": w_in[i].astype(BF16),
            "conv_a_w": conv_a_w[i], "conv_b_w": conv_b_w[i],
            "vec": jnp.stack([conv_b_bias[i], conv_b_ln_g[i], conv_b_ln_b[i], pool_scale[i],
                              sgu_ln_g[i], sgu_ln_b[i], zeros, zeros]),
            "pool_bd": _block_diag(pool_w[i]).astype(BF16),
            "sgu_w_p": sw_p, "sgu_mask_p": sm_p, "sgu_bias_p": sb_p,
            "sgu_w_s": sw_s, "sgu_mask_s": sm_s, "sgu_bias_s": sb_s,
            "w_branch": w_branch[i].astype(BF16), "w_out": w_out[i].astype(BF16),
            "norm_ffn_g": row(norm_ffn_g[i]), "norm_ple_g": row(norm_ple_g[i]),
            "ple_w_gate": ple_w_gate[i].astype(BF16), "ple_w_proj": ple_w_proj[i].astype(BF16),
        }
        in_place = i > 0
        xp, a_p, b_p, pool_p, v_p = _mixer_prompt(xp, lw, batch, seq, in_place)
        xs, a_s, b_s, pool_s, v_s = _mixer_sample(xs, lw, state_conv_a[i], state_conv_b[i], state_pool[i],
                                                  nseq, dseq, in_place)
        states_p.append((a_p, b_p, pool_p, v_p))
        states_s.append((a_s, b_s, pool_s, v_s))

        j = i // 2
        last = i == depth - 1
        if i % 2 == 0:
            lw.update({"ffn_w_gate": ffn_w_gate[j].astype(BF16), "ffn_w_up": ffn_w_up[j].astype(BF16),
                       "ffn_w_down": ffn_w_down[j].astype(BF16)})
            xp = _dense_ffn(xp, pp, i, lw)
            xs = _dense_ffn(xs, ps, i, lw)
        else:
            r_hi = router_w[j].astype(BF16)
            r_lo = (router_w[j] - r_hi.astype(F32)).astype(BF16)
            lane_pad = lambda a: jnp.pad(a, ((0, 0), (0, LANES - a.shape[1])))
            lw.update({"router_w": lane_pad(jnp.concatenate([r_hi, r_lo], axis=1)), "router_wh": lane_pad(r_hi),
                       "moe_w_gate": moe_w_gate[j].astype(BF16), "moe_w_up": moe_w_up[j].astype(BF16),
                       "moe_w_down": moe_w_down[j].astype(BF16)})
            idx, wgt, xn_tiled = _router(xp, xs, lw)
            slot_tok, tile_expert, n_active, pos = _routing_tables(idx, n_tok)
            ys = _experts(xn_tiled, slot_tok, tile_expert, n_active, lw)
            fin = row(final_norm_g) if last else None
            xp = _combine(xp, pp, i, pos, wgt, ys, lw, fin, 0)
            xs = _combine(xs, ps, i, pos, wgt, ys, lw, fin, n_prompt)
            if last:
                y_prompt, y_sample = xp, xs

    stack = lambda k, sts: jnp.stack([s[k] for s in sts])
    return (y_prompt.reshape(batch, seq, D_MODEL), y_sample.reshape(nseq, dseq, D_MODEL),
            stack(0, states_p), stack(1, states_p), stack(2, states_p), stack(3, states_p),
            stack(0, states_s), stack(1, states_s), stack(2, states_s), stack(3, states_s))
```

```python
import functools

import jax
import jax.numpy as jnp
import numpy as np
from jax import lax
from jax.experimental import pallas as pl
from jax.experimental.pallas import tpu as pltpu

F32 = jnp.float32
BF16 = jnp.bfloat16

D_MODEL = 1024
N_BRANCH = 4
D_BRANCH = 256
N_GROUPS = 4
GROUP_DIM = 64
CONV_A_W = 3
CONV_B_W = 31
POOL_WINDOWS = (2, 4, 8, 16)
POOL_MAX = 16
CHUNK = 128
PAST_LEN = 16384
D_PLE = 256
N_EXPERTS = 8
TOP_K = 2
EPS = 1e-6

OFF_A_H = 0
OFF_A_B = 256
OFF_A_C = 512
OFF_CF_A = 768
OFF_CF_B = 1024
OFF_POOL = 1280
OFF_U = 1536
OFF_V = 1792
OFF_GATE = 2048

LANES = 128
SUBLANES = 8
TM = 512
TL = 512
NB = 64
TF_EXPERT = 1792
ROW_CHUNK = 32
SEQ_CHUNK = 8
VMEM_LIMIT = 56 * 1024 * 1024

V_CB_BIAS, V_CB_LN_G, V_CB_LN_B, V_POOL_SCALE, V_SGU_LN_G, V_SGU_LN_B = range(6)


def _rms(x, g):
    return x * lax.rsqrt(jnp.mean(x * x, axis=-1, keepdims=True) + EPS) * g


def _ln(x, g, b):
    xc = x - jnp.mean(x, axis=-1, keepdims=True)
    var = jnp.mean(xc * xc, axis=-1, keepdims=True)
    return xc * lax.rsqrt(var + EPS) * g + b


def _bdot(a, w):
    return jnp.dot(a.astype(BF16), w, preferred_element_type=F32)


def _silu(x):
    return x * jax.nn.sigmoid(x)


def _const_spec(shape, layer=None):
    nd = len(shape)
    if layer is None:
        return pl.BlockSpec(shape, lambda *_: (0,) * nd, pipeline_mode=pl.Buffered(1))
    return pl.BlockSpec((None,) + tuple(shape), lambda *_: (layer,) + (0,) * nd, pipeline_mode=pl.Buffered(1))


def _sgu(u, v, swm, sbias):
    m = v.shape[0]
    group = lax.broadcasted_iota(jnp.int32, (CHUNK, D_BRANCH), 1) // GROUP_DIM
    outs = []
    for c in range(m // CHUNK):
        vc = v[c * CHUNK:(c + 1) * CHUNK].astype(BF16)
        r = jnp.dot(swm, vc, preferred_element_type=F32)
        s = sbias
        for g in range(N_GROUPS):
            s = s + jnp.where(group == g, r[g * CHUNK:(g + 1) * CHUNK], 0.0)
        outs.append(s)
    return u * jnp.concatenate(outs, axis=0)


GATE_PIECES_PER_BRANCH = D_MODEL // D_BRANCH
N_GATE_PIECES = N_BRANCH * GATE_PIECES_PER_BRANCH


def _gate_emitter(xnb, win_ref):
    pieces = []

    def emit(n=1):
        for _ in range(n):
            if len(pieces) < N_GATE_PIECES:
                lo = OFF_GATE + len(pieces) * D_BRANCH
                z = jnp.dot(xnb, win_ref[:, lo:lo + D_BRANCH], preferred_element_type=F32)
                pieces.append(jax.nn.sigmoid(z))
    return pieces, emit


def _gated_merge(x, branches, gates, wbr_ref, wout_ref):
    bb = [b.astype(BF16) for b in branches]
    cols = []
    for q in range(GATE_PIECES_PER_BRANCH):
        cs = slice(q * D_BRANCH, (q + 1) * D_BRANCH)
        m = None
        for i in range(N_BRANCH):
            t = gates[i * GATE_PIECES_PER_BRANCH + q] * \
                jnp.dot(bb[i], wbr_ref[i, :, cs], preferred_element_type=F32)
            m = t if m is None else m + t
        cols.append(m.astype(BF16))
    return x + jnp.dot(jnp.concatenate(cols, axis=1), wout_ref[...], preferred_element_type=F32)


def _pool_lane_windows(half):
    small, big = POOL_WINDOWS[2 * half], POOL_WINDOWS[2 * half + 1]
    lane = lax.broadcasted_iota(jnp.int32, (1, LANES), 1)
    is_big = lane >= GROUP_DIM
    return small, big, is_big


def _mixer_prompt_kernel(x_ref, g_ref, win_ref, caw_ref, cbw_ref, vec_ref, pw_ref, sw_ref, smask_ref,
                         sbias_ref, wbr_ref, wout_ref,
                         xo_ref, sta_ref, stb_ref, stp_ref, stv_ref,
                         sa, sb, sp):
    i = pl.program_id(1)
    ha, hb, hp = SUBLANES, 4 * SUBLANES, 2 * SUBLANES

    @pl.when(i == 0)
    def _():
        sa[0:ha, :] = jnp.zeros((ha, D_BRANCH), F32)
        sb[0:hb, :] = jnp.zeros((hb, D_BRANCH), F32)
        sp[0:hp, :] = jnp.zeros((hp, D_BRANCH), F32)

    @pl.when(i > 0)
    def _():
        sa[0:ha, :] = sa[TL:TL + ha, :]
        sb[0:hb, :] = sb[TL:TL + hb, :]
        sp[0:hp, :] = sp[TL:TL + hp, :]

    x = x_ref[...]
    xnb = _rms(x, g_ref[...]).astype(BF16)
    vec = vec_ref[...]

    def zc(off):
        return jnp.dot(xnb, win_ref[:, off:off + D_BRANCH], preferred_element_type=F32)

    gates, emit_gate = _gate_emitter(xnb, win_ref)
    n_chunks = TL // ROW_CHUNK

    def spread(c, total):
        return ((c + 1) * total) // n_chunks - (c * total) // n_chunks

    def tap_conv(s_ref, base, w_ref, ntaps, n_gates=0):
        outs = []
        for c in range(n_chunks):
            acc = None
            for k in range(ntaps):
                term = s_ref[pl.ds(base + k + c * ROW_CHUNK, ROW_CHUNK), :] * w_ref[k:k + 1, :]
                acc = term if acc is None else acc + term
            outs.append(acc)
            emit_gate(spread(c, n_gates))
        return jnp.concatenate(outs, axis=0)

    sa[ha:ha + TL, :] = zc(OFF_A_C) * zc(OFF_A_H)
    br_a = zc(OFF_A_B) * tap_conv(sa, ha - (CONV_A_W - 1), caw_ref, CONV_A_W)
    sta_ref[0] = sa[TL + ha - (CONV_A_W - 1):TL + ha, :]

    sb[hb:hb + TL, :] = zc(OFF_CF_A) * jax.nn.sigmoid(zc(OFF_CF_B))
    yb = tap_conv(sb, hb - (CONV_B_W - 1), cbw_ref, CONV_B_W, n_gates=3 * N_GATE_PIECES // 4)
    yb = yb + vec[V_CB_BIAS:V_CB_BIAS + 1]
    br_b = _silu(_ln(yb, vec[V_CB_LN_G:V_CB_LN_G + 1], vec[V_CB_LN_B:V_CB_LN_B + 1]))
    stb_ref[0] = sb[TL + hb - (CONV_B_W - 1):TL + hb, :]

    sp[hp:hp + TL, :] = zc(OFF_POOL)
    halves = []
    for half in range(2):
        small, big, is_big = _pool_lane_windows(half)
        lanes = slice(half * LANES, (half + 1) * LANES)
        wl = jnp.where(is_big, float(big), float(small))
        outs = []
        for c in range(n_chunks):
            r0 = hp + c * ROW_CHUNK
            acc = None
            for j in range(big):
                seg = sp[pl.ds(r0 - j, ROW_CHUNK), lanes]
                if j >= small:
                    seg = jnp.where(is_big, seg, 0.0)
                acc = seg if acc is None else acc + seg
            pos = i * TL + c * ROW_CHUNK + lax.broadcasted_iota(jnp.int32, (ROW_CHUNK, LANES), 0)
            cnt = jnp.minimum(wl, (pos + 1).astype(F32))
            outs.append(acc / cnt - sp[pl.ds(r0, ROW_CHUNK), lanes])
            if half == 1:
                emit_gate(spread(c, N_GATE_PIECES // 4))
        halves.append(jnp.concatenate(outs, axis=0))
    pooled = jnp.concatenate(halves, axis=1)
    br_c = _bdot(pooled, pw_ref[...]) * vec[V_POOL_SCALE:V_POOL_SCALE + 1]
    stp_ref[0] = sp[TL + hp - (POOL_MAX - 1):TL + hp, :]

    v = _ln(zc(OFF_V), vec[V_SGU_LN_G:V_SGU_LN_G + 1], vec[V_SGU_LN_B:V_SGU_LN_B + 1])
    stv_ref[0] = v[TL - CHUNK:TL]
    swm = jnp.where(smask_ref[...] > 0, sw_ref[...], 0.0).astype(BF16)
    br_d = _sgu(zc(OFF_U), v, swm, sbias_ref[...])

    emit_gate(N_GATE_PIECES)
    xo_ref[...] = _gated_merge(x, (br_a, br_b, br_c, br_d), gates, wbr_ref, wout_ref)


def _mixer_sample_kernel(x_ref, g_ref, win_ref, caw_ref, cbw_ref, vec_ref, pw_ref, sw_ref, smask_ref,
                         sbias_ref, wbr_ref, wout_ref, ina_ref, inb_ref, inp_ref,
                         xo_ref, sta_ref, stb_ref, stp_ref, stv_ref,
                         sa, sb, sp):
    seq = SUBLANES
    ha, hb, hp = SUBLANES, 4 * SUBLANES, 2 * SUBLANES
    sa[:, ha - (CONV_A_W - 1):ha, :] = ina_ref[...]
    sb[:, hb - (CONV_B_W - 1):hb, :] = inb_ref[...]
    sp[:, hp - (POOL_MAX - 1):hp, :] = inp_ref[...]

    x = x_ref[...]
    xnb = _rms(x, g_ref[...]).astype(BF16)
    vec = vec_ref[...]

    def zc(off):
        return jnp.dot(xnb, win_ref[:, off:off + D_BRANCH], preferred_element_type=F32)

    def to3(a):
        return a.reshape(NB, seq, D_BRANCH)

    def to2(a):
        return a.reshape(NB * seq, D_BRANCH)

    gates, emit_gate = _gate_emitter(xnb, win_ref)
    n_chunks = NB // SEQ_CHUNK

    def tap_conv(s_ref, base, w_ref, ntaps, gates_per_chunk=0):
        outs = []
        for c in range(n_chunks):
            acc = None
            for k in range(ntaps):
                seg = s_ref[c * SEQ_CHUNK:(c + 1) * SEQ_CHUNK, pl.ds(base + k, seq), :]
                term = seg * w_ref[k:k + 1, :][None]
                acc = term if acc is None else acc + term
            outs.append(acc)
            emit_gate(gates_per_chunk)
        return to2(jnp.concatenate(outs, axis=0))

    sa[:, ha:ha + seq, :] = to3(zc(OFF_A_C) * zc(OFF_A_H))
    br_a = zc(OFF_A_B) * tap_conv(sa, ha - (CONV_A_W - 1), caw_ref, CONV_A_W)
    sta_ref[...] = sa[:, ha + seq - (CONV_A_W - 1):ha + seq, :]

    sb[:, hb:hb + seq, :] = to3(zc(OFF_CF_A) * jax.nn.sigmoid(zc(OFF_CF_B)))
    yb = tap_conv(sb, hb - (CONV_B_W - 1), cbw_ref, CONV_B_W, gates_per_chunk=-(-N_GATE_PIECES // n_chunks))
    yb = yb + vec[V_CB_BIAS:V_CB_BIAS + 1]
    br_b = _silu(_ln(yb, vec[V_CB_LN_G:V_CB_LN_G + 1], vec[V_CB_LN_B:V_CB_LN_B + 1]))
    stb_ref[...] = sb[:, hb + seq - (CONV_B_W - 1):hb + seq, :]

    sp[:, hp:hp + seq, :] = to3(zc(OFF_POOL))
    halves = []
    for half in range(2):
        small, big, is_big = _pool_lane_windows(half)
        lanes = slice(half * LANES, (half + 1) * LANES)
        wl = jnp.where(is_big, float(big), float(small))[None]
        outs = []
        for c in range(NB // SEQ_CHUNK):
            rows = slice(c * SEQ_CHUNK, (c + 1) * SEQ_CHUNK)
            acc = None
            for j in range(big):
                seg = sp[rows, pl.ds(hp - j, seq), lanes]
                if j >= small:
                    seg = jnp.where(is_big[None], seg, 0.0)
                acc = seg if acc is None else acc + seg
            outs.append(acc / wl - sp[rows, pl.ds(hp, seq), lanes])
        halves.append(jnp.concatenate(outs, axis=0).reshape(NB * seq, LANES))
    pooled = jnp.concatenate(halves, axis=1)
    br_c = _bdot(pooled, pw_ref[...]) * vec[V_POOL_SCALE:V_POOL_SCALE + 1]
    stp_ref[...] = sp[:, hp + seq - (POOL_MAX - 1):hp + seq, :]

    v = _ln(zc(OFF_V), vec[V_SGU_LN_G:V_SGU_LN_G + 1], vec[V_SGU_LN_B:V_SGU_LN_B + 1])
    stv_ref[...] = to3(v)
    swm = jnp.where(smask_ref[...] > 0, sw_ref[...], 0.0).astype(BF16)
    br_d = _sgu(zc(OFF_U), v, swm, sbias_ref[...])

    emit_gate(N_GATE_PIECES)
    xo_ref[...] = _gated_merge(x, (br_a, br_b, br_c, br_d), gates, wbr_ref, wout_ref)


def _mixer_weight_specs(layer):
    return [
        _const_spec((1, D_MODEL), layer),
        _const_spec((D_MODEL, OFF_GATE + N_BRANCH * D_MODEL), layer),
        _const_spec((CONV_A_W, D_BRANCH), layer),
        _const_spec((CONV_B_W, D_BRANCH), layer),
        _const_spec((SUBLANES, D_BRANCH), layer),
        _const_spec((D_BRANCH, D_BRANCH), layer),
        _const_spec((N_GROUPS * CHUNK, CHUNK), layer),
        _const_spec((N_GROUPS * CHUNK, CHUNK)),
        _const_spec((CHUNK, D_BRANCH), layer),
        _const_spec((N_BRANCH, D_BRANCH, D_MODEL), layer),
        _const_spec((D_MODEL, D_MODEL), layer),
    ]


def _mixer_prompt(xbuf, lw, layer, batch, seq, in_place):
    nt = seq // TL
    rows = lambda b, i: (b * nt + i, 0)
    st = lambda n: pl.BlockSpec((1, n, D_BRANCH), lambda b, i: (b, 0, 0))
    return pl.pallas_call(
        _mixer_prompt_kernel,
        grid=(batch, nt),
        in_specs=[pl.BlockSpec((TL, D_MODEL), rows)] + _mixer_weight_specs(layer),
        out_specs=[pl.BlockSpec((TL, D_MODEL), rows), st(CONV_A_W - 1), st(CONV_B_W - 1),
                   st(POOL_MAX - 1), st(CHUNK)],
        out_shape=[jax.ShapeDtypeStruct(xbuf.shape, F32),
                   jax.ShapeDtypeStruct((batch, CONV_A_W - 1, D_BRANCH), F32),
                   jax.ShapeDtypeStruct((batch, CONV_B_W - 1, D_BRANCH), F32),
                   jax.ShapeDtypeStruct((batch, POOL_MAX - 1, D_BRANCH), F32),
                   jax.ShapeDtypeStruct((batch, CHUNK, D_BRANCH), F32)],
        scratch_shapes=[pltpu.VMEM((TL + SUBLANES, D_BRANCH), F32),
                        pltpu.VMEM((TL + 4 * SUBLANES, D_BRANCH), F32),
                        pltpu.VMEM((TL + 2 * SUBLANES, D_BRANCH), F32)],
        input_output_aliases={0: 0} if in_place else {},
        compiler_params=pltpu.CompilerParams(dimension_semantics=("arbitrary", "arbitrary"),
                                             vmem_limit_bytes=VMEM_LIMIT),
        name="mixer_prompt",
    )(xbuf, lw["norm_mix_g"], lw["w_in"], lw["conv_a_w"], lw["conv_b_w"], lw["vec"], lw["pool_bd"],
      lw["sgu_w_p"], lw["sgu_mask_p"], lw["sgu_bias_p"], lw["w_branch"], lw["w_out"])


def _mixer_sample(xbuf, lw, layer, st_a, st_b, st_p, nseq, seq, in_place):
    assert seq == SUBLANES and nseq % NB == 0
    rows = lambda i: (i, 0)
    st = lambda n: pl.BlockSpec((NB, n, D_BRANCH), lambda i: (i, 0, 0))
    st_in = lambda n: pl.BlockSpec((None, NB, n, D_BRANCH), lambda i: (layer, i, 0, 0))
    return pl.pallas_call(
        _mixer_sample_kernel,
        grid=(nseq // NB,),
        in_specs=[pl.BlockSpec((NB * seq, D_MODEL), rows)] + _mixer_weight_specs(layer)
                 + [st_in(CONV_A_W - 1), st_in(CONV_B_W - 1), st_in(POOL_MAX - 1)],
        out_specs=[pl.BlockSpec((NB * seq, D_MODEL), rows), st(CONV_A_W - 1), st(CONV_B_W - 1),
                   st(POOL_MAX - 1), st(seq)],
        out_shape=[jax.ShapeDtypeStruct(xbuf.shape, F32),
                   jax.ShapeDtypeStruct((nseq, CONV_A_W - 1, D_BRANCH), F32),
                   jax.ShapeDtypeStruct((nseq, CONV_B_W - 1, D_BRANCH), F32),
                   jax.ShapeDtypeStruct((nseq, POOL_MAX - 1, D_BRANCH), F32),
                   jax.ShapeDtypeStruct((nseq, seq, D_BRANCH), F32)],
        scratch_shapes=[pltpu.VMEM((NB, 2 * SUBLANES, D_BRANCH), F32),
                        pltpu.VMEM((NB, 5 * SUBLANES, D_BRANCH), F32),
                        pltpu.VMEM((NB, 3 * SUBLANES, D_BRANCH), F32)],
        input_output_aliases={0: 0} if in_place else {},
        compiler_params=pltpu.CompilerParams(dimension_semantics=("arbitrary",),
                                             vmem_limit_bytes=VMEM_LIMIT),
        name="mixer_sample",
    )(xbuf, lw["norm_mix_g"], lw["w_in"], lw["conv_a_w"], lw["conv_b_w"], lw["vec"], lw["pool_bd"],
      lw["sgu_w_s"], lw["sgu_mask_s"], lw["sgu_bias_s"], lw["w_branch"], lw["w_out"],
      st_a, st_b, st_p)


def _ple_tail(x, p, gple_ref, wpg_ref, wpp_ref):
    gate = jax.nn.sigmoid(_bdot(_rms(x, gple_ref[...]), wpg_ref[...]))
    return x + gate * _bdot(p, wpp_ref[...])


def _dense_ffn_kernel(n_chunks, x_ref, p_ref, gffn_ref, wg_ref, wu_ref, wd_ref, gple_ref, wpg_ref, wpp_ref,
                      xo_ref):
    x = x_ref[...]
    xnb = _rms(x, gffn_ref[...]).astype(BF16)
    tf = wg_ref.shape[1] // n_chunks
    f = None
    for c in range(n_chunks):
        cols = slice(c * tf, (c + 1) * tf)
        h = _silu(jnp.dot(xnb, wg_ref[:, cols], preferred_element_type=F32)) * \
            jnp.dot(xnb, wu_ref[:, cols], preferred_element_type=F32)
        t = _bdot(h, wd_ref[cols, :])
        f = t if f is None else f + t
    xo_ref[...] = _ple_tail(x + f, p_ref[...], gple_ref, wpg_ref, wpp_ref)


def _dense_ffn(xbuf, p_all, layer, lw):
    t = xbuf.shape[0]
    d_ff = lw["ffn_w_gate"].shape[2]
    j = layer // 2
    rows = lambda i: (i, 0)
    return pl.pallas_call(
        functools.partial(_dense_ffn_kernel, 2),
        grid=(t // TM,),
        in_specs=[pl.BlockSpec((TM, D_MODEL), rows), pl.BlockSpec((None, TM, D_PLE), lambda i: (layer, i, 0)),
                  _const_spec((1, D_MODEL), layer), _const_spec((D_MODEL, d_ff), j),
                  _const_spec((D_MODEL, d_ff), j), _const_spec((d_ff, D_MODEL), j),
                  _const_spec((1, D_MODEL), layer), _const_spec((D_MODEL, D_MODEL), layer),
                  _const_spec((D_PLE, D_MODEL), layer)],
        out_specs=pl.BlockSpec((TM, D_MODEL), rows),
        out_shape=jax.ShapeDtypeStruct(xbuf.shape, F32),
        input_output_aliases={0: 0},
        compiler_params=pltpu.CompilerParams(dimension_semantics=("arbitrary",),
                                             vmem_limit_bytes=VMEM_LIMIT),
        name="dense_ffn_ple",
    )(xbuf, p_all, lw["norm_ffn_g"], lw["ffn_w_gate"], lw["ffn_w_up"], lw["ffn_w_down"],
      lw["norm_ple_g"], lw["ple_w_gate"], lw["ple_w_proj"])


def _router_kernel(n_first, xa_ref, xb_ref, g_ref, wr_ref, wrh_ref, idx_ref, wgt_ref, xn_ref):
    x = jnp.where(pl.program_id(0) < n_first, xa_ref[...], xb_ref[...])
    xn = _rms(x, g_ref[...])
    _store_row_tiled(xn_ref, xn)
    xh = xn.astype(BF16)
    xl = (xn - xh.astype(F32)).astype(BF16)
    d = jnp.dot(xh, wr_ref[...], preferred_element_type=F32) + \
        jnp.dot(xl, wrh_ref[...], preferred_element_type=F32)
    logits = d + pltpu.roll(d, LANES - N_EXPERTS, axis=1)
    lane = lax.broadcasted_iota(jnp.int32, logits.shape, 1)
    lane_f = lane.astype(F32)
    logits = jnp.where(lane < N_EXPERTS, logits, -jnp.inf)
    m1 = jnp.max(logits, axis=-1, keepdims=True)
    i1 = jnp.min(jnp.where(logits == m1, lane_f, float(LANES)), axis=-1, keepdims=True)
    rest = jnp.where(lane_f == i1, -jnp.inf, logits)
    m2 = jnp.max(rest, axis=-1, keepdims=True)
    i2 = jnp.min(jnp.where(rest == m2, lane_f, float(LANES)), axis=-1, keepdims=True)
    e2 = jnp.exp(m2 - m1)
    den = 1.0 + e2
    picks = jnp.where(lane == 0, i1, jnp.where(lane == 1, i2, 0.0))
    idx_ref[...] = jnp.transpose(picks)[0:TOP_K, :].astype(jnp.int32)
    wgt_ref[...] = jnp.where(lane == 0, 1.0 / den, jnp.where(lane == 1, e2 / den, 0.0))


def _router(xa, xb, layer, lw):
    j = layer // 2
    na, nb = xa.shape[0] // TM, xb.shape[0] // TM
    t = xa.shape[0] + xb.shape[0]
    rows = lambda i: (i, 0)
    return pl.pallas_call(
        functools.partial(_router_kernel, na),
        grid=(na + nb,),
        in_specs=[pl.BlockSpec((TM, D_MODEL), lambda i: (jnp.minimum(i, na - 1), 0)),
                  pl.BlockSpec((TM, D_MODEL), lambda i: (jnp.maximum(i - na, 0), 0)),
                  _const_spec((1, D_MODEL), layer), _const_spec((D_MODEL, LANES), j),
                  _const_spec((D_MODEL, LANES), j)],
        out_specs=[pl.BlockSpec((None, TOP_K, TM), lambda i: (i, 0, 0)), pl.BlockSpec((TM, LANES), rows),
                   pl.BlockSpec((TM * ROW_PIECES, LANES), rows)],
        out_shape=[jax.ShapeDtypeStruct((na + nb, TOP_K, TM), jnp.int32), jax.ShapeDtypeStruct((t, LANES), F32),
                   jax.ShapeDtypeStruct((t * ROW_PIECES, LANES), F32)],
        compiler_params=pltpu.CompilerParams(dimension_semantics=("arbitrary",)),
        name="router",
    )(xa, xb, lw["norm_ffn_g"], lw["router_w"], lw["router_wh"])


ROW_PIECES = D_MODEL // LANES
assert ROW_PIECES == SUBLANES
GATHER_UNROLL = 8


def _store_row_tiled(ref, x):
    for s in range(ROW_PIECES):
        ref[pl.ds(s, x.shape[0], stride=ROW_PIECES), :] = x[:, s * LANES:(s + 1) * LANES]


def _load_row_tiled(ref, n_rows):
    return jnp.concatenate([ref[pl.ds(s, n_rows, stride=ROW_PIECES), :] for s in range(ROW_PIECES)], axis=1)


def _start_row_gather(idx_ref, k, src_hbm, dst, sem):
    group = GATHER_UNROLL * ROW_PIECES

    def issue(it, carry):
        dst_base = pl.multiple_of(it * group, group)
        for u in range(GATHER_UNROLL):
            src_row = pl.multiple_of(idx_ref[k, it * GATHER_UNROLL + u] * ROW_PIECES, ROW_PIECES)
            pltpu.make_async_copy(src_hbm.at[pl.ds(src_row, ROW_PIECES)],
                                  dst.at[pl.ds(dst_base + u * ROW_PIECES, ROW_PIECES)],
                                  sem).start()
        return carry
    lax.fori_loop(0, dst.shape[0] // group, issue, 0)


def _wait_row_gather(src_hbm, dst, sem):
    pltpu.make_async_copy(src_hbm.at[pl.ds(0, dst.shape[0])], dst, sem).wait()


def _expert_kernel(n_fc, te_ref, na_ref, tokc_ref, tokn_ref, xn_hbm, wg_ref, wu_ref, wd_ref, ys_ref,
                   gbuf, xs, acc, sems):
    i, j = pl.program_id(0), pl.program_id(1)
    na = na_ref[0]
    slot = i % 2

    @pl.when(jnp.logical_and(i < na, j == 0))
    def _():
        @pl.when(i == 0)
        def _():
            _start_row_gather(tokc_ref, 0, xn_hbm, gbuf.at[0], sems.at[0])

        @pl.when(i + 1 < na)
        def _():
            _start_row_gather(tokn_ref, 0, xn_hbm, gbuf.at[1 - slot], sems.at[1 - slot])

        _wait_row_gather(xn_hbm, gbuf.at[slot], sems.at[slot])
        xs[...] = _load_row_tiled(gbuf.at[slot], TM).astype(BF16)

    def swiglu_piecewise(sink):
        h = _silu(jnp.dot(xs[...], wg_ref[...], preferred_element_type=F32)) * \
            jnp.dot(xs[...], wu_ref[...], preferred_element_type=F32)
        hb = h.astype(BF16)
        for q in range(D_MODEL // D_BRANCH):
            cs = slice(q * D_BRANCH, (q + 1) * D_BRANCH)
            sink(cs, jnp.dot(hb, wd_ref[:, cs], preferred_element_type=F32))

    def to_acc(first):
        def sink(cs, part):
            acc[:, cs] = part if first else acc[:, cs] + part
        return sink

    def to_out(with_acc):
        def sink(cs, part):
            tot = acc[:, cs] + part if with_acc else part
            for s in range(cs.start // LANES, cs.stop // LANES):
                lo = s * LANES - cs.start
                ys_ref[pl.ds(s, TM, stride=ROW_PIECES), :] = tot[:, lo:lo + LANES]
        return sink

    active = i < na
    if n_fc == 1:
        pl.when(active)(lambda: swiglu_piecewise(to_out(False)))
    else:
        pl.when(jnp.logical_and(active, j == 0))(lambda: swiglu_piecewise(to_acc(True)))
        if n_fc > 2:
            pl.when(jnp.logical_and(active, jnp.logical_and(j > 0, j < n_fc - 1)))(
                lambda: swiglu_piecewise(to_acc(False)))
        pl.when(jnp.logical_and(active, j == n_fc - 1))(lambda: swiglu_piecewise(to_out(True)))

    @pl.when(jnp.logical_and(i >= na, j == 0))
    def _():
        ys_ref[...] = jnp.zeros(ys_ref.shape, F32)


def _experts(xn_tiled, slot_tok, tile_expert, n_active, layer, lw):
    n_tiles = tile_expert.shape[0]
    d_ff = lw["moe_w_gate"].shape[3]
    n_fc = d_ff // TF_EXPERT
    m = layer // 2

    def fcol(i, j, na):
        return jnp.where(i < na[0], j, n_fc - 1)

    grid_spec = pltpu.PrefetchScalarGridSpec(
        num_scalar_prefetch=2,
        grid=(n_tiles, n_fc),
        in_specs=[
            pl.BlockSpec((None, 1, TM), lambda i, j, te, na: (i, 0, 0), memory_space=pltpu.SMEM),
            pl.BlockSpec((None, 1, TM), lambda i, j, te, na: (jnp.minimum(i + 1, n_tiles - 1), 0, 0),
                         memory_space=pltpu.SMEM),
            pl.BlockSpec(memory_space=pl.ANY),
            pl.BlockSpec((None, None, D_MODEL, TF_EXPERT),
                         lambda i, j, te, na: (m, te[i], 0, fcol(i, j, na))),
            pl.BlockSpec((None, None, D_MODEL, TF_EXPERT),
                         lambda i, j, te, na: (m, te[i], 0, fcol(i, j, na))),
            pl.BlockSpec((None, None, TF_EXPERT, D_MODEL),
                         lambda i, j, te, na: (m, te[i], fcol(i, j, na), 0)),
        ],
        out_specs=pl.BlockSpec((TM * ROW_PIECES, LANES), lambda i, j, te, na: (i, 0)),
        scratch_shapes=[pltpu.VMEM((2, TM * ROW_PIECES, LANES), F32), pltpu.VMEM((TM, D_MODEL), BF16),
                        pltpu.VMEM((TM, D_MODEL), F32), pltpu.SemaphoreType.DMA((2,))],
    )
    return pl.pallas_call(
        functools.partial(_expert_kernel, n_fc),
        grid_spec=grid_spec,
        out_shape=jax.ShapeDtypeStruct((n_tiles * TM * ROW_PIECES, LANES), F32),
        compiler_params=pltpu.CompilerParams(dimension_semantics=("arbitrary", "arbitrary"),
                                             vmem_limit_bytes=VMEM_LIMIT),
        name="moe_experts",
    )(tile_expert, n_active, slot_tok, slot_tok, xn_tiled, lw["moe_w_gate"], lw["moe_w_up"], lw["moe_w_down"])


def _combine_kernel(final, n_steps, pos0c_ref, pos1c_ref, pos0n_ref, pos1n_ref, x_ref, p_ref, wgt_ref, ys_hbm,
                    gple_ref, wpg_ref, wpp_ref, gfin_ref, o_ref, buf, sems):
    i = pl.program_id(0)
    slot = i % 2

    def start(pos_refs, s):
        for k in range(TOP_K):
            _start_row_gather(pos_refs[k], 0, ys_hbm, buf.at[s, k], sems.at[s, k])

    @pl.when(i == 0)
    def _():
        start((pos0c_ref, pos1c_ref), 0)

    @pl.when(i + 1 < n_steps)
    def _():
        start((pos0n_ref, pos1n_ref), 1 - slot)

    for k in range(TOP_K):
        _wait_row_gather(ys_hbm, buf.at[slot, k], sems.at[slot, k])
    wgt = wgt_ref[...]
    y0 = _load_row_tiled(buf.at[slot, 0], TM)
    y1 = _load_row_tiled(buf.at[slot, 1], TM)
    x = x_ref[...] + (y0 * wgt[:, 0:1] + y1 * wgt[:, 1:2])
    x = _ple_tail(x, p_ref[...], gple_ref, wpg_ref, wpp_ref)
    o_ref[...] = _rms(x, gfin_ref[...]) if final else x


def _combine(xbuf, p_all, layer, pos, wgt, ys, lw, final_g, row0):
    blk0 = row0 // TM
    n_rows = xbuf.shape[0]
    n_steps = n_rows // TM
    rows = lambda i: (i, 0)
    tok_rows = lambda i: (blk0 + i, 0)
    def pos_spec(k, ahead):
        return pl.BlockSpec((None, None, 1, TM),
                            lambda i: (blk0 + jnp.minimum(i + ahead, n_steps - 1), k, 0, 0),
                            memory_space=pltpu.SMEM)
    final = final_g is not None
    gfin = final_g if final else jnp.ones((1, D_MODEL), F32)
    return pl.pallas_call(
        functools.partial(_combine_kernel, final, n_steps),
        grid=(n_steps,),
        in_specs=[pos_spec(0, 0), pos_spec(1, 0), pos_spec(0, 1), pos_spec(1, 1),
                  pl.BlockSpec((TM, D_MODEL), rows), pl.BlockSpec((None, TM, D_PLE), lambda i: (layer, i, 0)),
                  pl.BlockSpec((TM, LANES), tok_rows), pl.BlockSpec(memory_space=pl.ANY),
                  _const_spec((1, D_MODEL), layer), _const_spec((D_MODEL, D_MODEL), layer),
                  _const_spec((D_PLE, D_MODEL), layer), _const_spec((1, D_MODEL))],
        out_specs=pl.BlockSpec((TM, D_MODEL), lambda i: (i, 0)),
        out_shape=jax.ShapeDtypeStruct((n_rows, D_MODEL), F32),
        scratch_shapes=[pltpu.VMEM((2, TOP_K, TM * ROW_PIECES, LANES), F32),
                        pltpu.SemaphoreType.DMA((2, TOP_K))],
        compiler_params=pltpu.CompilerParams(dimension_semantics=("arbitrary",),
                                             vmem_limit_bytes=VMEM_LIMIT),
        name="moe_combine_ple",
    )(pos, pos, pos, pos, xbuf, p_all, wgt, ys, lw["norm_ple_g"], lw["ple_w_gate"],
      lw["ple_w_proj"], gfin)


def _routing_tables(idx, n_tok):
    n_pairs = n_tok * TOP_K
    n_tiles = n_pairs // TM + N_EXPERTS
    experts = jnp.arange(N_EXPERTS, dtype=jnp.int32)
    e_flat = idx.reshape(-1)
    pair = jnp.arange(n_pairs, dtype=jnp.int32)
    pair_tok = (pair // (TOP_K * TM)) * TM + pair % TM
    onehot = (experts[:, None] == e_flat[None, :]).astype(jnp.int32)
    csum = jnp.cumsum(onehot, axis=1)
    sizes = csum[:, -1]
    padded = ((sizes + TM - 1) // TM) * TM
    pend = jnp.cumsum(padded)
    pstart = pend - padded
    pos = jnp.sum(onehot * (csum - 1 + pstart[:, None]), axis=0)
    fill_e = jnp.repeat(experts, TM)
    fill_r = jnp.tile(jnp.arange(TM, dtype=jnp.int32), N_EXPERTS)
    fill_need = jnp.sum((fill_e[:, None] == experts[None, :]) * (padded - sizes)[None, :], axis=1)
    keys = jnp.concatenate([e_flat, jnp.where(fill_r < fill_need, fill_e, N_EXPERTS)])
    toks = jnp.concatenate([pair_tok, jnp.zeros((N_EXPERTS * TM,), jnp.int32)])
    _, slot_tok = lax.sort((keys, toks), num_keys=1, is_stable=True)
    n_active = (pend[-1] // TM).astype(jnp.int32)
    tile_row = jnp.minimum(jnp.arange(n_tiles, dtype=jnp.int32), n_active - 1)
    tile_expert = jnp.minimum(
        jnp.sum((tile_row[:, None] * TM >= pend[None, :]).astype(jnp.int32), axis=1), N_EXPERTS - 1)
    return (slot_tok.reshape(n_tiles, 1, TM), tile_expert, n_active.reshape(1),
            pos.reshape(n_tok // TM, TOP_K, 1, TM))


def _sgu_tables(sgu_w, sgu_b, seq):
    depth = sgu_w.shape[0]
    r = np.arange(CHUNK)
    bias = jnp.repeat(jnp.swapaxes(sgu_b, 1, 2), GROUP_DIM, axis=2)
    if seq >= CHUNK:
        w = sgu_w
        mask = r[None, :] <= r[:, None]
    else:
        reps = CHUNK // seq
        w = jnp.tile(sgu_w[:, :, :seq, :seq], (1, 1, reps, reps))
        mask = (r[:, None] // seq == r[None, :] // seq) & (r[None, :] % seq <= r[:, None] % seq)
        bias = jnp.tile(bias[:, :seq], (1, reps, 1))
    mask = np.tile(mask.astype(np.float32), (N_GROUPS, 1))
    return w.reshape(depth, N_GROUPS * CHUNK, CHUNK), jnp.asarray(mask), bias


def _block_diag(w):
    out = jnp.zeros((w.shape[0], D_BRANCH, D_BRANCH), w.dtype)
    for g in range(N_GROUPS):
        out = out.at[:, g * GROUP_DIM:(g + 1) * GROUP_DIM, g * GROUP_DIM:(g + 1) * GROUP_DIM].set(w[:, g])
    return out


def kernel(x_prompt, x_sample, state_conv_a, state_conv_b, state_pool, p_prompt, p_sample, norm_mix_g, w_in, conv_a_w, conv_b_w, conv_b_bias, conv_b_ln_g, conv_b_ln_b, pool_w, pool_scale, sgu_ln_g, sgu_ln_b, sgu_w, sgu_b, w_branch, w_out, norm_ffn_g, ffn_w_gate, ffn_w_up, ffn_w_down, router_w, moe_w_gate, moe_w_up, moe_w_down, norm_ple_g, ple_w_gate, ple_w_proj, final_norm_g):
    batch, seq, _ = x_prompt.shape
    nseq, dseq, _ = x_sample.shape
    depth = w_in.shape[0]
    assert depth % 2 == 0
    assert PAST_LEN + 1 >= POOL_MAX
    assert seq % TL == 0 and TL % CHUNK == 0 and TL >= CONV_B_W + 1
    n_prompt, n_sample = batch * seq, nseq * dseq
    n_tok = n_prompt + n_sample
    assert n_prompt % TM == 0 and n_sample % TM == 0

    xp, xs = x_prompt.reshape(n_prompt, D_MODEL), x_sample.reshape(n_sample, D_MODEL)
    pp, ps = p_prompt.reshape(depth, n_prompt, D_PLE), p_sample.reshape(depth, n_sample, D_PLE)
    rows3 = lambda a: a.reshape(a.shape[0], 1, a.shape[1])
    zeros = jnp.zeros_like(conv_b_bias)
    sw_p, sm_p, sb_p = _sgu_tables(sgu_w, sgu_b, seq)
    sw_s, sm_s, sb_s = _sgu_tables(sgu_w, sgu_b, dseq)
    r_hi = router_w.astype(BF16)
    r_lo = (router_w - r_hi.astype(F32)).astype(BF16)
    lane_pad = lambda a: jnp.pad(a, ((0, 0), (0, 0), (0, LANES - a.shape[2])))
    lw = {
        "norm_mix_g": rows3(norm_mix_g), "w_in": w_in.astype(BF16),
        "conv_a_w": conv_a_w, "conv_b_w": conv_b_w,
        "vec": jnp.stack([conv_b_bias, conv_b_ln_g, conv_b_ln_b, pool_scale, sgu_ln_g, sgu_ln_b, zeros, zeros],
                         axis=1),
        "pool_bd": _block_diag(pool_w).astype(BF16),
        "sgu_w_p": sw_p, "sgu_mask_p": sm_p, "sgu_bias_p": sb_p,
        "sgu_w_s": sw_s, "sgu_mask_s": sm_s, "sgu_bias_s": sb_s,
        "w_branch": w_branch.astype(BF16), "w_out": w_out.astype(BF16),
        "norm_ffn_g": rows3(norm_ffn_g), "norm_ple_g": rows3(norm_ple_g),
        "ple_w_gate": ple_w_gate.astype(BF16), "ple_w_proj": ple_w_proj.astype(BF16),
        "ffn_w_gate": ffn_w_gate.astype(BF16), "ffn_w_up": ffn_w_up.astype(BF16),
        "ffn_w_down": ffn_w_down.astype(BF16),
        "router_w": lane_pad(jnp.concatenate([r_hi, r_lo], axis=2)), "router_wh": lane_pad(r_hi),
        "moe_w_gate": moe_w_gate.astype(BF16), "moe_w_up": moe_w_up.astype(BF16),
        "moe_w_down": moe_w_down.astype(BF16),
    }

    states_p, states_s = [], []
    y_prompt = y_sample = None
    for i in range(depth):
        in_place = i > 0
        xp, a_p, b_p, pool_p, v_p = _mixer_prompt(xp, lw, i, batch, seq, in_place)
        xs, a_s, b_s, pool_s, v_s = _mixer_sample(xs, lw, i, state_conv_a, state_conv_b, state_pool,
                                                  nseq, dseq, in_place)
        states_p.append((a_p, b_p, pool_p, v_p))
        states_s.append((a_s, b_s, pool_s, v_s))

        last = i == depth - 1
        if i % 2 == 0:
            xp = _dense_ffn(xp, pp, i, lw)
            xs = _dense_ffn(xs, ps, i, lw)
        else:
            idx, wgt, xn_tiled = _router(xp, xs, i, lw)
            slot_tok, tile_expert, n_active, pos = _routing_tables(idx, n_tok)
            ys = _experts(xn_tiled, slot_tok, tile_expert, n_active, i, lw)
            fin = final_norm_g.reshape(1, D_MODEL) if last else None
            xp = _combine(xp, pp, i, pos, wgt, ys, lw, fin, 0)
            xs = _combine(xs, ps, i, pos, wgt, ys, lw, fin, n_prompt)
            if last:
                y_prompt, y_sample = xp, xs

    stack = lambda k, sts: jnp.stack([s[k] for s in sts])
    return (y_prompt.reshape(batch, seq, D_MODEL), y_sample.reshape(nseq, dseq, D_MODEL),
            stack(0, states_p), stack(1, states_p), stack(2, states_p), stack(3, states_p),
            stack(0, states_s), stack(1, states_s), stack(2, states_s), stack(3, states_s))
```

```python
import functools

import jax
import jax.numpy as jnp
import numpy as np
from jax import lax
from jax.experimental import pallas as pl
from jax.experimental.pallas import tpu as pltpu

F32 = jnp.float32
BF16 = jnp.bfloat16

D_MODEL = 1024
N_BRANCH = 4
D_BRANCH = 256
N_GROUPS = 4
GROUP_DIM = 64
CONV_A_W = 3
CONV_B_W = 31
POOL_WINDOWS = (2, 4, 8, 16)
POOL_MAX = 16
CHUNK = 128
PAST_LEN = 16384
D_PLE = 256
N_EXPERTS = 8
TOP_K = 2
EPS = 1e-6

OFF_A_H = 0
OFF_A_B = 256
OFF_A_C = 512
OFF_CF_A = 768
OFF_CF_B = 1024
OFF_POOL = 1280
OFF_U = 1536
OFF_V = 1792
OFF_GATE = 2048

LANES = 128
SUBLANES = 8
TM = 512
TL = 512
NB = 64
TF_EXPERT = 1792
ROW_CHUNK = 32
SEQ_CHUNK = 8
VMEM_LIMIT = 56 * 1024 * 1024

V_CB_BIAS, V_CB_LN_G, V_CB_LN_B, V_POOL_SCALE, V_SGU_LN_G, V_SGU_LN_B = range(6)


def _rms(x, g):
    return x * lax.rsqrt(jnp.mean(x * x, axis=-1, keepdims=True) + EPS) * g


def _ln(x, g, b):
    xc = x - jnp.mean(x, axis=-1, keepdims=True)
    var = jnp.mean(xc * xc, axis=-1, keepdims=True)
    return xc * lax.rsqrt(var + EPS) * g + b


def _bdot(a, w):
    return jnp.dot(a.astype(BF16), w, preferred_element_type=F32)


def _silu(x):
    return x * jax.nn.sigmoid(x)


def _const_spec(shape, layer=None):
    nd = len(shape)
    if layer is None:
        return pl.BlockSpec(shape, lambda *_: (0,) * nd, pipeline_mode=pl.Buffered(1))
    return pl.BlockSpec((None,) + tuple(shape), lambda *_: (layer,) + (0,) * nd, pipeline_mode=pl.Buffered(1))


def _sgu(u, v, swm, sbias):
    m = v.shape[0]
    group = lax.broadcasted_iota(jnp.int32, (CHUNK, D_BRANCH), 1) // GROUP_DIM
    outs = []
    for c in range(m // CHUNK):
        vc = v[c * CHUNK:(c + 1) * CHUNK].astype(BF16)
        r = jnp.dot(swm, vc, preferred_element_type=F32)
        s = sbias
        for g in range(N_GROUPS):
            s = s + jnp.where(group == g, r[g * CHUNK:(g + 1) * CHUNK], 0.0)
        outs.append(s)
    return u * jnp.concatenate(outs, axis=0)


GATE_PIECES_PER_BRANCH = D_MODEL // D_BRANCH
N_GATE_PIECES = N_BRANCH * GATE_PIECES_PER_BRANCH


def _gate_emitter(xnb, win_ref):
    pieces = []

    def emit(n=1):
        for _ in range(n):
            if len(pieces) < N_GATE_PIECES:
                lo = OFF_GATE + len(pieces) * D_BRANCH
                z = jnp.dot(xnb, win_ref[:, lo:lo + D_BRANCH], preferred_element_type=F32)
                pieces.append(jax.nn.sigmoid(z))
    return pieces, emit


def _gated_merge(x, branches, gates, wbr_ref, wout_ref):
    bb = [b.astype(BF16) for b in branches]
    cols = []
    for q in range(GATE_PIECES_PER_BRANCH):
        cs = slice(q * D_BRANCH, (q + 1) * D_BRANCH)
        m = None
        for i in range(N_BRANCH):
            t = gates[i * GATE_PIECES_PER_BRANCH + q] * \
                jnp.dot(bb[i], wbr_ref[i, :, cs], preferred_element_type=F32)
            m = t if m is None else m + t
        cols.append(m.astype(BF16))
    return x + jnp.dot(jnp.concatenate(cols, axis=1), wout_ref[...], preferred_element_type=F32)


def _pool_lane_windows(half):
    small, big = POOL_WINDOWS[2 * half], POOL_WINDOWS[2 * half + 1]
    lane = lax.broadcasted_iota(jnp.int32, (1, LANES), 1)
    is_big = lane >= GROUP_DIM
    return small, big, is_big


def _mixer_prompt_kernel(x_ref, g_ref, win_ref, caw_ref, cbw_ref, vec_ref, pw_ref, sw_ref, smask_ref,
                         sbias_ref, wbr_ref, wout_ref,
                         xo_ref, sta_ref, stb_ref, stp_ref, stv_ref,
                         sa, sb, sp):
    i = pl.program_id(1)
    ha, hb, hp = SUBLANES, 4 * SUBLANES, 2 * SUBLANES

    @pl.when(i == 0)
    def _():
        sa[0:ha, :] = jnp.zeros((ha, D_BRANCH), F32)
        sb[0:hb, :] = jnp.zeros((hb, D_BRANCH), F32)
        sp[0:hp, :] = jnp.zeros((hp, D_BRANCH), F32)

    @pl.when(i > 0)
    def _():
        sa[0:ha, :] = sa[TL:TL + ha, :]
        sb[0:hb, :] = sb[TL:TL + hb, :]
        sp[0:hp, :] = sp[TL:TL + hp, :]

    x = x_ref[...]
    xnb = _rms(x, g_ref[...]).astype(BF16)
    vec = vec_ref[...]

    def zc(off):
        return jnp.dot(xnb, win_ref[:, off:off + D_BRANCH], preferred_element_type=F32)

    gates, emit_gate = _gate_emitter(xnb, win_ref)
    n_chunks = TL // ROW_CHUNK

    def spread(c, total):
        return ((c + 1) * total) // n_chunks - (c * total) // n_chunks

    def tap_conv(s_ref, base, w_ref, ntaps, n_gates=0):
        outs = []
        for c in range(n_chunks):
            acc = None
            for k in range(ntaps):
                term = s_ref[pl.ds(base + k + c * ROW_CHUNK, ROW_CHUNK), :] * w_ref[k:k + 1, :]
                acc = term if acc is None else acc + term
            outs.append(acc)
            emit_gate(spread(c, n_gates))
        return jnp.concatenate(outs, axis=0)

    sa[ha:ha + TL, :] = zc(OFF_A_C) * zc(OFF_A_H)
    br_a = zc(OFF_A_B) * tap_conv(sa, ha - (CONV_A_W - 1), caw_ref, CONV_A_W)
    sta_ref[0] = sa[TL + ha - (CONV_A_W - 1):TL + ha, :]

    sb[hb:hb + TL, :] = zc(OFF_CF_A) * jax.nn.sigmoid(zc(OFF_CF_B))
    yb = tap_conv(sb, hb - (CONV_B_W - 1), cbw_ref, CONV_B_W, n_gates=3 * N_GATE_PIECES // 4)
    yb = yb + vec[V_CB_BIAS:V_CB_BIAS + 1]
    br_b = _silu(_ln(yb, vec[V_CB_LN_G:V_CB_LN_G + 1], vec[V_CB_LN_B:V_CB_LN_B + 1]))
    stb_ref[0] = sb[TL + hb - (CONV_B_W - 1):TL + hb, :]

    sp[hp:hp + TL, :] = zc(OFF_POOL)
    halves = []
    for half in range(2):
        small, big, is_big = _pool_lane_windows(half)
        lanes = slice(half * LANES, (half + 1) * LANES)
        wl = jnp.where(is_big, float(big), float(small))
        outs = []
        for c in range(n_chunks):
            r0 = hp + c * ROW_CHUNK
            acc = None
            for j in range(big):
                seg = sp[pl.ds(r0 - j, ROW_CHUNK), lanes]
                if j >= small:
                    seg = jnp.where(is_big, seg, 0.0)
                acc = seg if acc is None else acc + seg
            pos = i * TL + c * ROW_CHUNK + lax.broadcasted_iota(jnp.int32, (ROW_CHUNK, LANES), 0)
            cnt = jnp.minimum(wl, (pos + 1).astype(F32))
            outs.append(acc / cnt - sp[pl.ds(r0, ROW_CHUNK), lanes])
            if half == 1:
                emit_gate(spread(c, N_GATE_PIECES // 4))
        halves.append(jnp.concatenate(outs, axis=0))
    pooled = jnp.concatenate(halves, axis=1)
    br_c = _bdot(pooled, pw_ref[...]) * vec[V_POOL_SCALE:V_POOL_SCALE + 1]
    stp_ref[0] = sp[TL + hp - (POOL_MAX - 1):TL + hp, :]

    v = _ln(zc(OFF_V), vec[V_SGU_LN_G:V_SGU_LN_G + 1], vec[V_SGU_LN_B:V_SGU_LN_B + 1])
    stv_ref[0] = v[TL - CHUNK:TL]
    swm = jnp.where(smask_ref[...] > 0, sw_ref[...], 0.0).astype(BF16)
    br_d = _sgu(zc(OFF_U), v, swm, sbias_ref[...])

    emit_gate(N_GATE_PIECES)
    xo_ref[...] = _gated_merge(x, (br_a, br_b, br_c, br_d), gates, wbr_ref, wout_ref)


def _mixer_sample_kernel(x_ref, g_ref, win_ref, caw_ref, cbw_ref, vec_ref, pw_ref, sw_ref, smask_ref,
                         sbias_ref, wbr_ref, wout_ref, ina_ref, inb_ref, inp_ref,
                         xo_ref, sta_ref, stb_ref, stp_ref, stv_ref,
                         sa, sb, sp):
    seq = SUBLANES
    ha, hb, hp = SUBLANES, 4 * SUBLANES, 2 * SUBLANES
    sa[:, ha - (CONV_A_W - 1):ha, :] = ina_ref[...]
    sb[:, hb - (CONV_B_W - 1):hb, :] = inb_ref[...]
    sp[:, hp - (POOL_MAX - 1):hp, :] = inp_ref[...]

    x = x_ref[...]
    xnb = _rms(x, g_ref[...]).astype(BF16)
    vec = vec_ref[...]

    def zc(off):
        return jnp.dot(xnb, win_ref[:, off:off + D_BRANCH], preferred_element_type=F32)

    def to3(a):
        return a.reshape(NB, seq, D_BRANCH)

    def to2(a):
        return a.reshape(NB * seq, D_BRANCH)

    gates, emit_gate = _gate_emitter(xnb, win_ref)
    n_chunks = NB // SEQ_CHUNK

    def tap_conv(s_ref, base, w_ref, ntaps, gates_per_chunk=0):
        outs = []
        for c in range(n_chunks):
            acc = None
            for k in range(ntaps):
                seg = s_ref[c * SEQ_CHUNK:(c + 1) * SEQ_CHUNK, pl.ds(base + k, seq), :]
                term = seg * w_ref[k:k + 1, :][None]
                acc = term if acc is None else acc + term
            outs.append(acc)
            emit_gate(gates_per_chunk)
        return to2(jnp.concatenate(outs, axis=0))

    sa[:, ha:ha + seq, :] = to3(zc(OFF_A_C) * zc(OFF_A_H))
    br_a = zc(OFF_A_B) * tap_conv(sa, ha - (CONV_A_W - 1), caw_ref, CONV_A_W)
    sta_ref[...] = sa[:, ha + seq - (CONV_A_W - 1):ha + seq, :]

    sb[:, hb:hb + seq, :] = to3(zc(OFF_CF_A) * jax.nn.sigmoid(zc(OFF_CF_B)))
    yb = tap_conv(sb, hb - (CONV_B_W - 1), cbw_ref, CONV_B_W, gates_per_chunk=-(-N_GATE_PIECES // n_chunks))
    yb = yb + vec[V_CB_BIAS:V_CB_BIAS + 1]
    br_b = _silu(_ln(yb, vec[V_CB_LN_G:V_CB_LN_G + 1], vec[V_CB_LN_B:V_CB_LN_B + 1]))
    stb_ref[...] = sb[:, hb + seq - (CONV_B_W - 1):hb + seq, :]

    sp[:, hp:hp + seq, :] = to3(zc(OFF_POOL))
    halves = []
    for half in range(2):
        small, big, is_big = _pool_lane_windows(half)
        lanes = slice(half * LANES, (half + 1) * LANES)
        wl = jnp.where(is_big, float(big), float(small))[None]
        outs = []
        for c in range(NB // SEQ_CHUNK):
            rows = slice(c * SEQ_CHUNK, (c + 1) * SEQ_CHUNK)
            acc = None
            for j in range(big):
                seg = sp[rows, pl.ds(hp - j, seq), lanes]
                if j >= small:
                    seg = jnp.where(is_big[None], seg, 0.0)
                acc = seg if acc is None else acc + seg
            outs.append(acc / wl - sp[rows, pl.ds(hp, seq), lanes])
        halves.append(jnp.concatenate(outs, axis=0).reshape(NB * seq, LANES))
    pooled = jnp.concatenate(halves, axis=1)
    br_c = _bdot(pooled, pw_ref[...]) * vec[V_POOL_SCALE:V_POOL_SCALE + 1]
    stp_ref[...] = sp[:, hp + seq - (POOL_MAX - 1):hp + seq, :]

    v = _ln(zc(OFF_V), vec[V_SGU_LN_G:V_SGU_LN_G + 1], vec[V_SGU_LN_B:V_SGU_LN_B + 1])
    stv_ref[...] = to3(v)
    swm = jnp.where(smask_ref[...] > 0, sw_ref[...], 0.0).astype(BF16)
    br_d = _sgu(zc(OFF_U), v, swm, sbias_ref[...])

    emit_gate(N_GATE_PIECES)
    xo_ref[...] = _gated_merge(x, (br_a, br_b, br_c, br_d), gates, wbr_ref, wout_ref)


def _mixer_weight_specs(layer):
    return [
        _const_spec((1, D_MODEL), layer),
        _const_spec((D_MODEL, OFF_GATE + N_BRANCH * D_MODEL), layer),
        _const_spec((CONV_A_W, D_BRANCH), layer),
        _const_spec((CONV_B_W, D_BRANCH), layer),
        _const_spec((SUBLANES, D_BRANCH), layer),
        _const_spec((D_BRANCH, D_BRANCH), layer),
        _const_spec((N_GROUPS * CHUNK, CHUNK), layer),
        _const_spec((N_GROUPS * CHUNK, CHUNK)),
        _const_spec((CHUNK, D_BRANCH), layer),
        _const_spec((N_BRANCH, D_BRANCH, D_MODEL), layer),
        _const_spec((D_MODEL, D_MODEL), layer),
    ]


def _mixer_prompt(xbuf, lw, layer, batch, seq, in_place):
    nt = seq // TL
    rows = lambda b, i: (b * nt + i, 0)
    st = lambda n: pl.BlockSpec((1, n, D_BRANCH), lambda b, i: (b, 0, 0))
    return pl.pallas_call(
        _mixer_prompt_kernel,
        grid=(batch, nt),
        in_specs=[pl.BlockSpec((TL, D_MODEL), rows)] + _mixer_weight_specs(layer),
        out_specs=[pl.BlockSpec((TL, D_MODEL), rows), st(CONV_A_W - 1), st(CONV_B_W - 1),
                   st(POOL_MAX - 1), st(CHUNK)],
        out_shape=[jax.ShapeDtypeStruct(xbuf.shape, F32),
                   jax.ShapeDtypeStruct((batch, CONV_A_W - 1, D_BRANCH), F32),
                   jax.ShapeDtypeStruct((batch, CONV_B_W - 1, D_BRANCH), F32),
                   jax.ShapeDtypeStruct((batch, POOL_MAX - 1, D_BRANCH), F32),
                   jax.ShapeDtypeStruct((batch, CHUNK, D_BRANCH), F32)],
        scratch_shapes=[pltpu.VMEM((TL + SUBLANES, D_BRANCH), F32),
                        pltpu.VMEM((TL + 4 * SUBLANES, D_BRANCH), F32),
                        pltpu.VMEM((TL + 2 * SUBLANES, D_BRANCH), F32)],
        input_output_aliases={0: 0} if in_place else {},
        compiler_params=pltpu.CompilerParams(dimension_semantics=("arbitrary", "arbitrary"),
                                             vmem_limit_bytes=VMEM_LIMIT),
        name="mixer_prompt",
    )(xbuf, lw["norm_mix_g"], lw["w_in"], lw["conv_a_w"], lw["conv_b_w"], lw["vec"], lw["pool_bd"],
      lw["sgu_w_p"], lw["sgu_mask_p"], lw["sgu_bias_p"], lw["w_branch"], lw["w_out"])


def _mixer_sample(xbuf, lw, layer, st_a, st_b, st_p, nseq, seq, in_place):
    assert seq == SUBLANES and nseq % NB == 0
    rows = lambda i: (i, 0)
    st = lambda n: pl.BlockSpec((NB, n, D_BRANCH), lambda i: (i, 0, 0))
    st_in = lambda n: pl.BlockSpec((None, NB, n, D_BRANCH), lambda i: (layer, i, 0, 0))
    return pl.pallas_call(
        _mixer_sample_kernel,
        grid=(nseq // NB,),
        in_specs=[pl.BlockSpec((NB * seq, D_MODEL), rows)] + _mixer_weight_specs(layer)
                 + [st_in(CONV_A_W - 1), st_in(CONV_B_W - 1), st_in(POOL_MAX - 1)],
        out_specs=[pl.BlockSpec((NB * seq, D_MODEL), rows), st(CONV_A_W - 1), st(CONV_B_W - 1),
                   st(POOL_MAX - 1), st(seq)],
        out_shape=[jax.ShapeDtypeStruct(xbuf.shape, F32),
                   jax.ShapeDtypeStruct((nseq, CONV_A_W - 1, D_BRANCH), F32),
                   jax.ShapeDtypeStruct((nseq, CONV_B_W - 1, D_BRANCH), F32),
                   jax.ShapeDtypeStruct((nseq, POOL_MAX - 1, D_BRANCH), F32),
                   jax.ShapeDtypeStruct((nseq, seq, D_BRANCH), F32)],
        scratch_shapes=[pltpu.VMEM((NB, 2 * SUBLANES, D_BRANCH), F32),
                        pltpu.VMEM((NB, 5 * SUBLANES, D_BRANCH), F32),
                        pltpu.VMEM((NB, 3 * SUBLANES, D_BRANCH), F32)],
        input_output_aliases={0: 0} if in_place else {},
        compiler_params=pltpu.CompilerParams(dimension_semantics=("arbitrary",),
                                             vmem_limit_bytes=VMEM_LIMIT),
        name="mixer_sample",
    )(xbuf, lw["norm_mix_g"], lw["w_in"], lw["conv_a_w"], lw["conv_b_w"], lw["vec"], lw["pool_bd"],
      lw["sgu_w_s"], lw["sgu_mask_s"], lw["sgu_bias_s"], lw["w_branch"], lw["w_out"],
      st_a, st_b, st_p)


def _ple_tail(x, p, gple_ref, wpg_ref, wpp_ref):
    gate = jax.nn.sigmoid(_bdot(_rms(x, gple_ref[...]), wpg_ref[...]))
    return x + gate * _bdot(p, wpp_ref[...])


def _dense_ffn_kernel(n_chunks, x_ref, p_ref, gffn_ref, wg_ref, wu_ref, wd_ref, gple_ref, wpg_ref, wpp_ref,
                      xo_ref):
    x = x_ref[...]
    xnb = _rms(x, gffn_ref[...]).astype(BF16)
    tf = wg_ref.shape[1] // n_chunks
    f = None
    for c in range(n_chunks):
        cols = slice(c * tf, (c + 1) * tf)
        h = _silu(jnp.dot(xnb, wg_ref[:, cols], preferred_element_type=F32)) * \
            jnp.dot(xnb, wu_ref[:, cols], preferred_element_type=F32)
        t = _bdot(h, wd_ref[cols, :])
        f = t if f is None else f + t
    xo_ref[...] = _ple_tail(x + f, p_ref[...], gple_ref, wpg_ref, wpp_ref)


def _dense_ffn(xbuf, p_all, layer, lw):
    t = xbuf.shape[0]
    d_ff = lw["ffn_w_gate"].shape[2]
    j = layer // 2
    rows = lambda i: (i, 0)
    return pl.pallas_call(
        functools.partial(_dense_ffn_kernel, 2),
        grid=(t // TM,),
        in_specs=[pl.BlockSpec((TM, D_MODEL), rows), pl.BlockSpec((None, TM, D_PLE), lambda i: (layer, i, 0)),
                  _const_spec((1, D_MODEL), layer), _const_spec((D_MODEL, d_ff), j),
                  _const_spec((D_MODEL, d_ff), j), _const_spec((d_ff, D_MODEL), j),
                  _const_spec((1, D_MODEL), layer), _const_spec((D_MODEL, D_MODEL), layer),
                  _const_spec((D_PLE, D_MODEL), layer)],
        out_specs=pl.BlockSpec((TM, D_MODEL), rows),
        out_shape=jax.ShapeDtypeStruct(xbuf.shape, F32),
        input_output_aliases={0: 0},
        compiler_params=pltpu.CompilerParams(dimension_semantics=("arbitrary",),
                                             vmem_limit_bytes=VMEM_LIMIT),
        name="dense_ffn_ple",
    )(xbuf, p_all, lw["norm_ffn_g"], lw["ffn_w_gate"], lw["ffn_w_up"], lw["ffn_w_down"],
      lw["norm_ple_g"], lw["ple_w_gate"], lw["ple_w_proj"])


def _router_kernel(n_first, xa_ref, xb_ref, g_ref, wr_ref, wrh_ref, idx_ref, wgt_ref, xn_ref):
    x = jnp.where(pl.program_id(0) < n_first, xa_ref[...], xb_ref[...])
    xn = _rms(x, g_ref[...])
    _store_row_tiled(xn_ref, xn)
    xh = xn.astype(BF16)
    xl = (xn - xh.astype(F32)).astype(BF16)
    d = jnp.dot(xh, wr_ref[...], preferred_element_type=F32) + \
        jnp.dot(xl, wrh_ref[...], preferred_element_type=F32)
    logits = d + pltpu.roll(d, LANES - N_EXPERTS, axis=1)
    lane = lax.broadcasted_iota(jnp.int32, logits.shape, 1)
    lane_f = lane.astype(F32)
    logits = jnp.where(lane < N_EXPERTS, logits, -jnp.inf)
    m1 = jnp.max(logits, axis=-1, keepdims=True)
    i1 = jnp.min(jnp.where(logits == m1, lane_f, float(LANES)), axis=-1, keepdims=True)
    rest = jnp.where(lane_f == i1, -jnp.inf, logits)
    m2 = jnp.max(rest, axis=-1, keepdims=True)
    i2 = jnp.min(jnp.where(rest == m2, lane_f, float(LANES)), axis=-1, keepdims=True)
    e2 = jnp.exp(m2 - m1)
    den = 1.0 + e2
    picks = jnp.where(lane == 0, i1, jnp.where(lane == 1, i2, 0.0))
    idx_ref[...] = jnp.transpose(picks)[0:TOP_K, :].astype(jnp.int32)
    wgt_ref[...] = jnp.where(lane == 0, 1.0 / den, jnp.where(lane == 1, e2 / den, 0.0))


def _router(xa, xb, layer, lw):
    j = layer // 2
    na, nb = xa.shape[0] // TM, xb.shape[0] // TM
    t = xa.shape[0] + xb.shape[0]
    rows = lambda i: (i, 0)
    return pl.pallas_call(
        functools.partial(_router_kernel, na),
        grid=(na + nb,),
        in_specs=[pl.BlockSpec((TM, D_MODEL), lambda i: (jnp.minimum(i, na - 1), 0)),
                  pl.BlockSpec((TM, D_MODEL), lambda i: (jnp.maximum(i - na, 0), 0)),
                  _const_spec((1, D_MODEL), layer), _const_spec((D_MODEL, LANES), j),
                  _const_spec((D_MODEL, LANES), j)],
        out_specs=[pl.BlockSpec((None, TOP_K, TM), lambda i: (i, 0, 0)), pl.BlockSpec((TM, LANES), rows),
                   pl.BlockSpec((TM * ROW_PIECES, LANES), rows)],
        out_shape=[jax.ShapeDtypeStruct((na + nb, TOP_K, TM), jnp.int32), jax.ShapeDtypeStruct((t, LANES), F32),
                   jax.ShapeDtypeStruct((t * ROW_PIECES, LANES), F32)],
        compiler_params=pltpu.CompilerParams(dimension_semantics=("arbitrary",)),
        name="router",
    )(xa, xb, lw["norm_ffn_g"], lw["router_w"], lw["router_wh"])


ROW_PIECES = D_MODEL // LANES
assert ROW_PIECES == SUBLANES
GATHER_UNROLL = 8


def _store_row_tiled(ref, x):
    for s in range(ROW_PIECES):
        ref[pl.ds(s, x.shape[0], stride=ROW_PIECES), :] = x[:, s * LANES:(s + 1) * LANES]


def _load_row_tiled(ref, n_rows):
    return jnp.concatenate([ref[pl.ds(s, n_rows, stride=ROW_PIECES), :] for s in range(ROW_PIECES)], axis=1)


def _start_row_gather(idx_ref, k, src_hbm, dst, sem):
    group = GATHER_UNROLL * ROW_PIECES

    def issue(it, carry):
        dst_base = pl.multiple_of(it * group, group)
        for u in range(GATHER_UNROLL):
            src_row = pl.multiple_of(idx_ref[k, it * GATHER_UNROLL + u] * ROW_PIECES, ROW_PIECES)
            pltpu.make_async_copy(src_hbm.at[pl.ds(src_row, ROW_PIECES)],
                                  dst.at[pl.ds(dst_base + u * ROW_PIECES, ROW_PIECES)],
                                  sem).start()
        return carry
    lax.fori_loop(0, dst.shape[0] // group, issue, 0)


def _wait_row_gather(src_hbm, dst, sem):
    pltpu.make_async_copy(src_hbm.at[pl.ds(0, dst.shape[0])], dst, sem).wait()


def _expert_kernel(n_fc, te_ref, na_ref, tokc_ref, tokn_ref, xn_hbm, wg_ref, wu_ref, wd_ref, ys_ref,
                   gbuf, xs, acc, sems):
    i, j = pl.program_id(0), pl.program_id(1)
    na = na_ref[0]
    slot = i % 2

    @pl.when(jnp.logical_and(i < na, j == 0))
    def _():
        @pl.when(i == 0)
        def _():
            _start_row_gather(tokc_ref, 0, xn_hbm, gbuf.at[0], sems.at[0])

        @pl.when(i + 1 < na)
        def _():
            _start_row_gather(tokn_ref, 0, xn_hbm, gbuf.at[1 - slot], sems.at[1 - slot])

        _wait_row_gather(xn_hbm, gbuf.at[slot], sems.at[slot])
        xs[...] = _load_row_tiled(gbuf.at[slot], TM).astype(BF16)

    def swiglu_piecewise(sink):
        h = _silu(jnp.dot(xs[...], wg_ref[...], preferred_element_type=F32)) * \
            jnp.dot(xs[...], wu_ref[...], preferred_element_type=F32)
        hb = h.astype(BF16)
        for q in range(D_MODEL // D_BRANCH):
            cs = slice(q * D_BRANCH, (q + 1) * D_BRANCH)
            sink(cs, jnp.dot(hb, wd_ref[:, cs], preferred_element_type=F32))

    def to_acc(first):
        def sink(cs, part):
            acc[:, cs] = part if first else acc[:, cs] + part
        return sink

    def to_out(with_acc):
        def sink(cs, part):
            tot = acc[:, cs] + part if with_acc else part
            for s in range(cs.start // LANES, cs.stop // LANES):
                lo = s * LANES - cs.start
                ys_ref[pl.ds(s, TM, stride=ROW_PIECES), :] = tot[:, lo:lo + LANES]
        return sink

    active = i < na
    if n_fc == 1:
        pl.when(active)(lambda: swiglu_piecewise(to_out(False)))
    else:
        pl.when(jnp.logical_and(active, j == 0))(lambda: swiglu_piecewise(to_acc(True)))
        if n_fc > 2:
            pl.when(jnp.logical_and(active, jnp.logical_and(j > 0, j < n_fc - 1)))(
                lambda: swiglu_piecewise(to_acc(False)))
        pl.when(jnp.logical_and(active, j == n_fc - 1))(lambda: swiglu_piecewise(to_out(True)))

    @pl.when(jnp.logical_and(i >= na, j == 0))
    def _():
        ys_ref[...] = jnp.zeros(ys_ref.shape, F32)


def _experts(xn_tiled, slot_tok, tile_expert, n_active, layer, lw):
    n_tiles = tile_expert.shape[0]
    d_ff = lw["moe_w_gate"].shape[3]
    n_fc = d_ff // TF_EXPERT
    m = layer // 2

    def fcol(i, j, na):
        return jnp.where(i < na[0], j, n_fc - 1)

    grid_spec = pltpu.PrefetchScalarGridSpec(
        num_scalar_prefetch=2,
        grid=(n_tiles, n_fc),
        in_specs=[
            pl.BlockSpec((None, 1, TM), lambda i, j, te, na: (i, 0, 0), memory_space=pltpu.SMEM),
            pl.BlockSpec((None, 1, TM), lambda i, j, te, na: (jnp.minimum(i + 1, n_tiles - 1), 0, 0),
                         memory_space=pltpu.SMEM),
            pl.BlockSpec(memory_space=pl.ANY),
            pl.BlockSpec((None, None, D_MODEL, TF_EXPERT),
                         lambda i, j, te, na: (m, te[i], 0, fcol(i, j, na))),
            pl.BlockSpec((None, None, D_MODEL, TF_EXPERT),
                         lambda i, j, te, na: (m, te[i], 0, fcol(i, j, na))),
            pl.BlockSpec((None, None, TF_EXPERT, D_MODEL),
                         lambda i, j, te, na: (m, te[i], fcol(i, j, na), 0)),
        ],
        out_specs=pl.BlockSpec((TM * ROW_PIECES, LANES), lambda i, j, te, na: (i, 0)),
        scratch_shapes=[pltpu.VMEM((2, TM * ROW_PIECES, LANES), F32), pltpu.VMEM((TM, D_MODEL), BF16),
                        pltpu.VMEM((TM, D_MODEL), F32), pltpu.SemaphoreType.DMA((2,))],
    )
    return pl.pallas_call(
        functools.partial(_expert_kernel, n_fc),
        grid_spec=grid_spec,
        out_shape=jax.ShapeDtypeStruct((n_tiles * TM * ROW_PIECES, LANES), F32),
        compiler_params=pltpu.CompilerParams(dimension_semantics=("arbitrary", "arbitrary"),
                                             vmem_limit_bytes=VMEM_LIMIT),
        name="moe_experts",
    )(tile_expert, n_active, slot_tok, slot_tok, xn_tiled, lw["moe_w_gate"], lw["moe_w_up"], lw["moe_w_down"])


def _combine_kernel(final, n_steps, pos0c_ref, pos1c_ref, pos0n_ref, pos1n_ref, x_ref, p_ref, wgt_ref, ys_hbm,
                    gple_ref, wpg_ref, wpp_ref, gfin_ref, o_ref, buf, xa, gn, sems):
    i = pl.program_id(0)

    def start(pos_refs, s):
        for k in range(TOP_K):
            _start_row_gather(pos_refs[k], 0, ys_hbm, buf.at[s, k], sems.at[s, k])

    @pl.when(i == 0)
    def _():
        start((pos0c_ref, pos1c_ref), 0)
        xa[1] = jnp.zeros((TM, D_MODEL), F32)
        gn[1] = jnp.zeros((TM, D_MODEL), BF16)

    def step(s):
        @pl.when(i + 1 < n_steps)
        def _():
            start((pos0n_ref, pos1n_ref), 1 - s)

        @pl.when(i < n_steps)
        def _():
            for k in range(TOP_K):
                _wait_row_gather(ys_hbm, buf.at[s, k], sems.at[s, k])

        wgt = wgt_ref[...]
        y0 = _load_row_tiled(buf.at[s, 0], TM)
        y1 = _load_row_tiled(buf.at[s, 1], TM)
        x1 = x_ref[...] + (y0 * wgt[:, 0:1] + y1 * wgt[:, 1:2])
        xa[s] = x1
        gn[s] = _rms(x1, gple_ref[...]).astype(BF16)

        gate = jax.nn.sigmoid(jnp.dot(gn[1 - s], wpg_ref[...], preferred_element_type=F32))
        x2 = xa[1 - s] + gate * _bdot(p_ref[...], wpp_ref[...])
        o_ref[...] = _rms(x2, gfin_ref[...]) if final else x2

    pl.when(i % 2 == 0)(lambda: step(0))
    pl.when(i % 2 == 1)(lambda: step(1))


def _combine(xbuf, p_all, layer, pos, wgt, ys, lw, final_g, row0):
    blk0 = row0 // TM
    n_rows = xbuf.shape[0]
    n_steps = n_rows // TM
    assert n_steps >= 2
    cur = lambda i: jnp.minimum(i, n_steps - 1)
    prev = lambda i: jnp.maximum(i - 1, 0)
    def pos_spec(k, ahead):
        return pl.BlockSpec((None, None, 1, TM), lambda i: (blk0 + cur(i + ahead), k, 0, 0),
                            memory_space=pltpu.SMEM)
    final = final_g is not None
    gfin = final_g if final else jnp.ones((1, D_MODEL), F32)
    return pl.pallas_call(
        functools.partial(_combine_kernel, final, n_steps),
        grid=(n_steps + 1,),
        in_specs=[pos_spec(0, 0), pos_spec(1, 0), pos_spec(0, 1), pos_spec(1, 1),
                  pl.BlockSpec((TM, D_MODEL), lambda i: (cur(i), 0)),
                  pl.BlockSpec((None, TM, D_PLE), lambda i: (layer, prev(i), 0)),
                  pl.BlockSpec((TM, LANES), lambda i: (blk0 + cur(i), 0)), pl.BlockSpec(memory_space=pl.ANY),
                  _const_spec((1, D_MODEL), layer), _const_spec((D_MODEL, D_MODEL), layer),
                  _const_spec((D_PLE, D_MODEL), layer), _const_spec((1, D_MODEL))],
        out_specs=pl.BlockSpec((TM, D_MODEL), lambda i: (prev(i), 0)),
        out_shape=jax.ShapeDtypeStruct((n_rows, D_MODEL), F32),
        scratch_shapes=[pltpu.VMEM((2, TOP_K, TM * ROW_PIECES, LANES), F32),
                        pltpu.VMEM((2, TM, D_MODEL), F32), pltpu.VMEM((2, TM, D_MODEL), BF16),
                        pltpu.SemaphoreType.DMA((2, TOP_K))],
        compiler_params=pltpu.CompilerParams(dimension_semantics=("arbitrary",),
                                             vmem_limit_bytes=VMEM_LIMIT),
        name="moe_combine_ple",
    )(pos, pos, pos, pos, xbuf, p_all, wgt, ys, lw["norm_ple_g"], lw["ple_w_gate"],
      lw["ple_w_proj"], gfin)


def _routing_tables(idx, n_tok):
    n_pairs = n_tok * TOP_K
    n_tiles = n_pairs // TM + N_EXPERTS
    experts = jnp.arange(N_EXPERTS, dtype=jnp.int32)
    e_flat = idx.reshape(-1)
    pair = jnp.arange(n_pairs, dtype=jnp.int32)
    pair_tok = (pair // (TOP_K * TM)) * TM + pair % TM
    onehot = (experts[:, None] == e_flat[None, :]).astype(jnp.int32)
    csum = jnp.cumsum(onehot, axis=1)
    sizes = csum[:, -1]
    padded = ((sizes + TM - 1) // TM) * TM
    pend = jnp.cumsum(padded)
    pstart = pend - padded
    pos = jnp.sum(onehot * (csum - 1 + pstart[:, None]), axis=0)
    fill_e = jnp.repeat(experts, TM)
    fill_r = jnp.tile(jnp.arange(TM, dtype=jnp.int32), N_EXPERTS)
    fill_need = jnp.sum((fill_e[:, None] == experts[None, :]) * (padded - sizes)[None, :], axis=1)
    keys = jnp.concatenate([e_flat, jnp.where(fill_r < fill_need, fill_e, N_EXPERTS)])
    toks = jnp.concatenate([pair_tok, jnp.zeros((N_EXPERTS * TM,), jnp.int32)])
    _, slot_tok = lax.sort((keys, toks), num_keys=1, is_stable=True)
    n_active = (pend[-1] // TM).astype(jnp.int32)
    tile_row = jnp.minimum(jnp.arange(n_tiles, dtype=jnp.int32), n_active - 1)
    tile_expert = jnp.minimum(
        jnp.sum((tile_row[:, None] * TM >= pend[None, :]).astype(jnp.int32), axis=1), N_EXPERTS - 1)
    return (slot_tok.reshape(n_tiles, 1, TM), tile_expert, n_active.reshape(1),
            pos.reshape(n_tok // TM, TOP_K, 1, TM))


def _sgu_tables(sgu_w, sgu_b, seq):
    depth = sgu_w.shape[0]
    r = np.arange(CHUNK)
    bias = jnp.repeat(jnp.swapaxes(sgu_b, 1, 2), GROUP_DIM, axis=2)
    if seq >= CHUNK:
        w = sgu_w
        mask = r[None, :] <= r[:, None]
    else:
        reps = CHUNK // seq
        w = jnp.tile(sgu_w[:, :, :seq, :seq], (1, 1, reps, reps))
        mask = (r[:, None] // seq == r[None, :] // seq) & (r[None, :] % seq <= r[:, None] % seq)
        bias = jnp.tile(bias[:, :seq], (1, reps, 1))
    mask = np.tile(mask.astype(np.float32), (N_GROUPS, 1))
    return w.reshape(depth, N_GROUPS * CHUNK, CHUNK), jnp.asarray(mask), bias


def _block_diag(w):
    out = jnp.zeros((w.shape[0], D_BRANCH, D_BRANCH), w.dtype)
    for g in range(N_GROUPS):
        out = out.at[:, g * GROUP_DIM:(g + 1) * GROUP_DIM, g * GROUP_DIM:(g + 1) * GROUP_DIM].set(w[:, g])
    return out


def kernel(x_prompt, x_sample, state_conv_a, state_conv_b, state_pool, p_prompt, p_sample, norm_mix_g, w_in, conv_a_w, conv_b_w, conv_b_bias, conv_b_ln_g, conv_b_ln_b, pool_w, pool_scale, sgu_ln_g, sgu_ln_b, sgu_w, sgu_b, w_branch, w_out, norm_ffn_g, ffn_w_gate, ffn_w_up, ffn_w_down, router_w, moe_w_gate, moe_w_up, moe_w_down, norm_ple_g, ple_w_gate, ple_w_proj, final_norm_g):
    batch, seq, _ = x_prompt.shape
    nseq, dseq, _ = x_sample.shape
    depth = w_in.shape[0]
    assert depth % 2 == 0
    assert PAST_LEN + 1 >= POOL_MAX
    assert seq % TL == 0 and TL % CHUNK == 0 and TL >= CONV_B_W + 1
    n_prompt, n_sample = batch * seq, nseq * dseq
    n_tok = n_prompt + n_sample
    assert n_prompt % TM == 0 and n_sample % TM == 0

    xp, xs = x_prompt.reshape(n_prompt, D_MODEL), x_sample.reshape(n_sample, D_MODEL)
    pp, ps = p_prompt.reshape(depth, n_prompt, D_PLE), p_sample.reshape(depth, n_sample, D_PLE)
    rows3 = lambda a: a.reshape(a.shape[0], 1, a.shape[1])
    zeros = jnp.zeros_like(conv_b_bias)
    sw_p, sm_p, sb_p = _sgu_tables(sgu_w, sgu_b, seq)
    sw_s, sm_s, sb_s = _sgu_tables(sgu_w, sgu_b, dseq)
    r_hi = router_w.astype(BF16)
    r_lo = (router_w - r_hi.astype(F32)).astype(BF16)
    lane_pad = lambda a: jnp.pad(a, ((0, 0), (0, 0), (0, LANES - a.shape[2])))
    lw = {
        "norm_mix_g": rows3(norm_mix_g), "w_in": w_in.astype(BF16),
        "conv_a_w": conv_a_w, "conv_b_w": conv_b_w,
        "vec": jnp.stack([conv_b_bias, conv_b_ln_g, conv_b_ln_b, pool_scale, sgu_ln_g, sgu_ln_b, zeros, zeros],
                         axis=1),
        "pool_bd": _block_diag(pool_w).astype(BF16),
        "sgu_w_p": sw_p, "sgu_mask_p": sm_p, "sgu_bias_p": sb_p,
        "sgu_w_s": sw_s, "sgu_mask_s": sm_s, "sgu_bias_s": sb_s,
        "w_branch": w_branch.astype(BF16), "w_out": w_out.astype(BF16),
        "norm_ffn_g": rows3(norm_ffn_g), "norm_ple_g": rows3(norm_ple_g),
        "ple_w_gate": ple_w_gate.astype(BF16), "ple_w_proj": ple_w_proj.astype(BF16),
        "ffn_w_gate": ffn_w_gate.astype(BF16), "ffn_w_up": ffn_w_up.astype(BF16),
        "ffn_w_down": ffn_w_down.astype(BF16),
        "router_w": lane_pad(jnp.concatenate([r_hi, r_lo], axis=2)), "router_wh": lane_pad(r_hi),
        "moe_w_gate": moe_w_gate.astype(BF16), "moe_w_up": moe_w_up.astype(BF16),
        "moe_w_down": moe_w_down.astype(BF16),
    }

    states_p, states_s = [], []
    y_prompt = y_sample = None
    for i in range(depth):
        in_place = i > 0
        xp, a_p, b_p, pool_p, v_p = _mixer_prompt(xp, lw, i, batch, seq, in_place)
        xs, a_s, b_s, pool_s, v_s = _mixer_sample(xs, lw, i, state_conv_a, state_conv_b, state_pool,
                                                  nseq, dseq, in_place)
        states_p.append((a_p, b_p, pool_p, v_p))
        states_s.append((a_s, b_s, pool_s, v_s))

        last = i == depth - 1
        if i % 2 == 0:
            xp = _dense_ffn(xp, pp, i, lw)
            xs = _dense_ffn(xs, ps, i, lw)
        else:
            idx, wgt, xn_tiled = _router(xp, xs, i, lw)
            slot_tok, tile_expert, n_active, pos = _routing_tables(idx, n_tok)
            ys = _experts(xn_tiled, slot_tok, tile_expert, n_active, i, lw)
            fin = final_norm_g.reshape(1, D_MODEL) if last else None
            xp = _combine(xp, pp, i, pos, wgt, ys, lw, fin, 0)
            xs = _combine(xs, ps, i, pos, wgt, ys, lw, fin, n_prompt)
            if last:
                y_prompt, y_sample = xp, xs

    stack = lambda k, sts: jnp.stack([s[k] for s in sts])
    return (y_prompt.reshape(batch, seq, D_MODEL), y_sample.reshape(nseq, dseq, D_MODEL),
            stack(0, states_p), stack(1, states_p), stack(2, states_p), stack(3, states_p),
            stack(0, states_s), stack(1, states_s), stack(2, states_s), stack(3, states_s))
```

```python
import functools

import jax
import jax.numpy as jnp
import numpy as np
from jax import lax
from jax.experimental import pallas as pl
from jax.experimental.pallas import tpu as pltpu

F32 = jnp.float32
BF16 = jnp.bfloat16

D_MODEL = 1024
N_BRANCH = 4
D_BRANCH = 256
N_GROUPS = 4
GROUP_DIM = 64
CONV_A_W = 3
CONV_B_W = 31
POOL_WINDOWS = (2, 4, 8, 16)
POOL_MAX = 16
CHUNK = 128
PAST_LEN = 16384
D_PLE = 256
N_EXPERTS = 8
TOP_K = 2
EPS = 1e-6

OFF_A_H = 0
OFF_A_B = 256
OFF_A_C = 512
OFF_CF_A = 768
OFF_CF_B = 1024
OFF_POOL = 1280
OFF_U = 1536
OFF_V = 1792
OFF_GATE = 2048

LANES = 128
SUBLANES = 8
TM = 512
TL = 512
NB = 64
TF_EXPERT = 1792
ROW_CHUNK = 32
SEQ_CHUNK = 8
VMEM_LIMIT = 56 * 1024 * 1024

V_CB_BIAS, V_CB_LN_G, V_CB_LN_B, V_POOL_SCALE, V_SGU_LN_G, V_SGU_LN_B = range(6)


def _rms(x, g):
    return x * lax.rsqrt(jnp.mean(x * x, axis=-1, keepdims=True) + EPS) * g


def _ln(x, g, b):
    xc = x - jnp.mean(x, axis=-1, keepdims=True)
    var = jnp.mean(xc * xc, axis=-1, keepdims=True)
    return xc * lax.rsqrt(var + EPS) * g + b


def _bdot(a, w):
    return jnp.dot(a.astype(BF16), w, preferred_element_type=F32)


def _silu(x):
    return x * jax.nn.sigmoid(x)


def _const_spec(shape, layer=None):
    nd = len(shape)
    if layer is None:
        return pl.BlockSpec(shape, lambda *_: (0,) * nd, pipeline_mode=pl.Buffered(1))
    return pl.BlockSpec((None,) + tuple(shape), lambda *_: (layer,) + (0,) * nd, pipeline_mode=pl.Buffered(1))


def _sgu(u, v, swm, sbias):
    m = v.shape[0]
    group = lax.broadcasted_iota(jnp.int32, (CHUNK, D_BRANCH), 1) // GROUP_DIM
    outs = []
    for c in range(m // CHUNK):
        vc = v[c * CHUNK:(c + 1) * CHUNK].astype(BF16)
        r = jnp.dot(swm, vc, preferred_element_type=F32)
        s = sbias
        for g in range(N_GROUPS):
            s = s + jnp.where(group == g, r[g * CHUNK:(g + 1) * CHUNK], 0.0)
        outs.append(s)
    return u * jnp.concatenate(outs, axis=0)


GATE_PIECES_PER_BRANCH = D_MODEL // D_BRANCH
N_GATE_PIECES = N_BRANCH * GATE_PIECES_PER_BRANCH


def _gate_emitter(xnb, win_ref):
    pieces = []

    def emit(n=1):
        for _ in range(n):
            if len(pieces) < N_GATE_PIECES:
                lo = OFF_GATE + len(pieces) * D_BRANCH
                z = jnp.dot(xnb, win_ref[:, lo:lo + D_BRANCH], preferred_element_type=F32)
                pieces.append(jax.nn.sigmoid(z))
    return pieces, emit


def _gated_merge(x, branches, gates, wbr_ref, wout_ref):
    bb = [b.astype(BF16) for b in branches]
    cols = []
    for q in range(GATE_PIECES_PER_BRANCH):
        cs = slice(q * D_BRANCH, (q + 1) * D_BRANCH)
        m = None
        for i in range(N_BRANCH):
            t = gates[i * GATE_PIECES_PER_BRANCH + q] * \
                jnp.dot(bb[i], wbr_ref[i, :, cs], preferred_element_type=F32)
            m = t if m is None else m + t
        cols.append(m.astype(BF16))
    return x + jnp.dot(jnp.concatenate(cols, axis=1), wout_ref[...], preferred_element_type=F32)


def _pool_lane_windows(half):
    small, big = POOL_WINDOWS[2 * half], POOL_WINDOWS[2 * half + 1]
    lane = lax.broadcasted_iota(jnp.int32, (1, LANES), 1)
    is_big = lane >= GROUP_DIM
    return small, big, is_big


def _mixer_prompt_kernel(x_ref, g_ref, win_ref, caw_ref, cbw_ref, vec_ref, pw_ref, sw_ref, smask_ref,
                         sbias_ref, wbr_ref, wout_ref,
                         xo_ref, sta_ref, stb_ref, stp_ref, stv_ref,
                         sa, sb, sp, sh):
    i = pl.program_id(1)
    ha, hb, hp = SUBLANES, 4 * SUBLANES, 2 * SUBLANES

    @pl.when(i == 0)
    def _():
        sa[0:ha, :] = jnp.zeros((ha, D_BRANCH), F32)
        sb[0:hb, :] = jnp.zeros((hb, D_BRANCH), F32)
        sp[0:hp, :] = jnp.zeros((hp, D_BRANCH), F32)

    @pl.when(i > 0)
    def _():
        sa[0:ha, :] = sa[TL:TL + ha, :]
        sb[0:hb, :] = sb[TL:TL + hb, :]
        sp[0:hp, :] = sp[TL:TL + hp, :]

    x = x_ref[...]
    xnb = _rms(x, g_ref[...]).astype(BF16)
    vec = vec_ref[...]

    def zc(off):
        return jnp.dot(xnb, win_ref[:, off:off + D_BRANCH], preferred_element_type=F32)

    gates, emit_gate = _gate_emitter(xnb, win_ref)
    n_chunks = TL // ROW_CHUNK

    def spread(c, total):
        return ((c + 1) * total) // n_chunks - (c * total) // n_chunks

    def tap_conv(s_ref, base, w_ref, ntaps, n_gates=0, shifted=None):
        src = {}
        if shifted is not None:
            reach = {}
            for k in range(ntaps):
                q, r = divmod(base + k, SUBLANES)
                reach[r] = max(reach.get(r, 0), q)
            for r, qmax in sorted(reach.items()):
                if r:
                    n = TL + SUBLANES * qmax
                    shifted[r - 1, 0:n, :] = s_ref[r:r + n, :]
                    src[r] = shifted.at[r - 1]
                    emit_gate(r % 2)
                else:
                    src[r] = s_ref
        outs = []
        for c in range(n_chunks):
            acc = None
            for k in range(ntaps):
                if shifted is None:
                    seg = s_ref[pl.ds(base + k + c * ROW_CHUNK, ROW_CHUNK), :]
                else:
                    q, r = divmod(base + k, SUBLANES)
                    seg = src[r][pl.ds(SUBLANES * q + c * ROW_CHUNK, ROW_CHUNK), :]
                term = seg * w_ref[k:k + 1, :]
                acc = term if acc is None else acc + term
            outs.append(acc)
            emit_gate(spread(c, n_gates))
        return jnp.concatenate(outs, axis=0)

    sa[ha:ha + TL, :] = zc(OFF_A_C) * zc(OFF_A_H)
    br_a = zc(OFF_A_B) * tap_conv(sa, ha - (CONV_A_W - 1), caw_ref, CONV_A_W)
    sta_ref[0] = sa[TL + ha - (CONV_A_W - 1):TL + ha, :]

    sb[hb:hb + TL, :] = zc(OFF_CF_A) * jax.nn.sigmoid(zc(OFF_CF_B))
    yb = tap_conv(sb, hb - (CONV_B_W - 1), cbw_ref, CONV_B_W, n_gates=N_GATE_PIECES // 2, shifted=sh)
    yb = yb + vec[V_CB_BIAS:V_CB_BIAS + 1]
    br_b = _silu(_ln(yb, vec[V_CB_LN_G:V_CB_LN_G + 1], vec[V_CB_LN_B:V_CB_LN_B + 1]))
    stb_ref[0] = sb[TL + hb - (CONV_B_W - 1):TL + hb, :]

    sp[hp:hp + TL, :] = zc(OFF_POOL)
    halves = []
    for half in range(2):
        small, big, is_big = _pool_lane_windows(half)
        lanes = slice(half * LANES, (half + 1) * LANES)
        wl = jnp.where(is_big, float(big), float(small))
        outs = []
        for c in range(n_chunks):
            r0 = hp + c * ROW_CHUNK
            acc = None
            for j in range(big):
                seg = sp[pl.ds(r0 - j, ROW_CHUNK), lanes]
                if j >= small:
                    seg = jnp.where(is_big, seg, 0.0)
                acc = seg if acc is None else acc + seg
            pos = i * TL + c * ROW_CHUNK + lax.broadcasted_iota(jnp.int32, (ROW_CHUNK, LANES), 0)
            cnt = jnp.minimum(wl, (pos + 1).astype(F32))
            outs.append(acc / cnt - sp[pl.ds(r0, ROW_CHUNK), lanes])
            if half == 1:
                emit_gate(spread(c, N_GATE_PIECES // 4))
        halves.append(jnp.concatenate(outs, axis=0))
    pooled = jnp.concatenate(halves, axis=1)
    br_c = _bdot(pooled, pw_ref[...]) * vec[V_POOL_SCALE:V_POOL_SCALE + 1]
    stp_ref[0] = sp[TL + hp - (POOL_MAX - 1):TL + hp, :]

    v = _ln(zc(OFF_V), vec[V_SGU_LN_G:V_SGU_LN_G + 1], vec[V_SGU_LN_B:V_SGU_LN_B + 1])
    stv_ref[0] = v[TL - CHUNK:TL]
    swm = jnp.where(smask_ref[...] > 0, sw_ref[...], 0.0).astype(BF16)
    br_d = _sgu(zc(OFF_U), v, swm, sbias_ref[...])

    emit_gate(N_GATE_PIECES)
    xo_ref[...] = _gated_merge(x, (br_a, br_b, br_c, br_d), gates, wbr_ref, wout_ref)


def _mixer_sample_kernel(x_ref, g_ref, win_ref, caw_ref, cbw_ref, vec_ref, pw_ref, sw_ref, smask_ref,
                         sbias_ref, wbr_ref, wout_ref, ina_ref, inb_ref, inp_ref,
                         xo_ref, sta_ref, stb_ref, stp_ref, stv_ref,
                         sa, sb, sp):
    seq = SUBLANES
    ha, hb, hp = SUBLANES, 4 * SUBLANES, 2 * SUBLANES
    sa[:, ha - (CONV_A_W - 1):ha, :] = ina_ref[...]
    sb[:, hb - (CONV_B_W - 1):hb, :] = inb_ref[...]
    sp[:, hp - (POOL_MAX - 1):hp, :] = inp_ref[...]

    x = x_ref[...]
    xnb = _rms(x, g_ref[...]).astype(BF16)
    vec = vec_ref[...]

    def zc(off):
        return jnp.dot(xnb, win_ref[:, off:off + D_BRANCH], preferred_element_type=F32)

    def to3(a):
        return a.reshape(NB, seq, D_BRANCH)

    def to2(a):
        return a.reshape(NB * seq, D_BRANCH)

    gates, emit_gate = _gate_emitter(xnb, win_ref)
    n_chunks = NB // SEQ_CHUNK

    def tap_conv(s_ref, base, w_ref, ntaps, gates_per_chunk=0):
        outs = []
        for c in range(n_chunks):
            acc = None
            for k in range(ntaps):
                seg = s_ref[c * SEQ_CHUNK:(c + 1) * SEQ_CHUNK, pl.ds(base + k, seq), :]
                term = seg * w_ref[k:k + 1, :][None]
                acc = term if acc is None else acc + term
            outs.append(acc)
            emit_gate(gates_per_chunk)
        return to2(jnp.concatenate(outs, axis=0))

    sa[:, ha:ha + seq, :] = to3(zc(OFF_A_C) * zc(OFF_A_H))
    br_a = zc(OFF_A_B) * tap_conv(sa, ha - (CONV_A_W - 1), caw_ref, CONV_A_W)
    sta_ref[...] = sa[:, ha + seq - (CONV_A_W - 1):ha + seq, :]

    sb[:, hb:hb + seq, :] = to3(zc(OFF_CF_A) * jax.nn.sigmoid(zc(OFF_CF_B)))
    yb = tap_conv(sb, hb - (CONV_B_W - 1), cbw_ref, CONV_B_W, gates_per_chunk=-(-N_GATE_PIECES // n_chunks))
    yb = yb + vec[V_CB_BIAS:V_CB_BIAS + 1]
    br_b = _silu(_ln(yb, vec[V_CB_LN_G:V_CB_LN_G + 1], vec[V_CB_LN_B:V_CB_LN_B + 1]))
    stb_ref[...] = sb[:, hb + seq - (CONV_B_W - 1):hb + seq, :]

    sp[:, hp:hp + seq, :] = to3(zc(OFF_POOL))
    halves = []
    for half in range(2):
        small, big, is_big = _pool_lane_windows(half)
        lanes = slice(half * LANES, (half + 1) * LANES)
        wl = jnp.where(is_big, float(big), float(small))[None]
        outs = []
        for c in range(NB // SEQ_CHUNK):
            rows = slice(c * SEQ_CHUNK, (c + 1) * SEQ_CHUNK)
            acc = None
            for j in range(big):
                seg = sp[rows, pl.ds(hp - j, seq), lanes]
                if j >= small:
                    seg = jnp.where(is_big[None], seg, 0.0)
                acc = seg if acc is None else acc + seg
            outs.append(acc / wl - sp[rows, pl.ds(hp, seq), lanes])
        halves.append(jnp.concatenate(outs, axis=0).reshape(NB * seq, LANES))
    pooled = jnp.concatenate(halves, axis=1)
    br_c = _bdot(pooled, pw_ref[...]) * vec[V_POOL_SCALE:V_POOL_SCALE + 1]
    stp_ref[...] = sp[:, hp + seq - (POOL_MAX - 1):hp + seq, :]

    v = _ln(zc(OFF_V), vec[V_SGU_LN_G:V_SGU_LN_G + 1], vec[V_SGU_LN_B:V_SGU_LN_B + 1])
    stv_ref[...] = to3(v)
    swm = jnp.where(smask_ref[...] > 0, sw_ref[...], 0.0).astype(BF16)
    br_d = _sgu(zc(OFF_U), v, swm, sbias_ref[...])

    emit_gate(N_GATE_PIECES)
    xo_ref[...] = _gated_merge(x, (br_a, br_b, br_c, br_d), gates, wbr_ref, wout_ref)


def _mixer_weight_specs(layer):
    return [
        _const_spec((1, D_MODEL), layer),
        _const_spec((D_MODEL, OFF_GATE + N_BRANCH * D_MODEL), layer),
        _const_spec((CONV_A_W, D_BRANCH), layer),
        _const_spec((CONV_B_W, D_BRANCH), layer),
        _const_spec((SUBLANES, D_BRANCH), layer),
        _const_spec((D_BRANCH, D_BRANCH), layer),
        _const_spec((N_GROUPS * CHUNK, CHUNK), layer),
        _const_spec((N_GROUPS * CHUNK, CHUNK)),
        _const_spec((CHUNK, D_BRANCH), layer),
        _const_spec((N_BRANCH, D_BRANCH, D_MODEL), layer),
        _const_spec((D_MODEL, D_MODEL), layer),
    ]


def _mixer_prompt(xbuf, lw, layer, batch, seq, in_place):
    nt = seq // TL
    rows = lambda b, i: (b * nt + i, 0)
    st = lambda n: pl.BlockSpec((1, n, D_BRANCH), lambda b, i: (b, 0, 0))
    return pl.pallas_call(
        _mixer_prompt_kernel,
        grid=(batch, nt),
        in_specs=[pl.BlockSpec((TL, D_MODEL), rows)] + _mixer_weight_specs(layer),
        out_specs=[pl.BlockSpec((TL, D_MODEL), rows), st(CONV_A_W - 1), st(CONV_B_W - 1),
                   st(POOL_MAX - 1), st(CHUNK)],
        out_shape=[jax.ShapeDtypeStruct(xbuf.shape, F32),
                   jax.ShapeDtypeStruct((batch, CONV_A_W - 1, D_BRANCH), F32),
                   jax.ShapeDtypeStruct((batch, CONV_B_W - 1, D_BRANCH), F32),
                   jax.ShapeDtypeStruct((batch, POOL_MAX - 1, D_BRANCH), F32),
                   jax.ShapeDtypeStruct((batch, CHUNK, D_BRANCH), F32)],
        scratch_shapes=[pltpu.VMEM((TL + SUBLANES, D_BRANCH), F32),
                        pltpu.VMEM((TL + 4 * SUBLANES, D_BRANCH), F32),
                        pltpu.VMEM((TL + 2 * SUBLANES, D_BRANCH), F32),
                        pltpu.VMEM((SUBLANES - 1, TL + 4 * SUBLANES, D_BRANCH), F32)],
        input_output_aliases={0: 0} if in_place else {},
        compiler_params=pltpu.CompilerParams(dimension_semantics=("arbitrary", "arbitrary"),
                                             vmem_limit_bytes=VMEM_LIMIT),
        name="mixer_prompt",
    )(xbuf, lw["norm_mix_g"], lw["w_in"], lw["conv_a_w"], lw["conv_b_w"], lw["vec"], lw["pool_bd"],
      lw["sgu_w_p"], lw["sgu_mask_p"], lw["sgu_bias_p"], lw["w_branch"], lw["w_out"])


def _mixer_sample(xbuf, lw, layer, st_a, st_b, st_p, nseq, seq, in_place):
    assert seq == SUBLANES and nseq % NB == 0
    rows = lambda i: (i, 0)
    st = lambda n: pl.BlockSpec((NB, n, D_BRANCH), lambda i: (i, 0, 0))
    st_in = lambda n: pl.BlockSpec((None, NB, n, D_BRANCH), lambda i: (layer, i, 0, 0))
    return pl.pallas_call(
        _mixer_sample_kernel,
        grid=(nseq // NB,),
        in_specs=[pl.BlockSpec((NB * seq, D_MODEL), rows)] + _mixer_weight_specs(layer)
                 + [st_in(CONV_A_W - 1), st_in(CONV_B_W - 1), st_in(POOL_MAX - 1)],
        out_specs=[pl.BlockSpec((NB * seq, D_MODEL), rows), st(CONV_A_W - 1), st(CONV_B_W - 1),
                   st(POOL_MAX - 1), st(seq)],
        out_shape=[jax.ShapeDtypeStruct(xbuf.shape, F32),
                   jax.ShapeDtypeStruct((nseq, CONV_A_W - 1, D_BRANCH), F32),
                   jax.ShapeDtypeStruct((nseq, CONV_B_W - 1, D_BRANCH), F32),
                   jax.ShapeDtypeStruct((nseq, POOL_MAX - 1, D_BRANCH), F32),
                   jax.ShapeDtypeStruct((nseq, seq, D_BRANCH), F32)],
        scratch_shapes=[pltpu.VMEM((NB, 2 * SUBLANES, D_BRANCH), F32),
                        pltpu.VMEM((NB, 5 * SUBLANES, D_BRANCH), F32),
                        pltpu.VMEM((NB, 3 * SUBLANES, D_BRANCH), F32)],
        input_output_aliases={0: 0} if in_place else {},
        compiler_params=pltpu.CompilerParams(dimension_semantics=("arbitrary",),
                                             vmem_limit_bytes=VMEM_LIMIT),
        name="mixer_sample",
    )(xbuf, lw["norm_mix_g"], lw["w_in"], lw["conv_a_w"], lw["conv_b_w"], lw["vec"], lw["pool_bd"],
      lw["sgu_w_s"], lw["sgu_mask_s"], lw["sgu_bias_s"], lw["w_branch"], lw["w_out"],
      st_a, st_b, st_p)


def _ple_tail(x, p, gple_ref, wpg_ref, wpp_ref):
    gate = jax.nn.sigmoid(_bdot(_rms(x, gple_ref[...]), wpg_ref[...]))
    return x + gate * _bdot(p, wpp_ref[...])


def _dense_ffn_kernel(n_chunks, x_ref, p_ref, gffn_ref, wg_ref, wu_ref, wd_ref, gple_ref, wpg_ref, wpp_ref,
                      xo_ref):
    x = x_ref[...]
    xnb = _rms(x, gffn_ref[...]).astype(BF16)
    tf = wg_ref.shape[1] // n_chunks
    f = None
    for c in range(n_chunks):
        cols = slice(c * tf, (c + 1) * tf)
        h = _silu(jnp.dot(xnb, wg_ref[:, cols], preferred_element_type=F32)) * \
            jnp.dot(xnb, wu_ref[:, cols], preferred_element_type=F32)
        t = _bdot(h, wd_ref[cols, :])
        f = t if f is None else f + t
    xo_ref[...] = _ple_tail(x + f, p_ref[...], gple_ref, wpg_ref, wpp_ref)


def _dense_ffn(xbuf, p_all, layer, lw):
    t = xbuf.shape[0]
    d_ff = lw["ffn_w_gate"].shape[2]
    j = layer // 2
    rows = lambda i: (i, 0)
    return pl.pallas_call(
        functools.partial(_dense_ffn_kernel, 2),
        grid=(t // TM,),
        in_specs=[pl.BlockSpec((TM, D_MODEL), rows), pl.BlockSpec((None, TM, D_PLE), lambda i: (layer, i, 0)),
                  _const_spec((1, D_MODEL), layer), _const_spec((D_MODEL, d_ff), j),
                  _const_spec((D_MODEL, d_ff), j), _const_spec((d_ff, D_MODEL), j),
                  _const_spec((1, D_MODEL), layer), _const_spec((D_MODEL, D_MODEL), layer),
                  _const_spec((D_PLE, D_MODEL), layer)],
        out_specs=pl.BlockSpec((TM, D_MODEL), rows),
        out_shape=jax.ShapeDtypeStruct(xbuf.shape, F32),
        input_output_aliases={0: 0},
        compiler_params=pltpu.CompilerParams(dimension_semantics=("arbitrary",),
                                             vmem_limit_bytes=VMEM_LIMIT),
        name="dense_ffn_ple",
    )(xbuf, p_all, lw["norm_ffn_g"], lw["ffn_w_gate"], lw["ffn_w_up"], lw["ffn_w_down"],
      lw["norm_ple_g"], lw["ple_w_gate"], lw["ple_w_proj"])


def _router_kernel(n_first, xa_ref, xb_ref, g_ref, wr_ref, wrh_ref, idx_ref, wgt_ref, xn_ref):
    x = jnp.where(pl.program_id(0) < n_first, xa_ref[...], xb_ref[...])
    xn = _rms(x, g_ref[...])
    _store_row_tiled(xn_ref, xn)
    xh = xn.astype(BF16)
    xl = (xn - xh.astype(F32)).astype(BF16)
    d = jnp.dot(xh, wr_ref[...], preferred_element_type=F32) + \
        jnp.dot(xl, wrh_ref[...], preferred_element_type=F32)
    logits = d + pltpu.roll(d, LANES - N_EXPERTS, axis=1)
    lane = lax.broadcasted_iota(jnp.int32, logits.shape, 1)
    lane_f = lane.astype(F32)
    logits = jnp.where(lane < N_EXPERTS, logits, -jnp.inf)
    m1 = jnp.max(logits, axis=-1, keepdims=True)
    i1 = jnp.min(jnp.where(logits == m1, lane_f, float(LANES)), axis=-1, keepdims=True)
    rest = jnp.where(lane_f == i1, -jnp.inf, logits)
    m2 = jnp.max(rest, axis=-1, keepdims=True)
    i2 = jnp.min(jnp.where(rest == m2, lane_f, float(LANES)), axis=-1, keepdims=True)
    e2 = jnp.exp(m2 - m1)
    den = 1.0 + e2
    picks = jnp.where(lane == 0, i1, jnp.where(lane == 1, i2, 0.0))
    idx_ref[...] = jnp.transpose(picks)[0:TOP_K, :].astype(jnp.int32)
    wgt_ref[...] = jnp.where(lane == 0, 1.0 / den, jnp.where(lane == 1, e2 / den, 0.0))


def _router(xa, xb, layer, lw):
    j = layer // 2
    na, nb = xa.shape[0] // TM, xb.shape[0] // TM
    t = xa.shape[0] + xb.shape[0]
    rows = lambda i: (i, 0)
    return pl.pallas_call(
        functools.partial(_router_kernel, na),
        grid=(na + nb,),
        in_specs=[pl.BlockSpec((TM, D_MODEL), lambda i: (jnp.minimum(i, na - 1), 0)),
                  pl.BlockSpec((TM, D_MODEL), lambda i: (jnp.maximum(i - na, 0), 0)),
                  _const_spec((1, D_MODEL), layer), _const_spec((D_MODEL, LANES), j),
                  _const_spec((D_MODEL, LANES), j)],
        out_specs=[pl.BlockSpec((None, TOP_K, TM), lambda i: (i, 0, 0)), pl.BlockSpec((TM, LANES), rows),
                   pl.BlockSpec((TM * ROW_PIECES, LANES), rows)],
        out_shape=[jax.ShapeDtypeStruct((na + nb, TOP_K, TM), jnp.int32), jax.ShapeDtypeStruct((t, LANES), F32),
                   jax.ShapeDtypeStruct((t * ROW_PIECES, LANES), F32)],
        compiler_params=pltpu.CompilerParams(dimension_semantics=("arbitrary",)),
        name="router",
    )(xa, xb, lw["norm_ffn_g"], lw["router_w"], lw["router_wh"])


ROW_PIECES = D_MODEL // LANES
assert ROW_PIECES == SUBLANES
GATHER_UNROLL = 8


def _store_row_tiled(ref, x):
    for s in range(ROW_PIECES):
        ref[pl.ds(s, x.shape[0], stride=ROW_PIECES), :] = x[:, s * LANES:(s + 1) * LANES]


def _load_row_tiled(ref, n_rows):
    return jnp.concatenate([ref[pl.ds(s, n_rows, stride=ROW_PIECES), :] for s in range(ROW_PIECES)], axis=1)


def _start_row_gather(idx_ref, k, src_hbm, dst, sem):
    group = GATHER_UNROLL * ROW_PIECES

    def issue(it, carry):
        dst_base = pl.multiple_of(it * group, group)
        for u in range(GATHER_UNROLL):
            src_row = pl.multiple_of(idx_ref[k, it * GATHER_UNROLL + u] * ROW_PIECES, ROW_PIECES)
            pltpu.make_async_copy(src_hbm.at[pl.ds(src_row, ROW_PIECES)],
                                  dst.at[pl.ds(dst_base + u * ROW_PIECES, ROW_PIECES)],
                                  sem).start()
        return carry
    lax.fori_loop(0, dst.shape[0] // group, issue, 0)


def _wait_row_gather(src_hbm, dst, sem):
    pltpu.make_async_copy(src_hbm.at[pl.ds(0, dst.shape[0])], dst, sem).wait()


def _expert_kernel(n_fc, te_ref, na_ref, tokc_ref, tokn_ref, xn_hbm, wg_ref, wu_ref, wd_ref, ys_ref,
                   gbuf, xs, acc, sems):
    i, j = pl.program_id(0), pl.program_id(1)
    na = na_ref[0]
    slot = i % 2

    @pl.when(jnp.logical_and(i < na, j == 0))
    def _():
        @pl.when(i == 0)
        def _():
            _start_row_gather(tokc_ref, 0, xn_hbm, gbuf.at[0], sems.at[0])

        @pl.when(i + 1 < na)
        def _():
            _start_row_gather(tokn_ref, 0, xn_hbm, gbuf.at[1 - slot], sems.at[1 - slot])

        _wait_row_gather(xn_hbm, gbuf.at[slot], sems.at[slot])
        xs[...] = _load_row_tiled(gbuf.at[slot], TM).astype(BF16)

    def swiglu_piecewise(sink):
        h = _silu(jnp.dot(xs[...], wg_ref[...], preferred_element_type=F32)) * \
            jnp.dot(xs[...], wu_ref[...], preferred_element_type=F32)
        hb = h.astype(BF16)
        for q in range(D_MODEL // D_BRANCH):
            cs = slice(q * D_BRANCH, (q + 1) * D_BRANCH)
            sink(cs, jnp.dot(hb, wd_ref[:, cs], preferred_element_type=F32))

    def to_acc(first):
        def sink(cs, part):
            acc[:, cs] = part if first else acc[:, cs] + part
        return sink

    def to_out(with_acc):
        def sink(cs, part):
            tot = acc[:, cs] + part if with_acc else part
            for s in range(cs.start // LANES, cs.stop // LANES):
                lo = s * LANES - cs.start
                ys_ref[pl.ds(s, TM, stride=ROW_PIECES), :] = tot[:, lo:lo + LANES]
        return sink

    active = i < na
    if n_fc == 1:
        pl.when(active)(lambda: swiglu_piecewise(to_out(False)))
    else:
        pl.when(jnp.logical_and(active, j == 0))(lambda: swiglu_piecewise(to_acc(True)))
        if n_fc > 2:
            pl.when(jnp.logical_and(active, jnp.logical_and(j > 0, j < n_fc - 1)))(
                lambda: swiglu_piecewise(to_acc(False)))
        pl.when(jnp.logical_and(active, j == n_fc - 1))(lambda: swiglu_piecewise(to_out(True)))

    @pl.when(jnp.logical_and(i >= na, j == 0))
    def _():
        ys_ref[...] = jnp.zeros(ys_ref.shape, F32)


def _experts(xn_tiled, slot_tok, tile_expert, n_active, layer, lw):
    n_tiles = tile_expert.shape[0]
    d_ff = lw["moe_w_gate"].shape[3]
    n_fc = d_ff // TF_EXPERT
    m = layer // 2

    def fcol(i, j, na):
        return jnp.where(i < na[0], j, n_fc - 1)

    grid_spec = pltpu.PrefetchScalarGridSpec(
        num_scalar_prefetch=2,
        grid=(n_tiles, n_fc),
        in_specs=[
            pl.BlockSpec((None, 1, TM), lambda i, j, te, na: (i, 0, 0), memory_space=pltpu.SMEM),
            pl.BlockSpec((None, 1, TM), lambda i, j, te, na: (jnp.minimum(i + 1, n_tiles - 1), 0, 0),
                         memory_space=pltpu.SMEM),
            pl.BlockSpec(memory_space=pl.ANY),
            pl.BlockSpec((None, None, D_MODEL, TF_EXPERT),
                         lambda i, j, te, na: (m, te[i], 0, fcol(i, j, na))),
            pl.BlockSpec((None, None, D_MODEL, TF_EXPERT),
                         lambda i, j, te, na: (m, te[i], 0, fcol(i, j, na))),
            pl.BlockSpec((None, None, TF_EXPERT, D_MODEL),
                         lambda i, j, te, na: (m, te[i], fcol(i, j, na), 0)),
        ],
        out_specs=pl.BlockSpec((TM * ROW_PIECES, LANES), lambda i, j, te, na: (i, 0)),
        scratch_shapes=[pltpu.VMEM((2, TM * ROW_PIECES, LANES), F32), pltpu.VMEM((TM, D_MODEL), BF16),
                        pltpu.VMEM((TM, D_MODEL), F32), pltpu.SemaphoreType.DMA((2,))],
    )
    return pl.pallas_call(
        functools.partial(_expert_kernel, n_fc),
        grid_spec=grid_spec,
        out_shape=jax.ShapeDtypeStruct((n_tiles * TM * ROW_PIECES, LANES), F32),
        compiler_params=pltpu.CompilerParams(dimension_semantics=("arbitrary", "arbitrary"),
                                             vmem_limit_bytes=VMEM_LIMIT),
        name="moe_experts",
    )(tile_expert, n_active, slot_tok, slot_tok, xn_tiled, lw["moe_w_gate"], lw["moe_w_up"], lw["moe_w_down"])


def _combine_kernel(final, n_steps, pos0c_ref, pos1c_ref, pos0n_ref, pos1n_ref, x_ref, p_ref, wgt_ref, ys_hbm,
                    gple_ref, wpg_ref, wpp_ref, gfin_ref, o_ref, buf, xa, gn, sems):
    i = pl.program_id(0)

    def start(pos_refs, s):
        for k in range(TOP_K):
            _start_row_gather(pos_refs[k], 0, ys_hbm, buf.at[s, k], sems.at[s, k])

    @pl.when(i == 0)
    def _():
        start((pos0c_ref, pos1c_ref), 0)
        xa[1] = jnp.zeros((TM, D_MODEL), F32)
        gn[1] = jnp.zeros((TM, D_MODEL), BF16)

    def step(s):
        @pl.when(i + 1 < n_steps)
        def _():
            start((pos0n_ref, pos1n_ref), 1 - s)

        @pl.when(i < n_steps)
        def _():
            for k in range(TOP_K):
                _wait_row_gather(ys_hbm, buf.at[s, k], sems.at[s, k])

        wgt = wgt_ref[...]
        y0 = _load_row_tiled(buf.at[s, 0], TM)
        y1 = _load_row_tiled(buf.at[s, 1], TM)
        x1 = x_ref[...] + (y0 * wgt[:, 0:1] + y1 * wgt[:, 1:2])
        xa[s] = x1
        gn[s] = _rms(x1, gple_ref[...]).astype(BF16)

        gate = jax.nn.sigmoid(jnp.dot(gn[1 - s], wpg_ref[...], preferred_element_type=F32))
        x2 = xa[1 - s] + gate * _bdot(p_ref[...], wpp_ref[...])
        o_ref[...] = _rms(x2, gfin_ref[...]) if final else x2

    pl.when(i % 2 == 0)(lambda: step(0))
    pl.when(i % 2 == 1)(lambda: step(1))


def _combine(xbuf, p_all, layer, pos, wgt, ys, lw, final_g, row0):
    blk0 = row0 // TM
    n_rows = xbuf.shape[0]
    n_steps = n_rows // TM
    assert n_steps >= 2
    cur = lambda i: jnp.minimum(i, n_steps - 1)
    prev = lambda i: jnp.maximum(i - 1, 0)
    def pos_spec(k, ahead):
        return pl.BlockSpec((None, None, 1, TM), lambda i: (blk0 + cur(i + ahead), k, 0, 0),
                            memory_space=pltpu.SMEM)
    final = final_g is not None
    gfin = final_g if final else jnp.ones((1, D_MODEL), F32)
    return pl.pallas_call(
        functools.partial(_combine_kernel, final, n_steps),
        grid=(n_steps + 1,),
        in_specs=[pos_spec(0, 0), pos_spec(1, 0), pos_spec(0, 1), pos_spec(1, 1),
                  pl.BlockSpec((TM, D_MODEL), lambda i: (cur(i), 0)),
                  pl.BlockSpec((None, TM, D_PLE), lambda i: (layer, prev(i), 0)),
                  pl.BlockSpec((TM, LANES), lambda i: (blk0 + cur(i), 0)), pl.BlockSpec(memory_space=pl.ANY),
                  _const_spec((1, D_MODEL), layer), _const_spec((D_MODEL, D_MODEL), layer),
                  _const_spec((D_PLE, D_MODEL), layer), _const_spec((1, D_MODEL))],
        out_specs=pl.BlockSpec((TM, D_MODEL), lambda i: (prev(i), 0)),
        out_shape=jax.ShapeDtypeStruct((n_rows, D_MODEL), F32),
        scratch_shapes=[pltpu.VMEM((2, TOP_K, TM * ROW_PIECES, LANES), F32),
                        pltpu.VMEM((2, TM, D_MODEL), F32), pltpu.VMEM((2, TM, D_MODEL), BF16),
                        pltpu.SemaphoreType.DMA((2, TOP_K))],
        compiler_params=pltpu.CompilerParams(dimension_semantics=("arbitrary",),
                                             vmem_limit_bytes=VMEM_LIMIT),
        name="moe_combine_ple",
    )(pos, pos, pos, pos, xbuf, p_all, wgt, ys, lw["norm_ple_g"], lw["ple_w_gate"],
      lw["ple_w_proj"], gfin)


def _routing_tables(idx, n_tok):
    n_pairs = n_tok * TOP_K
    n_tiles = n_pairs // TM + N_EXPERTS
    experts = jnp.arange(N_EXPERTS, dtype=jnp.int32)
    e_flat = idx.reshape(-1)
    pair = jnp.arange(n_pairs, dtype=jnp.int32)
    pair_tok = (pair // (TOP_K * TM)) * TM + pair % TM
    onehot = (experts[:, None] == e_flat[None, :]).astype(jnp.int32)
    csum = jnp.cumsum(onehot, axis=1)
    sizes = csum[:, -1]
    padded = ((sizes + TM - 1) // TM) * TM
    pend = jnp.cumsum(padded)
    pstart = pend - padded
    pos = jnp.sum(onehot * (csum - 1 + pstart[:, None]), axis=0)
    fill_e = jnp.repeat(experts, TM)
    fill_r = jnp.tile(jnp.arange(TM, dtype=jnp.int32), N_EXPERTS)
    fill_need = jnp.sum((fill_e[:, None] == experts[None, :]) * (padded - sizes)[None, :], axis=1)
    keys = jnp.concatenate([e_flat, jnp.where(fill_r < fill_need, fill_e, N_EXPERTS)])
    toks = jnp.concatenate([pair_tok, jnp.zeros((N_EXPERTS * TM,), jnp.int32)])
    _, slot_tok = lax.sort((keys, toks), num_keys=1, is_stable=True)
    n_active = (pend[-1] // TM).astype(jnp.int32)
    tile_row = jnp.minimum(jnp.arange(n_tiles, dtype=jnp.int32), n_active - 1)
    tile_expert = jnp.minimum(
        jnp.sum((tile_row[:, None] * TM >= pend[None, :]).astype(jnp.int32), axis=1), N_EXPERTS - 1)
    return (slot_tok.reshape(n_tiles, 1, TM), tile_expert, n_active.reshape(1),
            pos.reshape(n_tok // TM, TOP_K, 1, TM))


def _sgu_tables(sgu_w, sgu_b, seq):
    depth = sgu_w.shape[0]
    r = np.arange(CHUNK)
    bias = jnp.repeat(jnp.swapaxes(sgu_b, 1, 2), GROUP_DIM, axis=2)
    if seq >= CHUNK:
        w = sgu_w
        mask = r[None, :] <= r[:, None]
    else:
        reps = CHUNK // seq
        w = jnp.tile(sgu_w[:, :, :seq, :seq], (1, 1, reps, reps))
        mask = (r[:, None] // seq == r[None, :] // seq) & (r[None, :] % seq <= r[:, None] % seq)
        bias = jnp.tile(bias[:, :seq], (1, reps, 1))
    mask = np.tile(mask.astype(np.float32), (N_GROUPS, 1))
    return w.reshape(depth, N_GROUPS * CHUNK, CHUNK), jnp.asarray(mask), bias


def _block_diag(w):
    out = jnp.zeros((w.shape[0], D_BRANCH, D_BRANCH), w.dtype)
    for g in range(N_GROUPS):
        out = out.at[:, g * GROUP_DIM:(g + 1) * GROUP_DIM, g * GROUP_DIM:(g + 1) * GROUP_DIM].set(w[:, g])
    return out


def kernel(x_prompt, x_sample, state_conv_a, state_conv_b, state_pool, p_prompt, p_sample, norm_mix_g, w_in, conv_a_w, conv_b_w, conv_b_bias, conv_b_ln_g, conv_b_ln_b, pool_w, pool_scale, sgu_ln_g, sgu_ln_b, sgu_w, sgu_b, w_branch, w_out, norm_ffn_g, ffn_w_gate, ffn_w_up, ffn_w_down, router_w, moe_w_gate, moe_w_up, moe_w_down, norm_ple_g, ple_w_gate, ple_w_proj, final_norm_g):
    batch, seq, _ = x_prompt.shape
    nseq, dseq, _ = x_sample.shape
    depth = w_in.shape[0]
    assert depth % 2 == 0
    assert PAST_LEN + 1 >= POOL_MAX
    assert seq % TL == 0 and TL % CHUNK == 0 and TL >= CONV_B_W + 1
    n_prompt, n_sample = batch * seq, nseq * dseq
    n_tok = n_prompt + n_sample
    assert n_prompt % TM == 0 and n_sample % TM == 0

    xp, xs = x_prompt.reshape(n_prompt, D_MODEL), x_sample.reshape(n_sample, D_MODEL)
    pp, ps = p_prompt.reshape(depth, n_prompt, D_PLE), p_sample.reshape(depth, n_sample, D_PLE)
    rows3 = lambda a: a.reshape(a.shape[0], 1, a.shape[1])
    zeros = jnp.zeros_like(conv_b_bias)
    sw_p, sm_p, sb_p = _sgu_tables(sgu_w, sgu_b, seq)
    sw_s, sm_s, sb_s = _sgu_tables(sgu_w, sgu_b, dseq)
    r_hi = router_w.astype(BF16)
    r_lo = (router_w - r_hi.astype(F32)).astype(BF16)
    lane_pad = lambda a: jnp.pad(a, ((0, 0), (0, 0), (0, LANES - a.shape[2])))
    lw = {
        "norm_mix_g": rows3(norm_mix_g), "w_in": w_in.astype(BF16),
        "conv_a_w": conv_a_w, "conv_b_w": conv_b_w,
        "vec": jnp.stack([conv_b_bias, conv_b_ln_g, conv_b_ln_b, pool_scale, sgu_ln_g, sgu_ln_b, zeros, zeros],
                         axis=1),
        "pool_bd": _block_diag(pool_w).astype(BF16),
        "sgu_w_p": sw_p, "sgu_mask_p": sm_p, "sgu_bias_p": sb_p,
        "sgu_w_s": sw_s, "sgu_mask_s": sm_s, "sgu_bias_s": sb_s,
        "w_branch": w_branch.astype(BF16), "w_out": w_out.astype(BF16),
        "norm_ffn_g": rows3(norm_ffn_g), "norm_ple_g": rows3(norm_ple_g),
        "ple_w_gate": ple_w_gate.astype(BF16), "ple_w_proj": ple_w_proj.astype(BF16),
        "ffn_w_gate": ffn_w_gate.astype(BF16), "ffn_w_up": ffn_w_up.astype(BF16),
        "ffn_w_down": ffn_w_down.astype(BF16),
        "router_w": lane_pad(jnp.concatenate([r_hi, r_lo], axis=2)), "router_wh": lane_pad(r_hi),
        "moe_w_gate": moe_w_gate.astype(BF16), "moe_w_up": moe_w_up.astype(BF16),
        "moe_w_down": moe_w_down.astype(BF16),
    }

    states_p, states_s = [], []
    y_prompt = y_sample = None
    for i in range(depth):
        in_place = i > 0
        xp, a_p, b_p, pool_p, v_p = _mixer_prompt(xp, lw, i, batch, seq, in_place)
        xs, a_s, b_s, pool_s, v_s = _mixer_sample(xs, lw, i, state_conv_a, state_conv_b, state_pool,
                                                  nseq, dseq, in_place)
        states_p.append((a_p, b_p, pool_p, v_p))
        states_s.append((a_s, b_s, pool_s, v_s))

        last = i == depth - 1
        if i % 2 == 0:
            xp = _dense_ffn(xp, pp, i, lw)
            xs = _dense_ffn(xs, ps, i, lw)
        else:
            idx, wgt, xn_tiled = _router(xp, xs, i, lw)
            slot_tok, tile_expert, n_active, pos = _routing_tables(idx, n_tok)
            ys = _experts(xn_tiled, slot_tok, tile_expert, n_active, i, lw)
            fin = final_norm_g.reshape(1, D_MODEL) if last else None
            xp = _combine(xp, pp, i, pos, wgt, ys, lw, fin, 0)
            xs = _combine(xs, ps, i, pos, wgt, ys, lw, fin, n_prompt)
            if last:
                y_prompt, y_sample = xp, xs

    stack = lambda k, sts: jnp.stack([s[k] for s in sts])
    return (y_prompt.reshape(batch, seq, D_MODEL), y_sample.reshape(nseq, dseq, D_MODEL),
            stack(0, states_p), stack(1, states_p), stack(2, states_p), stack(3, states_p),
            stack(0, states_s), stack(1, states_s), stack(2, states_s), stack(3, states_s))
```

```python
import functools

import jax
import jax.numpy as jnp
import numpy as np
from jax import lax
from jax.experimental import pallas as pl
from jax.experimental.pallas import tpu as pltpu

F32 = jnp.float32
BF16 = jnp.bfloat16

D_MODEL = 1024
N_BRANCH = 4
D_BRANCH = 256
N_GROUPS = 4
GROUP_DIM = 64
CONV_A_W = 3
CONV_B_W = 31
POOL_WINDOWS = (2, 4, 8, 16)
POOL_MAX = 16
CHUNK = 128
PAST_LEN = 16384
D_PLE = 256
N_EXPERTS = 8
TOP_K = 2
EPS = 1e-6

OFF_A_H = 0
OFF_A_B = 256
OFF_A_C = 512
OFF_CF_A = 768
OFF_CF_B = 1024
OFF_POOL = 1280
OFF_U = 1536
OFF_V = 1792
OFF_GATE = 2048

LANES = 128
SUBLANES = 8
TM = 512
TL = 512
NB = 64
TF_EXPERT = 1792
ROW_CHUNK = 32
SEQ_CHUNK = 8
VMEM_LIMIT = 56 * 1024 * 1024

V_CB_BIAS, V_CB_LN_G, V_CB_LN_B, V_POOL_SCALE, V_SGU_LN_G, V_SGU_LN_B = range(6)


def _rms(x, g):
    return x * lax.rsqrt(jnp.mean(x * x, axis=-1, keepdims=True) + EPS) * g


def _ln(x, g, b):
    xc = x - jnp.mean(x, axis=-1, keepdims=True)
    var = jnp.mean(xc * xc, axis=-1, keepdims=True)
    return xc * lax.rsqrt(var + EPS) * g + b


def _bdot(a, w):
    return jnp.dot(a.astype(BF16), w, preferred_element_type=F32)


def _silu(x):
    return x * jax.nn.sigmoid(x)


def _const_spec(shape, layer=None):
    nd = len(shape)
    if layer is None:
        return pl.BlockSpec(shape, lambda *_: (0,) * nd, pipeline_mode=pl.Buffered(1))
    return pl.BlockSpec((None,) + tuple(shape), lambda *_: (layer,) + (0,) * nd, pipeline_mode=pl.Buffered(1))


def _sgu(u, v, swm, sbias):
    m = v.shape[0]
    group = lax.broadcasted_iota(jnp.int32, (CHUNK, D_BRANCH), 1) // GROUP_DIM
    outs = []
    for c in range(m // CHUNK):
        vc = v[c * CHUNK:(c + 1) * CHUNK].astype(BF16)
        r = jnp.dot(swm, vc, preferred_element_type=F32)
        s = sbias
        for g in range(N_GROUPS):
            s = s + jnp.where(group == g, r[g * CHUNK:(g + 1) * CHUNK], 0.0)
        outs.append(s)
    return u * jnp.concatenate(outs, axis=0)


GATE_PIECES_PER_BRANCH = D_MODEL // D_BRANCH
N_GATE_PIECES = N_BRANCH * GATE_PIECES_PER_BRANCH


def _gate_emitter(xnb, win_ref):
    pieces = []

    def emit(n=1):
        for _ in range(n):
            if len(pieces) < N_GATE_PIECES:
                lo = OFF_GATE + len(pieces) * D_BRANCH
                z = jnp.dot(xnb, win_ref[:, lo:lo + D_BRANCH], preferred_element_type=F32)
                pieces.append(jax.nn.sigmoid(z))
    return pieces, emit


def _gated_merge(x, branches, gates, wbr_ref, wout_ref):
    bb = [b.astype(BF16) for b in branches]
    cols = []
    for q in range(GATE_PIECES_PER_BRANCH):
        cs = slice(q * D_BRANCH, (q + 1) * D_BRANCH)
        m = None
        for i in range(N_BRANCH):
            t = gates[i * GATE_PIECES_PER_BRANCH + q] * \
                jnp.dot(bb[i], wbr_ref[i, :, cs], preferred_element_type=F32)
            m = t if m is None else m + t
        cols.append(m.astype(BF16))
    return x + jnp.dot(jnp.concatenate(cols, axis=1), wout_ref[...], preferred_element_type=F32)


def _pool_lane_windows(half):
    small, big = POOL_WINDOWS[2 * half], POOL_WINDOWS[2 * half + 1]
    lane = lax.broadcasted_iota(jnp.int32, (1, LANES), 1)
    is_big = lane >= GROUP_DIM
    return small, big, is_big


def _mixer_prompt_kernel(x_ref, g_ref, win_ref, caw_ref, cbw_ref, vec_ref, pw_ref, sw_ref, smask_ref,
                         sbias_ref, wbr_ref, wout_ref,
                         xo_ref, sta_ref, stb_ref, stp_ref, stv_ref,
                         sa, sb, sp, sh):
    i = pl.program_id(1)
    ha, hb, hp = SUBLANES, 4 * SUBLANES, 2 * SUBLANES

    @pl.when(i == 0)
    def _():
        sa[0:ha, :] = jnp.zeros((ha, D_BRANCH), F32)
        sb[0:hb, :] = jnp.zeros((hb, D_BRANCH), F32)
        sp[0:hp, :] = jnp.zeros((hp, D_BRANCH), F32)

    @pl.when(i > 0)
    def _():
        sa[0:ha, :] = sa[TL:TL + ha, :]
        sb[0:hb, :] = sb[TL:TL + hb, :]
        sp[0:hp, :] = sp[TL:TL + hp, :]

    x = x_ref[...]
    xnb = _rms(x, g_ref[...]).astype(BF16)
    vec = vec_ref[...]

    def zc(off):
        return jnp.dot(xnb, win_ref[:, off:off + D_BRANCH], preferred_element_type=F32)

    gates, emit_gate = _gate_emitter(xnb, win_ref)
    n_chunks = TL // ROW_CHUNK

    def spread(c, total):
        return ((c + 1) * total) // n_chunks - (c * total) // n_chunks

    def tap_conv(s_ref, base, w_ref, ntaps, n_gates=0, shifted=None):
        src = {}
        if shifted is not None:
            reach = {}
            for k in range(ntaps):
                q, r = divmod(base + k, SUBLANES)
                reach[r] = max(reach.get(r, 0), q)
            for r, qmax in sorted(reach.items()):
                if r:
                    n = TL + SUBLANES * qmax
                    shifted[r - 1, 0:n, :] = s_ref[r:r + n, :]
                    src[r] = shifted.at[r - 1]
                    emit_gate(r % 2)
                else:
                    src[r] = s_ref
        outs = []
        for c in range(n_chunks):
            acc = None
            for k in range(ntaps):
                if shifted is None:
                    seg = s_ref[pl.ds(base + k + c * ROW_CHUNK, ROW_CHUNK), :]
                else:
                    q, r = divmod(base + k, SUBLANES)
                    seg = src[r][pl.ds(SUBLANES * q + c * ROW_CHUNK, ROW_CHUNK), :]
                term = seg * w_ref[k:k + 1, :]
                acc = term if acc is None else acc + term
            outs.append(acc)
            emit_gate(spread(c, n_gates))
        return jnp.concatenate(outs, axis=0)

    sa[ha:ha + TL, :] = zc(OFF_A_C) * zc(OFF_A_H)
    br_a = zc(OFF_A_B) * tap_conv(sa, ha - (CONV_A_W - 1), caw_ref, CONV_A_W)
    sta_ref[0] = sa[TL + ha - (CONV_A_W - 1):TL + ha, :]

    sb[hb:hb + TL, :] = zc(OFF_CF_A) * jax.nn.sigmoid(zc(OFF_CF_B))
    yb = tap_conv(sb, hb - (CONV_B_W - 1), cbw_ref, CONV_B_W, n_gates=N_GATE_PIECES // 2, shifted=sh)
    yb = yb + vec[V_CB_BIAS:V_CB_BIAS + 1]
    br_b = _silu(_ln(yb, vec[V_CB_LN_G:V_CB_LN_G + 1], vec[V_CB_LN_B:V_CB_LN_B + 1]))
    stb_ref[0] = sb[TL + hb - (CONV_B_W - 1):TL + hb, :]

    sp[hp:hp + TL, :] = zc(OFF_POOL)
    halves = []
    for half in range(2):
        small, big, is_big = _pool_lane_windows(half)
        lanes = slice(half * LANES, (half + 1) * LANES)
        wl = jnp.where(is_big, float(big), float(small))
        outs = []
        for c in range(n_chunks):
            r0 = hp + c * ROW_CHUNK
            acc = None
            for j in range(big):
                seg = sp[pl.ds(r0 - j, ROW_CHUNK), lanes]
                if j >= small:
                    seg = jnp.where(is_big, seg, 0.0)
                acc = seg if acc is None else acc + seg
            pos = i * TL + c * ROW_CHUNK + lax.broadcasted_iota(jnp.int32, (ROW_CHUNK, LANES), 0)
            cnt = jnp.minimum(wl, (pos + 1).astype(F32))
            outs.append(acc / cnt - sp[pl.ds(r0, ROW_CHUNK), lanes])
            if half == 1:
                emit_gate(spread(c, N_GATE_PIECES // 4))
        halves.append(jnp.concatenate(outs, axis=0))
    pooled = jnp.concatenate(halves, axis=1)
    br_c = _bdot(pooled, pw_ref[...]) * vec[V_POOL_SCALE:V_POOL_SCALE + 1]
    stp_ref[0] = sp[TL + hp - (POOL_MAX - 1):TL + hp, :]

    v = _ln(zc(OFF_V), vec[V_SGU_LN_G:V_SGU_LN_G + 1], vec[V_SGU_LN_B:V_SGU_LN_B + 1])
    stv_ref[0] = v[TL - CHUNK:TL]
    swm = jnp.where(smask_ref[...] > 0, sw_ref[...], 0.0).astype(BF16)
    br_d = _sgu(zc(OFF_U), v, swm, sbias_ref[...])

    emit_gate(N_GATE_PIECES)
    xo_ref[...] = _gated_merge(x, (br_a, br_b, br_c, br_d), gates, wbr_ref, wout_ref)


def _mixer_sample_kernel(x_ref, g_ref, win_ref, caw_ref, cbw_ref, vec_ref, pw_ref, sw_ref, smask_ref,
                         sbias_ref, wbr_ref, wout_ref, ina_ref, inb_ref, inp_ref,
                         xo_ref, sta_ref, stb_ref, stp_ref, stv_ref,
                         sa, sb, sp):
    seq = SUBLANES
    ha, hb, hp = SUBLANES, 4 * SUBLANES, 2 * SUBLANES
    sa[:, ha - (CONV_A_W - 1):ha, :] = ina_ref[...]
    sb[:, hb - (CONV_B_W - 1):hb, :] = inb_ref[...]
    sp[:, hp - (POOL_MAX - 1):hp, :] = inp_ref[...]

    x = x_ref[...]
    xnb = _rms(x, g_ref[...]).astype(BF16)
    vec = vec_ref[...]

    def zc(off):
        return jnp.dot(xnb, win_ref[:, off:off + D_BRANCH], preferred_element_type=F32)

    def to3(a):
        return a.reshape(NB, seq, D_BRANCH)

    def to2(a):
        return a.reshape(NB * seq, D_BRANCH)

    gates, emit_gate = _gate_emitter(xnb, win_ref)
    n_chunks = NB // SEQ_CHUNK

    def tap_conv(s_ref, base, w_ref, ntaps, gates_per_chunk=0):
        outs = []
        for c in range(n_chunks):
            acc = None
            for k in range(ntaps):
                seg = s_ref[c * SEQ_CHUNK:(c + 1) * SEQ_CHUNK, pl.ds(base + k, seq), :]
                term = seg * w_ref[k:k + 1, :][None]
                acc = term if acc is None else acc + term
            outs.append(acc)
            emit_gate(gates_per_chunk)
        return to2(jnp.concatenate(outs, axis=0))

    sa[:, ha:ha + seq, :] = to3(zc(OFF_A_C) * zc(OFF_A_H))
    br_a = zc(OFF_A_B) * tap_conv(sa, ha - (CONV_A_W - 1), caw_ref, CONV_A_W)
    sta_ref[...] = sa[:, ha + seq - (CONV_A_W - 1):ha + seq, :]

    sb[:, hb:hb + seq, :] = to3(zc(OFF_CF_A) * jax.nn.sigmoid(zc(OFF_CF_B)))
    yb = tap_conv(sb, hb - (CONV_B_W - 1), cbw_ref, CONV_B_W, gates_per_chunk=-(-N_GATE_PIECES // n_chunks))
    yb = yb + vec[V_CB_BIAS:V_CB_BIAS + 1]
    br_b = _silu(_ln(yb, vec[V_CB_LN_G:V_CB_LN_G + 1], vec[V_CB_LN_B:V_CB_LN_B + 1]))
    stb_ref[...] = sb[:, hb + seq - (CONV_B_W - 1):hb + seq, :]

    sp[:, hp:hp + seq, :] = to3(zc(OFF_POOL))
    halves = []
    for half in range(2):
        small, big, is_big = _pool_lane_windows(half)
        lanes = slice(half * LANES, (half + 1) * LANES)
        wl = jnp.where(is_big, float(big), float(small))[None]
        outs = []
        for c in range(NB // SEQ_CHUNK):
            rows = slice(c * SEQ_CHUNK, (c + 1) * SEQ_CHUNK)
            acc = None
            for j in range(big):
                seg = sp[rows, pl.ds(hp - j, seq), lanes]
                if j >= small:
                    seg = jnp.where(is_big[None], seg, 0.0)
                acc = seg if acc is None else acc + seg
            outs.append(acc / wl - sp[rows, pl.ds(hp, seq), lanes])
        halves.append(jnp.concatenate(outs, axis=0).reshape(NB * seq, LANES))
    pooled = jnp.concatenate(halves, axis=1)
    br_c = _bdot(pooled, pw_ref[...]) * vec[V_POOL_SCALE:V_POOL_SCALE + 1]
    stp_ref[...] = sp[:, hp + seq - (POOL_MAX - 1):hp + seq, :]

    v = _ln(zc(OFF_V), vec[V_SGU_LN_G:V_SGU_LN_G + 1], vec[V_SGU_LN_B:V_SGU_LN_B + 1])
    stv_ref[...] = to3(v)
    swm = jnp.where(smask_ref[...] > 0, sw_ref[...], 0.0).astype(BF16)
    br_d = _sgu(zc(OFF_U), v, swm, sbias_ref[...])

    emit_gate(N_GATE_PIECES)
    xo_ref[...] = _gated_merge(x, (br_a, br_b, br_c, br_d), gates, wbr_ref, wout_ref)


def _mixer_weight_specs(layer):
    return [
        _const_spec((1, D_MODEL), layer),
        _const_spec((D_MODEL, OFF_GATE + N_BRANCH * D_MODEL), layer),
        _const_spec((CONV_A_W, D_BRANCH), layer),
        _const_spec((CONV_B_W, D_BRANCH), layer),
        _const_spec((SUBLANES, D_BRANCH), layer),
        _const_spec((D_BRANCH, D_BRANCH), layer),
        _const_spec((N_GROUPS * CHUNK, CHUNK), layer),
        _const_spec((N_GROUPS * CHUNK, CHUNK)),
        _const_spec((CHUNK, D_BRANCH), layer),
        _const_spec((N_BRANCH, D_BRANCH, D_MODEL), layer),
        _const_spec((D_MODEL, D_MODEL), layer),
    ]


def _mixer_prompt(xbuf, lw, layer, batch, seq, in_place):
    nt = seq // TL
    rows = lambda b, i: (b * nt + i, 0)
    st = lambda n: pl.BlockSpec((1, n, D_BRANCH), lambda b, i: (b, 0, 0))
    return pl.pallas_call(
        _mixer_prompt_kernel,
        grid=(batch, nt),
        in_specs=[pl.BlockSpec((TL, D_MODEL), rows)] + _mixer_weight_specs(layer),
        out_specs=[pl.BlockSpec((TL, D_MODEL), rows), st(CONV_A_W - 1), st(CONV_B_W - 1),
                   st(POOL_MAX - 1), st(CHUNK)],
        out_shape=[jax.ShapeDtypeStruct(xbuf.shape, F32),
                   jax.ShapeDtypeStruct((batch, CONV_A_W - 1, D_BRANCH), F32),
                   jax.ShapeDtypeStruct((batch, CONV_B_W - 1, D_BRANCH), F32),
                   jax.ShapeDtypeStruct((batch, POOL_MAX - 1, D_BRANCH), F32),
                   jax.ShapeDtypeStruct((batch, CHUNK, D_BRANCH), F32)],
        scratch_shapes=[pltpu.VMEM((TL + SUBLANES, D_BRANCH), F32),
                        pltpu.VMEM((TL + 4 * SUBLANES, D_BRANCH), F32),
                        pltpu.VMEM((TL + 2 * SUBLANES, D_BRANCH), F32),
                        pltpu.VMEM((SUBLANES - 1, TL + 4 * SUBLANES, D_BRANCH), F32)],
        input_output_aliases={0: 0} if in_place else {},
        compiler_params=pltpu.CompilerParams(dimension_semantics=("arbitrary", "arbitrary"),
                                             vmem_limit_bytes=VMEM_LIMIT),
        name="mixer_prompt",
    )(xbuf, lw["norm_mix_g"], lw["w_in"], lw["conv_a_w"], lw["conv_b_w"], lw["vec"], lw["pool_bd"],
      lw["sgu_w_p"], lw["sgu_mask_p"], lw["sgu_bias_p"], lw["w_branch"], lw["w_out"])


def _mixer_sample(xbuf, lw, layer, st_a, st_b, st_p, nseq, seq, in_place):
    assert seq == SUBLANES and nseq % NB == 0
    rows = lambda i: (i, 0)
    st = lambda n: pl.BlockSpec((NB, n, D_BRANCH), lambda i: (i, 0, 0))
    st_in = lambda n: pl.BlockSpec((None, NB, n, D_BRANCH), lambda i: (layer, i, 0, 0))
    return pl.pallas_call(
        _mixer_sample_kernel,
        grid=(nseq // NB,),
        in_specs=[pl.BlockSpec((NB * seq, D_MODEL), rows)] + _mixer_weight_specs(layer)
                 + [st_in(CONV_A_W - 1), st_in(CONV_B_W - 1), st_in(POOL_MAX - 1)],
        out_specs=[pl.BlockSpec((NB * seq, D_MODEL), rows), st(CONV_A_W - 1), st(CONV_B_W - 1),
                   st(POOL_MAX - 1), st(seq)],
        out_shape=[jax.ShapeDtypeStruct(xbuf.shape, F32),
                   jax.ShapeDtypeStruct((nseq, CONV_A_W - 1, D_BRANCH), F32),
                   jax.ShapeDtypeStruct((nseq, CONV_B_W - 1, D_BRANCH), F32),
                   jax.ShapeDtypeStruct((nseq, POOL_MAX - 1, D_BRANCH), F32),
                   jax.ShapeDtypeStruct((nseq, seq, D_BRANCH), F32)],
        scratch_shapes=[pltpu.VMEM((NB, 2 * SUBLANES, D_BRANCH), F32),
                        pltpu.VMEM((NB, 5 * SUBLANES, D_BRANCH), F32),
                        pltpu.VMEM((NB, 3 * SUBLANES, D_BRANCH), F32)],
        input_output_aliases={0: 0} if in_place else {},
        compiler_params=pltpu.CompilerParams(dimension_semantics=("arbitrary",),
                                             vmem_limit_bytes=VMEM_LIMIT),
        name="mixer_sample",
    )(xbuf, lw["norm_mix_g"], lw["w_in"], lw["conv_a_w"], lw["conv_b_w"], lw["vec"], lw["pool_bd"],
      lw["sgu_w_s"], lw["sgu_mask_s"], lw["sgu_bias_s"], lw["w_branch"], lw["w_out"],
      st_a, st_b, st_p)


def _ple_tail(x, p, gple_ref, wpg_ref, wpp_ref):
    gate = jax.nn.sigmoid(_bdot(_rms(x, gple_ref[...]), wpg_ref[...]))
    return x + gate * _bdot(p, wpp_ref[...])


def _dense_ffn_kernel(n_chunks, x_ref, p_ref, gffn_ref, wg_ref, wu_ref, wd_ref, gple_ref, wpg_ref, wpp_ref,
                      xo_ref):
    x = x_ref[...]
    xnb = _rms(x, gffn_ref[...]).astype(BF16)
    tf = wg_ref.shape[1] // n_chunks
    f = None
    for c in range(n_chunks):
        cols = slice(c * tf, (c + 1) * tf)
        h = _silu(jnp.dot(xnb, wg_ref[:, cols], preferred_element_type=F32)) * \
            jnp.dot(xnb, wu_ref[:, cols], preferred_element_type=F32)
        t = _bdot(h, wd_ref[cols, :])
        f = t if f is None else f + t
    xo_ref[...] = _ple_tail(x + f, p_ref[...], gple_ref, wpg_ref, wpp_ref)


def _dense_ffn(xbuf, p_all, layer, lw):
    t = xbuf.shape[0]
    d_ff = lw["ffn_w_gate"].shape[2]
    j = layer // 2
    rows = lambda i: (i, 0)
    return pl.pallas_call(
        functools.partial(_dense_ffn_kernel, 2),
        grid=(t // TM,),
        in_specs=[pl.BlockSpec((TM, D_MODEL), rows), pl.BlockSpec((None, TM, D_PLE), lambda i: (layer, i, 0)),
                  _const_spec((1, D_MODEL), layer), _const_spec((D_MODEL, d_ff), j),
                  _const_spec((D_MODEL, d_ff), j), _const_spec((d_ff, D_MODEL), j),
                  _const_spec((1, D_MODEL), layer), _const_spec((D_MODEL, D_MODEL), layer),
                  _const_spec((D_PLE, D_MODEL), layer)],
        out_specs=pl.BlockSpec((TM, D_MODEL), rows),
        out_shape=jax.ShapeDtypeStruct(xbuf.shape, F32),
        input_output_aliases={0: 0},
        compiler_params=pltpu.CompilerParams(dimension_semantics=("arbitrary",),
                                             vmem_limit_bytes=VMEM_LIMIT),
        name="dense_ffn_ple",
    )(xbuf, p_all, lw["norm_ffn_g"], lw["ffn_w_gate"], lw["ffn_w_up"], lw["ffn_w_down"],
      lw["norm_ple_g"], lw["ple_w_gate"], lw["ple_w_proj"])


def _router_kernel(n_first, xa_ref, xb_ref, g_ref, wr_ref, wrh_ref, idx_ref, wgt_ref, xn_ref):
    x = jnp.where(pl.program_id(0) < n_first, xa_ref[...], xb_ref[...])
    xn = _rms(x, g_ref[...])
    _store_row_tiled(xn_ref, xn)
    xh = xn.astype(BF16)
    xl = (xn - xh.astype(F32)).astype(BF16)
    d = jnp.dot(xh, wr_ref[...], preferred_element_type=F32) + \
        jnp.dot(xl, wrh_ref[...], preferred_element_type=F32)
    logits = d + pltpu.roll(d, LANES - N_EXPERTS, axis=1)
    lane = lax.broadcasted_iota(jnp.int32, logits.shape, 1)
    lane_f = lane.astype(F32)
    logits = jnp.where(lane < N_EXPERTS, logits, -jnp.inf)
    m1 = jnp.max(logits, axis=-1, keepdims=True)
    i1 = jnp.min(jnp.where(logits == m1, lane_f, float(LANES)), axis=-1, keepdims=True)
    rest = jnp.where(lane_f == i1, -jnp.inf, logits)
    m2 = jnp.max(rest, axis=-1, keepdims=True)
    i2 = jnp.min(jnp.where(rest == m2, lane_f, float(LANES)), axis=-1, keepdims=True)
    e2 = jnp.exp(m2 - m1)
    den = 1.0 + e2
    picks = jnp.where(lane == 0, i1, jnp.where(lane == 1, i2, 0.0))
    idx_ref[...] = jnp.transpose(picks)[0:TOP_K, :].astype(jnp.int32)
    wgt_ref[...] = jnp.where(lane == 0, 1.0 / den, jnp.where(lane == 1, e2 / den, 0.0))


def _router(xa, xb, layer, lw):
    j = layer // 2
    na, nb = xa.shape[0] // TM, xb.shape[0] // TM
    t = xa.shape[0] + xb.shape[0]
    rows = lambda i: (i, 0)
    return pl.pallas_call(
        functools.partial(_router_kernel, na),
        grid=(na + nb,),
        in_specs=[pl.BlockSpec((TM, D_MODEL), lambda i: (jnp.minimum(i, na - 1), 0)),
                  pl.BlockSpec((TM, D_MODEL), lambda i: (jnp.maximum(i - na, 0), 0)),
                  _const_spec((1, D_MODEL), layer), _const_spec((D_MODEL, LANES), j),
                  _const_spec((D_MODEL, LANES), j)],
        out_specs=[pl.BlockSpec((None, TOP_K, TM), lambda i: (i, 0, 0)), pl.BlockSpec((TM, LANES), rows),
                   pl.BlockSpec((TM * ROW_PIECES, LANES), rows)],
        out_shape=[jax.ShapeDtypeStruct((na + nb, TOP_K, TM), jnp.int32), jax.ShapeDtypeStruct((t, LANES), F32),
                   jax.ShapeDtypeStruct((t * ROW_PIECES, LANES), F32)],
        compiler_params=pltpu.CompilerParams(dimension_semantics=("arbitrary",)),
        name="router",
    )(xa, xb, lw["norm_ffn_g"], lw["router_w"], lw["router_wh"])


ROW_PIECES = D_MODEL // LANES
assert ROW_PIECES == SUBLANES
GATHER_UNROLL = 8


def _store_row_tiled(ref, x):
    for s in range(ROW_PIECES):
        ref[pl.ds(s, x.shape[0], stride=ROW_PIECES), :] = x[:, s * LANES:(s + 1) * LANES]


def _load_row_tiled(ref, n_rows):
    return jnp.concatenate([ref[pl.ds(s, n_rows, stride=ROW_PIECES), :] for s in range(ROW_PIECES)], axis=1)


def _start_row_gather(idx_ref, k, src_hbm, dst, sem):
    group = GATHER_UNROLL * ROW_PIECES

    def issue(it, carry):
        dst_base = pl.multiple_of(it * group, group)
        for u in range(GATHER_UNROLL):
            src_row = pl.multiple_of(idx_ref[k, it * GATHER_UNROLL + u] * ROW_PIECES, ROW_PIECES)
            pltpu.make_async_copy(src_hbm.at[pl.ds(src_row, ROW_PIECES)],
                                  dst.at[pl.ds(dst_base + u * ROW_PIECES, ROW_PIECES)],
                                  sem).start()
        return carry
    lax.fori_loop(0, dst.shape[0] // group, issue, 0)


def _wait_row_gather(src_hbm, dst, sem):
    pltpu.make_async_copy(src_hbm.at[pl.ds(0, dst.shape[0])], dst, sem).wait()


def _expert_kernel(n_fc, te_ref, na_ref, tokc_ref, tokn_ref, xn_hbm, wg_ref, wu_ref, wd_ref, ys_ref,
                   gbuf, xs, acc, sems):
    i, j = pl.program_id(0), pl.program_id(1)
    na = na_ref[0]
    slot = i % 2
    active = i < na
    early = n_fc > 1

    def unpack(s):
        xs[...] = _load_row_tiled(gbuf.at[s], TM).astype(BF16)

    @pl.when(jnp.logical_and(active, j == 0))
    def _():
        @pl.when(i == 0)
        def _():
            _start_row_gather(tokc_ref, 0, xn_hbm, gbuf.at[0], sems.at[0])

        @pl.when(i + 1 < na)
        def _():
            _start_row_gather(tokn_ref, 0, xn_hbm, gbuf.at[1 - slot], sems.at[1 - slot])

        if early:
            @pl.when(i == 0)
            def _():
                _wait_row_gather(xn_hbm, gbuf.at[0], sems.at[0])
                unpack(0)
        else:
            _wait_row_gather(xn_hbm, gbuf.at[slot], sems.at[slot])
            unpack(slot)

    def swiglu_piecewise(sink):
        h = _silu(jnp.dot(xs[...], wg_ref[...], preferred_element_type=F32)) * \
            jnp.dot(xs[...], wu_ref[...], preferred_element_type=F32)
        hb = h.astype(BF16)
        for q in range(D_MODEL // D_BRANCH):
            cs = slice(q * D_BRANCH, (q + 1) * D_BRANCH)
            sink(cs, jnp.dot(hb, wd_ref[:, cs], preferred_element_type=F32))

    def to_acc(first):
        def sink(cs, part):
            acc[:, cs] = part if first else acc[:, cs] + part
        return sink

    def to_out(with_acc):
        def sink(cs, part):
            tot = acc[:, cs] + part if with_acc else part
            for s in range(cs.start // LANES, cs.stop // LANES):
                lo = s * LANES - cs.start
                ys_ref[pl.ds(s, TM, stride=ROW_PIECES), :] = tot[:, lo:lo + LANES]
        return sink

    if n_fc == 1:
        pl.when(active)(lambda: swiglu_piecewise(to_out(False)))
    else:
        pl.when(jnp.logical_and(active, j == 0))(lambda: swiglu_piecewise(to_acc(True)))
        if n_fc > 2:
            pl.when(jnp.logical_and(active, jnp.logical_and(j > 0, j < n_fc - 1)))(
                lambda: swiglu_piecewise(to_acc(False)))
        last = jnp.logical_and(active, j == n_fc - 1)

        @pl.when(jnp.logical_and(last, i + 1 < na))
        def _():
            _wait_row_gather(xn_hbm, gbuf.at[1 - slot], sems.at[1 - slot])

        @pl.when(last)
        def _():
            swiglu_piecewise(to_out(True))
            unpack(1 - slot)

    @pl.when(jnp.logical_and(i >= na, j == 0))
    def _():
        ys_ref[...] = jnp.zeros(ys_ref.shape, F32)


def _experts(xn_tiled, slot_tok, tile_expert, n_active, layer, lw):
    n_tiles = tile_expert.shape[0]
    d_ff = lw["moe_w_gate"].shape[3]
    n_fc = d_ff // TF_EXPERT
    m = layer // 2

    def fcol(i, j, na):
        return jnp.where(i < na[0], j, n_fc - 1)

    grid_spec = pltpu.PrefetchScalarGridSpec(
        num_scalar_prefetch=2,
        grid=(n_tiles, n_fc),
        in_specs=[
            pl.BlockSpec((None, 1, TM), lambda i, j, te, na: (i, 0, 0), memory_space=pltpu.SMEM),
            pl.BlockSpec((None, 1, TM), lambda i, j, te, na: (jnp.minimum(i + 1, n_tiles - 1), 0, 0),
                         memory_space=pltpu.SMEM),
            pl.BlockSpec(memory_space=pl.ANY),
            pl.BlockSpec((None, None, D_MODEL, TF_EXPERT),
                         lambda i, j, te, na: (m, te[i], 0, fcol(i, j, na))),
            pl.BlockSpec((None, None, D_MODEL, TF_EXPERT),
                         lambda i, j, te, na: (m, te[i], 0, fcol(i, j, na))),
            pl.BlockSpec((None, None, TF_EXPERT, D_MODEL),
                         lambda i, j, te, na: (m, te[i], fcol(i, j, na), 0)),
        ],
        out_specs=pl.BlockSpec((TM * ROW_PIECES, LANES), lambda i, j, te, na: (i, 0)),
        scratch_shapes=[pltpu.VMEM((2, TM * ROW_PIECES, LANES), F32), pltpu.VMEM((TM, D_MODEL), BF16),
                        pltpu.VMEM((TM, D_MODEL), F32), pltpu.SemaphoreType.DMA((2,))],
    )
    return pl.pallas_call(
        functools.partial(_expert_kernel, n_fc),
        grid_spec=grid_spec,
        out_shape=jax.ShapeDtypeStruct((n_tiles * TM * ROW_PIECES, LANES), F32),
        compiler_params=pltpu.CompilerParams(dimension_semantics=("arbitrary", "arbitrary"),
                                             vmem_limit_bytes=VMEM_LIMIT),
        name="moe_experts",
    )(tile_expert, n_active, slot_tok, slot_tok, xn_tiled, lw["moe_w_gate"], lw["moe_w_up"], lw["moe_w_down"])


def _combine_kernel(final, n_steps, pos0c_ref, pos1c_ref, pos0n_ref, pos1n_ref, x_ref, p_ref, wgt_ref, ys_hbm,
                    gple_ref, wpg_ref, wpp_ref, gfin_ref, o_ref, buf, xa, gn, sems):
    i = pl.program_id(0)

    def start(pos_refs, s):
        for k in range(TOP_K):
            _start_row_gather(pos_refs[k], 0, ys_hbm, buf.at[s, k], sems.at[s, k])

    @pl.when(i == 0)
    def _():
        start((pos0c_ref, pos1c_ref), 0)
        xa[1] = jnp.zeros((TM, D_MODEL), F32)
        gn[1] = jnp.zeros((TM, D_MODEL), BF16)

    def step(s):
        @pl.when(i + 1 < n_steps)
        def _():
            start((pos0n_ref, pos1n_ref), 1 - s)

        @pl.when(i < n_steps)
        def _():
            for k in range(TOP_K):
                _wait_row_gather(ys_hbm, buf.at[s, k], sems.at[s, k])

        wgt = wgt_ref[...]
        y0 = _load_row_tiled(buf.at[s, 0], TM)
        y1 = _load_row_tiled(buf.at[s, 1], TM)
        x1 = x_ref[...] + (y0 * wgt[:, 0:1] + y1 * wgt[:, 1:2])
        xa[s] = x1
        gn[s] = _rms(x1, gple_ref[...]).astype(BF16)

        gate = jax.nn.sigmoid(jnp.dot(gn[1 - s], wpg_ref[...], preferred_element_type=F32))
        x2 = xa[1 - s] + gate * _bdot(p_ref[...], wpp_ref[...])
        o_ref[...] = _rms(x2, gfin_ref[...]) if final else x2

    pl.when(i % 2 == 0)(lambda: step(0))
    pl.when(i % 2 == 1)(lambda: step(1))


def _combine(xbuf, p_all, layer, pos, wgt, ys, lw, final_g, row0):
    blk0 = row0 // TM
    n_rows = xbuf.shape[0]
    n_steps = n_rows // TM
    assert n_steps >= 2
    cur = lambda i: jnp.minimum(i, n_steps - 1)
    prev = lambda i: jnp.maximum(i - 1, 0)
    def pos_spec(k, ahead):
        return pl.BlockSpec((None, None, 1, TM), lambda i: (blk0 + cur(i + ahead), k, 0, 0),
                            memory_space=pltpu.SMEM)
    final = final_g is not None
    gfin = final_g if final else jnp.ones((1, D_MODEL), F32)
    return pl.pallas_call(
        functools.partial(_combine_kernel, final, n_steps),
        grid=(n_steps + 1,),
        in_specs=[pos_spec(0, 0), pos_spec(1, 0), pos_spec(0, 1), pos_spec(1, 1),
                  pl.BlockSpec((TM, D_MODEL), lambda i: (cur(i), 0)),
                  pl.BlockSpec((None, TM, D_PLE), lambda i: (layer, prev(i), 0)),
                  pl.BlockSpec((TM, LANES), lambda i: (blk0 + cur(i), 0)), pl.BlockSpec(memory_space=pl.ANY),
                  _const_spec((1, D_MODEL), layer), _const_spec((D_MODEL, D_MODEL), layer),
                  _const_spec((D_PLE, D_MODEL), layer), _const_spec((1, D_MODEL))],
        out_specs=pl.BlockSpec((TM, D_MODEL), lambda i: (prev(i), 0)),
        out_shape=jax.ShapeDtypeStruct((n_rows, D_MODEL), F32),
        scratch_shapes=[pltpu.VMEM((2, TOP_K, TM * ROW_PIECES, LANES), F32),
                        pltpu.VMEM((2, TM, D_MODEL), F32), pltpu.VMEM((2, TM, D_MODEL), BF16),
                        pltpu.SemaphoreType.DMA((2, TOP_K))],
        compiler_params=pltpu.CompilerParams(dimension_semantics=("arbitrary",),
                                             vmem_limit_bytes=VMEM_LIMIT),
        name="moe_combine_ple",
    )(pos, pos, pos, pos, xbuf, p_all, wgt, ys, lw["norm_ple_g"], lw["ple_w_gate"],
      lw["ple_w_proj"], gfin)


def _routing_tables(idx, n_tok):
    n_pairs = n_tok * TOP_K
    n_tiles = n_pairs // TM + N_EXPERTS
    experts = jnp.arange(N_EXPERTS, dtype=jnp.int32)
    e_flat = idx.reshape(-1)
    pair = jnp.arange(n_pairs, dtype=jnp.int32)
    pair_tok = (pair // (TOP_K * TM)) * TM + pair % TM
    onehot = (experts[:, None] == e_flat[None, :]).astype(jnp.int32)
    csum = jnp.cumsum(onehot, axis=1)
    sizes = csum[:, -1]
    padded = ((sizes + TM - 1) // TM) * TM
    pend = jnp.cumsum(padded)
    pstart = pend - padded
    pos = jnp.sum(onehot * (csum - 1 + pstart[:, None]), axis=0)
    fill_e = jnp.repeat(experts, TM)
    fill_r = jnp.tile(jnp.arange(TM, dtype=jnp.int32), N_EXPERTS)
    fill_need = jnp.sum((fill_e[:, None] == experts[None, :]) * (padded - sizes)[None, :], axis=1)
    keys = jnp.concatenate([e_flat, jnp.where(fill_r < fill_need, fill_e, N_EXPERTS)])
    toks = jnp.concatenate([pair_tok, jnp.zeros((N_EXPERTS * TM,), jnp.int32)])
    _, slot_tok = lax.sort((keys, toks), num_keys=1, is_stable=True)
    n_active = (pend[-1] // TM).astype(jnp.int32)
    tile_row = jnp.minimum(jnp.arange(n_tiles, dtype=jnp.int32), n_active - 1)
    tile_expert = jnp.minimum(
        jnp.sum((tile_row[:, None] * TM >= pend[None, :]).astype(jnp.int32), axis=1), N_EXPERTS - 1)
    return (slot_tok.reshape(n_tiles, 1, TM), tile_expert, n_active.reshape(1),
            pos.reshape(n_tok // TM, TOP_K, 1, TM))


def _sgu_tables(sgu_w, sgu_b, seq):
    depth = sgu_w.shape[0]
    r = np.arange(CHUNK)
    bias = jnp.repeat(jnp.swapaxes(sgu_b, 1, 2), GROUP_DIM, axis=2)
    if seq >= CHUNK:
        w = sgu_w
        mask = r[None, :] <= r[:, None]
    else:
        reps = CHUNK // seq
        w = jnp.tile(sgu_w[:, :, :seq, :seq], (1, 1, reps, reps))
        mask = (r[:, None] // seq == r[None, :] // seq) & (r[None, :] % seq <= r[:, None] % seq)
        bias = jnp.tile(bias[:, :seq], (1, reps, 1))
    mask = np.tile(mask.astype(np.float32), (N_GROUPS, 1))
    return w.reshape(depth, N_GROUPS * CHUNK, CHUNK), jnp.asarray(mask), bias


def _block_diag(w):
    out = jnp.zeros((w.shape[0], D_BRANCH, D_BRANCH), w.dtype)
    for g in range(N_GROUPS):
        out = out.at[:, g * GROUP_DIM:(g + 1) * GROUP_DIM, g * GROUP_DIM:(g + 1) * GROUP_DIM].set(w[:, g])
    return out


def kernel(x_prompt, x_sample, state_conv_a, state_conv_b, state_pool, p_prompt, p_sample, norm_mix_g, w_in, conv_a_w, conv_b_w, conv_b_bias, conv_b_ln_g, conv_b_ln_b, pool_w, pool_scale, sgu_ln_g, sgu_ln_b, sgu_w, sgu_b, w_branch, w_out, norm_ffn_g, ffn_w_gate, ffn_w_up, ffn_w_down, router_w, moe_w_gate, moe_w_up, moe_w_down, norm_ple_g, ple_w_gate, ple_w_proj, final_norm_g):
    batch, seq, _ = x_prompt.shape
    nseq, dseq, _ = x_sample.shape
    depth = w_in.shape[0]
    assert depth % 2 == 0
    assert PAST_LEN + 1 >= POOL_MAX
    assert seq % TL == 0 and TL % CHUNK == 0 and TL >= CONV_B_W + 1
    n_prompt, n_sample = batch * seq, nseq * dseq
    n_tok = n_prompt + n_sample
    assert n_prompt % TM == 0 and n_sample % TM == 0

    xp, xs = x_prompt.reshape(n_prompt, D_MODEL), x_sample.reshape(n_sample, D_MODEL)
    pp, ps = p_prompt.reshape(depth, n_prompt, D_PLE), p_sample.reshape(depth, n_sample, D_PLE)
    rows3 = lambda a: a.reshape(a.shape[0], 1, a.shape[1])
    zeros = jnp.zeros_like(conv_b_bias)
    sw_p, sm_p, sb_p = _sgu_tables(sgu_w, sgu_b, seq)
    sw_s, sm_s, sb_s = _sgu_tables(sgu_w, sgu_b, dseq)
    r_hi = router_w.astype(BF16)
    r_lo = (router_w - r_hi.astype(F32)).astype(BF16)
    lane_pad = lambda a: jnp.pad(a, ((0, 0), (0, 0), (0, LANES - a.shape[2])))
    lw = {
        "norm_mix_g": rows3(norm_mix_g), "w_in": w_in.astype(BF16),
        "conv_a_w": conv_a_w, "conv_b_w": conv_b_w,
        "vec": jnp.stack([conv_b_bias, conv_b_ln_g, conv_b_ln_b, pool_scale, sgu_ln_g, sgu_ln_b, zeros, zeros],
                         axis=1),
        "pool_bd": _block_diag(pool_w).astype(BF16),
        "sgu_w_p": sw_p, "sgu_mask_p": sm_p, "sgu_bias_p": sb_p,
        "sgu_w_s": sw_s, "sgu_mask_s": sm_s, "sgu_bias_s": sb_s,
        "w_branch": w_branch.astype(BF16), "w_out": w_out.astype(BF16),
        "norm_ffn_g": rows3(norm_ffn_g), "norm_ple_g": rows3(norm_ple_g),
        "ple_w_gate": ple_w_gate.astype(BF16), "ple_w_proj": ple_w_proj.astype(BF16),
        "ffn_w_gate": ffn_w_gate.astype(BF16), "ffn_w_up": ffn_w_up.astype(BF16),
        "ffn_w_down": ffn_w_down.astype(BF16),
        "router_w": lane_pad(jnp.concatenate([r_hi, r_lo], axis=2)), "router_wh": lane_pad(r_hi),
        "moe_w_gate": moe_w_gate.astype(BF16), "moe_w_up": moe_w_up.astype(BF16),
        "moe_w_down": moe_w_down.astype(BF16),
    }

    states_p, states_s = [], []
    y_prompt = y_sample = None
    for i in range(depth):
        in_place = i > 0
        xp, a_p, b_p, pool_p, v_p = _mixer_prompt(xp, lw, i, batch, seq, in_place)
        xs, a_s, b_s, pool_s, v_s = _mixer_sample(xs, lw, i, state_conv_a, state_conv_b, state_pool,
                                                  nseq, dseq, in_place)
        states_p.append((a_p, b_p, pool_p, v_p))
        states_s.append((a_s, b_s, pool_s, v_s))

        last = i == depth - 1
        if i % 2 == 0:
            xp = _dense_ffn(xp, pp, i, lw)
            xs = _dense_ffn(xs, ps, i, lw)
        else:
            idx, wgt, xn_tiled = _router(xp, xs, i, lw)
            slot_tok, tile_expert, n_active, pos = _routing_tables(idx, n_tok)
            ys = _experts(xn_tiled, slot_tok, tile_expert, n_active, i, lw)
            fin = final_norm_g.reshape(1, D_MODEL) if last else None
            xp = _combine(xp, pp, i, pos, wgt, ys, lw, fin, 0)
            xs = _combine(xs, ps, i, pos, wgt, ys, lw, fin, n_prompt)
            if last:
                y_prompt, y_sample = xp, xs

    stack = lambda k, sts: jnp.stack([s[k] for s in sts])
    return (y_prompt.reshape(batch, seq, D_MODEL), y_sample.reshape(nseq, dseq, D_MODEL),
            stack(0, states_p), stack(1, states_p), stack(2, states_p), stack(3, states_p),
            stack(0, states_s), stack(1, states_s), stack(2, states_s), stack(3, states_s))
```

```python
import functools

import jax
import jax.numpy as jnp
import numpy as np
from jax import lax
from jax.experimental import pallas as pl
from jax.experimental.pallas import tpu as pltpu

F32 = jnp.float32
BF16 = jnp.bfloat16

D_MODEL = 1024
N_BRANCH = 4
D_BRANCH = 256
N_GROUPS = 4
GROUP_DIM = 64
CONV_A_W = 3
CONV_B_W = 31
POOL_WINDOWS = (2, 4, 8, 16)
POOL_MAX = 16
CHUNK = 128
PAST_LEN = 16384
D_PLE = 256
N_EXPERTS = 8
TOP_K = 2
EPS = 1e-6

OFF_A_H = 0
OFF_A_B = 256
OFF_A_C = 512
OFF_CF_A = 768
OFF_CF_B = 1024
OFF_POOL = 1280
OFF_U = 1536
OFF_V = 1792
OFF_GATE = 2048

LANES = 128
SUBLANES = 8
TM = 512
TL = 512
NB = 64
TF_EXPERT = 1792
ROW_CHUNK = 32
SEQ_CHUNK = 8
VMEM_LIMIT = 56 * 1024 * 1024

V_CB_BIAS, V_CB_LN_G, V_CB_LN_B, V_POOL_SCALE, V_SGU_LN_G, V_SGU_LN_B = range(6)


def _rms(x, g):
    return x * lax.rsqrt(jnp.mean(x * x, axis=-1, keepdims=True) + EPS) * g


def _ln(x, g, b):
    xc = x - jnp.mean(x, axis=-1, keepdims=True)
    var = jnp.mean(xc * xc, axis=-1, keepdims=True)
    return xc * lax.rsqrt(var + EPS) * g + b


def _bdot(a, w):
    return jnp.dot(a.astype(BF16), w, preferred_element_type=F32)


def _silu(x):
    return x * jax.nn.sigmoid(x)


def _const_spec(shape, layer=None):
    nd = len(shape)
    if layer is None:
        return pl.BlockSpec(shape, lambda *_: (0,) * nd, pipeline_mode=pl.Buffered(1))
    return pl.BlockSpec((None,) + tuple(shape), lambda *_: (layer,) + (0,) * nd, pipeline_mode=pl.Buffered(1))


def _sgu(u, v, swm, sbias):
    m = v.shape[0]
    group = lax.broadcasted_iota(jnp.int32, (CHUNK, D_BRANCH), 1) // GROUP_DIM
    outs = []
    for c in range(m // CHUNK):
        vc = v[c * CHUNK:(c + 1) * CHUNK].astype(BF16)
        r = jnp.dot(swm, vc, preferred_element_type=F32)
        s = sbias
        for g in range(N_GROUPS):
            s = s + jnp.where(group == g, r[g * CHUNK:(g + 1) * CHUNK], 0.0)
        outs.append(s)
    return u * jnp.concatenate(outs, axis=0)


GATE_PIECES_PER_BRANCH = D_MODEL // D_BRANCH
N_GATE_PIECES = N_BRANCH * GATE_PIECES_PER_BRANCH


def _gate_emitter(xnb, win_ref):
    pieces = []

    def emit(n=1):
        for _ in range(n):
            if len(pieces) < N_GATE_PIECES:
                lo = OFF_GATE + len(pieces) * D_BRANCH
                z = jnp.dot(xnb, win_ref[:, lo:lo + D_BRANCH], preferred_element_type=F32)
                pieces.append(jax.nn.sigmoid(z))
    return pieces, emit


def _gated_merge(x, branches, gates, wbr_ref, wout_ref):
    bb = [b.astype(BF16) for b in branches]
    cols = []
    for q in range(GATE_PIECES_PER_BRANCH):
        cs = slice(q * D_BRANCH, (q + 1) * D_BRANCH)
        m = None
        for i in range(N_BRANCH):
            t = gates[i * GATE_PIECES_PER_BRANCH + q] * \
                jnp.dot(bb[i], wbr_ref[i, :, cs], preferred_element_type=F32)
            m = t if m is None else m + t
        cols.append(m.astype(BF16))
    return x + jnp.dot(jnp.concatenate(cols, axis=1), wout_ref[...], preferred_element_type=F32)


def _pool_lane_windows(half):
    small, big = POOL_WINDOWS[2 * half], POOL_WINDOWS[2 * half + 1]
    lane = lax.broadcasted_iota(jnp.int32, (1, LANES), 1)
    is_big = lane >= GROUP_DIM
    return small, big, is_big


def _mixer_prompt_kernel(x_ref, g_ref, win_ref, caw_ref, cbw_ref, vec_ref, pw_ref, sw_ref, smask_ref,
                         sbias_ref, wbr_ref, wout_ref,
                         xo_ref, sta_ref, stb_ref, stp_ref, stv_ref,
                         sa, sb, sp, sh):
    i = pl.program_id(1)
    ha, hb, hp = SUBLANES, 4 * SUBLANES, 2 * SUBLANES

    @pl.when(i == 0)
    def _():
        sa[0:ha, :] = jnp.zeros((ha, D_BRANCH), F32)
        sb[0:hb, :] = jnp.zeros((hb, D_BRANCH), F32)
        sp[0:hp, :] = jnp.zeros((hp, D_BRANCH), F32)

    @pl.when(i > 0)
    def _():
        sa[0:ha, :] = sa[TL:TL + ha, :]
        sb[0:hb, :] = sb[TL:TL + hb, :]
        sp[0:hp, :] = sp[TL:TL + hp, :]

    x = x_ref[...]
    xnb = _rms(x, g_ref[...]).astype(BF16)
    vec = vec_ref[...]

    def zc(off):
        return jnp.dot(xnb, win_ref[:, off:off + D_BRANCH], preferred_element_type=F32)

    gates, emit_gate = _gate_emitter(xnb, win_ref)
    n_chunks = TL // ROW_CHUNK

    def spread(c, total):
        return ((c + 1) * total) // n_chunks - (c * total) // n_chunks

    def tap_conv(s_ref, base, w_ref, ntaps, n_gates=0, shifted=None):
        src = {}
        if shifted is not None:
            reach = {}
            for k in range(ntaps):
                q, r = divmod(base + k, SUBLANES)
                reach[r] = max(reach.get(r, 0), q)
            for r, qmax in sorted(reach.items()):
                if r:
                    n = TL + SUBLANES * qmax
                    shifted[r - 1, 0:n, :] = s_ref[r:r + n, :]
                    src[r] = shifted.at[r - 1]
                    emit_gate(r % 2)
                else:
                    src[r] = s_ref
        outs = []
        for c in range(n_chunks):
            acc = None
            for k in range(ntaps):
                if shifted is None:
                    seg = s_ref[pl.ds(base + k + c * ROW_CHUNK, ROW_CHUNK), :]
                else:
                    q, r = divmod(base + k, SUBLANES)
                    seg = src[r][pl.ds(SUBLANES * q + c * ROW_CHUNK, ROW_CHUNK), :]
                term = seg * w_ref[k:k + 1, :]
                acc = term if acc is None else acc + term
            outs.append(acc)
            emit_gate(spread(c, n_gates))
        return jnp.concatenate(outs, axis=0)

    sa[ha:ha + TL, :] = zc(OFF_A_C) * zc(OFF_A_H)
    br_a = zc(OFF_A_B) * tap_conv(sa, ha - (CONV_A_W - 1), caw_ref, CONV_A_W)
    sta_ref[0] = sa[TL + ha - (CONV_A_W - 1):TL + ha, :]

    sb[hb:hb + TL, :] = zc(OFF_CF_A) * jax.nn.sigmoid(zc(OFF_CF_B))
    yb = tap_conv(sb, hb - (CONV_B_W - 1), cbw_ref, CONV_B_W, n_gates=N_GATE_PIECES // 2, shifted=sh)
    yb = yb + vec[V_CB_BIAS:V_CB_BIAS + 1]
    br_b = _silu(_ln(yb, vec[V_CB_LN_G:V_CB_LN_G + 1], vec[V_CB_LN_B:V_CB_LN_B + 1]))
    stb_ref[0] = sb[TL + hb - (CONV_B_W - 1):TL + hb, :]

    sp[hp:hp + TL, :] = zc(OFF_POOL)
    halves = []
    for half in range(2):
        small, big, is_big = _pool_lane_windows(half)
        lanes = slice(half * LANES, (half + 1) * LANES)
        wl = jnp.where(is_big, float(big), float(small))
        outs = []
        for c in range(n_chunks):
            r0 = hp + c * ROW_CHUNK
            acc = None
            for j in range(big):
                seg = sp[pl.ds(r0 - j, ROW_CHUNK), lanes]
                if j >= small:
                    seg = jnp.where(is_big, seg, 0.0)
                acc = seg if acc is None else acc + seg
            pos = i * TL + c * ROW_CHUNK + lax.broadcasted_iota(jnp.int32, (ROW_CHUNK, LANES), 0)
            cnt = jnp.minimum(wl, (pos + 1).astype(F32))
            outs.append(acc / cnt - sp[pl.ds(r0, ROW_CHUNK), lanes])
            if half == 1:
                emit_gate(spread(c, N_GATE_PIECES // 4))
        halves.append(jnp.concatenate(outs, axis=0))
    pooled = jnp.concatenate(halves, axis=1)
    br_c = _bdot(pooled, pw_ref[...]) * vec[V_POOL_SCALE:V_POOL_SCALE + 1]
    stp_ref[0] = sp[TL + hp - (POOL_MAX - 1):TL + hp, :]

    v = _ln(zc(OFF_V), vec[V_SGU_LN_G:V_SGU_LN_G + 1], vec[V_SGU_LN_B:V_SGU_LN_B + 1])
    stv_ref[0] = v[TL - CHUNK:TL]
    swm = jnp.where(smask_ref[...] > 0, sw_ref[...], 0.0).astype(BF16)
    br_d = _sgu(zc(OFF_U), v, swm, sbias_ref[...])

    emit_gate(N_GATE_PIECES)
    xo_ref[...] = _gated_merge(x, (br_a, br_b, br_c, br_d), gates, wbr_ref, wout_ref)


def _mixer_sample_kernel(x_ref, g_ref, win_ref, caw_ref, cbw_ref, vec_ref, pw_ref, sw_ref, smask_ref,
                         sbias_ref, wbr_ref, wout_ref, ina_ref, inb_ref, inp_ref,
                         xo_ref, sta_ref, stb_ref, stp_ref, stv_ref,
                         sa, sb, sp):
    seq = SUBLANES
    ha, hb, hp = SUBLANES, 4 * SUBLANES, 2 * SUBLANES
    sa[:, ha - (CONV_A_W - 1):ha, :] = ina_ref[...]
    sb[:, hb - (CONV_B_W - 1):hb, :] = inb_ref[...]
    sp[:, hp - (POOL_MAX - 1):hp, :] = inp_ref[...]

    x = x_ref[...]
    xnb = _rms(x, g_ref[...]).astype(BF16)
    vec = vec_ref[...]

    def zc(off):
        return jnp.dot(xnb, win_ref[:, off:off + D_BRANCH], preferred_element_type=F32)

    def to3(a):
        return a.reshape(NB, seq, D_BRANCH)

    def to2(a):
        return a.reshape(NB * seq, D_BRANCH)

    gates, emit_gate = _gate_emitter(xnb, win_ref)
    n_chunks = NB // SEQ_CHUNK

    def tap_conv(s_ref, base, w_ref, ntaps, gates_per_chunk=0):
        outs = []
        for c in range(n_chunks):
            acc = None
            for k in range(ntaps):
                seg = s_ref[c * SEQ_CHUNK:(c + 1) * SEQ_CHUNK, pl.ds(base + k, seq), :]
                term = seg * w_ref[k:k + 1, :][None]
                acc = term if acc is None else acc + term
            outs.append(acc)
            emit_gate(gates_per_chunk)
        return to2(jnp.concatenate(outs, axis=0))

    sa[:, ha:ha + seq, :] = to3(zc(OFF_A_C) * zc(OFF_A_H))
    br_a = zc(OFF_A_B) * tap_conv(sa, ha - (CONV_A_W - 1), caw_ref, CONV_A_W)
    sta_ref[...] = sa[:, ha + seq - (CONV_A_W - 1):ha + seq, :]

    sb[:, hb:hb + seq, :] = to3(zc(OFF_CF_A) * jax.nn.sigmoid(zc(OFF_CF_B)))
    yb = tap_conv(sb, hb - (CONV_B_W - 1), cbw_ref, CONV_B_W, gates_per_chunk=-(-N_GATE_PIECES // n_chunks))
    yb = yb + vec[V_CB_BIAS:V_CB_BIAS + 1]
    br_b = _silu(_ln(yb, vec[V_CB_LN_G:V_CB_LN_G + 1], vec[V_CB_LN_B:V_CB_LN_B + 1]))
    stb_ref[...] = sb[:, hb + seq - (CONV_B_W - 1):hb + seq, :]

    sp[:, hp:hp + seq, :] = to3(zc(OFF_POOL))
    halves = []
    for half in range(2):
        small, big, is_big = _pool_lane_windows(half)
        lanes = slice(half * LANES, (half + 1) * LANES)
        wl = jnp.where(is_big, float(big), float(small))[None]
        outs = []
        for c in range(NB // SEQ_CHUNK):
            rows = slice(c * SEQ_CHUNK, (c + 1) * SEQ_CHUNK)
            acc = None
            for j in range(big):
                seg = sp[rows, pl.ds(hp - j, seq), lanes]
                if j >= small:
                    seg = jnp.where(is_big[None], seg, 0.0)
                acc = seg if acc is None else acc + seg
            outs.append(acc / wl - sp[rows, pl.ds(hp, seq), lanes])
        halves.append(jnp.concatenate(outs, axis=0).reshape(NB * seq, LANES))
    pooled = jnp.concatenate(halves, axis=1)
    br_c = _bdot(pooled, pw_ref[...]) * vec[V_POOL_SCALE:V_POOL_SCALE + 1]
    stp_ref[...] = sp[:, hp + seq - (POOL_MAX - 1):hp + seq, :]

    v = _ln(zc(OFF_V), vec[V_SGU_LN_G:V_SGU_LN_G + 1], vec[V_SGU_LN_B:V_SGU_LN_B + 1])
    stv_ref[...] = to3(v)
    swm = jnp.where(smask_ref[...] > 0, sw_ref[...], 0.0).astype(BF16)
    br_d = _sgu(zc(OFF_U), v, swm, sbias_ref[...])

    emit_gate(N_GATE_PIECES)
    xo_ref[...] = _gated_merge(x, (br_a, br_b, br_c, br_d), gates, wbr_ref, wout_ref)


def _mixer_weight_specs(layer):
    return [
        _const_spec((1, D_MODEL), layer),
        _const_spec((D_MODEL, OFF_GATE + N_BRANCH * D_MODEL), layer),
        _const_spec((CONV_A_W, D_BRANCH), layer),
        _const_spec((CONV_B_W, D_BRANCH), layer),
        _const_spec((SUBLANES, D_BRANCH), layer),
        _const_spec((D_BRANCH, D_BRANCH), layer),
        _const_spec((N_GROUPS * CHUNK, CHUNK), layer),
        _const_spec((N_GROUPS * CHUNK, CHUNK)),
        _const_spec((CHUNK, D_BRANCH), layer),
        _const_spec((N_BRANCH, D_BRANCH, D_MODEL), layer),
        _const_spec((D_MODEL, D_MODEL), layer),
    ]


def _mixer_prompt(xbuf, lw, layer, batch, seq, in_place):
    nt = seq // TL
    rows = lambda b, i: (b * nt + i, 0)
    st = lambda n: pl.BlockSpec((1, n, D_BRANCH), lambda b, i: (b, 0, 0))
    return pl.pallas_call(
        _mixer_prompt_kernel,
        grid=(batch, nt),
        in_specs=[pl.BlockSpec((TL, D_MODEL), rows)] + _mixer_weight_specs(layer),
        out_specs=[pl.BlockSpec((TL, D_MODEL), rows), st(CONV_A_W - 1), st(CONV_B_W - 1),
                   st(POOL_MAX - 1), st(CHUNK)],
        out_shape=[jax.ShapeDtypeStruct(xbuf.shape, F32),
                   jax.ShapeDtypeStruct((batch, CONV_A_W - 1, D_BRANCH), F32),
                   jax.ShapeDtypeStruct((batch, CONV_B_W - 1, D_BRANCH), F32),
                   jax.ShapeDtypeStruct((batch, POOL_MAX - 1, D_BRANCH), F32),
                   jax.ShapeDtypeStruct((batch, CHUNK, D_BRANCH), F32)],
        scratch_shapes=[pltpu.VMEM((TL + SUBLANES, D_BRANCH), F32),
                        pltpu.VMEM((TL + 4 * SUBLANES, D_BRANCH), F32),
                        pltpu.VMEM((TL + 2 * SUBLANES, D_BRANCH), F32),
                        pltpu.VMEM((SUBLANES - 1, TL + 4 * SUBLANES, D_BRANCH), F32)],
        input_output_aliases={0: 0} if in_place else {},
        compiler_params=pltpu.CompilerParams(dimension_semantics=("arbitrary", "arbitrary"),
                                             vmem_limit_bytes=VMEM_LIMIT),
        name="mixer_prompt",
    )(xbuf, lw["norm_mix_g"], lw["w_in"], lw["conv_a_w"], lw["conv_b_w"], lw["vec"], lw["pool_bd"],
      lw["sgu_w_p"], lw["sgu_mask_p"], lw["sgu_bias_p"], lw["w_branch"], lw["w_out"])


def _mixer_sample(xbuf, lw, layer, st_a, st_b, st_p, nseq, seq, in_place):
    assert seq == SUBLANES and nseq % NB == 0
    rows = lambda i: (i, 0)
    st = lambda n: pl.BlockSpec((NB, n, D_BRANCH), lambda i: (i, 0, 0))
    st_in = lambda n: pl.BlockSpec((None, NB, n, D_BRANCH), lambda i: (layer, i, 0, 0))
    return pl.pallas_call(
        _mixer_sample_kernel,
        grid=(nseq // NB,),
        in_specs=[pl.BlockSpec((NB * seq, D_MODEL), rows)] + _mixer_weight_specs(layer)
                 + [st_in(CONV_A_W - 1), st_in(CONV_B_W - 1), st_in(POOL_MAX - 1)],
        out_specs=[pl.BlockSpec((NB * seq, D_MODEL), rows), st(CONV_A_W - 1), st(CONV_B_W - 1),
                   st(POOL_MAX - 1), st(seq)],
        out_shape=[jax.ShapeDtypeStruct(xbuf.shape, F32),
                   jax.ShapeDtypeStruct((nseq, CONV_A_W - 1, D_BRANCH), F32),
                   jax.ShapeDtypeStruct((nseq, CONV_B_W - 1, D_BRANCH), F32),
                   jax.ShapeDtypeStruct((nseq, POOL_MAX - 1, D_BRANCH), F32),
                   jax.ShapeDtypeStruct((nseq, seq, D_BRANCH), F32)],
        scratch_shapes=[pltpu.VMEM((NB, 2 * SUBLANES, D_BRANCH), F32),
                        pltpu.VMEM((NB, 5 * SUBLANES, D_BRANCH), F32),
                        pltpu.VMEM((NB, 3 * SUBLANES, D_BRANCH), F32)],
        input_output_aliases={0: 0} if in_place else {},
        compiler_params=pltpu.CompilerParams(dimension_semantics=("arbitrary",),
                                             vmem_limit_bytes=VMEM_LIMIT),
        name="mixer_sample",
    )(xbuf, lw["norm_mix_g"], lw["w_in"], lw["conv_a_w"], lw["conv_b_w"], lw["vec"], lw["pool_bd"],
      lw["sgu_w_s"], lw["sgu_mask_s"], lw["sgu_bias_s"], lw["w_branch"], lw["w_out"],
      st_a, st_b, st_p)


def _ple_tail(x, p, gple_ref, wpg_ref, wpp_ref):
    gate = jax.nn.sigmoid(_bdot(_rms(x, gple_ref[...]), wpg_ref[...]))
    return x + gate * _bdot(p, wpp_ref[...])


def _dense_ffn_kernel(n_chunks, x_ref, p_ref, gffn_ref, wg_ref, wu_ref, wd_ref, gple_ref, wpg_ref, wpp_ref,
                      xo_ref):
    x = x_ref[...]
    xnb = _rms(x, gffn_ref[...]).astype(BF16)
    tf = wg_ref.shape[1] // n_chunks
    f = None
    for c in range(n_chunks):
        cols = slice(c * tf, (c + 1) * tf)
        h = _silu(jnp.dot(xnb, wg_ref[:, cols], preferred_element_type=F32)) * \
            jnp.dot(xnb, wu_ref[:, cols], preferred_element_type=F32)
        t = _bdot(h, wd_ref[cols, :])
        f = t if f is None else f + t
    xo_ref[...] = _ple_tail(x + f, p_ref[...], gple_ref, wpg_ref, wpp_ref)


def _dense_ffn(xbuf, p_all, layer, lw):
    t = xbuf.shape[0]
    d_ff = lw["ffn_w_gate"].shape[2]
    j = layer // 2
    rows = lambda i: (i, 0)
    return pl.pallas_call(
        functools.partial(_dense_ffn_kernel, 2),
        grid=(t // TM,),
        in_specs=[pl.BlockSpec((TM, D_MODEL), rows), pl.BlockSpec((None, TM, D_PLE), lambda i: (layer, i, 0)),
                  _const_spec((1, D_MODEL), layer), _const_spec((D_MODEL, d_ff), j),
                  _const_spec((D_MODEL, d_ff), j), _const_spec((d_ff, D_MODEL), j),
                  _const_spec((1, D_MODEL), layer), _const_spec((D_MODEL, D_MODEL), layer),
                  _const_spec((D_PLE, D_MODEL), layer)],
        out_specs=pl.BlockSpec((TM, D_MODEL), rows),
        out_shape=jax.ShapeDtypeStruct(xbuf.shape, F32),
        input_output_aliases={0: 0},
        compiler_params=pltpu.CompilerParams(dimension_semantics=("arbitrary",),
                                             vmem_limit_bytes=VMEM_LIMIT),
        name="dense_ffn_ple",
    )(xbuf, p_all, lw["norm_ffn_g"], lw["ffn_w_gate"], lw["ffn_w_up"], lw["ffn_w_down"],
      lw["norm_ple_g"], lw["ple_w_gate"], lw["ple_w_proj"])


def _router_kernel(n_first, xa_ref, xb_ref, g_ref, wr_ref, wrh_ref, idx_ref, wgt_ref, xn_ref):
    x = jnp.where(pl.program_id(0) < n_first, xa_ref[...], xb_ref[...])
    xn = _rms(x, g_ref[...])
    _store_row_tiled(xn_ref, xn)
    xh = xn.astype(BF16)
    xl = (xn - xh.astype(F32)).astype(BF16)
    d = jnp.dot(xh, wr_ref[...], preferred_element_type=F32) + \
        jnp.dot(xl, wrh_ref[...], preferred_element_type=F32)
    logits = d + pltpu.roll(d, LANES - N_EXPERTS, axis=1)
    lane = lax.broadcasted_iota(jnp.int32, logits.shape, 1)
    lane_f = lane.astype(F32)
    logits = jnp.where(lane < N_EXPERTS, logits, -jnp.inf)
    m1 = jnp.max(logits, axis=-1, keepdims=True)
    i1 = jnp.min(jnp.where(logits == m1, lane_f, float(LANES)), axis=-1, keepdims=True)
    rest = jnp.where(lane_f == i1, -jnp.inf, logits)
    m2 = jnp.max(rest, axis=-1, keepdims=True)
    i2 = jnp.min(jnp.where(rest == m2, lane_f, float(LANES)), axis=-1, keepdims=True)
    e2 = jnp.exp(m2 - m1)
    den = 1.0 + e2
    picks = jnp.where(lane == 0, i1, jnp.where(lane == 1, i2, 0.0))
    idx_ref[...] = jnp.transpose(picks)[0:TOP_K, :].astype(jnp.int32)
    wgt_ref[...] = jnp.where(lane == 0, 1.0 / den, jnp.where(lane == 1, e2 / den, 0.0))


def _router(xa, xb, layer, lw):
    j = layer // 2
    na, nb = xa.shape[0] // TM, xb.shape[0] // TM
    t = xa.shape[0] + xb.shape[0]
    rows = lambda i: (i, 0)
    return pl.pallas_call(
        functools.partial(_router_kernel, na),
        grid=(na + nb,),
        in_specs=[pl.BlockSpec((TM, D_MODEL), lambda i: (jnp.minimum(i, na - 1), 0)),
                  pl.BlockSpec((TM, D_MODEL), lambda i: (jnp.maximum(i - na, 0), 0)),
                  _const_spec((1, D_MODEL), layer), _const_spec((D_MODEL, LANES), j),
                  _const_spec((D_MODEL, LANES), j)],
        out_specs=[pl.BlockSpec((None, TOP_K, TM), lambda i: (i, 0, 0)), pl.BlockSpec((TM, LANES), rows),
                   pl.BlockSpec((TM * ROW_PIECES, LANES), rows)],
        out_shape=[jax.ShapeDtypeStruct((na + nb, TOP_K, TM), jnp.int32), jax.ShapeDtypeStruct((t, LANES), F32),
                   jax.ShapeDtypeStruct((t * ROW_PIECES, LANES), F32)],
        compiler_params=pltpu.CompilerParams(dimension_semantics=("arbitrary",)),
        name="router",
    )(xa, xb, lw["norm_ffn_g"], lw["router_w"], lw["router_wh"])


ROW_PIECES = D_MODEL // LANES
assert ROW_PIECES == SUBLANES
GATHER_UNROLL = 8


def _store_row_tiled(ref, x):
    for s in range(ROW_PIECES):
        ref[pl.ds(s, x.shape[0], stride=ROW_PIECES), :] = x[:, s * LANES:(s + 1) * LANES]


def _load_row_tiled(ref, n_rows):
    return jnp.concatenate([ref[pl.ds(s, n_rows, stride=ROW_PIECES), :] for s in range(ROW_PIECES)], axis=1)


def _start_row_gather(idx_ref, k, src_hbm, dst, sem):
    group = GATHER_UNROLL * ROW_PIECES

    def issue(it, carry):
        dst_base = pl.multiple_of(it * group, group)
        for u in range(GATHER_UNROLL):
            src_row = pl.multiple_of(idx_ref[k, it * GATHER_UNROLL + u] * ROW_PIECES, ROW_PIECES)
            pltpu.make_async_copy(src_hbm.at[pl.ds(src_row, ROW_PIECES)],
                                  dst.at[pl.ds(dst_base + u * ROW_PIECES, ROW_PIECES)],
                                  sem).start()
        return carry
    lax.fori_loop(0, dst.shape[0] // group, issue, 0)


def _wait_row_gather(src_hbm, dst, sem):
    pltpu.make_async_copy(src_hbm.at[pl.ds(0, dst.shape[0])], dst, sem).wait()


def _expert_kernel(n_fc, te_ref, na_ref, tokc_ref, tokn_ref, xn_hbm, wg_ref, wu_ref, wd_ref, ys_ref,
                   gbuf, xs, acc, sems):
    i, j = pl.program_id(0), pl.program_id(1)
    na = na_ref[0]
    slot = i % 2

    @pl.when(jnp.logical_and(i < na, j == 0))
    def _():
        @pl.when(i == 0)
        def _():
            _start_row_gather(tokc_ref, 0, xn_hbm, gbuf.at[0], sems.at[0])

        @pl.when(i + 1 < na)
        def _():
            _start_row_gather(tokn_ref, 0, xn_hbm, gbuf.at[1 - slot], sems.at[1 - slot])

        _wait_row_gather(xn_hbm, gbuf.at[slot], sems.at[slot])
        xs[...] = _load_row_tiled(gbuf.at[slot], TM).astype(BF16)

    def swiglu_piecewise(sink):
        hs = []
        for c in range(wg_ref.shape[1] // D_BRANCH):
            cols = slice(c * D_BRANCH, (c + 1) * D_BRANCH)
            g = jnp.dot(xs[...], wg_ref[:, cols], preferred_element_type=F32)
            u = jnp.dot(xs[...], wu_ref[:, cols], preferred_element_type=F32)
            hs.append((_silu(g) * u).astype(BF16))
        hb = jnp.concatenate(hs, axis=1)
        for q in range(D_MODEL // D_BRANCH):
            cs = slice(q * D_BRANCH, (q + 1) * D_BRANCH)
            sink(cs, jnp.dot(hb, wd_ref[:, cs], preferred_element_type=F32))

    def to_acc(first):
        def sink(cs, part):
            acc[:, cs] = part if first else acc[:, cs] + part
        return sink

    def to_out(with_acc):
        def sink(cs, part):
            tot = acc[:, cs] + part if with_acc else part
            for s in range(cs.start // LANES, cs.stop // LANES):
                lo = s * LANES - cs.start
                ys_ref[pl.ds(s, TM, stride=ROW_PIECES), :] = tot[:, lo:lo + LANES]
        return sink

    active = i < na
    if n_fc == 1:
        pl.when(active)(lambda: swiglu_piecewise(to_out(False)))
    else:
        pl.when(jnp.logical_and(active, j == 0))(lambda: swiglu_piecewise(to_acc(True)))
        if n_fc > 2:
            pl.when(jnp.logical_and(active, jnp.logical_and(j > 0, j < n_fc - 1)))(
                lambda: swiglu_piecewise(to_acc(False)))
        pl.when(jnp.logical_and(active, j == n_fc - 1))(lambda: swiglu_piecewise(to_out(True)))

    @pl.when(jnp.logical_and(i >= na, j == 0))
    def _():
        ys_ref[...] = jnp.zeros(ys_ref.shape, F32)


def _experts(xn_tiled, slot_tok, tile_expert, n_active, layer, lw):
    n_tiles = tile_expert.shape[0]
    d_ff = lw["moe_w_gate"].shape[3]
    n_fc = d_ff // TF_EXPERT
    m = layer // 2

    def fcol(i, j, na):
        return jnp.where(i < na[0], j, n_fc - 1)

    grid_spec = pltpu.PrefetchScalarGridSpec(
        num_scalar_prefetch=2,
        grid=(n_tiles, n_fc),
        in_specs=[
            pl.BlockSpec((None, 1, TM), lambda i, j, te, na: (i, 0, 0), memory_space=pltpu.SMEM),
            pl.BlockSpec((None, 1, TM), lambda i, j, te, na: (jnp.minimum(i + 1, n_tiles - 1), 0, 0),
                         memory_space=pltpu.SMEM),
            pl.BlockSpec(memory_space=pl.ANY),
            pl.BlockSpec((None, None, D_MODEL, TF_EXPERT),
                         lambda i, j, te, na: (m, te[i], 0, fcol(i, j, na))),
            pl.BlockSpec((None, None, D_MODEL, TF_EXPERT),
                         lambda i, j, te, na: (m, te[i], 0, fcol(i, j, na))),
            pl.BlockSpec((None, None, TF_EXPERT, D_MODEL),
                         lambda i, j, te, na: (m, te[i], fcol(i, j, na), 0)),
        ],
        out_specs=pl.BlockSpec((TM * ROW_PIECES, LANES), lambda i, j, te, na: (i, 0)),
        scratch_shapes=[pltpu.VMEM((2, TM * ROW_PIECES, LANES), F32), pltpu.VMEM((TM, D_MODEL), BF16),
                        pltpu.VMEM((TM, D_MODEL), F32), pltpu.SemaphoreType.DMA((2,))],
    )
    return pl.pallas_call(
        functools.partial(_expert_kernel, n_fc),
        grid_spec=grid_spec,
        out_shape=jax.ShapeDtypeStruct((n_tiles * TM * ROW_PIECES, LANES), F32),
        compiler_params=pltpu.CompilerParams(dimension_semantics=("arbitrary", "arbitrary"),
                                             vmem_limit_bytes=VMEM_LIMIT),
        name="moe_experts",
    )(tile_expert, n_active, slot_tok, slot_tok, xn_tiled, lw["moe_w_gate"], lw["moe_w_up"], lw["moe_w_down"])


def _combine_kernel(final, n_steps, pos0c_ref, pos1c_ref, pos0n_ref, pos1n_ref, x_ref, p_ref, wgt_ref, ys_hbm,
                    gple_ref, wpg_ref, wpp_ref, gfin_ref, o_ref, buf, xa, gn, sems):
    i = pl.program_id(0)

    def start(pos_refs, s):
        for k in range(TOP_K):
            _start_row_gather(pos_refs[k], 0, ys_hbm, buf.at[s, k], sems.at[s, k])

    @pl.when(i == 0)
    def _():
        start((pos0c_ref, pos1c_ref), 0)
        xa[1] = jnp.zeros((TM, D_MODEL), F32)
        gn[1] = jnp.zeros((TM, D_MODEL), BF16)

    def step(s):
        @pl.when(i + 1 < n_steps)
        def _():
            start((pos0n_ref, pos1n_ref), 1 - s)

        @pl.when(i < n_steps)
        def _():
            for k in range(TOP_K):
                _wait_row_gather(ys_hbm, buf.at[s, k], sems.at[s, k])

        wgt = wgt_ref[...]
        y0 = _load_row_tiled(buf.at[s, 0], TM)
        y1 = _load_row_tiled(buf.at[s, 1], TM)
        x1 = x_ref[...] + (y0 * wgt[:, 0:1] + y1 * wgt[:, 1:2])
        xa[s] = x1
        gn[s] = _rms(x1, gple_ref[...]).astype(BF16)

        gate = jax.nn.sigmoid(jnp.dot(gn[1 - s], wpg_ref[...], preferred_element_type=F32))
        x2 = xa[1 - s] + gate * _bdot(p_ref[...], wpp_ref[...])
        o_ref[...] = _rms(x2, gfin_ref[...]) if final else x2

    pl.when(i % 2 == 0)(lambda: step(0))
    pl.when(i % 2 == 1)(lambda: step(1))


def _combine(xbuf, p_all, layer, pos, wgt, ys, lw, final_g, row0):
    blk0 = row0 // TM
    n_rows = xbuf.shape[0]
    n_steps = n_rows // TM
    assert n_steps >= 2
    cur = lambda i: jnp.minimum(i, n_steps - 1)
    prev = lambda i: jnp.maximum(i - 1, 0)
    def pos_spec(k, ahead):
        return pl.BlockSpec((None, None, 1, TM), lambda i: (blk0 + cur(i + ahead), k, 0, 0),
                            memory_space=pltpu.SMEM)
    final = final_g is not None
    gfin = final_g if final else jnp.ones((1, D_MODEL), F32)
    return pl.pallas_call(
        functools.partial(_combine_kernel, final, n_steps),
        grid=(n_steps + 1,),
        in_specs=[pos_spec(0, 0), pos_spec(1, 0), pos_spec(0, 1), pos_spec(1, 1),
                  pl.BlockSpec((TM, D_MODEL), lambda i: (cur(i), 0)),
                  pl.BlockSpec((None, TM, D_PLE), lambda i: (layer, prev(i), 0)),
                  pl.BlockSpec((TM, LANES), lambda i: (blk0 + cur(i), 0)), pl.BlockSpec(memory_space=pl.ANY),
                  _const_spec((1, D_MODEL), layer), _const_spec((D_MODEL, D_MODEL), layer),
                  _const_spec((D_PLE, D_MODEL), layer), _const_spec((1, D_MODEL))],
        out_specs=pl.BlockSpec((TM, D_MODEL), lambda i: (prev(i), 0)),
        out_shape=jax.ShapeDtypeStruct((n_rows, D_MODEL), F32),
        scratch_shapes=[pltpu.VMEM((2, TOP_K, TM * ROW_PIECES, LANES), F32),
                        pltpu.VMEM((2, TM, D_MODEL), F32), pltpu.VMEM((2, TM, D_MODEL), BF16),
                        pltpu.SemaphoreType.DMA((2, TOP_K))],
        compiler_params=pltpu.CompilerParams(dimension_semantics=("arbitrary",),
                                             vmem_limit_bytes=VMEM_LIMIT),
        name="moe_combine_ple",
    )(pos, pos, pos, pos, xbuf, p_all, wgt, ys, lw["norm_ple_g"], lw["ple_w_gate"],
      lw["ple_w_proj"], gfin)


def _routing_tables(idx, n_tok):
    n_pairs = n_tok * TOP_K
    n_tiles = n_pairs // TM + N_EXPERTS
    experts = jnp.arange(N_EXPERTS, dtype=jnp.int32)
    e_flat = idx.reshape(-1)
    pair = jnp.arange(n_pairs, dtype=jnp.int32)
    pair_tok = (pair // (TOP_K * TM)) * TM + pair % TM
    onehot = (experts[:, None] == e_flat[None, :]).astype(jnp.int32)
    csum = jnp.cumsum(onehot, axis=1)
    sizes = csum[:, -1]
    padded = ((sizes + TM - 1) // TM) * TM
    pend = jnp.cumsum(padded)
    pstart = pend - padded
    pos = jnp.sum(onehot * (csum - 1 + pstart[:, None]), axis=0)
    fill_e = jnp.repeat(experts, TM)
    fill_r = jnp.tile(jnp.arange(TM, dtype=jnp.int32), N_EXPERTS)
    fill_need = jnp.sum((fill_e[:, None] == experts[None, :]) * (padded - sizes)[None, :], axis=1)
    keys = jnp.concatenate([e_flat, jnp.where(fill_r < fill_need, fill_e, N_EXPERTS)])
    toks = jnp.concatenate([pair_tok, jnp.zeros((N_EXPERTS * TM,), jnp.int32)])
    _, slot_tok = lax.sort((keys, toks), num_keys=1, is_stable=True)
    n_active = (pend[-1] // TM).astype(jnp.int32)
    tile_row = jnp.minimum(jnp.arange(n_tiles, dtype=jnp.int32), n_active - 1)
    tile_expert = jnp.minimum(
        jnp.sum((tile_row[:, None] * TM >= pend[None, :]).astype(jnp.int32), axis=1), N_EXPERTS - 1)
    return (slot_tok.reshape(n_tiles, 1, TM), tile_expert, n_active.reshape(1),
            pos.reshape(n_tok // TM, TOP_K, 1, TM))


def _sgu_tables(sgu_w, sgu_b, seq):
    depth = sgu_w.shape[0]
    r = np.arange(CHUNK)
    bias = jnp.repeat(jnp.swapaxes(sgu_b, 1, 2), GROUP_DIM, axis=2)
    if seq >= CHUNK:
        w = sgu_w
        mask = r[None, :] <= r[:, None]
    else:
        reps = CHUNK // seq
        w = jnp.tile(sgu_w[:, :, :seq, :seq], (1, 1, reps, reps))
        mask = (r[:, None] // seq == r[None, :] // seq) & (r[None, :] % seq <= r[:, None] % seq)
        bias = jnp.tile(bias[:, :seq], (1, reps, 1))
    mask = np.tile(mask.astype(np.float32), (N_GROUPS, 1))
    return w.reshape(depth, N_GROUPS * CHUNK, CHUNK), jnp.asarray(mask), bias


def _block_diag(w):
    out = jnp.zeros((w.shape[0], D_BRANCH, D_BRANCH), w.dtype)
    for g in range(N_GROUPS):
        out = out.at[:, g * GROUP_DIM:(g + 1) * GROUP_DIM, g * GROUP_DIM:(g + 1) * GROUP_DIM].set(w[:, g])
    return out


def kernel(x_prompt, x_sample, state_conv_a, state_conv_b, state_pool, p_prompt, p_sample, norm_mix_g, w_in, conv_a_w, conv_b_w, conv_b_bias, conv_b_ln_g, conv_b_ln_b, pool_w, pool_scale, sgu_ln_g, sgu_ln_b, sgu_w, sgu_b, w_branch, w_out, norm_ffn_g, ffn_w_gate, ffn_w_up, ffn_w_down, router_w, moe_w_gate, moe_w_up, moe_w_down, norm_ple_g, ple_w_gate, ple_w_proj, final_norm_g):
    batch, seq, _ = x_prompt.shape
    nseq, dseq, _ = x_sample.shape
    depth = w_in.shape[0]
    assert depth % 2 == 0
    assert PAST_LEN + 1 >= POOL_MAX
    assert seq % TL == 0 and TL % CHUNK == 0 and TL >= CONV_B_W + 1
    n_prompt, n_sample = batch * seq, nseq * dseq
    n_tok = n_prompt + n_sample
    assert n_prompt % TM == 0 and n_sample % TM == 0

    xp, xs = x_prompt.reshape(n_prompt, D_MODEL), x_sample.reshape(n_sample, D_MODEL)
    pp, ps = p_prompt.reshape(depth, n_prompt, D_PLE), p_sample.reshape(depth, n_sample, D_PLE)
    rows3 = lambda a: a.reshape(a.shape[0], 1, a.shape[1])
    zeros = jnp.zeros_like(conv_b_bias)
    sw_p, sm_p, sb_p = _sgu_tables(sgu_w, sgu_b, seq)
    sw_s, sm_s, sb_s = _sgu_tables(sgu_w, sgu_b, dseq)
    r_hi = router_w.astype(BF16)
    r_lo = (router_w - r_hi.astype(F32)).astype(BF16)
    lane_pad = lambda a: jnp.pad(a, ((0, 0), (0, 0), (0, LANES - a.shape[2])))
    lw = {
        "norm_mix_g": rows3(norm_mix_g), "w_in": w_in.astype(BF16),
        "conv_a_w": conv_a_w, "conv_b_w": conv_b_w,
        "vec": jnp.stack([conv_b_bias, conv_b_ln_g, conv_b_ln_b, pool_scale, sgu_ln_g, sgu_ln_b, zeros, zeros],
                         axis=1),
        "pool_bd": _block_diag(pool_w).astype(BF16),
        "sgu_w_p": sw_p, "sgu_mask_p": sm_p, "sgu_bias_p": sb_p,
        "sgu_w_s": sw_s, "sgu_mask_s": sm_s, "sgu_bias_s": sb_s,
        "w_branch": w_branch.astype(BF16), "w_out": w_out.astype(BF16),
        "norm_ffn_g": rows3(norm_ffn_g), "norm_ple_g": rows3(norm_ple_g),
        "ple_w_gate": ple_w_gate.astype(BF16), "ple_w_proj": ple_w_proj.astype(BF16),
        "ffn_w_gate": ffn_w_gate.astype(BF16), "ffn_w_up": ffn_w_up.astype(BF16),
        "ffn_w_down": ffn_w_down.astype(BF16),
        "router_w": lane_pad(jnp.concatenate([r_hi, r_lo], axis=2)), "router_wh": lane_pad(r_hi),
        "moe_w_gate": moe_w_gate.astype(BF16), "moe_w_up": moe_w_up.astype(BF16),
        "moe_w_down": moe_w_down.astype(BF16),
    }

    states_p, states_s = [], []
    y_prompt = y_sample = None
    for i in range(depth):
        in_place = i > 0
        xp, a_p, b_p, pool_p, v_p = _mixer_prompt(xp, lw, i, batch, seq, in_place)
        xs, a_s, b_s, pool_s, v_s = _mixer_sample(xs, lw, i, state_conv_a, state_conv_b, state_pool,
                                                  nseq, dseq, in_place)
        states_p.append((a_p, b_p, pool_p, v_p))
        states_s.append((a_s, b_s, pool_s, v_s))

        last = i == depth - 1
        if i % 2 == 0:
            xp = _dense_ffn(xp, pp, i, lw)
            xs = _dense_ffn(xs, ps, i, lw)
        else:
            idx, wgt, xn_tiled = _router(xp, xs, i, lw)
            slot_tok, tile_expert, n_active, pos = _routing_tables(idx, n_tok)
            ys = _experts(xn_tiled, slot_tok, tile_expert, n_active, i, lw)
            fin = final_norm_g.reshape(1, D_MODEL) if last else None
            xp = _combine(xp, pp, i, pos, wgt, ys, lw, fin, 0)
            xs = _combine(xs, ps, i, pos, wgt, ys, lw, fin, n_prompt)
            if last:
                y_prompt, y_sample = xp, xs

    stack = lambda k, sts: jnp.stack([s[k] for s in sts])
    return (y_prompt.reshape(batch, seq, D_MODEL), y_sample.reshape(nseq, dseq, D_MODEL),
            stack(0, states_p), stack(1, states_p), stack(2, states_p), stack(3, states_p),
            stack(0, states_s), stack(1, states_s), stack(2, states_s), stack(3, states_s))
```

```python
import functools

import jax
import jax.numpy as jnp
import numpy as np
from jax import lax
from jax.experimental import pallas as pl
from jax.experimental.pallas import tpu as pltpu

F32 = jnp.float32
BF16 = jnp.bfloat16

D_MODEL = 1024
N_BRANCH = 4
D_BRANCH = 256
N_GROUPS = 4
GROUP_DIM = 64
CONV_A_W = 3
CONV_B_W = 31
POOL_WINDOWS = (2, 4, 8, 16)
POOL_MAX = 16
CHUNK = 128
PAST_LEN = 16384
D_PLE = 256
N_EXPERTS = 8
TOP_K = 2
EPS = 1e-6

OFF_A_H = 0
OFF_A_B = 256
OFF_A_C = 512
OFF_CF_A = 768
OFF_CF_B = 1024
OFF_POOL = 1280
OFF_U = 1536
OFF_V = 1792
OFF_GATE = 2048

LANES = 128
SUBLANES = 8
TM = 512
TL = 512
NB = 64
TF_EXPERT = 1792
ROW_CHUNK = 32
SEQ_CHUNK = 8
VMEM_LIMIT = 56 * 1024 * 1024

V_CB_BIAS, V_CB_LN_G, V_CB_LN_B, V_POOL_SCALE, V_SGU_LN_G, V_SGU_LN_B = range(6)


def _rms(x, g):
    return x * lax.rsqrt(jnp.mean(x * x, axis=-1, keepdims=True) + EPS) * g


def _ln(x, g, b):
    xc = x - jnp.mean(x, axis=-1, keepdims=True)
    var = jnp.mean(xc * xc, axis=-1, keepdims=True)
    return xc * lax.rsqrt(var + EPS) * g + b


def _bdot(a, w):
    return jnp.dot(a.astype(BF16), w, preferred_element_type=F32)


def _silu(x):
    return x * jax.nn.sigmoid(x)


def _const_spec(shape, layer=None):
    nd = len(shape)
    if layer is None:
        return pl.BlockSpec(shape, lambda *_: (0,) * nd, pipeline_mode=pl.Buffered(1))
    return pl.BlockSpec((None,) + tuple(shape), lambda *_: (layer,) + (0,) * nd, pipeline_mode=pl.Buffered(1))


def _sgu(u, v, swm, sbias):
    m = v.shape[0]
    group = lax.broadcasted_iota(jnp.int32, (CHUNK, D_BRANCH), 1) // GROUP_DIM
    outs = []
    for c in range(m // CHUNK):
        vc = v[c * CHUNK:(c + 1) * CHUNK].astype(BF16)
        r = jnp.dot(swm, vc, preferred_element_type=F32)
        s = sbias
        for g in range(N_GROUPS):
            s = s + jnp.where(group == g, r[g * CHUNK:(g + 1) * CHUNK], 0.0)
        outs.append(s)
    return u * jnp.concatenate(outs, axis=0)


GATE_PIECES_PER_BRANCH = D_MODEL // D_BRANCH
N_GATE_PIECES = N_BRANCH * GATE_PIECES_PER_BRANCH


def _gate_emitter(xnb, win_ref):
    pieces = []

    def emit(n=1):
        for _ in range(n):
            if len(pieces) < N_GATE_PIECES:
                lo = OFF_GATE + len(pieces) * D_BRANCH
                z = jnp.dot(xnb, win_ref[:, lo:lo + D_BRANCH], preferred_element_type=F32)
                pieces.append(jax.nn.sigmoid(z))
    return pieces, emit


def _gated_merge(x, branches, gates, wbr_ref, wout_ref):
    bb = [b.astype(BF16) for b in branches]
    cols = []
    for q in range(GATE_PIECES_PER_BRANCH):
        cs = slice(q * D_BRANCH, (q + 1) * D_BRANCH)
        m = None
        for i in range(N_BRANCH):
            t = gates[i * GATE_PIECES_PER_BRANCH + q] * \
                jnp.dot(bb[i], wbr_ref[i, :, cs], preferred_element_type=F32)
            m = t if m is None else m + t
        cols.append(m.astype(BF16))
    return x + jnp.dot(jnp.concatenate(cols, axis=1), wout_ref[...], preferred_element_type=F32)


def _pool_lane_windows(half):
    small, big = POOL_WINDOWS[2 * half], POOL_WINDOWS[2 * half + 1]
    lane = lax.broadcasted_iota(jnp.int32, (1, LANES), 1)
    is_big = lane >= GROUP_DIM
    return small, big, is_big


def _mixer_prompt_kernel(x_ref, g_ref, win_ref, caw_ref, cbw_ref, vec_ref, pw_ref, sw_ref, smask_ref,
                         sbias_ref, wbr_ref, wout_ref,
                         xo_ref, sta_ref, stb_ref, stp_ref, stv_ref,
                         sa, sb, sp, sh):
    i = pl.program_id(1)
    ha, hb, hp = SUBLANES, 4 * SUBLANES, 2 * SUBLANES

    @pl.when(i == 0)
    def _():
        sa[0:ha, :] = jnp.zeros((ha, D_BRANCH), F32)
        sb[0:hb, :] = jnp.zeros((hb, D_BRANCH), F32)
        sp[0:hp, :] = jnp.zeros((hp, D_BRANCH), F32)

    @pl.when(i > 0)
    def _():
        sa[0:ha, :] = sa[TL:TL + ha, :]
        sb[0:hb, :] = sb[TL:TL + hb, :]
        sp[0:hp, :] = sp[TL:TL + hp, :]

    x = x_ref[...]
    xnb = _rms(x, g_ref[...]).astype(BF16)
    vec = vec_ref[...]

    def zc(off):
        return jnp.dot(xnb, win_ref[:, off:off + D_BRANCH], preferred_element_type=F32)

    gates, emit_gate = _gate_emitter(xnb, win_ref)
    n_chunks = TL // ROW_CHUNK

    def spread(c, total):
        return ((c + 1) * total) // n_chunks - (c * total) // n_chunks

    def tap_conv(s_ref, base, w_ref, ntaps, n_gates=0, shifted=None):
        src = {}
        if shifted is not None:
            reach = {}
            for k in range(ntaps):
                q, r = divmod(base + k, SUBLANES)
                reach[r] = max(reach.get(r, 0), q)
            for r, qmax in sorted(reach.items()):
                if r:
                    n = TL + SUBLANES * qmax
                    shifted[r - 1, 0:n, :] = s_ref[r:r + n, :]
                    src[r] = shifted.at[r - 1]
                    emit_gate(r % 2)
                else:
                    src[r] = s_ref
        outs = []
        for c in range(n_chunks):
            acc = None
            for k in range(ntaps):
                if shifted is None:
                    seg = s_ref[pl.ds(base + k + c * ROW_CHUNK, ROW_CHUNK), :]
                else:
                    q, r = divmod(base + k, SUBLANES)
                    seg = src[r][pl.ds(SUBLANES * q + c * ROW_CHUNK, ROW_CHUNK), :]
                term = seg * w_ref[k:k + 1, :]
                acc = term if acc is None else acc + term
            outs.append(acc)
            emit_gate(spread(c, n_gates))
        return jnp.concatenate(outs, axis=0)

    sa[ha:ha + TL, :] = zc(OFF_A_C) * zc(OFF_A_H)
    br_a = zc(OFF_A_B) * tap_conv(sa, ha - (CONV_A_W - 1), caw_ref, CONV_A_W)
    sta_ref[0] = sa[TL + ha - (CONV_A_W - 1):TL + ha, :]

    sb[hb:hb + TL, :] = zc(OFF_CF_A) * jax.nn.sigmoid(zc(OFF_CF_B))
    yb = tap_conv(sb, hb - (CONV_B_W - 1), cbw_ref, CONV_B_W, n_gates=N_GATE_PIECES // 2, shifted=sh)
    yb = yb + vec[V_CB_BIAS:V_CB_BIAS + 1]
    br_b = _silu(_ln(yb, vec[V_CB_LN_G:V_CB_LN_G + 1], vec[V_CB_LN_B:V_CB_LN_B + 1]))
    stb_ref[0] = sb[TL + hb - (CONV_B_W - 1):TL + hb, :]

    sp[hp:hp + TL, :] = zc(OFF_POOL)
    halves = []
    for half in range(2):
        small, big, is_big = _pool_lane_windows(half)
        lanes = slice(half * LANES, (half + 1) * LANES)
        wl = jnp.where(is_big, float(big), float(small))
        outs = []
        for c in range(n_chunks):
            r0 = hp + c * ROW_CHUNK
            acc = None
            for j in range(big):
                seg = sp[pl.ds(r0 - j, ROW_CHUNK), lanes]
                if j >= small:
                    seg = jnp.where(is_big, seg, 0.0)
                acc = seg if acc is None else acc + seg
            pos = i * TL + c * ROW_CHUNK + lax.broadcasted_iota(jnp.int32, (ROW_CHUNK, LANES), 0)
            cnt = jnp.minimum(wl, (pos + 1).astype(F32))
            outs.append(acc / cnt - sp[pl.ds(r0, ROW_CHUNK), lanes])
            if half == 1:
                emit_gate(spread(c, N_GATE_PIECES // 4))
        halves.append(jnp.concatenate(outs, axis=0))
    pooled = jnp.concatenate(halves, axis=1)
    br_c = _bdot(pooled, pw_ref[...]) * vec[V_POOL_SCALE:V_POOL_SCALE + 1]
    stp_ref[0] = sp[TL + hp - (POOL_MAX - 1):TL + hp, :]

    v = _ln(zc(OFF_V), vec[V_SGU_LN_G:V_SGU_LN_G + 1], vec[V_SGU_LN_B:V_SGU_LN_B + 1])
    stv_ref[0] = v[TL - CHUNK:TL]
    swm = jnp.where(smask_ref[...] > 0, sw_ref[...], 0.0).astype(BF16)
    br_d = _sgu(zc(OFF_U), v, swm, sbias_ref[...])

    emit_gate(N_GATE_PIECES)
    xo_ref[...] = _gated_merge(x, (br_a, br_b, br_c, br_d), gates, wbr_ref, wout_ref)


def _mixer_sample_kernel(x_ref, g_ref, win_ref, caw_ref, cbw_ref, vec_ref, pw_ref, sw_ref, smask_ref,
                         sbias_ref, wbr_ref, wout_ref, ina_ref, inb_ref, inp_ref,
                         xo_ref, sta_ref, stb_ref, stp_ref, stv_ref,
                         sa, sb, sp):
    seq = SUBLANES
    ha, hb, hp = SUBLANES, 4 * SUBLANES, 2 * SUBLANES
    sa[:, ha - (CONV_A_W - 1):ha, :] = ina_ref[...]
    sb[:, hb - (CONV_B_W - 1):hb, :] = inb_ref[...]
    sp[:, hp - (POOL_MAX - 1):hp, :] = inp_ref[...]

    x = x_ref[...]
    xnb = _rms(x, g_ref[...]).astype(BF16)
    vec = vec_ref[...]

    def zc(off):
        return jnp.dot(xnb, win_ref[:, off:off + D_BRANCH], preferred_element_type=F32)

    def to3(a):
        return a.reshape(NB, seq, D_BRANCH)

    def to2(a):
        return a.reshape(NB * seq, D_BRANCH)

    gates, emit_gate = _gate_emitter(xnb, win_ref)
    n_chunks = NB // SEQ_CHUNK

    def tap_conv(s_ref, base, w_ref, ntaps, gates_per_chunk=0):
        outs = []
        for c in range(n_chunks):
            acc = None
            for k in range(ntaps):
                seg = s_ref[c * SEQ_CHUNK:(c + 1) * SEQ_CHUNK, pl.ds(base + k, seq), :]
                term = seg * w_ref[k:k + 1, :][None]
                acc = term if acc is None else acc + term
            outs.append(acc)
            emit_gate(gates_per_chunk)
        return to2(jnp.concatenate(outs, axis=0))

    sa[:, ha:ha + seq, :] = to3(zc(OFF_A_C) * zc(OFF_A_H))
    br_a = zc(OFF_A_B) * tap_conv(sa, ha - (CONV_A_W - 1), caw_ref, CONV_A_W)
    sta_ref[...] = sa[:, ha + seq - (CONV_A_W - 1):ha + seq, :]

    sb[:, hb:hb + seq, :] = to3(zc(OFF_CF_A) * jax.nn.sigmoid(zc(OFF_CF_B)))
    yb = tap_conv(sb, hb - (CONV_B_W - 1), cbw_ref, CONV_B_W, gates_per_chunk=-(-N_GATE_PIECES // n_chunks))
    yb = yb + vec[V_CB_BIAS:V_CB_BIAS + 1]
    br_b = _silu(_ln(yb, vec[V_CB_LN_G:V_CB_LN_G + 1], vec[V_CB_LN_B:V_CB_LN_B + 1]))
    stb_ref[...] = sb[:, hb + seq - (CONV_B_W - 1):hb + seq, :]

    sp[:, hp:hp + seq, :] = to3(zc(OFF_POOL))
    halves = []
    for half in range(2):
        small, big, is_big = _pool_lane_windows(half)
        lanes = slice(half * LANES, (half + 1) * LANES)
        wl = jnp.where(is_big, float(big), float(small))[None]
        outs = []
        for c in range(NB // SEQ_CHUNK):
            rows = slice(c * SEQ_CHUNK, (c + 1) * SEQ_CHUNK)
            acc = None
            for j in range(big):
                seg = sp[rows, pl.ds(hp - j, seq), lanes]
                if j >= small:
                    seg = jnp.where(is_big[None], seg, 0.0)
                acc = seg if acc is None else acc + seg
            outs.append(acc / wl - sp[rows, pl.ds(hp, seq), lanes])
        halves.append(jnp.concatenate(outs, axis=0).reshape(NB * seq, LANES))
    pooled = jnp.concatenate(halves, axis=1)
    br_c = _bdot(pooled, pw_ref[...]) * vec[V_POOL_SCALE:V_POOL_SCALE + 1]
    stp_ref[...] = sp[:, hp + seq - (POOL_MAX - 1):hp + seq, :]

    v = _ln(zc(OFF_V), vec[V_SGU_LN_G:V_SGU_LN_G + 1], vec[V_SGU_LN_B:V_SGU_LN_B + 1])
    stv_ref[...] = to3(v)
    swm = jnp.where(smask_ref[...] > 0, sw_ref[...], 0.0).astype(BF16)
    br_d = _sgu(zc(OFF_U), v, swm, sbias_ref[...])

    emit_gate(N_GATE_PIECES)
    xo_ref[...] = _gated_merge(x, (br_a, br_b, br_c, br_d), gates, wbr_ref, wout_ref)


def _mixer_weight_specs(layer):
    return [
        _const_spec((1, D_MODEL), layer),
        _const_spec((D_MODEL, OFF_GATE + N_BRANCH * D_MODEL), layer),
        _const_spec((CONV_A_W, D_BRANCH), layer),
        _const_spec((CONV_B_W, D_BRANCH), layer),
        _const_spec((SUBLANES, D_BRANCH), layer),
        _const_spec((D_BRANCH, D_BRANCH), layer),
        _const_spec((N_GROUPS * CHUNK, CHUNK), layer),
        _const_spec((N_GROUPS * CHUNK, CHUNK)),
        _const_spec((CHUNK, D_BRANCH), layer),
        _const_spec((N_BRANCH, D_BRANCH, D_MODEL), layer),
        _const_spec((D_MODEL, D_MODEL), layer),
    ]


def _mixer_prompt(xbuf, lw, layer, batch, seq, in_place):
    nt = seq // TL
    rows = lambda b, i: (b * nt + i, 0)
    st = lambda n: pl.BlockSpec((1, n, D_BRANCH), lambda b, i: (b, 0, 0))
    return pl.pallas_call(
        _mixer_prompt_kernel,
        grid=(batch, nt),
        in_specs=[pl.BlockSpec((TL, D_MODEL), rows)] + _mixer_weight_specs(layer),
        out_specs=[pl.BlockSpec((TL, D_MODEL), rows), st(CONV_A_W - 1), st(CONV_B_W - 1),
                   st(POOL_MAX - 1), st(CHUNK)],
        out_shape=[jax.ShapeDtypeStruct(xbuf.shape, F32),
                   jax.ShapeDtypeStruct((batch, CONV_A_W - 1, D_BRANCH), F32),
                   jax.ShapeDtypeStruct((batch, CONV_B_W - 1, D_BRANCH), F32),
                   jax.ShapeDtypeStruct((batch, POOL_MAX - 1, D_BRANCH), F32),
                   jax.ShapeDtypeStruct((batch, CHUNK, D_BRANCH), F32)],
        scratch_shapes=[pltpu.VMEM((TL + SUBLANES, D_BRANCH), F32),
                        pltpu.VMEM((TL + 4 * SUBLANES, D_BRANCH), F32),
                        pltpu.VMEM((TL + 2 * SUBLANES, D_BRANCH), F32),
                        pltpu.VMEM((SUBLANES - 1, TL + 4 * SUBLANES, D_BRANCH), F32)],
        input_output_aliases={0: 0} if in_place else {},
        compiler_params=pltpu.CompilerParams(dimension_semantics=("arbitrary", "arbitrary"),
                                             vmem_limit_bytes=VMEM_LIMIT),
        name="mixer_prompt",
    )(xbuf, lw["norm_mix_g"], lw["w_in"], lw["conv_a_w"], lw["conv_b_w"], lw["vec"], lw["pool_bd"],
      lw["sgu_w_p"], lw["sgu_mask_p"], lw["sgu_bias_p"], lw["w_branch"], lw["w_out"])


def _mixer_sample(xbuf, lw, layer, st_a, st_b, st_p, nseq, seq, in_place):
    assert seq == SUBLANES and nseq % NB == 0
    rows = lambda i: (i, 0)
    st = lambda n: pl.BlockSpec((NB, n, D_BRANCH), lambda i: (i, 0, 0))
    st_in = lambda n: pl.BlockSpec((None, NB, n, D_BRANCH), lambda i: (layer, i, 0, 0))
    return pl.pallas_call(
        _mixer_sample_kernel,
        grid=(nseq // NB,),
        in_specs=[pl.BlockSpec((NB * seq, D_MODEL), rows)] + _mixer_weight_specs(layer)
                 + [st_in(CONV_A_W - 1), st_in(CONV_B_W - 1), st_in(POOL_MAX - 1)],
        out_specs=[pl.BlockSpec((NB * seq, D_MODEL), rows), st(CONV_A_W - 1), st(CONV_B_W - 1),
                   st(POOL_MAX - 1), st(seq)],
        out_shape=[jax.ShapeDtypeStruct(xbuf.shape, F32),
                   jax.ShapeDtypeStruct((nseq, CONV_A_W - 1, D_BRANCH), F32),
                   jax.ShapeDtypeStruct((nseq, CONV_B_W - 1, D_BRANCH), F32),
                   jax.ShapeDtypeStruct((nseq, POOL_MAX - 1, D_BRANCH), F32),
                   jax.ShapeDtypeStruct((nseq, seq, D_BRANCH), F32)],
        scratch_shapes=[pltpu.VMEM((NB, 2 * SUBLANES, D_BRANCH), F32),
                        pltpu.VMEM((NB, 5 * SUBLANES, D_BRANCH), F32),
                        pltpu.VMEM((NB, 3 * SUBLANES, D_BRANCH), F32)],
        input_output_aliases={0: 0} if in_place else {},
        compiler_params=pltpu.CompilerParams(dimension_semantics=("arbitrary",),
                                             vmem_limit_bytes=VMEM_LIMIT),
        name="mixer_sample",
    )(xbuf, lw["norm_mix_g"], lw["w_in"], lw["conv_a_w"], lw["conv_b_w"], lw["vec"], lw["pool_bd"],
      lw["sgu_w_s"], lw["sgu_mask_s"], lw["sgu_bias_s"], lw["w_branch"], lw["w_out"],
      st_a, st_b, st_p)


def _ple_tail(x, p, gple_ref, wpg_ref, wpp_ref):
    gate = jax.nn.sigmoid(_bdot(_rms(x, gple_ref[...]), wpg_ref[...]))
    return x + gate * _bdot(p, wpp_ref[...])


def _dense_ffn_kernel(x_ref, p_ref, gffn_ref, wg_ref, wu_ref, wd_ref, gple_ref, wpg_ref, wpp_ref,
                      xo_ref):
    x = x_ref[...]
    xnb = _rms(x, gffn_ref[...]).astype(BF16)
    hs = []
    for c in range(wg_ref.shape[1] // D_BRANCH):
        cols = slice(c * D_BRANCH, (c + 1) * D_BRANCH)
        g = jnp.dot(xnb, wg_ref[:, cols], preferred_element_type=F32)
        u = jnp.dot(xnb, wu_ref[:, cols], preferred_element_type=F32)
        hs.append((_silu(g) * u).astype(BF16))
    f = jnp.dot(jnp.concatenate(hs, axis=1), wd_ref[...], preferred_element_type=F32)
    xo_ref[...] = _ple_tail(x + f, p_ref[...], gple_ref, wpg_ref, wpp_ref)


def _dense_ffn(xbuf, p_all, layer, lw):
    t = xbuf.shape[0]
    d_ff = lw["ffn_w_gate"].shape[2]
    j = layer // 2
    rows = lambda i: (i, 0)
    return pl.pallas_call(
        _dense_ffn_kernel,
        grid=(t // TM,),
        in_specs=[pl.BlockSpec((TM, D_MODEL), rows), pl.BlockSpec((None, TM, D_PLE), lambda i: (layer, i, 0)),
                  _const_spec((1, D_MODEL), layer), _const_spec((D_MODEL, d_ff), j),
                  _const_spec((D_MODEL, d_ff), j), _const_spec((d_ff, D_MODEL), j),
                  _const_spec((1, D_MODEL), layer), _const_spec((D_MODEL, D_MODEL), layer),
                  _const_spec((D_PLE, D_MODEL), layer)],
        out_specs=pl.BlockSpec((TM, D_MODEL), rows),
        out_shape=jax.ShapeDtypeStruct(xbuf.shape, F32),
        input_output_aliases={0: 0},
        compiler_params=pltpu.CompilerParams(dimension_semantics=("arbitrary",),
                                             vmem_limit_bytes=VMEM_LIMIT),
        name="dense_ffn_ple",
    )(xbuf, p_all, lw["norm_ffn_g"], lw["ffn_w_gate"], lw["ffn_w_up"], lw["ffn_w_down"],
      lw["norm_ple_g"], lw["ple_w_gate"], lw["ple_w_proj"])


def _router_kernel(n_first, xa_ref, xb_ref, g_ref, wr_ref, wrh_ref, idx_ref, wgt_ref, xn_ref):
    x = jnp.where(pl.program_id(0) < n_first, xa_ref[...], xb_ref[...])
    xn = _rms(x, g_ref[...])
    _store_row_tiled(xn_ref, xn)
    xh = xn.astype(BF16)
    xl = (xn - xh.astype(F32)).astype(BF16)
    d = jnp.dot(xh, wr_ref[...], preferred_element_type=F32) + \
        jnp.dot(xl, wrh_ref[...], preferred_element_type=F32)
    logits = d + pltpu.roll(d, LANES - N_EXPERTS, axis=1)
    lane = lax.broadcasted_iota(jnp.int32, logits.shape, 1)
    lane_f = lane.astype(F32)
    logits = jnp.where(lane < N_EXPERTS, logits, -jnp.inf)
    m1 = jnp.max(logits, axis=-1, keepdims=True)
    i1 = jnp.min(jnp.where(logits == m1, lane_f, float(LANES)), axis=-1, keepdims=True)
    rest = jnp.where(lane_f == i1, -jnp.inf, logits)
    m2 = jnp.max(rest, axis=-1, keepdims=True)
    i2 = jnp.min(jnp.where(rest == m2, lane_f, float(LANES)), axis=-1, keepdims=True)
    e2 = jnp.exp(m2 - m1)
    den = 1.0 + e2
    picks = jnp.where(lane == 0, i1, jnp.where(lane == 1, i2, 0.0))
    idx_ref[...] = jnp.transpose(picks)[0:TOP_K, :].astype(jnp.int32)
    wgt_ref[...] = jnp.where(lane == 0, 1.0 / den, jnp.where(lane == 1, e2 / den, 0.0))


def _router(xa, xb, layer, lw):
    j = layer // 2
    na, nb = xa.shape[0] // TM, xb.shape[0] // TM
    t = xa.shape[0] + xb.shape[0]
    rows = lambda i: (i, 0)
    return pl.pallas_call(
        functools.partial(_router_kernel, na),
        grid=(na + nb,),
        in_specs=[pl.BlockSpec((TM, D_MODEL), lambda i: (jnp.minimum(i, na - 1), 0)),
                  pl.BlockSpec((TM, D_MODEL), lambda i: (jnp.maximum(i - na, 0), 0)),
                  _const_spec((1, D_MODEL), layer), _const_spec((D_MODEL, LANES), j),
                  _const_spec((D_MODEL, LANES), j)],
        out_specs=[pl.BlockSpec((None, TOP_K, TM), lambda i: (i, 0, 0)), pl.BlockSpec((TM, LANES), rows),
                   pl.BlockSpec((TM * ROW_PIECES, LANES), rows)],
        out_shape=[jax.ShapeDtypeStruct((na + nb, TOP_K, TM), jnp.int32), jax.ShapeDtypeStruct((t, LANES), F32),
                   jax.ShapeDtypeStruct((t * ROW_PIECES, LANES), F32)],
        compiler_params=pltpu.CompilerParams(dimension_semantics=("arbitrary",)),
        name="router",
    )(xa, xb, lw["norm_ffn_g"], lw["router_w"], lw["router_wh"])


ROW_PIECES = D_MODEL // LANES
assert ROW_PIECES == SUBLANES
GATHER_UNROLL = 8


def _store_row_tiled(ref, x):
    for s in range(ROW_PIECES):
        ref[pl.ds(s, x.shape[0], stride=ROW_PIECES), :] = x[:, s * LANES:(s + 1) * LANES]


def _load_row_tiled(ref, n_rows):
    return jnp.concatenate([ref[pl.ds(s, n_rows, stride=ROW_PIECES), :] for s in range(ROW_PIECES)], axis=1)


def _start_row_gather(idx_ref, k, src_hbm, dst, sem):
    group = GATHER_UNROLL * ROW_PIECES

    def issue(it, carry):
        dst_base = pl.multiple_of(it * group, group)
        for u in range(GATHER_UNROLL):
            src_row = pl.multiple_of(idx_ref[k, it * GATHER_UNROLL + u] * ROW_PIECES, ROW_PIECES)
            pltpu.make_async_copy(src_hbm.at[pl.ds(src_row, ROW_PIECES)],
                                  dst.at[pl.ds(dst_base + u * ROW_PIECES, ROW_PIECES)],
                                  sem).start()
        return carry
    lax.fori_loop(0, dst.shape[0] // group, issue, 0)


def _wait_row_gather(src_hbm, dst, sem):
    pltpu.make_async_copy(src_hbm.at[pl.ds(0, dst.shape[0])], dst, sem).wait()


def _expert_kernel(n_fc, te_ref, na_ref, tokc_ref, tokn_ref, xn_hbm, wg_ref, wu_ref, wd_ref, ys_ref,
                   gbuf, xs, acc, sems):
    i, j = pl.program_id(0), pl.program_id(1)
    na = na_ref[0]
    slot = i % 2

    @pl.when(jnp.logical_and(i < na, j == 0))
    def _():
        @pl.when(i == 0)
        def _():
            _start_row_gather(tokc_ref, 0, xn_hbm, gbuf.at[0], sems.at[0])

        @pl.when(i + 1 < na)
        def _():
            _start_row_gather(tokn_ref, 0, xn_hbm, gbuf.at[1 - slot], sems.at[1 - slot])

        _wait_row_gather(xn_hbm, gbuf.at[slot], sems.at[slot])
        xs[...] = _load_row_tiled(gbuf.at[slot], TM).astype(BF16)

    def swiglu_piecewise(sink):
        hs = []
        for c in range(wg_ref.shape[1] // D_BRANCH):
            cols = slice(c * D_BRANCH, (c + 1) * D_BRANCH)
            g = jnp.dot(xs[...], wg_ref[:, cols], preferred_element_type=F32)
            u = jnp.dot(xs[...], wu_ref[:, cols], preferred_element_type=F32)
            hs.append((_silu(g) * u).astype(BF16))
        hb = jnp.concatenate(hs, axis=1)
        for q in range(D_MODEL // D_BRANCH):
            cs = slice(q * D_BRANCH, (q + 1) * D_BRANCH)
            sink(cs, jnp.dot(hb, wd_ref[:, cs], preferred_element_type=F32))

    def to_acc(first):
        def sink(cs, part):
            acc[:, cs] = part if first else acc[:, cs] + part
        return sink

    def to_out(with_acc):
        def sink(cs, part):
            tot = acc[:, cs] + part if with_acc else part
            for s in range(cs.start // LANES, cs.stop // LANES):
                lo = s * LANES - cs.start
                ys_ref[pl.ds(s, TM, stride=ROW_PIECES), :] = tot[:, lo:lo + LANES]
        return sink

    active = i < na
    if n_fc == 1:
        pl.when(active)(lambda: swiglu_piecewise(to_out(False)))
    else:
        pl.when(jnp.logical_and(active, j == 0))(lambda: swiglu_piecewise(to_acc(True)))
        if n_fc > 2:
            pl.when(jnp.logical_and(active, jnp.logical_and(j > 0, j < n_fc - 1)))(
                lambda: swiglu_piecewise(to_acc(False)))
        pl.when(jnp.logical_and(active, j == n_fc - 1))(lambda: swiglu_piecewise(to_out(True)))

    @pl.when(jnp.logical_and(i >= na, j == 0))
    def _():
        ys_ref[...] = jnp.zeros(ys_ref.shape, F32)


def _experts(xn_tiled, slot_tok, tile_expert, n_active, layer, lw):
    n_tiles = tile_expert.shape[0]
    d_ff = lw["moe_w_gate"].shape[3]
    n_fc = d_ff // TF_EXPERT
    m = layer // 2

    def fcol(i, j, na):
        return jnp.where(i < na[0], j, n_fc - 1)

    grid_spec = pltpu.PrefetchScalarGridSpec(
        num_scalar_prefetch=2,
        grid=(n_tiles, n_fc),
        in_specs=[
            pl.BlockSpec((None, 1, TM), lambda i, j, te, na: (i, 0, 0), memory_space=pltpu.SMEM),
            pl.BlockSpec((None, 1, TM), lambda i, j, te, na: (jnp.minimum(i + 1, n_tiles - 1), 0, 0),
                         memory_space=pltpu.SMEM),
            pl.BlockSpec(memory_space=pl.ANY),
            pl.BlockSpec((None, None, D_MODEL, TF_EXPERT),
                         lambda i, j, te, na: (m, te[i], 0, fcol(i, j, na))),
            pl.BlockSpec((None, None, D_MODEL, TF_EXPERT),
                         lambda i, j, te, na: (m, te[i], 0, fcol(i, j, na))),
            pl.BlockSpec((None, None, TF_EXPERT, D_MODEL),
                         lambda i, j, te, na: (m, te[i], fcol(i, j, na), 0)),
        ],
        out_specs=pl.BlockSpec((TM * ROW_PIECES, LANES), lambda i, j, te, na: (i, 0)),
        scratch_shapes=[pltpu.VMEM((2, TM * ROW_PIECES, LANES), F32), pltpu.VMEM((TM, D_MODEL), BF16),
                        pltpu.VMEM((TM, D_MODEL), F32), pltpu.SemaphoreType.DMA((2,))],
    )
    return pl.pallas_call(
        functools.partial(_expert_kernel, n_fc),
        grid_spec=grid_spec,
        out_shape=jax.ShapeDtypeStruct((n_tiles * TM * ROW_PIECES, LANES), F32),
        compiler_params=pltpu.CompilerParams(dimension_semantics=("arbitrary", "arbitrary"),
                                             vmem_limit_bytes=VMEM_LIMIT),
        name="moe_experts",
    )(tile_expert, n_active, slot_tok, slot_tok, xn_tiled, lw["moe_w_gate"], lw["moe_w_up"], lw["moe_w_down"])


def _combine_kernel(final, n_steps, pos0c_ref, pos1c_ref, pos0n_ref, pos1n_ref, x_ref, p_ref, wgt_ref, ys_hbm,
                    gple_ref, wpg_ref, wpp_ref, gfin_ref, o_ref, buf, xa, gn, sems):
    i = pl.program_id(0)

    def start(pos_refs, s):
        for k in range(TOP_K):
            _start_row_gather(pos_refs[k], 0, ys_hbm, buf.at[s, k], sems.at[s, k])

    @pl.when(i == 0)
    def _():
        start((pos0c_ref, pos1c_ref), 0)
        xa[1] = jnp.zeros((TM, D_MODEL), F32)
        gn[1] = jnp.zeros((TM, D_MODEL), BF16)

    def step(s):
        @pl.when(i + 1 < n_steps)
        def _():
            start((pos0n_ref, pos1n_ref), 1 - s)

        @pl.when(i < n_steps)
        def _():
            for k in range(TOP_K):
                _wait_row_gather(ys_hbm, buf.at[s, k], sems.at[s, k])

        wgt = wgt_ref[...]
        y0 = _load_row_tiled(buf.at[s, 0], TM)
        y1 = _load_row_tiled(buf.at[s, 1], TM)
        x1 = x_ref[...] + (y0 * wgt[:, 0:1] + y1 * wgt[:, 1:2])
        xa[s] = x1
        gn[s] = _rms(x1, gple_ref[...]).astype(BF16)

        gate = jax.nn.sigmoid(jnp.dot(gn[1 - s], wpg_ref[...], preferred_element_type=F32))
        x2 = xa[1 - s] + gate * _bdot(p_ref[...], wpp_ref[...])
        o_ref[...] = _rms(x2, gfin_ref[...]) if final else x2

    pl.when(i % 2 == 0)(lambda: step(0))
    pl.when(i % 2 == 1)(lambda: step(1))


def _combine(xbuf, p_all, layer, pos, wgt, ys, lw, final_g, row0):
    blk0 = row0 // TM
    n_rows = xbuf.shape[0]
    n_steps = n_rows // TM
    assert n_steps >= 2
    cur = lambda i: jnp.minimum(i, n_steps - 1)
    prev = lambda i: jnp.maximum(i - 1, 0)
    def pos_spec(k, ahead):
        return pl.BlockSpec((None, None, 1, TM), lambda i: (blk0 + cur(i + ahead), k, 0, 0),
                            memory_space=pltpu.SMEM)
    final = final_g is not None
    gfin = final_g if final else jnp.ones((1, D_MODEL), F32)
    return pl.pallas_call(
        functools.partial(_combine_kernel, final, n_steps),
        grid=(n_steps + 1,),
        in_specs=[pos_spec(0, 0), pos_spec(1, 0), pos_spec(0, 1), pos_spec(1, 1),
                  pl.BlockSpec((TM, D_MODEL), lambda i: (cur(i), 0)),
                  pl.BlockSpec((None, TM, D_PLE), lambda i: (layer, prev(i), 0)),
                  pl.BlockSpec((TM, LANES), lambda i: (blk0 + cur(i), 0)), pl.BlockSpec(memory_space=pl.ANY),
                  _const_spec((1, D_MODEL), layer), _const_spec((D_MODEL, D_MODEL), layer),
                  _const_spec((D_PLE, D_MODEL), layer), _const_spec((1, D_MODEL))],
        out_specs=pl.BlockSpec((TM, D_MODEL), lambda i: (prev(i), 0)),
        out_shape=jax.ShapeDtypeStruct((n_rows, D_MODEL), F32),
        scratch_shapes=[pltpu.VMEM((2, TOP_K, TM * ROW_PIECES, LANES), F32),
                        pltpu.VMEM((2, TM, D_MODEL), F32), pltpu.VMEM((2, TM, D_MODEL), BF16),
                        pltpu.SemaphoreType.DMA((2, TOP_K))],
        compiler_params=pltpu.CompilerParams(dimension_semantics=("arbitrary",),
                                             vmem_limit_bytes=VMEM_LIMIT),
        name="moe_combine_ple",
    )(pos, pos, pos, pos, xbuf, p_all, wgt, ys, lw["norm_ple_g"], lw["ple_w_gate"],
      lw["ple_w_proj"], gfin)


def _routing_tables(idx, n_tok):
    n_pairs = n_tok * TOP_K
    n_tiles = n_pairs // TM + N_EXPERTS
    experts = jnp.arange(N_EXPERTS, dtype=jnp.int32)
    e_flat = idx.reshape(-1)
    pair = jnp.arange(n_pairs, dtype=jnp.int32)
    pair_tok = (pair // (TOP_K * TM)) * TM + pair % TM
    onehot = (experts[:, None] == e_flat[None, :]).astype(jnp.int32)
    csum = jnp.cumsum(onehot, axis=1)
    sizes = csum[:, -1]
    padded = ((sizes + TM - 1) // TM) * TM
    pend = jnp.cumsum(padded)
    pstart = pend - padded
    pos = jnp.sum(onehot * (csum - 1 + pstart[:, None]), axis=0)
    fill_e = jnp.repeat(experts, TM)
    fill_r = jnp.tile(jnp.arange(TM, dtype=jnp.int32), N_EXPERTS)
    fill_need = jnp.sum((fill_e[:, None] == experts[None, :]) * (padded - sizes)[None, :], axis=1)
    keys = jnp.concatenate([e_flat, jnp.where(fill_r < fill_need, fill_e, N_EXPERTS)])
    toks = jnp.concatenate([pair_tok, jnp.zeros((N_EXPERTS * TM,), jnp.int32)])
    _, slot_tok = lax.sort((keys, toks), num_keys=1, is_stable=True)
    n_active = (pend[-1] // TM).astype(jnp.int32)
    tile_row = jnp.minimum(jnp.arange(n_tiles, dtype=jnp.int32), n_active - 1)
    tile_expert = jnp.minimum(
        jnp.sum((tile_row[:, None] * TM >= pend[None, :]).astype(jnp.int32), axis=1), N_EXPERTS - 1)
    return (slot_tok.reshape(n_tiles, 1, TM), tile_expert, n_active.reshape(1),
            pos.reshape(n_tok // TM, TOP_K, 1, TM))


def _sgu_tables(sgu_w, sgu_b, seq):
    depth = sgu_w.shape[0]
    r = np.arange(CHUNK)
    bias = jnp.repeat(jnp.swapaxes(sgu_b, 1, 2), GROUP_DIM, axis=2)
    if seq >= CHUNK:
        w = sgu_w
        mask = r[None, :] <= r[:, None]
    else:
        reps = CHUNK // seq
        w = jnp.tile(sgu_w[:, :, :seq, :seq], (1, 1, reps, reps))
        mask = (r[:, None] // seq == r[None, :] // seq) & (r[None, :] % seq <= r[:, None] % seq)
        bias = jnp.tile(bias[:, :seq], (1, reps, 1))
    mask = np.tile(mask.astype(np.float32), (N_GROUPS, 1))
    return w.reshape(depth, N_GROUPS * CHUNK, CHUNK), jnp.asarray(mask), bias


def _block_diag(w):
    out = jnp.zeros((w.shape[0], D_BRANCH, D_BRANCH), w.dtype)
    for g in range(N_GROUPS):
        out = out.at[:, g * GROUP_DIM:(g + 1) * GROUP_DIM, g * GROUP_DIM:(g + 1) * GROUP_DIM].set(w[:, g])
    return out


def kernel(x_prompt, x_sample, state_conv_a, state_conv_b, state_pool, p_prompt, p_sample, norm_mix_g, w_in, conv_a_w, conv_b_w, conv_b_bias, conv_b_ln_g, conv_b_ln_b, pool_w, pool_scale, sgu_ln_g, sgu_ln_b, sgu_w, sgu_b, w_branch, w_out, norm_ffn_g, ffn_w_gate, ffn_w_up, ffn_w_down, router_w, moe_w_gate, moe_w_up, moe_w_down, norm_ple_g, ple_w_gate, ple_w_proj, final_norm_g):
    batch, seq, _ = x_prompt.shape
    nseq, dseq, _ = x_sample.shape
    depth = w_in.shape[0]
    assert depth % 2 == 0
    assert PAST_LEN + 1 >= POOL_MAX
    assert seq % TL == 0 and TL % CHUNK == 0 and TL >= CONV_B_W + 1
    n_prompt, n_sample = batch * seq, nseq * dseq
    n_tok = n_prompt + n_sample
    assert n_prompt % TM == 0 and n_sample % TM == 0

    xp, xs = x_prompt.reshape(n_prompt, D_MODEL), x_sample.reshape(n_sample, D_MODEL)
    pp, ps = p_prompt.reshape(depth, n_prompt, D_PLE), p_sample.reshape(depth, n_sample, D_PLE)
    rows3 = lambda a: a.reshape(a.shape[0], 1, a.shape[1])
    zeros = jnp.zeros_like(conv_b_bias)
    sw_p, sm_p, sb_p = _sgu_tables(sgu_w, sgu_b, seq)
    sw_s, sm_s, sb_s = _sgu_tables(sgu_w, sgu_b, dseq)
    r_hi = router_w.astype(BF16)
    r_lo = (router_w - r_hi.astype(F32)).astype(BF16)
    lane_pad = lambda a: jnp.pad(a, ((0, 0), (0, 0), (0, LANES - a.shape[2])))
    lw = {
        "norm_mix_g": rows3(norm_mix_g), "w_in": w_in.astype(BF16),
        "conv_a_w": conv_a_w, "conv_b_w": conv_b_w,
        "vec": jnp.stack([conv_b_bias, conv_b_ln_g, conv_b_ln_b, pool_scale, sgu_ln_g, sgu_ln_b, zeros, zeros],
                         axis=1),
        "pool_bd": _block_diag(pool_w).astype(BF16),
        "sgu_w_p": sw_p, "sgu_mask_p": sm_p, "sgu_bias_p": sb_p,
        "sgu_w_s": sw_s, "sgu_mask_s": sm_s, "sgu_bias_s": sb_s,
        "w_branch": w_branch.astype(BF16), "w_out": w_out.astype(BF16),
        "norm_ffn_g": rows3(norm_ffn_g), "norm_ple_g": rows3(norm_ple_g),
        "ple_w_gate": ple_w_gate.astype(BF16), "ple_w_proj": ple_w_proj.astype(BF16),
        "ffn_w_gate": ffn_w_gate.astype(BF16), "ffn_w_up": ffn_w_up.astype(BF16),
        "ffn_w_down": ffn_w_down.astype(BF16),
        "router_w": lane_pad(jnp.concatenate([r_hi, r_lo], axis=2)), "router_wh": lane_pad(r_hi),
        "moe_w_gate": moe_w_gate.astype(BF16), "moe_w_up": moe_w_up.astype(BF16),
        "moe_w_down": moe_w_down.astype(BF16),
    }

    states_p, states_s = [], []
    y_prompt = y_sample = None
    for i in range(depth):
        in_place = i > 0
        xp, a_p, b_p, pool_p, v_p = _mixer_prompt(xp, lw, i, batch, seq, in_place)
        xs, a_s, b_s, pool_s, v_s = _mixer_sample(xs, lw, i, state_conv_a, state_conv_b, state_pool,
                                                  nseq, dseq, in_place)
        states_p.append((a_p, b_p, pool_p, v_p))
        states_s.append((a_s, b_s, pool_s, v_s))

        last = i == depth - 1
        if i % 2 == 0:
            xp = _dense_ffn(xp, pp, i, lw)
            xs = _dense_ffn(xs, ps, i, lw)
        else:
            idx, wgt, xn_tiled = _router(xp, xs, i, lw)
            slot_tok, tile_expert, n_active, pos = _routing_tables(idx, n_tok)
            ys = _experts(xn_tiled, slot_tok, tile_expert, n_active, i, lw)
            fin = final_norm_g.reshape(1, D_MODEL) if last else None
            xp = _combine(xp, pp, i, pos, wgt, ys, lw, fin, 0)
            xs = _combine(xs, ps, i, pos, wgt, ys, lw, fin, n_prompt)
            if last:
                y_prompt, y_sample = xp, xs

    stack = lambda k, sts: jnp.stack([s[k] for s in sts])
    return (y_prompt.reshape(batch, seq, D_MODEL), y_sample.reshape(nseq, dseq, D_MODEL),
            stack(0, states_p), stack(1, states_p), stack(2, states_p), stack(3, states_p),
            stack(0, states_s), stack(1, states_s), stack(2, states_s), stack(3, states_s))
```

```python
import functools

import jax
import jax.numpy as jnp
import numpy as np
from jax import lax
from jax.experimental import pallas as pl
from jax.experimental.pallas import tpu as pltpu

F32 = jnp.float32
BF16 = jnp.bfloat16

D_MODEL = 1024
N_BRANCH = 4
D_BRANCH = 256
N_GROUPS = 4
GROUP_DIM = 64
CONV_A_W = 3
CONV_B_W = 31
POOL_WINDOWS = (2, 4, 8, 16)
POOL_MAX = 16
CHUNK = 128
PAST_LEN = 16384
D_PLE = 256
N_EXPERTS = 8
TOP_K = 2
EPS = 1e-6

OFF_A_H = 0
OFF_A_B = 256
OFF_A_C = 512
OFF_CF_A = 768
OFF_CF_B = 1024
OFF_POOL = 1280
OFF_U = 1536
OFF_V = 1792
OFF_GATE = 2048

LANES = 128
SUBLANES = 8
TM = 512
TL = 512
NB = 64
TF_EXPERT = 1792
ROW_CHUNK = 64
SEQ_CHUNK = 8
VMEM_LIMIT = 56 * 1024 * 1024

V_CB_BIAS, V_CB_LN_G, V_CB_LN_B, V_POOL_SCALE, V_SGU_LN_G, V_SGU_LN_B = range(6)


def _rms(x, g):
    return x * lax.rsqrt(jnp.mean(x * x, axis=-1, keepdims=True) + EPS) * g


def _ln(x, g, b):
    xc = x - jnp.mean(x, axis=-1, keepdims=True)
    var = jnp.mean(xc * xc, axis=-1, keepdims=True)
    return xc * lax.rsqrt(var + EPS) * g + b


def _bdot(a, w):
    return jnp.dot(a.astype(BF16), w, preferred_element_type=F32)


def _silu(x):
    return x * jax.nn.sigmoid(x)


def _const_spec(shape, layer=None):
    nd = len(shape)
    if layer is None:
        return pl.BlockSpec(shape, lambda *_: (0,) * nd, pipeline_mode=pl.Buffered(1))
    return pl.BlockSpec((None,) + tuple(shape), lambda *_: (layer,) + (0,) * nd, pipeline_mode=pl.Buffered(1))


def _sgu(u, v, swm, sbias):
    m = v.shape[0]
    group = lax.broadcasted_iota(jnp.int32, (CHUNK, D_BRANCH), 1) // GROUP_DIM
    outs = []
    for c in range(m // CHUNK):
        vc = v[c * CHUNK:(c + 1) * CHUNK].astype(BF16)
        r = jnp.dot(swm, vc, preferred_element_type=F32)
        s = sbias
        for g in range(N_GROUPS):
            s = s + jnp.where(group == g, r[g * CHUNK:(g + 1) * CHUNK], 0.0)
        outs.append(s)
    return u * jnp.concatenate(outs, axis=0)


GATE_PIECES_PER_BRANCH = D_MODEL // D_BRANCH
N_GATE_PIECES = N_BRANCH * GATE_PIECES_PER_BRANCH


def _gate_emitter(xnb, win_ref):
    pieces = []

    def emit(n=1):
        for _ in range(n):
            if len(pieces) < N_GATE_PIECES:
                lo = OFF_GATE + len(pieces) * D_BRANCH
                z = jnp.dot(xnb, win_ref[:, lo:lo + D_BRANCH], preferred_element_type=F32)
                pieces.append(jax.nn.sigmoid(z))
    return pieces, emit


def _gated_merge(x, branches, gates, wbr_ref, wout_ref):
    bb = [b.astype(BF16) for b in branches]
    cols = []
    for q in range(GATE_PIECES_PER_BRANCH):
        cs = slice(q * D_BRANCH, (q + 1) * D_BRANCH)
        m = None
        for i in range(N_BRANCH):
            t = gates[i * GATE_PIECES_PER_BRANCH + q] * \
                jnp.dot(bb[i], wbr_ref[i, :, cs], preferred_element_type=F32)
            m = t if m is None else m + t
        cols.append(m.astype(BF16))
    return x + jnp.dot(jnp.concatenate(cols, axis=1), wout_ref[...], preferred_element_type=F32)


def _pool_lane_windows(half):
    small, big = POOL_WINDOWS[2 * half], POOL_WINDOWS[2 * half + 1]
    lane = lax.broadcasted_iota(jnp.int32, (1, LANES), 1)
    is_big = lane >= GROUP_DIM
    return small, big, is_big


def _mixer_prompt_kernel(x_ref, g_ref, win_ref, caw_ref, cbw_ref, vec_ref, pw_ref, sw_ref, smask_ref,
                         sbias_ref, wbr_ref, wout_ref,
                         xo_ref, sta_ref, stb_ref, stp_ref, stv_ref,
                         sa, sb, sp, sh):
    i = pl.program_id(1)
    ha, hb, hp = SUBLANES, 4 * SUBLANES, 2 * SUBLANES

    @pl.when(i == 0)
    def _():
        sa[0:ha, :] = jnp.zeros((ha, D_BRANCH), F32)
        sb[0:hb, :] = jnp.zeros((hb, D_BRANCH), F32)
        sp[0:hp, :] = jnp.zeros((hp, D_BRANCH), F32)

    @pl.when(i > 0)
    def _():
        sa[0:ha, :] = sa[TL:TL + ha, :]
        sb[0:hb, :] = sb[TL:TL + hb, :]
        sp[0:hp, :] = sp[TL:TL + hp, :]

    x = x_ref[...]
    xnb = _rms(x, g_ref[...]).astype(BF16)
    vec = vec_ref[...]

    def zc(off):
        return jnp.dot(xnb, win_ref[:, off:off + D_BRANCH], preferred_element_type=F32)

    gates, emit_gate = _gate_emitter(xnb, win_ref)
    n_chunks = TL // ROW_CHUNK

    def spread(c, total):
        return ((c + 1) * total) // n_chunks - (c * total) // n_chunks

    def tap_conv(s_ref, base, w_ref, ntaps, n_gates=0, shifted=None):
        src = {}
        if shifted is not None:
            reach = {}
            for k in range(ntaps):
                q, r = divmod(base + k, SUBLANES)
                reach[r] = max(reach.get(r, 0), q)
            for r, qmax in sorted(reach.items()):
                if r:
                    n = TL + SUBLANES * qmax
                    shifted[r - 1, 0:n, :] = s_ref[r:r + n, :]
                    src[r] = shifted.at[r - 1]
                    emit_gate(r % 2)
                else:
                    src[r] = s_ref
        outs = []
        for c in range(n_chunks):
            acc = None
            for k in range(ntaps):
                if shifted is None:
                    seg = s_ref[pl.ds(base + k + c * ROW_CHUNK, ROW_CHUNK), :]
                else:
                    q, r = divmod(base + k, SUBLANES)
                    seg = src[r][pl.ds(SUBLANES * q + c * ROW_CHUNK, ROW_CHUNK), :]
                term = seg * w_ref[k:k + 1, :]
                acc = term if acc is None else acc + term
            outs.append(acc)
            emit_gate(spread(c, n_gates))
        return jnp.concatenate(outs, axis=0)

    sa[ha:ha + TL, :] = zc(OFF_A_C) * zc(OFF_A_H)
    br_a = zc(OFF_A_B) * tap_conv(sa, ha - (CONV_A_W - 1), caw_ref, CONV_A_W)
    sta_ref[0] = sa[TL + ha - (CONV_A_W - 1):TL + ha, :]

    sb[hb:hb + TL, :] = zc(OFF_CF_A) * jax.nn.sigmoid(zc(OFF_CF_B))
    yb = tap_conv(sb, hb - (CONV_B_W - 1), cbw_ref, CONV_B_W, n_gates=N_GATE_PIECES // 2, shifted=sh)
    yb = yb + vec[V_CB_BIAS:V_CB_BIAS + 1]
    br_b = _silu(_ln(yb, vec[V_CB_LN_G:V_CB_LN_G + 1], vec[V_CB_LN_B:V_CB_LN_B + 1]))
    stb_ref[0] = sb[TL + hb - (CONV_B_W - 1):TL + hb, :]

    sp[hp:hp + TL, :] = zc(OFF_POOL)
    halves = []
    for half in range(2):
        small, big, is_big = _pool_lane_windows(half)
        lanes = slice(half * LANES, (half + 1) * LANES)
        wl = jnp.where(is_big, float(big), float(small))
        outs = []
        for c in range(n_chunks):
            r0 = hp + c * ROW_CHUNK
            acc = None
            for j in range(big):
                seg = sp[pl.ds(r0 - j, ROW_CHUNK), lanes]
                if j >= small:
                    seg = jnp.where(is_big, seg, 0.0)
                acc = seg if acc is None else acc + seg
            pos = i * TL + c * ROW_CHUNK + lax.broadcasted_iota(jnp.int32, (ROW_CHUNK, LANES), 0)
            cnt = jnp.minimum(wl, (pos + 1).astype(F32))
            outs.append(acc / cnt - sp[pl.ds(r0, ROW_CHUNK), lanes])
            if half == 1:
                emit_gate(spread(c, N_GATE_PIECES // 4))
        halves.append(jnp.concatenate(outs, axis=0))
    pooled = jnp.concatenate(halves, axis=1)
    br_c = _bdot(pooled, pw_ref[...]) * vec[V_POOL_SCALE:V_POOL_SCALE + 1]
    stp_ref[0] = sp[TL + hp - (POOL_MAX - 1):TL + hp, :]

    v = _ln(zc(OFF_V), vec[V_SGU_LN_G:V_SGU_LN_G + 1], vec[V_SGU_LN_B:V_SGU_LN_B + 1])
    stv_ref[0] = v[TL - CHUNK:TL]
    swm = jnp.where(smask_ref[...] > 0, sw_ref[...], 0.0).astype(BF16)
    br_d = _sgu(zc(OFF_U), v, swm, sbias_ref[...])

    emit_gate(N_GATE_PIECES)
    xo_ref[...] = _gated_merge(x, (br_a, br_b, br_c, br_d), gates, wbr_ref, wout_ref)


def _mixer_sample_kernel(x_ref, g_ref, win_ref, caw_ref, cbw_ref, vec_ref, pw_ref, sw_ref, smask_ref,
                         sbias_ref, wbr_ref, wout_ref, ina_ref, inb_ref, inp_ref,
                         xo_ref, sta_ref, stb_ref, stp_ref, stv_ref,
                         sa, sb, sp):
    seq = SUBLANES
    ha, hb, hp = SUBLANES, 4 * SUBLANES, 2 * SUBLANES
    sa[:, ha - (CONV_A_W - 1):ha, :] = ina_ref[...]
    sb[:, hb - (CONV_B_W - 1):hb, :] = inb_ref[...]
    sp[:, hp - (POOL_MAX - 1):hp, :] = inp_ref[...]

    x = x_ref[...]
    xnb = _rms(x, g_ref[...]).astype(BF16)
    vec = vec_ref[...]

    def zc(off):
        return jnp.dot(xnb, win_ref[:, off:off + D_BRANCH], preferred_element_type=F32)

    def to3(a):
        return a.reshape(NB, seq, D_BRANCH)

    def to2(a):
        return a.reshape(NB * seq, D_BRANCH)

    gates, emit_gate = _gate_emitter(xnb, win_ref)
    n_chunks = NB // SEQ_CHUNK

    def tap_conv(s_ref, base, w_ref, ntaps, gates_per_chunk=0):
        outs = []
        for c in range(n_chunks):
            acc = None
            for k in range(ntaps):
                seg = s_ref[c * SEQ_CHUNK:(c + 1) * SEQ_CHUNK, pl.ds(base + k, seq), :]
                term = seg * w_ref[k:k + 1, :][None]
                acc = term if acc is None else acc + term
            outs.append(acc)
            emit_gate(gates_per_chunk)
        return to2(jnp.concatenate(outs, axis=0))

    sa[:, ha:ha + seq, :] = to3(zc(OFF_A_C) * zc(OFF_A_H))
    br_a = zc(OFF_A_B) * tap_conv(sa, ha - (CONV_A_W - 1), caw_ref, CONV_A_W)
    sta_ref[...] = sa[:, ha + seq - (CONV_A_W - 1):ha + seq, :]

    sb[:, hb:hb + seq, :] = to3(zc(OFF_CF_A) * jax.nn.sigmoid(zc(OFF_CF_B)))
    yb = tap_conv(sb, hb - (CONV_B_W - 1), cbw_ref, CONV_B_W, gates_per_chunk=-(-N_GATE_PIECES // n_chunks))
    yb = yb + vec[V_CB_BIAS:V_CB_BIAS + 1]
    br_b = _silu(_ln(yb, vec[V_CB_LN_G:V_CB_LN_G + 1], vec[V_CB_LN_B:V_CB_LN_B + 1]))
    stb_ref[...] = sb[:, hb + seq - (CONV_B_W - 1):hb + seq, :]

    sp[:, hp:hp + seq, :] = to3(zc(OFF_POOL))
    halves = []
    for half in range(2):
        small, big, is_big = _pool_lane_windows(half)
        lanes = slice(half * LANES, (half + 1) * LANES)
        wl = jnp.where(is_big, float(big), float(small))[None]
        outs = []
        for c in range(NB // SEQ_CHUNK):
            rows = slice(c * SEQ_CHUNK, (c + 1) * SEQ_CHUNK)
            acc = None
            for j in range(big):
                seg = sp[rows, pl.ds(hp - j, seq), lanes]
                if j >= small:
                    seg = jnp.where(is_big[None], seg, 0.0)
                acc = seg if acc is None else acc + seg
            outs.append(acc / wl - sp[rows, pl.ds(hp, seq), lanes])
        halves.append(jnp.concatenate(outs, axis=0).reshape(NB * seq, LANES))
    pooled = jnp.concatenate(halves, axis=1)
    br_c = _bdot(pooled, pw_ref[...]) * vec[V_POOL_SCALE:V_POOL_SCALE + 1]
    stp_ref[...] = sp[:, hp + seq - (POOL_MAX - 1):hp + seq, :]

    v = _ln(zc(OFF_V), vec[V_SGU_LN_G:V_SGU_LN_G + 1], vec[V_SGU_LN_B:V_SGU_LN_B + 1])
    stv_ref[...] = to3(v)
    swm = jnp.where(smask_ref[...] > 0, sw_ref[...], 0.0).astype(BF16)
    br_d = _sgu(zc(OFF_U), v, swm, sbias_ref[...])

    emit_gate(N_GATE_PIECES)
    xo_ref[...] = _gated_merge(x, (br_a, br_b, br_c, br_d), gates, wbr_ref, wout_ref)


def _mixer_weight_specs(layer):
    return [
        _const_spec((1, D_MODEL), layer),
        _const_spec((D_MODEL, OFF_GATE + N_BRANCH * D_MODEL), layer),
        _const_spec((CONV_A_W, D_BRANCH), layer),
        _const_spec((CONV_B_W, D_BRANCH), layer),
        _const_spec((SUBLANES, D_BRANCH), layer),
        _const_spec((D_BRANCH, D_BRANCH), layer),
        _const_spec((N_GROUPS * CHUNK, CHUNK), layer),
        _const_spec((N_GROUPS * CHUNK, CHUNK)),
        _const_spec((CHUNK, D_BRANCH), layer),
        _const_spec((N_BRANCH, D_BRANCH, D_MODEL), layer),
        _const_spec((D_MODEL, D_MODEL), layer),
    ]


def _mixer_prompt(xbuf, lw, layer, batch, seq, in_place):
    nt = seq // TL
    rows = lambda b, i: (b * nt + i, 0)
    st = lambda n: pl.BlockSpec((1, n, D_BRANCH), lambda b, i: (b, 0, 0))
    return pl.pallas_call(
        _mixer_prompt_kernel,
        grid=(batch, nt),
        in_specs=[pl.BlockSpec((TL, D_MODEL), rows)] + _mixer_weight_specs(layer),
        out_specs=[pl.BlockSpec((TL, D_MODEL), rows), st(CONV_A_W - 1), st(CONV_B_W - 1),
                   st(POOL_MAX - 1), st(CHUNK)],
        out_shape=[jax.ShapeDtypeStruct(xbuf.shape, F32),
                   jax.ShapeDtypeStruct((batch, CONV_A_W - 1, D_BRANCH), F32),
                   jax.ShapeDtypeStruct((batch, CONV_B_W - 1, D_BRANCH), F32),
                   jax.ShapeDtypeStruct((batch, POOL_MAX - 1, D_BRANCH), F32),
                   jax.ShapeDtypeStruct((batch, CHUNK, D_BRANCH), F32)],
        scratch_shapes=[pltpu.VMEM((TL + SUBLANES, D_BRANCH), F32),
                        pltpu.VMEM((TL + 4 * SUBLANES, D_BRANCH), F32),
                        pltpu.VMEM((TL + 2 * SUBLANES, D_BRANCH), F32),
                        pltpu.VMEM((SUBLANES - 1, TL + 4 * SUBLANES, D_BRANCH), F32)],
        input_output_aliases={0: 0} if in_place else {},
        compiler_params=pltpu.CompilerParams(dimension_semantics=("arbitrary", "arbitrary"),
                                             vmem_limit_bytes=VMEM_LIMIT),
        name="mixer_prompt",
    )(xbuf, lw["norm_mix_g"], lw["w_in"], lw["conv_a_w"], lw["conv_b_w"], lw["vec"], lw["pool_bd"],
      lw["sgu_w_p"], lw["sgu_mask_p"], lw["sgu_bias_p"], lw["w_branch"], lw["w_out"])


def _mixer_sample(xbuf, lw, layer, st_a, st_b, st_p, nseq, seq, in_place):
    assert seq == SUBLANES and nseq % NB == 0
    rows = lambda i: (i, 0)
    st = lambda n: pl.BlockSpec((NB, n, D_BRANCH), lambda i: (i, 0, 0))
    st_in = lambda n: pl.BlockSpec((None, NB, n, D_BRANCH), lambda i: (layer, i, 0, 0))
    return pl.pallas_call(
        _mixer_sample_kernel,
        grid=(nseq // NB,),
        in_specs=[pl.BlockSpec((NB * seq, D_MODEL), rows)] + _mixer_weight_specs(layer)
                 + [st_in(CONV_A_W - 1), st_in(CONV_B_W - 1), st_in(POOL_MAX - 1)],
        out_specs=[pl.BlockSpec((NB * seq, D_MODEL), rows), st(CONV_A_W - 1), st(CONV_B_W - 1),
                   st(POOL_MAX - 1), st(seq)],
        out_shape=[jax.ShapeDtypeStruct(xbuf.shape, F32),
                   jax.ShapeDtypeStruct((nseq, CONV_A_W - 1, D_BRANCH), F32),
                   jax.ShapeDtypeStruct((nseq, CONV_B_W - 1, D_BRANCH), F32),
                   jax.ShapeDtypeStruct((nseq, POOL_MAX - 1, D_BRANCH), F32),
                   jax.ShapeDtypeStruct((nseq, seq, D_BRANCH), F32)],
        scratch_shapes=[pltpu.VMEM((NB, 2 * SUBLANES, D_BRANCH), F32),
                        pltpu.VMEM((NB, 5 * SUBLANES, D_BRANCH), F32),
                        pltpu.VMEM((NB, 3 * SUBLANES, D_BRANCH), F32)],
        input_output_aliases={0: 0} if in_place else {},
        compiler_params=pltpu.CompilerParams(dimension_semantics=("arbitrary",),
                                             vmem_limit_bytes=VMEM_LIMIT),
        name="mixer_sample",
    )(xbuf, lw["norm_mix_g"], lw["w_in"], lw["conv_a_w"], lw["conv_b_w"], lw["vec"], lw["pool_bd"],
      lw["sgu_w_s"], lw["sgu_mask_s"], lw["sgu_bias_s"], lw["w_branch"], lw["w_out"],
      st_a, st_b, st_p)


def _ple_tail(x, p, gple_ref, wpg_ref, wpp_ref):
    gate = jax.nn.sigmoid(_bdot(_rms(x, gple_ref[...]), wpg_ref[...]))
    return x + gate * _bdot(p, wpp_ref[...])


def _dense_ffn_kernel(x_ref, p_ref, gffn_ref, wg_ref, wu_ref, wd_ref, gple_ref, wpg_ref, wpp_ref,
                      xo_ref):
    x = x_ref[...]
    xnb = _rms(x, gffn_ref[...]).astype(BF16)
    hs = []
    for c in range(wg_ref.shape[1] // D_BRANCH):
        cols = slice(c * D_BRANCH, (c + 1) * D_BRANCH)
        g = jnp.dot(xnb, wg_ref[:, cols], preferred_element_type=F32)
        u = jnp.dot(xnb, wu_ref[:, cols], preferred_element_type=F32)
        hs.append((_silu(g) * u).astype(BF16))
    f = jnp.dot(jnp.concatenate(hs, axis=1), wd_ref[...], preferred_element_type=F32)
    xo_ref[...] = _ple_tail(x + f, p_ref[...], gple_ref, wpg_ref, wpp_ref)


def _dense_ffn(xbuf, p_all, layer, lw):
    t = xbuf.shape[0]
    d_ff = lw["ffn_w_gate"].shape[2]
    j = layer // 2
    rows = lambda i: (i, 0)
    return pl.pallas_call(
        _dense_ffn_kernel,
        grid=(t // TM,),
        in_specs=[pl.BlockSpec((TM, D_MODEL), rows), pl.BlockSpec((None, TM, D_PLE), lambda i: (layer, i, 0)),
                  _const_spec((1, D_MODEL), layer), _const_spec((D_MODEL, d_ff), j),
                  _const_spec((D_MODEL, d_ff), j), _const_spec((d_ff, D_MODEL), j),
                  _const_spec((1, D_MODEL), layer), _const_spec((D_MODEL, D_MODEL), layer),
                  _const_spec((D_PLE, D_MODEL), layer)],
        out_specs=pl.BlockSpec((TM, D_MODEL), rows),
        out_shape=jax.ShapeDtypeStruct(xbuf.shape, F32),
        input_output_aliases={0: 0},
        compiler_params=pltpu.CompilerParams(dimension_semantics=("arbitrary",),
                                             vmem_limit_bytes=VMEM_LIMIT),
        name="dense_ffn_ple",
    )(xbuf, p_all, lw["norm_ffn_g"], lw["ffn_w_gate"], lw["ffn_w_up"], lw["ffn_w_down"],
      lw["norm_ple_g"], lw["ple_w_gate"], lw["ple_w_proj"])


def _router_kernel(n_first, xa_ref, xb_ref, g_ref, wr_ref, wrh_ref, idx_ref, wgt_ref, xn_ref):
    x = jnp.where(pl.program_id(0) < n_first, xa_ref[...], xb_ref[...])
    xn = _rms(x, g_ref[...])
    _store_row_tiled(xn_ref, xn)
    xh = xn.astype(BF16)
    xl = (xn - xh.astype(F32)).astype(BF16)
    d = jnp.dot(xh, wr_ref[...], preferred_element_type=F32) + \
        jnp.dot(xl, wrh_ref[...], preferred_element_type=F32)
    logits = d + pltpu.roll(d, LANES - N_EXPERTS, axis=1)
    lane = lax.broadcasted_iota(jnp.int32, logits.shape, 1)
    lane_f = lane.astype(F32)
    logits = jnp.where(lane < N_EXPERTS, logits, -jnp.inf)
    m1 = jnp.max(logits, axis=-1, keepdims=True)
    i1 = jnp.min(jnp.where(logits == m1, lane_f, float(LANES)), axis=-1, keepdims=True)
    rest = jnp.where(lane_f == i1, -jnp.inf, logits)
    m2 = jnp.max(rest, axis=-1, keepdims=True)
    i2 = jnp.min(jnp.where(rest == m2, lane_f, float(LANES)), axis=-1, keepdims=True)
    e2 = jnp.exp(m2 - m1)
    den = 1.0 + e2
    picks = jnp.where(lane == 0, i1, jnp.where(lane == 1, i2, 0.0))
    idx_ref[...] = jnp.transpose(picks)[0:TOP_K, :].astype(jnp.int32)
    wgt_ref[...] = jnp.where(lane == 0, 1.0 / den, jnp.where(lane == 1, e2 / den, 0.0))


def _router(xa, xb, layer, lw):
    j = layer // 2
    na, nb = xa.shape[0] // TM, xb.shape[0] // TM
    t = xa.shape[0] + xb.shape[0]
    rows = lambda i: (i, 0)
    return pl.pallas_call(
        functools.partial(_router_kernel, na),
        grid=(na + nb,),
        in_specs=[pl.BlockSpec((TM, D_MODEL), lambda i: (jnp.minimum(i, na - 1), 0)),
                  pl.BlockSpec((TM, D_MODEL), lambda i: (jnp.maximum(i - na, 0), 0)),
                  _const_spec((1, D_MODEL), layer), _const_spec((D_MODEL, LANES), j),
                  _const_spec((D_MODEL, LANES), j)],
        out_specs=[pl.BlockSpec((None, TOP_K, TM), lambda i: (i, 0, 0)), pl.BlockSpec((TM, LANES), rows),
                   pl.BlockSpec((TM * ROW_PIECES, LANES), rows)],
        out_shape=[jax.ShapeDtypeStruct((na + nb, TOP_K, TM), jnp.int32), jax.ShapeDtypeStruct((t, LANES), F32),
                   jax.ShapeDtypeStruct((t * ROW_PIECES, LANES), F32)],
        compiler_params=pltpu.CompilerParams(dimension_semantics=("arbitrary",)),
        name="router",
    )(xa, xb, lw["norm_ffn_g"], lw["router_w"], lw["router_wh"])


ROW_PIECES = D_MODEL // LANES
assert ROW_PIECES == SUBLANES
GATHER_UNROLL = 8


def _store_row_tiled(ref, x):
    for s in range(ROW_PIECES):
        ref[pl.ds(s, x.shape[0], stride=ROW_PIECES), :] = x[:, s * LANES:(s + 1) * LANES]


def _load_row_tiled(ref, n_rows):
    return jnp.concatenate([ref[pl.ds(s, n_rows, stride=ROW_PIECES), :] for s in range(ROW_PIECES)], axis=1)


def _start_row_gather(idx_ref, k, src_hbm, dst, sem):
    group = GATHER_UNROLL * ROW_PIECES

    def issue(it, carry):
        dst_base = pl.multiple_of(it * group, group)
        for u in range(GATHER_UNROLL):
            src_row = pl.multiple_of(idx_ref[k, it * GATHER_UNROLL + u] * ROW_PIECES, ROW_PIECES)
            pltpu.make_async_copy(src_hbm.at[pl.ds(src_row, ROW_PIECES)],
                                  dst.at[pl.ds(dst_base + u * ROW_PIECES, ROW_PIECES)],
                                  sem).start()
        return carry
    lax.fori_loop(0, dst.shape[0] // group, issue, 0)


def _wait_row_gather(src_hbm, dst, sem):
    pltpu.make_async_copy(src_hbm.at[pl.ds(0, dst.shape[0])], dst, sem).wait()


def _expert_kernel(n_fc, te_ref, na_ref, tokc_ref, tokn_ref, xn_hbm, wg_ref, wu_ref, wd_ref, ys_ref,
                   gbuf, xs, acc, sems):
    i, j = pl.program_id(0), pl.program_id(1)
    na = na_ref[0]
    slot = i % 2

    @pl.when(jnp.logical_and(i < na, j == 0))
    def _():
        @pl.when(i == 0)
        def _():
            _start_row_gather(tokc_ref, 0, xn_hbm, gbuf.at[0], sems.at[0])

        @pl.when(i + 1 < na)
        def _():
            _start_row_gather(tokn_ref, 0, xn_hbm, gbuf.at[1 - slot], sems.at[1 - slot])

        _wait_row_gather(xn_hbm, gbuf.at[slot], sems.at[slot])
        xs[...] = _load_row_tiled(gbuf.at[slot], TM).astype(BF16)

    def swiglu_piecewise(sink):
        hs = []
        for c in range(wg_ref.shape[1] // D_BRANCH):
            cols = slice(c * D_BRANCH, (c + 1) * D_BRANCH)
            g = jnp.dot(xs[...], wg_ref[:, cols], preferred_element_type=F32)
            u = jnp.dot(xs[...], wu_ref[:, cols], preferred_element_type=F32)
            hs.append((_silu(g) * u).astype(BF16))
        hb = jnp.concatenate(hs, axis=1)
        for q in range(D_MODEL // D_BRANCH):
            cs = slice(q * D_BRANCH, (q + 1) * D_BRANCH)
            sink(cs, jnp.dot(hb, wd_ref[:, cs], preferred_element_type=F32))

    def to_acc(first):
        def sink(cs, part):
            acc[:, cs] = part if first else acc[:, cs] + part
        return sink

    def to_out(with_acc):
        def sink(cs, part):
            tot = acc[:, cs] + part if with_acc else part
            for s in range(cs.start // LANES, cs.stop // LANES):
                lo = s * LANES - cs.start
                ys_ref[pl.ds(s, TM, stride=ROW_PIECES), :] = tot[:, lo:lo + LANES]
        return sink

    active = i < na
    if n_fc == 1:
        pl.when(active)(lambda: swiglu_piecewise(to_out(False)))
    else:
        pl.when(jnp.logical_and(active, j == 0))(lambda: swiglu_piecewise(to_acc(True)))
        if n_fc > 2:
            pl.when(jnp.logical_and(active, jnp.logical_and(j > 0, j < n_fc - 1)))(
                lambda: swiglu_piecewise(to_acc(False)))
        pl.when(jnp.logical_and(active, j == n_fc - 1))(lambda: swiglu_piecewise(to_out(True)))

    @pl.when(jnp.logical_and(i >= na, j == 0))
    def _():
        ys_ref[...] = jnp.zeros(ys_ref.shape, F32)


def _experts(xn_tiled, slot_tok, tile_expert, n_active, layer, lw):
    n_tiles = tile_expert.shape[0]
    d_ff = lw["moe_w_gate"].shape[3]
    n_fc = d_ff // TF_EXPERT
    m = layer // 2

    def fcol(i, j, na):
        return jnp.where(i < na[0], j, n_fc - 1)

    grid_spec = pltpu.PrefetchScalarGridSpec(
        num_scalar_prefetch=2,
        grid=(n_tiles, n_fc),
        in_specs=[
            pl.BlockSpec((None, 1, TM), lambda i, j, te, na: (i, 0, 0), memory_space=pltpu.SMEM),
            pl.BlockSpec((None, 1, TM), lambda i, j, te, na: (jnp.minimum(i + 1, n_tiles - 1), 0, 0),
                         memory_space=pltpu.SMEM),
            pl.BlockSpec(memory_space=pl.ANY),
            pl.BlockSpec((None, None, D_MODEL, TF_EXPERT),
                         lambda i, j, te, na: (m, te[i], 0, fcol(i, j, na))),
            pl.BlockSpec((None, None, D_MODEL, TF_EXPERT),
                         lambda i, j, te, na: (m, te[i], 0, fcol(i, j, na))),
            pl.BlockSpec((None, None, TF_EXPERT, D_MODEL),
                         lambda i, j, te, na: (m, te[i], fcol(i, j, na), 0)),
        ],
        out_specs=pl.BlockSpec((TM * ROW_PIECES, LANES), lambda i, j, te, na: (i, 0)),
        scratch_shapes=[pltpu.VMEM((2, TM * ROW_PIECES, LANES), F32), pltpu.VMEM((TM, D_MODEL), BF16),
                        pltpu.VMEM((TM, D_MODEL), F32), pltpu.SemaphoreType.DMA((2,))],
    )
    return pl.pallas_call(
        functools.partial(_expert_kernel, n_fc),
        grid_spec=grid_spec,
        out_shape=jax.ShapeDtypeStruct((n_tiles * TM * ROW_PIECES, LANES), F32),
        compiler_params=pltpu.CompilerParams(dimension_semantics=("arbitrary", "arbitrary"),
                                             vmem_limit_bytes=VMEM_LIMIT),
        name="moe_experts",
    )(tile_expert, n_active, slot_tok, slot_tok, xn_tiled, lw["moe_w_gate"], lw["moe_w_up"], lw["moe_w_down"])


def _combine_kernel(final, n_steps, pos0c_ref, pos1c_ref, pos0n_ref, pos1n_ref, x_ref, p_ref, wgt_ref, ys_hbm,
                    gple_ref, wpg_ref, wpp_ref, gfin_ref, o_ref, buf, xa, gn, sems):
    i = pl.program_id(0)

    def start(pos_refs, s):
        for k in range(TOP_K):
            _start_row_gather(pos_refs[k], 0, ys_hbm, buf.at[s, k], sems.at[s, k])

    @pl.when(i == 0)
    def _():
        start((pos0c_ref, pos1c_ref), 0)
        xa[1] = jnp.zeros((TM, D_MODEL), F32)
        gn[1] = jnp.zeros((TM, D_MODEL), BF16)

    def step(s):
        @pl.when(i + 1 < n_steps)
        def _():
            start((pos0n_ref, pos1n_ref), 1 - s)

        @pl.when(i < n_steps)
        def _():
            for k in range(TOP_K):
                _wait_row_gather(ys_hbm, buf.at[s, k], sems.at[s, k])

        wgt = wgt_ref[...]
        y0 = _load_row_tiled(buf.at[s, 0], TM)
        y1 = _load_row_tiled(buf.at[s, 1], TM)
        x1 = x_ref[...] + (y0 * wgt[:, 0:1] + y1 * wgt[:, 1:2])
        xa[s] = x1
        gn[s] = _rms(x1, gple_ref[...]).astype(BF16)

        gate = jax.nn.sigmoid(jnp.dot(gn[1 - s], wpg_ref[...], preferred_element_type=F32))
        x2 = xa[1 - s] + gate * _bdot(p_ref[...], wpp_ref[...])
        o_ref[...] = _rms(x2, gfin_ref[...]) if final else x2

    pl.when(i % 2 == 0)(lambda: step(0))
    pl.when(i % 2 == 1)(lambda: step(1))


def _combine(xbuf, p_all, layer, pos, wgt, ys, lw, final_g, row0):
    blk0 = row0 // TM
    n_rows = xbuf.shape[0]
    n_steps = n_rows // TM
    assert n_steps >= 2
    cur = lambda i: jnp.minimum(i, n_steps - 1)
    prev = lambda i: jnp.maximum(i - 1, 0)
    def pos_spec(k, ahead):
        return pl.BlockSpec((None, None, 1, TM), lambda i: (blk0 + cur(i + ahead), k, 0, 0),
                            memory_space=pltpu.SMEM)
    final = final_g is not None
    gfin = final_g if final else jnp.ones((1, D_MODEL), F32)
    return pl.pallas_call(
        functools.partial(_combine_kernel, final, n_steps),
        grid=(n_steps + 1,),
        in_specs=[pos_spec(0, 0), pos_spec(1, 0), pos_spec(0, 1), pos_spec(1, 1),
                  pl.BlockSpec((TM, D_MODEL), lambda i: (cur(i), 0)),
                  pl.BlockSpec((None, TM, D_PLE), lambda i: (layer, prev(i), 0)),
                  pl.BlockSpec((TM, LANES), lambda i: (blk0 + cur(i), 0)), pl.BlockSpec(memory_space=pl.ANY),
                  _const_spec((1, D_MODEL), layer), _const_spec((D_MODEL, D_MODEL), layer),
                  _const_spec((D_PLE, D_MODEL), layer), _const_spec((1, D_MODEL))],
        out_specs=pl.BlockSpec((TM, D_MODEL), lambda i: (prev(i), 0)),
        out_shape=jax.ShapeDtypeStruct((n_rows, D_MODEL), F32),
        scratch_shapes=[pltpu.VMEM((2, TOP_K, TM * ROW_PIECES, LANES), F32),
                        pltpu.VMEM((2, TM, D_MODEL), F32), pltpu.VMEM((2, TM, D_MODEL), BF16),
                        pltpu.SemaphoreType.DMA((2, TOP_K))],
        compiler_params=pltpu.CompilerParams(dimension_semantics=("arbitrary",),
                                             vmem_limit_bytes=VMEM_LIMIT),
        name="moe_combine_ple",
    )(pos, pos, pos, pos, xbuf, p_all, wgt, ys, lw["norm_ple_g"], lw["ple_w_gate"],
      lw["ple_w_proj"], gfin)


def _routing_tables(idx, n_tok):
    n_pairs = n_tok * TOP_K
    n_tiles = n_pairs // TM + N_EXPERTS
    experts = jnp.arange(N_EXPERTS, dtype=jnp.int32)
    e_flat = idx.reshape(-1)
    pair = jnp.arange(n_pairs, dtype=jnp.int32)
    pair_tok = (pair // (TOP_K * TM)) * TM + pair % TM
    onehot = (experts[:, None] == e_flat[None, :]).astype(jnp.int32)
    csum = jnp.cumsum(onehot, axis=1)
    sizes = csum[:, -1]
    padded = ((sizes + TM - 1) // TM) * TM
    pend = jnp.cumsum(padded)
    pstart = pend - padded
    pos = jnp.sum(onehot * (csum - 1 + pstart[:, None]), axis=0)
    fill_e = jnp.repeat(experts, TM)
    fill_r = jnp.tile(jnp.arange(TM, dtype=jnp.int32), N_EXPERTS)
    fill_need = jnp.sum((fill_e[:, None] == experts[None, :]) * (padded - sizes)[None, :], axis=1)
    keys = jnp.concatenate([e_flat, jnp.where(fill_r < fill_need, fill_e, N_EXPERTS)])
    toks = jnp.concatenate([pair_tok, jnp.zeros((N_EXPERTS * TM,), jnp.int32)])
    _, slot_tok = lax.sort((keys, toks), num_keys=1, is_stable=True)
    n_active = (pend[-1] // TM).astype(jnp.int32)
    tile_row = jnp.minimum(jnp.arange(n_tiles, dtype=jnp.int32), n_active - 1)
    tile_expert = jnp.minimum(
        jnp.sum((tile_row[:, None] * TM >= pend[None, :]).astype(jnp.int32), axis=1), N_EXPERTS - 1)
    return (slot_tok.reshape(n_tiles, 1, TM), tile_expert, n_active.reshape(1),
            pos.reshape(n_tok // TM, TOP_K, 1, TM))


def _sgu_tables(sgu_w, sgu_b, seq):
    depth = sgu_w.shape[0]
    r = np.arange(CHUNK)
    bias = jnp.repeat(jnp.swapaxes(sgu_b, 1, 2), GROUP_DIM, axis=2)
    if seq >= CHUNK:
        w = sgu_w
        mask = r[None, :] <= r[:, None]
    else:
        reps = CHUNK // seq
        w = jnp.tile(sgu_w[:, :, :seq, :seq], (1, 1, reps, reps))
        mask = (r[:, None] // seq == r[None, :] // seq) & (r[None, :] % seq <= r[:, None] % seq)
        bias = jnp.tile(bias[:, :seq], (1, reps, 1))
    mask = np.tile(mask.astype(np.float32), (N_GROUPS, 1))
    return w.reshape(depth, N_GROUPS * CHUNK, CHUNK), jnp.asarray(mask), bias


def _block_diag(w):
    out = jnp.zeros((w.shape[0], D_BRANCH, D_BRANCH), w.dtype)
    for g in range(N_GROUPS):
        out = out.at[:, g * GROUP_DIM:(g + 1) * GROUP_DIM, g * GROUP_DIM:(g + 1) * GROUP_DIM].set(w[:, g])
    return out


def kernel(x_prompt, x_sample, state_conv_a, state_conv_b, state_pool, p_prompt, p_sample, norm_mix_g, w_in, conv_a_w, conv_b_w, conv_b_bias, conv_b_ln_g, conv_b_ln_b, pool_w, pool_scale, sgu_ln_g, sgu_ln_b, sgu_w, sgu_b, w_branch, w_out, norm_ffn_g, ffn_w_gate, ffn_w_up, ffn_w_down, router_w, moe_w_gate, moe_w_up, moe_w_down, norm_ple_g, ple_w_gate, ple_w_proj, final_norm_g):
    batch, seq, _ = x_prompt.shape
    nseq, dseq, _ = x_sample.shape
    depth = w_in.shape[0]
    assert depth % 2 == 0
    assert PAST_LEN + 1 >= POOL_MAX
    assert seq % TL == 0 and TL % CHUNK == 0 and TL >= CONV_B_W + 1
    n_prompt, n_sample = batch * seq, nseq * dseq
    n_tok = n_prompt + n_sample
    assert n_prompt % TM == 0 and n_sample % TM == 0

    xp, xs = x_prompt.reshape(n_prompt, D_MODEL), x_sample.reshape(n_sample, D_MODEL)
    pp, ps = p_prompt.reshape(depth, n_prompt, D_PLE), p_sample.reshape(depth, n_sample, D_PLE)
    rows3 = lambda a: a.reshape(a.shape[0], 1, a.shape[1])
    zeros = jnp.zeros_like(conv_b_bias)
    sw_p, sm_p, sb_p = _sgu_tables(sgu_w, sgu_b, seq)
    sw_s, sm_s, sb_s = _sgu_tables(sgu_w, sgu_b, dseq)
    r_hi = router_w.astype(BF16)
    r_lo = (router_w - r_hi.astype(F32)).astype(BF16)
    lane_pad = lambda a: jnp.pad(a, ((0, 0), (0, 0), (0, LANES - a.shape[2])))
    lw = {
        "norm_mix_g": rows3(norm_mix_g), "w_in": w_in.astype(BF16),
        "conv_a_w": conv_a_w, "conv_b_w": conv_b_w,
        "vec": jnp.stack([conv_b_bias, conv_b_ln_g, conv_b_ln_b, pool_scale, sgu_ln_g, sgu_ln_b, zeros, zeros],
                         axis=1),
        "pool_bd": _block_diag(pool_w).astype(BF16),
        "sgu_w_p": sw_p, "sgu_mask_p": sm_p, "sgu_bias_p": sb_p,
        "sgu_w_s": sw_s, "sgu_mask_s": sm_s, "sgu_bias_s": sb_s,
        "w_branch": w_branch.astype(BF16), "w_out": w_out.astype(BF16),
        "norm_ffn_g": rows3(norm_ffn_g), "norm_ple_g": rows3(norm_ple_g),
        "ple_w_gate": ple_w_gate.astype(BF16), "ple_w_proj": ple_w_proj.astype(BF16),
        "ffn_w_gate": ffn_w_gate.astype(BF16), "ffn_w_up": ffn_w_up.astype(BF16),
        "ffn_w_down": ffn_w_down.astype(BF16),
        "router_w": lane_pad(jnp.concatenate([r_hi, r_lo], axis=2)), "router_wh": lane_pad(r_hi),
        "moe_w_gate": moe_w_gate.astype(BF16), "moe_w_up": moe_w_up.astype(BF16),
        "moe_w_down": moe_w_down.astype(BF16),
    }

    states_p, states_s = [], []
    y_prompt = y_sample = None
    for i in range(depth):
        in_place = i > 0
        xp, a_p, b_p, pool_p, v_p = _mixer_prompt(xp, lw, i, batch, seq, in_place)
        xs, a_s, b_s, pool_s, v_s = _mixer_sample(xs, lw, i, state_conv_a, state_conv_b, state_pool,
                                                  nseq, dseq, in_place)
        states_p.append((a_p, b_p, pool_p, v_p))
        states_s.append((a_s, b_s, pool_s, v_s))

        last = i == depth - 1
        if i % 2 == 0:
            xp = _dense_ffn(xp, pp, i, lw)
            xs = _dense_ffn(xs, ps, i, lw)
        else:
            idx, wgt, xn_tiled = _router(xp, xs, i, lw)
            slot_tok, tile_expert, n_active, pos = _routing_tables(idx, n_tok)
            ys = _experts(xn_tiled, slot_tok, tile_expert, n_active, i, lw)
            fin = final_norm_g.reshape(1, D_MODEL) if last else None
            xp = _combine(xp, pp, i, pos, wgt, ys, lw, fin, 0)
            xs = _combine(xs, ps, i, pos, wgt, ys, lw, fin, n_prompt)
            if last:
                y_prompt, y_sample = xp, xs

    stack = lambda k, sts: jnp.stack([s[k] for s in sts])
    return (y_prompt.reshape(batch, seq, D_MODEL), y_sample.reshape(nseq, dseq, D_MODEL),
            stack(0, states_p), stack(1, states_p), stack(2, states_p), stack(3, states_p),
            stack(0, states_s), stack(1, states_s), stack(2, states_s), stack(3, states_s))
```
